```python
import jax, jax.numpy as jnp
from jax import lax
import numpy as np

D_MODEL = 1024
BATCH = 8
SEQ = 8192
DEPTH = 1

SBA_HEADS = 8
SBA_HEAD_DIM = 64
SBA_WIDTH = SBA_HEADS * SBA_HEAD_DIM
Q_BLOCK = 128
SGU_GROUPS = 8
SGU_GROUP_DIM = 64
SGU_WIDTH = SGU_GROUPS * SGU_GROUP_DIM
CHUNK = 128
N_BRANCHES = 2
IN_COLS = 3 * SBA_WIDTH + 2 * SGU_WIDTH + N_BRANCHES * D_MODEL
N_GROUPS = 4
EXPERTS_PER_GROUP = 8
N_EXPERTS = N_GROUPS * EXPERTS_PER_GROUP
TOP_K = 2
D_EXPERT = D_MODEL // 2
EXPERT_BLOCK = 128
N_MOD = 6
EPS = 1e-6

kernel_name = "hybrid_stickbreak_sgu_hmoe_block"


def rms_norm(x, g):
    xf = x.astype(jnp.float32)
    y = xf * lax.rsqrt(jnp.mean(xf * xf, axis=-1, keepdims=True) + EPS)
    return y.astype(x.dtype) * g


def modulate(xn, shift, scale):
    return xn * (1 + scale[:, None, :]) + shift[:, None, :]


def stick_breaking_attention(q, k, v):
    B, H, S, dh = q.shape
    nb = S // Q_BLOCK
    qb = q.reshape(B, H, nb, Q_BLOCK, dh).transpose(2, 0, 1, 3, 4)
    kf = k.astype(jnp.float32)
    kpos = jnp.arange(S)
    scale = dh ** -0.5

    def block(args):
        qi, i = args
        qpos = i * Q_BLOCK + jnp.arange(Q_BLOCK)
        causal = kpos[None, :] < qpos[:, None]
        z = jnp.einsum('bhqd,bhkd->bhqk', qi.astype(jnp.float32), kf) * scale
        log_stay = jnp.where(causal, jax.nn.log_sigmoid(-z), 0.0)
        after = lax.cumsum(log_stay, axis=3, reverse=True) - log_stay
        w = jnp.where(causal, jnp.exp(jax.nn.log_sigmoid(z) + after), 0.0)
        return jnp.einsum('bhqk,bhkd->bhqd', w.astype(v.dtype), v)

    out = lax.map(block, (qb, jnp.arange(nb)))
    return out.transpose(1, 2, 0, 3, 4).reshape(B, H, S, dh)


def spatial_gating(uv, g_v, w_s, b_s):
    B, S, _ = uv.shape
    u, v = jnp.split(jax.nn.gelu(uv), 2, axis=-1)
    vf = v.astype(jnp.float32)
    mu = jnp.mean(vf, axis=-1, keepdims=True)
    var = jnp.mean(jnp.square(vf - mu), axis=-1, keepdims=True)
    vn = ((vf - mu) * lax.rsqrt(var + EPS)).astype(v.dtype) * g_v
    w_causal = w_s * jnp.tril(jnp.ones((CHUNK, CHUNK), w_s.dtype))
    vc = vn.reshape(B, S // CHUNK, CHUNK, SGU_GROUPS, SGU_GROUP_DIM)
    mix = jnp.einsum('gts,bnsgc->bntgc', w_causal, vc) + b_s.T[None, None, :, :, None]
    return u * mix.reshape(B, S, SGU_WIDTH)


def hierarchical_moe(xn, w_rg, b_rg, w_re, b_re, w_gate, w_up, w_down):
    B, S, D = xn.shape
    N = B * S
    xt = xn.reshape(N, D)
    g_logits = (xt @ w_rg + b_rg).astype(jnp.float32)
    g_prob = jax.nn.softmax(g_logits, axis=-1)
    g_sel = jnp.argmax(g_logits, axis=-1)
    g_w = jnp.take_along_axis(g_prob, g_sel[:, None], axis=-1)[:, 0]
    e_logits = (xt @ w_re + b_re).astype(jnp.float32).reshape(N, N_GROUPS, EXPERTS_PER_GROUP)
    e_logits = jnp.take_along_axis(e_logits, g_sel[:, None, None], axis=1)[:, 0]
    top_l, top_j = lax.top_k(e_logits, TOP_K)
    top_w = jax.nn.softmax(top_l, axis=-1) * g_w[:, None]
    eid = (g_sel[:, None] * EXPERTS_PER_GROUP + top_j).reshape(-1)
    tok = jnp.repeat(jnp.arange(N), TOP_K)
    wts = top_w.reshape(-1)
    M = eid.shape[0]

    order = jnp.argsort(eid)
    e_s, tok_s, w_s = eid[order], tok[order], wts[order]
    counts = jnp.bincount(eid, length=N_EXPERTS)
    padded = (counts + EXPERT_BLOCK - 1) // EXPERT_BLOCK * EXPERT_BLOCK
    start = jnp.cumsum(counts) - counts
    pend = jnp.cumsum(padded)
    pstart = pend - padded
    dest = pstart[e_s] + jnp.arange(M) - start[e_s]
    m_pad = M + N_EXPERTS * EXPERT_BLOCK
    nblk = m_pad // EXPERT_BLOCK
    tok_buf = jnp.full((m_pad,), N, jnp.int32).at[dest].set(tok_s.astype(jnp.int32))
    w_buf = jnp.zeros((m_pad,), jnp.float32).at[dest].set(w_s)
    blk_e = jnp.minimum(jnp.searchsorted(pend, jnp.arange(nblk) * EXPERT_BLOCK, side='right'),
                        N_EXPERTS - 1)
    x_pad = jnp.concatenate([xt, jnp.zeros((1, D), xt.dtype)], axis=0)
    xb = x_pad[tok_buf].reshape(nblk, EXPERT_BLOCK, D)

    def expert_block(args):
        xblk, e = args
        hid = jax.nn.silu(xblk @ w_gate[e]) * (xblk @ w_up[e])
        return hid @ w_down[e]

    yb = lax.map(expert_block, (xb, blk_e)).reshape(m_pad, D)
    y = jax.ops.segment_sum(yb.astype(jnp.float32) * w_buf[:, None], tok_buf, num_segments=N + 1)[:N]
    return y.astype(xn.dtype).reshape(B, S, D)


def setup_inputs(seed: int = 0) -> dict:
    key = jax.random.key(seed)
    ks = jax.random.split(key, 24)
    f32 = jnp.float32
    L, D = DEPTH, D_MODEL
    nrm = lambda k, shp, s: jax.random.normal(k, shp, f32) * s
    return {
        "x": nrm(ks[0], (BATCH, SEQ, D), 1.0),
        "c": nrm(ks[1], (BATCH, D), 1.0),
        "g_mix": 1.0 + nrm(ks[2], (L, D), 0.02),
        "g_ffn": 1.0 + nrm(ks[3], (L, D), 0.02),
        "w_ada": nrm(ks[4], (L, D, N_MOD * D), 0.5 * D ** -0.5),
        "b_ada": nrm(ks[5], (L, N_MOD * D), 0.02),
        "w_in": nrm(ks[6], (L, D, IN_COLS), D ** -0.5),
        "w_sba_out": nrm(ks[7], (L, SBA_WIDTH, D), SBA_WIDTH ** -0.5),
        "g_sgu": 1.0 + nrm(ks[8], (L, SGU_WIDTH), 0.02),
        "w_spatial": nrm(ks[9], (L, SGU_GROUPS, CHUNK, CHUNK), CHUNK ** -0.5),
        "b_spatial": 1.0 + nrm(ks[10], (L, SGU_GROUPS, CHUNK), 0.02),
        "w_sgu_out": nrm(ks[11], (L, SGU_WIDTH, D), SGU_WIDTH ** -0.5),
        "w_out": nrm(ks[12], (L, D, D), D ** -0.5),
        "w_router_group": nrm(ks[13], (L, D, N_GROUPS), D ** -0.5),
        "b_router_group": nrm(ks[14], (L, N_GROUPS), 0.01),
        "w_router_expert": nrm(ks[15], (L, D, N_EXPERTS), D ** -0.5),
        "b_router_expert": nrm(ks[16], (L, N_EXPERTS), 0.01),
        "w_expert_gate": nrm(ks[17], (L, N_EXPERTS, D, D_EXPERT), D ** -0.5),
        "w_expert_up": nrm(ks[18], (L, N_EXPERTS, D, D_EXPERT), D ** -0.5),
        "w_expert_down": nrm(ks[19], (L, N_EXPERTS, D_EXPERT, D), D_EXPERT ** -0.5),
        "g_final": 1.0 + nrm(ks[20], (D,), 0.02),
    }


def reference(x, c, g_mix, g_ffn, w_ada, b_ada, w_in, w_sba_out, g_sgu, w_spatial, b_spatial,
              w_sgu_out, w_out, w_router_group, b_router_group, w_router_expert, b_router_expert,
              w_expert_gate, w_expert_up, w_expert_down, g_final):
    B, S, D = x.shape
    h = x
    c_act = jax.nn.silu(c)
    splits = [SBA_WIDTH, 2 * SBA_WIDTH, 3 * SBA_WIDTH, 3 * SBA_WIDTH + 2 * SGU_WIDTH]

    def heads(t):
        return t.reshape(B, S, SBA_HEADS, SBA_HEAD_DIM).transpose(0, 2, 1, 3)

    for l in range(DEPTH):
        mod = c_act @ w_ada[l] + b_ada[l]
        sh_m, sc_m, gt_m, sh_f, sc_f, gt_f = jnp.split(mod, N_MOD, axis=-1)

        n = modulate(rms_norm(h, g_mix[l]), sh_m, sc_m)
        proj = n @ w_in[l]
        q, k, v, uv, gates = jnp.split(proj, splits, axis=-1)
        y_a = stick_breaking_attention(heads(q), heads(k), heads(v))
        y_a = y_a.transpose(0, 2, 1, 3).reshape(B, S, SBA_WIDTH)
        y_b = spatial_gating(uv, g_sgu[l], w_spatial[l], b_spatial[l])
        gate_a, gate_b = jnp.split(jax.nn.sigmoid(gates), N_BRANCHES, axis=-1)
        merged = gate_a * (y_a @ w_sba_out[l]) + gate_b * (y_b @ w_sgu_out[l])
        h = h + gt_m[:, None, :] * (merged @ w_out[l])

        n = modulate(rms_norm(h, g_ffn[l]), sh_f, sc_f)
        y = hierarchical_moe(n, w_router_group[l], b_router_group[l], w_router_expert[l],
                             b_router_expert[l], w_expert_gate[l], w_expert_up[l], w_expert_down[l])
        h = h + gt_f[:, None, :] * y
    return rms_norm(h, g_final)
```

```python
import functools

import jax
import jax.numpy as jnp
from jax import lax
from jax.experimental import pallas as pl
from jax.experimental.pallas import tpu as pltpu

F32 = jnp.float32
BF16 = jnp.bfloat16

HEAD_DIM = 64
TOP_K = 2
EPS = 1e-6
LANES = 128
ATTN_BLOCK = 256
EXPERT_ROWS = 256
TOKEN_TILE = 256
PROJ_TILE = 512
VMEM_LIMIT = 56 * 1024 * 1024


def _cparams(*sem):
    return pltpu.CompilerParams(dimension_semantics=sem, vmem_limit_bytes=VMEM_LIMIT)


def _split_bf16(a):
    hi = a.astype(BF16)
    lo = (a - hi.astype(F32)).astype(BF16)
    return hi, lo


def _dot(a, b):
    return jnp.dot(a, b, preferred_element_type=F32)


def _dot3(a, b):
    ah, al = _split_bf16(a)
    bh, bl = _split_bf16(b)
    return _dot(ah, bh) + _dot(ah, bl) + _dot(al, bh)


def _rms(x, g):
    ms = jnp.mean(x * x, axis=-1, keepdims=True)
    return x * lax.rsqrt(ms + EPS) * g


def _mod_kernel(c_ref, w_ref, b_ref, o_ref):
    c = c_ref[...]
    ca = c * (1.0 / (1.0 + jnp.exp(-c)))
    o_ref[...] = _dot3(ca, w_ref[...]) + b_ref[...]


def _mod_call(c, w_ada, b_ada):
    bsz, d = c.shape
    n = w_ada.shape[1]
    tn = n // 4 if n % (4 * LANES) == 0 else n
    return pl.pallas_call(
        _mod_kernel,
        out_shape=jax.ShapeDtypeStruct((bsz, n), F32),
        grid=(n // tn,),
        in_specs=[pl.BlockSpec((bsz, d), lambda j: (0, 0)),
                  pl.BlockSpec((d, tn), lambda j: (0, j)),
                  pl.BlockSpec((1, tn), lambda j: (0, j))],
        out_specs=pl.BlockSpec((bsz, tn), lambda j: (0, j)),
        compiler_params=_cparams("arbitrary"),
        name="mod",
    )(c, w_ada, b_ada.reshape(1, n))


def _proj_kernel(x_ref, sh_ref, sc_ref, g_ref, w_ref, q_ref, k_ref, v_ref, uv_ref, gt_ref, *, widths):
    x = x_ref[...]
    n = _rms(x, g_ref[...]) * (1.0 + sc_ref[0]) + sh_ref[0]
    nb = n.astype(BF16)
    off = 0
    for ref, wd in zip((q_ref, k_ref, v_ref, uv_ref, gt_ref), widths):
        p = _dot(nb, w_ref[:, off:off + wd])
        if ref is q_ref:
            p = p * (HEAD_DIM ** -0.5)
        ref[...] = p.astype(ref.dtype)
        off += wd


def _proj_call(x2, sh, sc, g, w_in_bf, seq, widths):
    n_tok, d = x2.shape
    tm = min(PROJ_TILE, seq)
    per_b = seq // tm
    cols = w_in_bf.shape[1]
    tok = lambda w: pl.BlockSpec((tm, w), lambda i: (i, 0))
    vec = pl.BlockSpec((1, 1, d), lambda i: (i // per_b, 0, 0))
    return pl.pallas_call(
        functools.partial(_proj_kernel, widths=widths),
        out_shape=[jax.ShapeDtypeStruct((n_tok, w), BF16) for w in widths],
        grid=(n_tok // tm,),
        in_specs=[tok(d), vec, vec,
                  pl.BlockSpec((1, d), lambda i: (0, 0)),
                  pl.BlockSpec((d, cols), lambda i: (0, 0))],
        out_specs=[tok(w) for w in widths],
        compiler_params=_cparams("arbitrary"),
        name="proj",
    )(x2, sh, sc, g, w_in_bf)


def _attn_tile(qh, kh, vh, u_tri, carry, causal):
    s = lax.dot_general(qh, kh, (((1,), (1,)), ((), ())), preferred_element_type=F32)
    sp = jnp.maximum(s, 0.0) + jnp.log(1.0 + jnp.exp(-jnp.abs(s)))
    log_beta = s - sp
    if causal is not None:
        sp = jnp.where(causal, sp, 0.0)
    hi, lo = _split_bf16(sp)
    cum = _dot(hi, u_tri) + _dot(lo, u_tri)
    w = jnp.exp(log_beta - cum - carry)
    if causal is not None:
        w = jnp.where(causal, w, 0.0)
    pv = _dot(w.astype(BF16), vh)
    return pv, carry + jnp.sum(sp, axis=1, keepdims=True)


def _attn_kernel(q_ref, k_ref, v_ref, u_ref, o_ref, *, heads, blk):
    qi = pl.program_id(1)
    u_tri = u_ref[...]
    row = lax.broadcasted_iota(jnp.int32, (blk, blk), 0)
    col = lax.broadcasted_iota(jnp.int32, (blk, blk), 1)
    causal = col < row
    for h in range(heads):
        lanes = slice(h * HEAD_DIM, (h + 1) * HEAD_DIM)
        qh = q_ref[0, :, lanes]
        start = pl.multiple_of(qi * blk, blk)
        acc, carry = _attn_tile(qh, k_ref[0, pl.ds(start, blk), lanes], v_ref[0, pl.ds(start, blk), lanes],
                                u_tri, jnp.zeros((blk, 1), F32), causal)

        def body(i, state, qh=qh, lanes=lanes):
            acc, carry = state
            st = pl.multiple_of((qi - 1 - i) * blk, blk)
            pv, carry = _attn_tile(qh, k_ref[0, pl.ds(st, blk), lanes], v_ref[0, pl.ds(st, blk), lanes],
                                   u_tri, carry, None)
            return acc + pv, carry

        acc, carry = lax.fori_loop(0, qi, body, (acc, carry))
        o_ref[0, :, lanes] = acc.astype(o_ref.dtype)


def _attn_call(q, k, v, heads):
    bsz, seq, width = q.shape
    blk = min(ATTN_BLOCK, seq)
    r = lax.broadcasted_iota(jnp.int32, (blk, blk), 0)
    c = lax.broadcasted_iota(jnp.int32, (blk, blk), 1)
    u_tri = (r > c).astype(BF16)
    full = pl.BlockSpec((1, seq, width), lambda b, i: (b, 0, 0))
    return pl.pallas_call(
        functools.partial(_attn_kernel, heads=heads, blk=blk),
        out_shape=jax.ShapeDtypeStruct((bsz, seq, width), BF16),
        grid=(bsz, seq // blk),
        in_specs=[pl.BlockSpec((1, blk, width), lambda b, i: (b, i, 0)), full, full,
                  pl.BlockSpec((blk, blk), lambda b, i: (0, 0))],
        out_specs=pl.BlockSpec((1, blk, width), lambda b, i: (b, i, 0)),
        compiler_params=_cparams("arbitrary", "arbitrary"),
        name="attn",
    )(q, k, v, u_tri)


def _gelu_tanh(x):
    return 0.5 * x * (1.0 + jnp.tanh(0.7978845608028654 * (x + 0.044715 * (x * x * x))))


def _sigmoid(x):
    return 1.0 / (1.0 + jnp.exp(-x))


def _mix_kernel(ya_ref, uv_ref, gt_ref, x_ref, gtm_ref, shf_ref, scf_ref, gsgu_ref, ws_ref, bs_ref,
                wa_ref, wb_ref, wo_ref, gffn_ref, wr_ref, br_ref, ltri_ref,
                h_ref, xn_ref, route_ref, cnt_ref, yb_scr, cnt_scr,
                *, chunk, groups, n_groups, per_group):
    step = pl.program_id(0)
    tm = x_ref.shape[0]
    sgu = gsgu_ref.shape[1]
    gdim = sgu // groups

    @pl.when(step == 0)
    def _():
        cnt_scr[...] = jnp.zeros_like(cnt_scr)

    act = _gelu_tanh(uv_ref[...].astype(F32))
    u = act[:, :sgu]
    v = act[:, sgu:]
    mu = jnp.mean(v, axis=-1, keepdims=True)
    vc = v - mu
    var = jnp.mean(vc * vc, axis=-1, keepdims=True)
    vn = vc * lax.rsqrt(var + EPS) * gsgu_ref[...]
    r = lax.broadcasted_iota(jnp.int32, (chunk, chunk), 0)
    c = lax.broadcasted_iota(jnp.int32, (chunk, chunk), 1)
    wcs = [jnp.where(r >= c, ws_ref[g], 0.0).astype(BF16) for g in range(groups)]
    pair = LANES // gdim
    lane = lax.broadcasted_iota(jnp.int32, (chunk, LANES), 1)
    for ci in range(tm // chunk):
        rows = slice(ci * chunk, (ci + 1) * chunk)
        for p in range(groups // pair):
            lanes = slice(p * LANES, (p + 1) * LANES)
            slab = vn[rows, lanes]
            mix = bs_ref[:, lanes]
            for j in range(pair):
                sel = (lane >= j * gdim) & (lane < (j + 1) * gdim)
                mix = mix + _dot(wcs[p * pair + j], jnp.where(sel, slab, 0.0).astype(BF16))
            yb_scr[rows, lanes] = (u[rows, lanes] * mix).astype(BF16)

    d = x_ref.shape[1]
    gates = _sigmoid(gt_ref[...].astype(F32))
    merged = gates[:, :d] * _dot(ya_ref[...], wa_ref[...]) + gates[:, d:] * _dot(yb_scr[...], wb_ref[...])
    h = x_ref[...] + gtm_ref[0] * _dot(merged.astype(BF16), wo_ref[...])
    h_ref[...] = h

    xn = _rms(h, gffn_ref[...]) * (1.0 + scf_ref[0]) + shf_ref[0]
    xn_ref[...] = xn
    logits = _dot3(xn, wr_ref[...]) + br_ref[...]
    ln = lax.broadcasted_iota(jnp.int32, (tm, LANES), 1)
    neg = -jnp.inf
    gl = jnp.where(ln < n_groups, logits, neg)
    gmax = jnp.max(gl, axis=1, keepdims=True)
    g_sel = jnp.min(jnp.where(gl == gmax, ln, LANES), axis=1, keepdims=True)
    g_w = 1.0 / jnp.sum(jnp.where(ln < n_groups, jnp.exp(logits - gmax), 0.0), axis=1, keepdims=True)
    lo_lane = n_groups + per_group * g_sel
    el = jnp.where((ln >= lo_lane) & (ln < lo_lane + per_group), logits, neg)
    m1 = jnp.max(el, axis=1, keepdims=True)
    i1 = jnp.min(jnp.where(el == m1, ln, LANES), axis=1, keepdims=True)
    el2 = jnp.where(ln == i1, neg, el)
    m2 = jnp.max(el2, axis=1, keepdims=True)
    i2 = jnp.min(jnp.where(el2 == m2, ln, LANES), axis=1, keepdims=True)
    t = jnp.exp(m2 - m1)
    w1 = g_w / (1.0 + t)
    w2 = g_w * t / (1.0 + t)

    oh1 = ln == i1
    oh2 = ln == i2
    both = jnp.where(oh1 | oh2, 1.0, 0.0)
    before = _dot(ltri_ref[...], both.astype(BF16)) + cnt_scr[...]
    rank1 = jnp.sum(jnp.where(oh1, before, 0.0), axis=1, keepdims=True)
    rank2 = jnp.sum(jnp.where(oh2, before, 0.0), axis=1, keepdims=True)
    cnt_scr[...] = cnt_scr[...] + jnp.sum(both, axis=0, keepdims=True)
    cnt_ref[...] = cnt_scr[...]

    e1 = (i1 - n_groups).astype(F32)
    e2 = (i2 - n_groups).astype(F32)
    out = jnp.zeros((tm, LANES), F32)
    for idx, val in enumerate((e1, e2, w1, w2, rank1, rank2)):
        out = jnp.where(ln == idx, val, out)
    route_ref[...] = out


def _mix_call(ya, uv, gates, x2, gtm, shf, scf, g_sgu, w_spatial, bias_full, wa, wb, wo, g_ffn, wr, br,
              seq, n_groups, per_group):
    n_tok, d = x2.shape
    tm = min(TOKEN_TILE, seq)
    per_b = seq // tm
    groups, chunk, _ = w_spatial.shape
    sgu = g_sgu.shape[1]
    r = lax.broadcasted_iota(jnp.int32, (tm, tm), 0)
    c = lax.broadcasted_iota(jnp.int32, (tm, tm), 1)
    ltri = (r > c).astype(BF16)
    tok = lambda w: pl.BlockSpec((tm, w), lambda i: (i, 0))
    vec = pl.BlockSpec((1, 1, d), lambda i: (i // per_b, 0, 0))
    const = lambda a: pl.BlockSpec(a.shape, lambda i: (0,) * a.ndim)
    ins = [ya, uv, gates, x2, gtm, shf, scf, g_sgu, w_spatial, bias_full, wa, wb, wo, g_ffn, wr, br, ltri]
    in_specs = [tok(ya.shape[1]), tok(uv.shape[1]), tok(gates.shape[1]), tok(d), vec, vec, vec]
    in_specs += [const(a) for a in ins[7:]]
    return pl.pallas_call(
        functools.partial(_mix_kernel, chunk=chunk, groups=groups, n_groups=n_groups, per_group=per_group),
        out_shape=[jax.ShapeDtypeStruct((n_tok, d), F32), jax.ShapeDtypeStruct((n_tok, d), F32),
                   jax.ShapeDtypeStruct((n_tok, LANES), F32), jax.ShapeDtypeStruct((1, LANES), F32)],
        grid=(n_tok // tm,),
        in_specs=in_specs,
        out_specs=[tok(d), tok(d), tok(LANES), pl.BlockSpec((1, LANES), lambda i: (0, 0))],
        scratch_shapes=[pltpu.VMEM((tm, sgu), BF16), pltpu.VMEM((1, LANES), F32)],
        compiler_params=_cparams("arbitrary"),
        name="mix",
    )(*ins)


def _row_copy(src, s, dst, d, sem):
    return pltpu.make_async_copy(src.at[pl.ds(s, 1)], dst.at[pl.ds(d, 1)], sem)


def _dispatch_kernel(dest_ref, xn_ref, init_ref, xs_ref, sem, *, tm):
    del init_ref
    base = pl.program_id(0) * tm

    def copies(j):
        return [_row_copy(xn_ref, base + j, xs_ref, dest_ref[0, 0, TOP_K * j + k], sem) for k in range(TOP_K)]

    def start(j, _):
        for cp in copies(j):
            cp.start()
        return 0

    def wait(j, _):
        for cp in copies(j):
            cp.wait()
        return 0

    lax.fori_loop(0, tm, start, 0)
    lax.fori_loop(0, tm, wait, 0)


def _dispatch_call(dest, xn, m_pad, seq):
    n_tok, d = xn.shape
    tm = min(TOKEN_TILE, seq)
    steps = n_tok // tm
    any_spec = pl.BlockSpec(memory_space=pl.ANY)
    return pl.pallas_call(
        functools.partial(_dispatch_kernel, tm=tm),
        out_shape=jax.ShapeDtypeStruct((m_pad, d), F32),
        grid=(steps,),
        in_specs=[pl.BlockSpec((1, 1, TOP_K * tm), lambda i: (i, 0, 0), memory_space=pltpu.SMEM),
                  any_spec, any_spec],
        out_specs=any_spec,
        scratch_shapes=[pltpu.SemaphoreType.DMA(())],
        input_output_aliases={2: 0},
        compiler_params=_cparams("arbitrary"),
        name="dispatch",
    )(dest.reshape(steps, 1, TOP_K * tm), xn, jnp.zeros((m_pad, d), F32))


def _expert_kernel(blk_e_ref, used_ref, xs_ref, wg_ref, wu_ref, wd_ref, y_ref):
    del blk_e_ref

    @pl.when(pl.program_id(0) < used_ref[0])
    def _():
        xb = xs_ref[...].astype(BF16)
        g = _dot(xb, wg_ref[0])
        hid = g * _sigmoid(g) * _dot(xb, wu_ref[0])
        y_ref[...] = _dot(hid.astype(BF16), wd_ref[0])

    @pl.when(pl.program_id(0) >= used_ref[0])
    def _():
        y_ref[...] = jnp.zeros_like(y_ref)


def _expert_call(blk_e, used, xs, wg, wu, wd):
    m_pad, d = xs.shape
    f = wg.shape[2]
    rows = EXPERT_ROWS
    return pl.pallas_call(
        _expert_kernel,
        out_shape=jax.ShapeDtypeStruct((m_pad, d), F32),
        grid_spec=pltpu.PrefetchScalarGridSpec(
            num_scalar_prefetch=2,
            grid=(m_pad // rows,),
            in_specs=[pl.BlockSpec((rows, d), lambda i, be, us: (i, 0)),
                      pl.BlockSpec((1, d, f), lambda i, be, us: (be[i], 0, 0)),
                      pl.BlockSpec((1, d, f), lambda i, be, us: (be[i], 0, 0)),
                      pl.BlockSpec((1, f, d), lambda i, be, us: (be[i], 0, 0))],
            out_specs=pl.BlockSpec((rows, d), lambda i, be, us: (i, 0))),
        compiler_params=_cparams("arbitrary"),
        name="experts",
    )(blk_e, used, xs, wg, wu, wd)


def _combine_kernel(dest_ref, h_ref, route_ref, gtf_ref, gfin_ref, y_ref, o_ref, buf, sem, *, tm, final):
    def copies(j):
        return [pltpu.make_async_copy(y_ref.at[pl.ds(dest_ref[0, 0, TOP_K * j + k], 1)],
                                      buf.at[k, pl.ds(j, 1)], sem) for k in range(TOP_K)]

    def start(j, _):
        for cp in copies(j):
            cp.start()
        return 0

    def wait(j, _):
        for cp in copies(j):
            cp.wait()
        return 0

    lax.fori_loop(0, tm, start, 0)
    lax.fori_loop(0, tm, wait, 0)
    route = route_ref[...]
    y = route[:, 2:3] * buf[0] + route[:, 3:4] * buf[1]
    h = h_ref[...] + gtf_ref[0] * y
    o_ref[...] = _rms(h, gfin_ref[...]) if final else h


def _combine_call(dest, h, route, gtf, g_final, yb, seq, final):
    n_tok, d = h.shape
    tm = min(TOKEN_TILE, seq)
    per_b = seq // tm
    steps = n_tok // tm
    tok = lambda w: pl.BlockSpec((tm, w), lambda i: (i, 0))
    return pl.pallas_call(
        functools.partial(_combine_kernel, tm=tm, final=final),
        out_shape=jax.ShapeDtypeStruct((n_tok, d), F32),
        grid=(steps,),
        in_specs=[pl.BlockSpec((1, 1, TOP_K * tm), lambda i: (i, 0, 0), memory_space=pltpu.SMEM),
                  tok(d), tok(LANES),
                  pl.BlockSpec((1, 1, d), lambda i: (i // per_b, 0, 0)),
                  pl.BlockSpec((1, d), lambda i: (0, 0)),
                  pl.BlockSpec(memory_space=pl.ANY)],
        out_specs=tok(d),
        scratch_shapes=[pltpu.VMEM((TOP_K, tm, d), F32), pltpu.SemaphoreType.DMA(())],
        compiler_params=_cparams("arbitrary"),
        name="combine",
    )(dest.reshape(steps, 1, TOP_K * tm), h, route, gtf, g_final, yb)


def _routing_plan(route, counts, n_experts, m_pad):
    eid = route[:, 0:TOP_K].astype(jnp.int32)
    rank = route[:, 4:4 + TOP_K].astype(jnp.int32)
    cnt = counts.astype(jnp.int32)
    padded = (cnt + EXPERT_ROWS - 1) // EXPERT_ROWS * EXPERT_ROWS
    pend = jnp.cumsum(padded)
    pstart = pend - padded
    dest = jnp.take(pstart, eid) + rank
    nblk = m_pad // EXPERT_ROWS
    blk_e = jnp.minimum(jnp.searchsorted(pend, jnp.arange(nblk, dtype=jnp.int32) * EXPERT_ROWS, side='right'),
                        n_experts - 1).astype(jnp.int32)
    used = (pend[-1:] // EXPERT_ROWS).astype(jnp.int32)
    return dest, blk_e, used


def kernel(x, c, g_mix, g_ffn, w_ada, b_ada, w_in, w_sba_out, g_sgu, w_spatial, b_spatial, w_sgu_out, w_out,
           w_router_group, b_router_group, w_router_expert, b_router_expert, w_expert_gate, w_expert_up,
           w_expert_down, g_final):
    bsz, seq, d = x.shape
    depth = w_in.shape[0]
    sba = w_sba_out.shape[1]
    sgu = g_sgu.shape[1]
    heads = sba // HEAD_DIM
    groups, chunk = w_spatial.shape[1], w_spatial.shape[2]
    n_groups = w_router_group.shape[2]
    n_experts = w_router_expert.shape[2]
    per_group = n_experts // n_groups
    n_tok = bsz * seq
    m_pad = n_tok * TOP_K + n_experts * EXPERT_ROWS
    widths = (sba, sba, sba, 2 * sgu, 2 * d)
    assert seq % chunk == 0 and LANES % (sgu // groups) == 0 and n_groups + n_experts <= LANES

    h = x.reshape(n_tok, d)
    for l in range(depth):
        mod = _mod_call(c, w_ada[l], b_ada[l])
        sh_m, sc_m, gt_m, sh_f, sc_f, gt_f = [mod[:, i * d:(i + 1) * d].reshape(bsz, 1, d) for i in range(6)]

        q, k, v, uv, gates = _proj_call(h, sh_m, sc_m, g_mix[l].reshape(1, d), w_in[l].astype(BF16), seq, widths)
        ya = _attn_call(q.reshape(bsz, seq, sba), k.reshape(bsz, seq, sba), v.reshape(bsz, seq, sba), heads)

        bias_full = jnp.repeat(b_spatial[l].T, sgu // groups, axis=1)
        wr = jnp.concatenate([w_router_group[l], w_router_expert[l]], axis=1)
        wr = jnp.pad(wr, ((0, 0), (0, LANES - wr.shape[1])))
        br = jnp.concatenate([b_router_group[l], b_router_expert[l]])
        br = jnp.pad(br, (0, LANES - br.shape[0])).reshape(1, LANES)
        h1, xn, route, counts = _mix_call(
            ya.reshape(n_tok, sba), uv, gates, h, gt_m, sh_f, sc_f, g_sgu[l].reshape(1, sgu), w_spatial[l],
            bias_full, w_sba_out[l].astype(BF16), w_sgu_out[l].astype(BF16), w_out[l].astype(BF16),
            g_ffn[l].reshape(1, d), wr, br, seq, n_groups, per_group)

        dest, blk_e, used = _routing_plan(route, counts[0, n_groups:n_groups + n_experts], n_experts, m_pad)
        xs = _dispatch_call(dest, xn, m_pad, seq)
        yb = _expert_call(blk_e, used, xs, w_expert_gate[l].astype(BF16), w_expert_up[l].astype(BF16),
                          w_expert_down[l].astype(BF16))
        h = _combine_call(dest, h1, route, gt_f, g_final.reshape(1, d), yb, seq, final=l == depth - 1)
    return h.reshape(bsz, seq, d)
```

```python
import functools

import jax
import jax.numpy as jnp
from jax import lax
from jax.experimental import pallas as pl
from jax.experimental.pallas import tpu as pltpu

F32 = jnp.float32
BF16 = jnp.bfloat16

HEAD_DIM = 64
TOP_K = 2
EPS = 1e-6
LOG2E = 1.4426950408889634
ATTN_EXIT_LOG2 = 160.0
LANES = 128
ATTN_BLOCK = 256
EXPERT_ROWS = 256
TOKEN_TILE = 256
PROJ_TILE = 512
VMEM_LIMIT = 56 * 1024 * 1024


def _cparams(*sem):
    return pltpu.CompilerParams(dimension_semantics=sem, vmem_limit_bytes=VMEM_LIMIT)


def _split_bf16(a):
    hi = a.astype(BF16)
    lo = (a - hi.astype(F32)).astype(BF16)
    return hi, lo


def _dot(a, b):
    return jnp.dot(a, b, preferred_element_type=F32)


def _dot3(a, b):
    ah, al = _split_bf16(a)
    bh, bl = _split_bf16(b)
    return _dot(ah, bh) + _dot(ah, bl) + _dot(al, bh)


def _rms(x, g):
    ms = jnp.mean(x * x, axis=-1, keepdims=True)
    return x * lax.rsqrt(ms + EPS) * g


def _mod_kernel(c_ref, w_ref, b_ref, o_ref):
    c = c_ref[...]
    ca = c * (1.0 / (1.0 + jnp.exp(-c)))
    o_ref[...] = _dot3(ca, w_ref[...]) + b_ref[...]


def _mod_call(c, w_ada, b_ada):
    bsz, d = c.shape
    n = w_ada.shape[1]
    tn = n // 4 if n % (4 * LANES) == 0 else n
    return pl.pallas_call(
        _mod_kernel,
        out_shape=jax.ShapeDtypeStruct((bsz, n), F32),
        grid=(n // tn,),
        in_specs=[pl.BlockSpec((bsz, d), lambda j: (0, 0)),
                  pl.BlockSpec((d, tn), lambda j: (0, j)),
                  pl.BlockSpec((1, tn), lambda j: (0, j))],
        out_specs=pl.BlockSpec((bsz, tn), lambda j: (0, j)),
        compiler_params=_cparams("arbitrary"),
        name="mod",
    )(c, w_ada, b_ada.reshape(1, n))


def _proj_kernel(x_ref, sh_ref, sc_ref, g_ref, w_ref, q_ref, k_ref, v_ref, uv_ref, gt_ref, *, widths):
    x = x_ref[...]
    n = _rms(x, g_ref[...]) * (1.0 + sc_ref[0]) + sh_ref[0]
    nb = n.astype(BF16)
    off = 0
    for ref, wd in zip((q_ref, k_ref, v_ref, uv_ref, gt_ref), widths):
        p = _dot(nb, w_ref[:, off:off + wd])
        if ref is q_ref:
            p = p * (HEAD_DIM ** -0.5 * LOG2E)
        ref[...] = p.astype(ref.dtype)
        off += wd


def _proj_call(x2, sh, sc, g, w_in_bf, seq, widths):
    n_tok, d = x2.shape
    tm = min(PROJ_TILE, seq)
    per_b = seq // tm
    cols = w_in_bf.shape[1]
    tok = lambda w: pl.BlockSpec((tm, w), lambda i: (i, 0))
    vec = pl.BlockSpec((1, 1, d), lambda i: (i // per_b, 0, 0))
    return pl.pallas_call(
        functools.partial(_proj_kernel, widths=widths),
        out_shape=[jax.ShapeDtypeStruct((n_tok, w), BF16) for w in widths],
        grid=(n_tok // tm,),
        in_specs=[tok(d), vec, vec,
                  pl.BlockSpec((1, d), lambda i: (0, 0)),
                  pl.BlockSpec((d, cols), lambda i: (0, 0))],
        out_specs=[tok(w) for w in widths],
        compiler_params=_cparams("arbitrary"),
        name="proj",
    )(x2, sh, sc, g, w_in_bf)


def _attn_tile(qm, kp, vp, u_tri, carry, causal):
    s = lax.dot_general(qm, kp, (((1,), (1,)), ((), ())), preferred_element_type=F32)
    sp = jnp.maximum(s, 0.0) + jnp.log2(1.0 + jnp.exp2(-jnp.abs(s)))
    log_beta = s - sp
    if causal is not None:
        sp = jnp.where(causal, sp, 0.0)
    hi, lo = _split_bf16(sp)
    cum = _dot(hi, u_tri) + _dot(lo, u_tri)
    w = jnp.exp2(log_beta - cum - carry)
    if causal is not None:
        w = jnp.where(causal, w, 0.0)
    pv = _dot(w.astype(BF16), vp)
    return pv, carry + jnp.sum(sp, axis=1, keepdims=True)


def _attn_kernel(q_ref, k_ref, v_ref, u_ref, o_ref, qm_ref, acc_ref, carry_ref, *, heads, blk):
    qi = pl.program_id(1)
    u_tri = u_ref[...]
    row = lax.broadcasted_iota(jnp.int32, (blk, blk), 0)
    col = lax.broadcasted_iota(jnp.int32, (blk, blk), 1)
    causal = col < row
    per_slab = LANES // HEAD_DIM
    lane = lax.broadcasted_iota(jnp.int32, (blk, LANES), 1)
    own = [(lane >= j * HEAD_DIM) & (lane < (j + 1) * HEAD_DIM) for j in range(per_slab)]
    for h in range(heads):
        slab = slice(h // per_slab * LANES, (h // per_slab + 1) * LANES)
        qm_ref[h] = jnp.where(own[h % per_slab], q_ref[0, :, slab], 0.0).astype(BF16)

    def key_tile(start, first):
        cmin = None
        for p in range(heads // per_slab):
            slab = slice(p * LANES, (p + 1) * LANES)
            kp = k_ref[0, pl.ds(start, blk), slab]
            vp = v_ref[0, pl.ds(start, blk), slab]
            upd = None
            for j in range(per_slab):
                h = p * per_slab + j
                carry = jnp.zeros((blk, 1), F32) if first else carry_ref[h]
                pv, carry = _attn_tile(qm_ref[h], kp, vp, u_tri, carry, causal if first else None)
                carry_ref[h] = carry
                upd = pv if upd is None else jnp.where(own[j], pv, upd)
                cmin = carry if cmin is None else jnp.minimum(cmin, carry)
            acc_ref[:, slab] = upd if first else acc_ref[:, slab] + upd
        return jnp.min(cmin)

    cmin = key_tile(pl.multiple_of(qi * blk, blk), True)

    def cond(state):
        i, cmin = state
        return (i < qi) & (cmin < ATTN_EXIT_LOG2)

    def body(state):
        i, _ = state
        return i + 1, key_tile(pl.multiple_of((qi - 1 - i) * blk, blk), False)

    lax.while_loop(cond, body, (jnp.int32(0), cmin))
    o_ref[0] = acc_ref[...].astype(o_ref.dtype)


def _attn_call(q, k, v, heads):
    bsz, seq, width = q.shape
    blk = min(ATTN_BLOCK, seq)
    r = lax.broadcasted_iota(jnp.int32, (blk, blk), 0)
    c = lax.broadcasted_iota(jnp.int32, (blk, blk), 1)
    u_tri = (r > c).astype(BF16)
    full = pl.BlockSpec((1, seq, width), lambda b, i: (b, 0, 0))
    return pl.pallas_call(
        functools.partial(_attn_kernel, heads=heads, blk=blk),
        out_shape=jax.ShapeDtypeStruct((bsz, seq, width), BF16),
        grid=(bsz, seq // blk),
        in_specs=[pl.BlockSpec((1, blk, width), lambda b, i: (b, i, 0)), full, full,
                  pl.BlockSpec((blk, blk), lambda b, i: (0, 0))],
        out_specs=pl.BlockSpec((1, blk, width), lambda b, i: (b, i, 0)),
        scratch_shapes=[pltpu.VMEM((heads, blk, LANES), BF16), pltpu.VMEM((blk, width), F32),
                        pltpu.VMEM((heads, blk, 1), F32)],
        compiler_params=_cparams("arbitrary", "arbitrary"),
        name="attn",
    )(q, k, v, u_tri)


def _gelu_tanh(x):
    return 0.5 * x * (1.0 + jnp.tanh(0.7978845608028654 * (x + 0.044715 * (x * x * x))))


def _sigmoid(x):
    return 1.0 / (1.0 + jnp.exp(-x))


def _mix_kernel(ya_ref, uv_ref, gt_ref, x_ref, gtm_ref, shf_ref, scf_ref, gsgu_ref, ws_ref, bs_ref,
                wa_ref, wb_ref, wo_ref, gffn_ref, wr_ref, br_ref, ltri_ref,
                h_ref, xn_ref, route_ref, cnt_ref, yb_scr, cnt_scr,
                *, chunk, groups, n_groups, per_group):
    step = pl.program_id(0)
    tm = x_ref.shape[0]
    sgu = gsgu_ref.shape[1]
    gdim = sgu // groups

    @pl.when(step == 0)
    def _():
        cnt_scr[...] = jnp.zeros_like(cnt_scr)

    act = _gelu_tanh(uv_ref[...].astype(F32))
    u = act[:, :sgu]
    v = act[:, sgu:]
    mu = jnp.mean(v, axis=-1, keepdims=True)
    vc = v - mu
    var = jnp.mean(vc * vc, axis=-1, keepdims=True)
    vn = vc * lax.rsqrt(var + EPS) * gsgu_ref[...]
    r = lax.broadcasted_iota(jnp.int32, (chunk, chunk), 0)
    c = lax.broadcasted_iota(jnp.int32, (chunk, chunk), 1)
    wcs = [jnp.where(r >= c, ws_ref[g], 0.0).astype(BF16) for g in range(groups)]
    pair = LANES // gdim
    lane = lax.broadcasted_iota(jnp.int32, (chunk, LANES), 1)
    for ci in range(tm // chunk):
        rows = slice(ci * chunk, (ci + 1) * chunk)
        for p in range(groups // pair):
            lanes = slice(p * LANES, (p + 1) * LANES)
            slab = vn[rows, lanes]
            mix = bs_ref[:, lanes]
            for j in range(pair):
                sel = (lane >= j * gdim) & (lane < (j + 1) * gdim)
                mix = mix + _dot(wcs[p * pair + j], jnp.where(sel, slab, 0.0).astype(BF16))
            yb_scr[rows, lanes] = (u[rows, lanes] * mix).astype(BF16)

    d = x_ref.shape[1]
    gates = _sigmoid(gt_ref[...].astype(F32))
    merged = gates[:, :d] * _dot(ya_ref[...], wa_ref[...]) + gates[:, d:] * _dot(yb_scr[...], wb_ref[...])
    h = x_ref[...] + gtm_ref[0] * _dot(merged.astype(BF16), wo_ref[...])
    h_ref[...] = h

    xn = _rms(h, gffn_ref[...]) * (1.0 + scf_ref[0]) + shf_ref[0]
    xn_ref[...] = xn
    logits = _dot3(xn, wr_ref[...]) + br_ref[...]
    ln = lax.broadcasted_iota(jnp.int32, (tm, LANES), 1)
    neg = -jnp.inf
    gl = jnp.where(ln < n_groups, logits, neg)
    gmax = jnp.max(gl, axis=1, keepdims=True)
    g_sel = jnp.min(jnp.where(gl == gmax, ln, LANES), axis=1, keepdims=True)
    g_w = 1.0 / jnp.sum(jnp.where(ln < n_groups, jnp.exp(logits - gmax), 0.0), axis=1, keepdims=True)
    lo_lane = n_groups + per_group * g_sel
    el = jnp.where((ln >= lo_lane) & (ln < lo_lane + per_group), logits, neg)
    m1 = jnp.max(el, axis=1, keepdims=True)
    i1 = jnp.min(jnp.where(el == m1, ln, LANES), axis=1, keepdims=True)
    el2 = jnp.where(ln == i1, neg, el)
    m2 = jnp.max(el2, axis=1, keepdims=True)
    i2 = jnp.min(jnp.where(el2 == m2, ln, LANES), axis=1, keepdims=True)
    t = jnp.exp(m2 - m1)
    w1 = g_w / (1.0 + t)
    w2 = g_w * t / (1.0 + t)

    oh1 = ln == i1
    oh2 = ln == i2
    both = jnp.where(oh1 | oh2, 1.0, 0.0)
    before = _dot(ltri_ref[...], both.astype(BF16)) + cnt_scr[...]
    rank1 = jnp.sum(jnp.where(oh1, before, 0.0), axis=1, keepdims=True)
    rank2 = jnp.sum(jnp.where(oh2, before, 0.0), axis=1, keepdims=True)
    cnt_scr[...] = cnt_scr[...] + jnp.sum(both, axis=0, keepdims=True)
    cnt_ref[...] = cnt_scr[...]

    e1 = (i1 - n_groups).astype(F32)
    e2 = (i2 - n_groups).astype(F32)
    out = jnp.zeros((tm, LANES), F32)
    for idx, val in enumerate((e1, e2, w1, w2, rank1, rank2)):
        out = jnp.where(ln == idx, val, out)
    route_ref[...] = out


def _mix_call(ya, uv, gates, x2, gtm, shf, scf, g_sgu, w_spatial, bias_full, wa, wb, wo, g_ffn, wr, br,
              seq, n_groups, per_group):
    n_tok, d = x2.shape
    tm = min(TOKEN_TILE, seq)
    per_b = seq // tm
    groups, chunk, _ = w_spatial.shape
    sgu = g_sgu.shape[1]
    r = lax.broadcasted_iota(jnp.int32, (tm, tm), 0)
    c = lax.broadcasted_iota(jnp.int32, (tm, tm), 1)
    ltri = (r > c).astype(BF16)
    tok = lambda w: pl.BlockSpec((tm, w), lambda i: (i, 0))
    vec = pl.BlockSpec((1, 1, d), lambda i: (i // per_b, 0, 0))
    const = lambda a: pl.BlockSpec(a.shape, lambda i: (0,) * a.ndim)
    ins = [ya, uv, gates, x2, gtm, shf, scf, g_sgu, w_spatial, bias_full, wa, wb, wo, g_ffn, wr, br, ltri]
    in_specs = [tok(ya.shape[1]), tok(uv.shape[1]), tok(gates.shape[1]), tok(d), vec, vec, vec]
    in_specs += [const(a) for a in ins[7:]]
    return pl.pallas_call(
        functools.partial(_mix_kernel, chunk=chunk, groups=groups, n_groups=n_groups, per_group=per_group),
        out_shape=[jax.ShapeDtypeStruct((n_tok, d), F32), jax.ShapeDtypeStruct((n_tok, d), F32),
                   jax.ShapeDtypeStruct((n_tok, LANES), F32), jax.ShapeDtypeStruct((1, LANES), F32)],
        grid=(n_tok // tm,),
        in_specs=in_specs,
        out_specs=[tok(d), tok(d), tok(LANES), pl.BlockSpec((1, LANES), lambda i: (0, 0))],
        scratch_shapes=[pltpu.VMEM((tm, sgu), BF16), pltpu.VMEM((1, LANES), F32)],
        compiler_params=_cparams("arbitrary"),
        name="mix",
    )(*ins)


def _row_copy(src, s, dst, d, sem):
    return pltpu.make_async_copy(src.at[pl.ds(s, 1)], dst.at[pl.ds(d, 1)], sem)


def _dispatch_kernel(dest_ref, xn_ref, init_ref, xs_ref, sem, *, tm):
    del init_ref

    def copies(j):
        return [_row_copy(xn_ref, j, xs_ref, dest_ref[0, 0, TOP_K * j + k], sem) for k in range(TOP_K)]

    def start(j, _):
        for cp in copies(j):
            cp.start()
        return 0

    def wait(j, _):
        for cp in copies(j):
            cp.wait()
        return 0

    lax.fori_loop(0, tm, start, 0)
    lax.fori_loop(0, tm, wait, 0)


def _dispatch_call(dest, xn, m_pad, seq):
    n_tok, d = xn.shape
    tm = min(TOKEN_TILE, seq)
    steps = n_tok // tm
    any_spec = pl.BlockSpec(memory_space=pl.ANY)
    return pl.pallas_call(
        functools.partial(_dispatch_kernel, tm=tm),
        out_shape=jax.ShapeDtypeStruct((m_pad, d), F32),
        grid=(steps,),
        in_specs=[pl.BlockSpec((1, 1, TOP_K * tm), lambda i: (i, 0, 0), memory_space=pltpu.SMEM),
                  pl.BlockSpec((tm, d), lambda i: (i, 0)), any_spec],
        out_specs=any_spec,
        scratch_shapes=[pltpu.SemaphoreType.DMA(())],
        input_output_aliases={2: 0},
        compiler_params=_cparams("arbitrary"),
        name="dispatch",
    )(dest.reshape(steps, 1, TOP_K * tm), xn, jnp.zeros((m_pad, d), F32))


def _expert_kernel(blk_e_ref, used_ref, xs_ref, wg_ref, wu_ref, wd_ref, y_ref):
    del blk_e_ref

    @pl.when(pl.program_id(0) < used_ref[0])
    def _():
        xb = xs_ref[...].astype(BF16)
        g = _dot(xb, wg_ref[0])
        hid = g * _sigmoid(g) * _dot(xb, wu_ref[0])
        y_ref[...] = _dot(hid.astype(BF16), wd_ref[0])

    @pl.when(pl.program_id(0) >= used_ref[0])
    def _():
        y_ref[...] = jnp.zeros_like(y_ref)


def _expert_call(blk_e, used, xs, wg, wu, wd):
    m_pad, d = xs.shape
    f = wg.shape[2]
    rows = EXPERT_ROWS
    return pl.pallas_call(
        _expert_kernel,
        out_shape=jax.ShapeDtypeStruct((m_pad, d), F32),
        grid_spec=pltpu.PrefetchScalarGridSpec(
            num_scalar_prefetch=2,
            grid=(m_pad // rows,),
            in_specs=[pl.BlockSpec((rows, d), lambda i, be, us: (i, 0)),
                      pl.BlockSpec((1, d, f), lambda i, be, us: (be[i], 0, 0)),
                      pl.BlockSpec((1, d, f), lambda i, be, us: (be[i], 0, 0)),
                      pl.BlockSpec((1, f, d), lambda i, be, us: (be[i], 0, 0))],
            out_specs=pl.BlockSpec((rows, d), lambda i, be, us: (i, 0))),
        compiler_params=_cparams("arbitrary"),
        name="experts",
    )(blk_e, used, xs, wg, wu, wd)


def _combine_kernel(dest_ref, h_ref, route_ref, gtf_ref, gfin_ref, y_ref, o_ref, buf, sem, *, tm, final):
    def copies(j):
        return [pltpu.make_async_copy(y_ref.at[pl.ds(dest_ref[0, 0, TOP_K * j + k], 1)],
                                      buf.at[k, pl.ds(j, 1)], sem) for k in range(TOP_K)]

    def start(j, _):
        for cp in copies(j):
            cp.start()
        return 0

    def wait(j, _):
        for cp in copies(j):
            cp.wait()
        return 0

    lax.fori_loop(0, tm, start, 0)
    lax.fori_loop(0, tm, wait, 0)
    route = route_ref[...]
    y = route[:, 2:3] * buf[0] + route[:, 3:4] * buf[1]
    h = h_ref[...] + gtf_ref[0] * y
    o_ref[...] = _rms(h, gfin_ref[...]) if final else h


def _combine_call(dest, h, route, gtf, g_final, yb, seq, final):
    n_tok, d = h.shape
    tm = min(TOKEN_TILE, seq)
    per_b = seq // tm
    steps = n_tok // tm
    tok = lambda w: pl.BlockSpec((tm, w), lambda i: (i, 0))
    return pl.pallas_call(
        functools.partial(_combine_kernel, tm=tm, final=final),
        out_shape=jax.ShapeDtypeStruct((n_tok, d), F32),
        grid=(steps,),
        in_specs=[pl.BlockSpec((1, 1, TOP_K * tm), lambda i: (i, 0, 0), memory_space=pltpu.SMEM),
                  tok(d), tok(LANES),
                  pl.BlockSpec((1, 1, d), lambda i: (i // per_b, 0, 0)),
                  pl.BlockSpec((1, d), lambda i: (0, 0)),
                  pl.BlockSpec(memory_space=pl.ANY)],
        out_specs=tok(d),
        scratch_shapes=[pltpu.VMEM((TOP_K, tm, d), F32), pltpu.SemaphoreType.DMA(())],
        compiler_params=_cparams("arbitrary"),
        name="combine",
    )(dest.reshape(steps, 1, TOP_K * tm), h, route, gtf, g_final, yb)


def _routing_plan(route, counts, n_experts, m_pad):
    eid = route[:, 0:TOP_K].astype(jnp.int32)
    rank = route[:, 4:4 + TOP_K].astype(jnp.int32)
    cnt = counts.astype(jnp.int32)
    padded = (cnt + EXPERT_ROWS - 1) // EXPERT_ROWS * EXPERT_ROWS
    pend = jnp.cumsum(padded)
    pstart = pend - padded
    dest = jnp.take(pstart, eid) + rank
    nblk = m_pad // EXPERT_ROWS
    blk_e = jnp.minimum(jnp.searchsorted(pend, jnp.arange(nblk, dtype=jnp.int32) * EXPERT_ROWS, side='right'),
                        n_experts - 1).astype(jnp.int32)
    used = (pend[-1:] // EXPERT_ROWS).astype(jnp.int32)
    return dest, blk_e, used


def kernel(x, c, g_mix, g_ffn, w_ada, b_ada, w_in, w_sba_out, g_sgu, w_spatial, b_spatial, w_sgu_out, w_out,
           w_router_group, b_router_group, w_router_expert, b_router_expert, w_expert_gate, w_expert_up,
           w_expert_down, g_final):
    bsz, seq, d = x.shape
    depth = w_in.shape[0]
    sba = w_sba_out.shape[1]
    sgu = g_sgu.shape[1]
    heads = sba // HEAD_DIM
    groups, chunk = w_spatial.shape[1], w_spatial.shape[2]
    n_groups = w_router_group.shape[2]
    n_experts = w_router_expert.shape[2]
    per_group = n_experts // n_groups
    n_tok = bsz * seq
    m_pad = n_tok * TOP_K + n_experts * EXPERT_ROWS
    widths = (sba, sba, sba, 2 * sgu, 2 * d)
    assert seq % chunk == 0 and LANES % (sgu // groups) == 0 and n_groups + n_experts <= LANES

    h = x.reshape(n_tok, d)
    for l in range(depth):
        mod = _mod_call(c, w_ada[l], b_ada[l])
        sh_m, sc_m, gt_m, sh_f, sc_f, gt_f = [mod[:, i * d:(i + 1) * d].reshape(bsz, 1, d) for i in range(6)]

        q, k, v, uv, gates = _proj_call(h, sh_m, sc_m, g_mix[l].reshape(1, d), w_in[l].astype(BF16), seq, widths)
        ya = _attn_call(q.reshape(bsz, seq, sba), k.reshape(bsz, seq, sba), v.reshape(bsz, seq, sba), heads)

        bias_full = jnp.repeat(b_spatial[l].T, sgu // groups, axis=1)
        wr = jnp.concatenate([w_router_group[l], w_router_expert[l]], axis=1)
        wr = jnp.pad(wr, ((0, 0), (0, LANES - wr.shape[1])))
        br = jnp.concatenate([b_router_group[l], b_router_expert[l]])
        br = jnp.pad(br, (0, LANES - br.shape[0])).reshape(1, LANES)
        h1, xn, route, counts = _mix_call(
            ya.reshape(n_tok, sba), uv, gates, h, gt_m, sh_f, sc_f, g_sgu[l].reshape(1, sgu), w_spatial[l],
            bias_full, w_sba_out[l].astype(BF16), w_sgu_out[l].astype(BF16), w_out[l].astype(BF16),
            g_ffn[l].reshape(1, d), wr, br, seq, n_groups, per_group)

        dest, blk_e, used = _routing_plan(route, counts[0, n_groups:n_groups + n_experts], n_experts, m_pad)
        xs = _dispatch_call(dest, xn, m_pad, seq)
        yb = _expert_call(blk_e, used, xs, w_expert_gate[l].astype(BF16), w_expert_up[l].astype(BF16),
                          w_expert_down[l].astype(BF16))
        h = _combine_call(dest, h1, route, gt_f, g_final.reshape(1, d), yb, seq, final=l == depth - 1)
    return h.reshape(bsz, seq, d)
```

```python
import functools

import jax
import jax.numpy as jnp
from jax import lax
from jax.experimental import pallas as pl
from jax.experimental.pallas import tpu as pltpu

F32 = jnp.float32
BF16 = jnp.bfloat16

HEAD_DIM = 64
TOP_K = 2
EPS = 1e-6
LOG2E = 1.4426950408889634
ATTN_EXIT_LOG2 = 160.0
LANES = 128
ATTN_BLOCK = 256
EXPERT_ROWS = 256
TOKEN_TILE = 256
PROJ_TILE = 512
ROUTE_COLS = 8
DMA_UNROLL = 8
VMEM_LIMIT = 56 * 1024 * 1024


def _cparams(*sem):
    return pltpu.CompilerParams(dimension_semantics=sem, vmem_limit_bytes=VMEM_LIMIT)


def _split_bf16(a):
    hi = a.astype(BF16)
    lo = (a - hi.astype(F32)).astype(BF16)
    return hi, lo


def _dot(a, b):
    return jnp.dot(a, b, preferred_element_type=F32)


def _dot3(a, b):
    ah, al = _split_bf16(a)
    bh, bl = _split_bf16(b)
    return _dot(ah, bh) + _dot(ah, bl) + _dot(al, bh)


def _rms(x, g):
    ms = jnp.mean(x * x, axis=-1, keepdims=True)
    return x * lax.rsqrt(ms + EPS) * g


def _mod_kernel(c_ref, w_ref, b_ref, o_ref):
    c = c_ref[...]
    ca = c * (1.0 / (1.0 + jnp.exp(-c)))
    o_ref[...] = _dot3(ca, w_ref[...]) + b_ref[...]


def _mod_call(c, w_ada, b_ada):
    bsz, d = c.shape
    n = w_ada.shape[1]
    tn = n // 4 if n % (4 * LANES) == 0 else n
    return pl.pallas_call(
        _mod_kernel,
        out_shape=jax.ShapeDtypeStruct((bsz, n), F32),
        grid=(n // tn,),
        in_specs=[pl.BlockSpec((bsz, d), lambda j: (0, 0)),
                  pl.BlockSpec((d, tn), lambda j: (0, j)),
                  pl.BlockSpec((1, tn), lambda j: (0, j))],
        out_specs=pl.BlockSpec((bsz, tn), lambda j: (0, j)),
        compiler_params=_cparams("arbitrary"),
        name="mod",
    )(c, w_ada, b_ada.reshape(1, n))


def _proj_kernel(x_ref, sh_ref, sc_ref, g_ref, w_ref, q_ref, k_ref, v_ref, uv_ref, gt_ref, *, widths):
    x = x_ref[...]
    n = _rms(x, g_ref[...]) * (1.0 + sc_ref[0]) + sh_ref[0]
    nb = n.astype(BF16)
    off = 0
    for ref, wd in zip((q_ref, k_ref, v_ref, uv_ref, gt_ref), widths):
        p = _dot(nb, w_ref[:, off:off + wd])
        if ref is q_ref:
            p = p * (HEAD_DIM ** -0.5 * LOG2E)
        ref[...] = p.astype(ref.dtype)
        off += wd


def _proj_call(x2, sh, sc, g, w_in_bf, seq, widths):
    n_tok, d = x2.shape
    tm = min(PROJ_TILE, seq)
    per_b = seq // tm
    cols = w_in_bf.shape[1]
    tok = lambda w: pl.BlockSpec((tm, w), lambda i: (i, 0))
    vec = pl.BlockSpec((1, 1, d), lambda i: (i // per_b, 0, 0))
    return pl.pallas_call(
        functools.partial(_proj_kernel, widths=widths),
        out_shape=[jax.ShapeDtypeStruct((n_tok, w), BF16) for w in widths],
        grid=(n_tok // tm,),
        in_specs=[tok(d), vec, vec,
                  pl.BlockSpec((1, d), lambda i: (0, 0)),
                  pl.BlockSpec((d, cols), lambda i: (0, 0))],
        out_specs=[tok(w) for w in widths],
        compiler_params=_cparams("arbitrary"),
        name="proj",
    )(x2, sh, sc, g, w_in_bf)


def _attn_kernel(q_ref, k_ref, v_ref, u_ref, o_ref, qm_ref, acc_ref, carry_ref, *, heads, blk):
    qi = pl.program_id(1)
    u_tri = u_ref[...]
    row = lax.broadcasted_iota(jnp.int32, (blk, blk), 0)
    col = lax.broadcasted_iota(jnp.int32, (blk, blk), 1)
    causal = col < row
    per_slab = LANES // HEAD_DIM
    lane = lax.broadcasted_iota(jnp.int32, (blk, LANES), 1)
    own = [(lane >= j * HEAD_DIM) & (lane < (j + 1) * HEAD_DIM) for j in range(per_slab)]
    for h in range(heads):
        slab = slice(h // per_slab * LANES, (h // per_slab + 1) * LANES)
        qm_ref[h] = jnp.where(own[h % per_slab], q_ref[0, :, slab], 0.0).astype(BF16)

    def key_tile(start, first):
        mask = causal if first else None
        slabs = [slice(p * LANES, (p + 1) * LANES) for p in range(heads // per_slab)]
        s_all = [lax.dot_general(qm_ref[h], k_ref[0, pl.ds(start, blk), slabs[h // per_slab]],
                                 (((1,), (1,)), ((), ())), preferred_element_type=F32) for h in range(heads)]
        sp_all, lb_all = [], []
        for s in s_all:
            sp = jnp.maximum(s, 0.0) + jnp.log2(1.0 + jnp.exp2(-jnp.abs(s)))
            lb_all.append(s - sp)
            sp_all.append(sp if mask is None else jnp.where(mask, sp, 0.0))
        cum_all = []
        for sp in sp_all:
            hi, lo = _split_bf16(sp)
            cum_all.append(_dot(hi, u_tri) + _dot(lo, u_tri))
        w_all = []
        cmin = None
        for h in range(heads):
            carry = jnp.zeros((blk, 1), F32) if first else carry_ref[h]
            w = jnp.exp2(lb_all[h] - cum_all[h] - carry)
            w_all.append((w if mask is None else jnp.where(mask, w, 0.0)).astype(BF16))
            carry = carry + jnp.sum(sp_all[h], axis=1, keepdims=True)
            carry_ref[h] = carry
            cmin = carry if cmin is None else jnp.minimum(cmin, carry)
        for p, slab in enumerate(slabs):
            vp = v_ref[0, pl.ds(start, blk), slab]
            upd = None
            for j in range(per_slab):
                pv = _dot(w_all[p * per_slab + j], vp)
                upd = pv if upd is None else jnp.where(own[j], pv, upd)
            acc_ref[:, slab] = upd if first else acc_ref[:, slab] + upd
        return jnp.min(cmin)

    cmin = key_tile(pl.multiple_of(qi * blk, blk), True)

    def cond(state):
        i, cmin = state
        return (i < qi) & (cmin < ATTN_EXIT_LOG2)

    def body(state):
        i, _ = state
        return i + 1, key_tile(pl.multiple_of((qi - 1 - i) * blk, blk), False)

    lax.while_loop(cond, body, (jnp.int32(0), cmin))
    o_ref[0] = acc_ref[...].astype(o_ref.dtype)


def _attn_call(q, k, v, heads):
    bsz, seq, width = q.shape
    blk = min(ATTN_BLOCK, seq)
    r = lax.broadcasted_iota(jnp.int32, (blk, blk), 0)
    c = lax.broadcasted_iota(jnp.int32, (blk, blk), 1)
    u_tri = (r > c).astype(BF16)
    full = pl.BlockSpec((1, seq, width), lambda b, i: (b, 0, 0))
    return pl.pallas_call(
        functools.partial(_attn_kernel, heads=heads, blk=blk),
        out_shape=jax.ShapeDtypeStruct((bsz, seq, width), BF16),
        grid=(bsz, seq // blk),
        in_specs=[pl.BlockSpec((1, blk, width), lambda b, i: (b, i, 0)), full, full,
                  pl.BlockSpec((blk, blk), lambda b, i: (0, 0))],
        out_specs=pl.BlockSpec((1, blk, width), lambda b, i: (b, i, 0)),
        scratch_shapes=[pltpu.VMEM((heads, blk, LANES), BF16), pltpu.VMEM((blk, width), F32),
                        pltpu.VMEM((heads, blk, 1), F32)],
        compiler_params=_cparams("arbitrary", "arbitrary"),
        name="attn",
    )(q, k, v, u_tri)


def _gelu_tanh(x):
    return 0.5 * x * (1.0 + jnp.tanh(0.7978845608028654 * (x + 0.044715 * (x * x * x))))


def _sigmoid(x):
    return 1.0 / (1.0 + jnp.exp(-x))


def _mix_kernel(ya_ref, uv_ref, gt_ref, x_ref, gtm_ref, shf_ref, scf_ref, gsgu_ref, ws_ref, bs_ref,
                wa_ref, wb_ref, wo_ref, gffn_ref, wr_ref, br_ref, ltri_ref,
                h_ref, xn_ref, route_ref, cnt_ref, yb_scr, cnt_scr,
                *, chunk, groups, n_groups, per_group):
    step = pl.program_id(0)
    tm = x_ref.shape[0]
    sgu = gsgu_ref.shape[1]
    gdim = sgu // groups

    @pl.when(step == 0)
    def _():
        cnt_scr[...] = jnp.zeros_like(cnt_scr)

    act = _gelu_tanh(uv_ref[...].astype(F32))
    u = act[:, :sgu]
    v = act[:, sgu:]
    mu = jnp.mean(v, axis=-1, keepdims=True)
    vc = v - mu
    var = jnp.mean(vc * vc, axis=-1, keepdims=True)
    vn = vc * lax.rsqrt(var + EPS) * gsgu_ref[...]
    r = lax.broadcasted_iota(jnp.int32, (chunk, chunk), 0)
    c = lax.broadcasted_iota(jnp.int32, (chunk, chunk), 1)
    wcs = [jnp.where(r >= c, ws_ref[g], 0.0).astype(BF16) for g in range(groups)]
    pair = LANES // gdim
    lane = lax.broadcasted_iota(jnp.int32, (chunk, LANES), 1)
    for ci in range(tm // chunk):
        rows = slice(ci * chunk, (ci + 1) * chunk)
        for p in range(groups // pair):
            lanes = slice(p * LANES, (p + 1) * LANES)
            slab = vn[rows, lanes]
            mix = bs_ref[:, lanes]
            for j in range(pair):
                sel = (lane >= j * gdim) & (lane < (j + 1) * gdim)
                mix = mix + _dot(wcs[p * pair + j], jnp.where(sel, slab, 0.0).astype(BF16))
            yb_scr[rows, lanes] = (u[rows, lanes] * mix).astype(BF16)

    d = x_ref.shape[1]
    gates = _sigmoid(gt_ref[...].astype(F32))
    merged = gates[:, :d] * _dot(ya_ref[...], wa_ref[...]) + gates[:, d:] * _dot(yb_scr[...], wb_ref[...])
    h = x_ref[...] + gtm_ref[0] * _dot(merged.astype(BF16), wo_ref[...])
    h_ref[...] = h

    xn = _rms(h, gffn_ref[...]) * (1.0 + scf_ref[0]) + shf_ref[0]
    xn_ref[...] = xn
    logits = _dot3(xn, wr_ref[...]) + br_ref[...]
    ln = lax.broadcasted_iota(jnp.int32, (tm, LANES), 1)
    neg = -jnp.inf
    gl = jnp.where(ln < n_groups, logits, neg)
    gmax = jnp.max(gl, axis=1, keepdims=True)
    g_sel = jnp.min(jnp.where(gl == gmax, ln, LANES), axis=1, keepdims=True)
    g_w = 1.0 / jnp.sum(jnp.where(ln < n_groups, jnp.exp(logits - gmax), 0.0), axis=1, keepdims=True)
    lo_lane = n_groups + per_group * g_sel
    el = jnp.where((ln >= lo_lane) & (ln < lo_lane + per_group), logits, neg)
    m1 = jnp.max(el, axis=1, keepdims=True)
    i1 = jnp.min(jnp.where(el == m1, ln, LANES), axis=1, keepdims=True)
    el2 = jnp.where(ln == i1, neg, el)
    m2 = jnp.max(el2, axis=1, keepdims=True)
    i2 = jnp.min(jnp.where(el2 == m2, ln, LANES), axis=1, keepdims=True)
    t = jnp.exp(m2 - m1)
    w1 = g_w / (1.0 + t)
    w2 = g_w * t / (1.0 + t)

    oh1 = ln == i1
    oh2 = ln == i2
    both = jnp.where(oh1 | oh2, 1.0, 0.0)
    before = _dot(ltri_ref[...], both.astype(BF16)) + cnt_scr[...]
    rank1 = jnp.sum(jnp.where(oh1, before, 0.0), axis=1, keepdims=True)
    rank2 = jnp.sum(jnp.where(oh2, before, 0.0), axis=1, keepdims=True)
    cnt_scr[...] = cnt_scr[...] + jnp.sum(both, axis=0, keepdims=True)
    cnt_ref[...] = cnt_scr[...]

    e1 = (i1 - n_groups).astype(F32)
    e2 = (i2 - n_groups).astype(F32)
    out = jnp.zeros((tm, LANES), F32)
    for idx, val in enumerate((e1, e2, w1, w2, rank1, rank2)):
        out = jnp.where(ln == idx, val, out)
    route_ref[...] = out[:, :ROUTE_COLS]


def _mix_call(ya, uv, gates, x2, gtm, shf, scf, g_sgu, w_spatial, bias_full, wa, wb, wo, g_ffn, wr, br,
              seq, n_groups, per_group):
    n_tok, d = x2.shape
    tm = min(TOKEN_TILE, seq)
    per_b = seq // tm
    groups, chunk, _ = w_spatial.shape
    sgu = g_sgu.shape[1]
    r = lax.broadcasted_iota(jnp.int32, (tm, tm), 0)
    c = lax.broadcasted_iota(jnp.int32, (tm, tm), 1)
    ltri = (r > c).astype(BF16)
    tok = lambda w: pl.BlockSpec((tm, w), lambda i: (i, 0))
    vec = pl.BlockSpec((1, 1, d), lambda i: (i // per_b, 0, 0))
    const = lambda a: pl.BlockSpec(a.shape, lambda i: (0,) * a.ndim)
    ins = [ya, uv, gates, x2, gtm, shf, scf, g_sgu, w_spatial, bias_full, wa, wb, wo, g_ffn, wr, br, ltri]
    in_specs = [tok(ya.shape[1]), tok(uv.shape[1]), tok(gates.shape[1]), tok(d), vec, vec, vec]
    in_specs += [const(a) for a in ins[7:]]
    return pl.pallas_call(
        functools.partial(_mix_kernel, chunk=chunk, groups=groups, n_groups=n_groups, per_group=per_group),
        out_shape=[jax.ShapeDtypeStruct((n_tok, d), F32), jax.ShapeDtypeStruct((n_tok, d), F32),
                   jax.ShapeDtypeStruct((n_tok, ROUTE_COLS), F32), jax.ShapeDtypeStruct((1, LANES), F32)],
        grid=(n_tok // tm,),
        in_specs=in_specs,
        out_specs=[tok(d), tok(d), tok(ROUTE_COLS), pl.BlockSpec((1, LANES), lambda i: (0, 0))],
        scratch_shapes=[pltpu.VMEM((tm, sgu), BF16), pltpu.VMEM((1, LANES), F32)],
        compiler_params=_cparams("arbitrary"),
        name="mix",
    )(*ins)


def _row_copy(src, s, dst, d, sem):
    return pltpu.make_async_copy(src.at[pl.ds(s, 1)], dst.at[pl.ds(d, 1)], sem)


def _dispatch_kernel(zblk_ref, dest_ref, xn_ref, xs_ref, zero_scr, sem, zsem, *, tm, rows):
    @pl.when(pl.program_id(0) == 0)
    def _():
        zero_scr[...] = jnp.zeros_like(zero_scr)

        def zero_copy(j):
            start = pl.multiple_of(zblk_ref[j] * rows, rows)
            return pltpu.make_async_copy(zero_scr, xs_ref.at[pl.ds(start, rows)], zsem)

        for j in range(zblk_ref.shape[0]):
            pl.when(zblk_ref[j] >= 0)(lambda j=j: zero_copy(j).start())
        for j in range(zblk_ref.shape[0]):
            pl.when(zblk_ref[j] >= 0)(lambda j=j: zero_copy(j).wait())

    def copies(j):
        return [_row_copy(xn_ref, j, xs_ref, dest_ref[0, 0, TOP_K * j + k], sem) for k in range(TOP_K)]

    def start(j, _):
        for cp in copies(j):
            cp.start()
        return 0

    def wait(j, _):
        for cp in copies(j):
            cp.wait()
        return 0

    lax.fori_loop(0, tm, start, 0, unroll=DMA_UNROLL)
    lax.fori_loop(0, tm, wait, 0, unroll=DMA_UNROLL)


def _dispatch_call(zero_blocks, dest, xn, m_pad, seq):
    n_tok, d = xn.shape
    tm = min(TOKEN_TILE, seq)
    steps = n_tok // tm
    return pl.pallas_call(
        functools.partial(_dispatch_kernel, tm=tm, rows=EXPERT_ROWS),
        out_shape=jax.ShapeDtypeStruct((m_pad, d), F32),
        grid_spec=pltpu.PrefetchScalarGridSpec(
            num_scalar_prefetch=1,
            grid=(steps,),
            in_specs=[pl.BlockSpec((1, 1, TOP_K * tm), lambda i, zb: (i, 0, 0), memory_space=pltpu.SMEM),
                      pl.BlockSpec((tm, d), lambda i, zb: (i, 0))],
            out_specs=pl.BlockSpec(memory_space=pl.ANY),
            scratch_shapes=[pltpu.VMEM((EXPERT_ROWS, d), F32), pltpu.SemaphoreType.DMA(()),
                            pltpu.SemaphoreType.DMA(())]),
        compiler_params=_cparams("arbitrary"),
        name="dispatch",
    )(zero_blocks, dest.reshape(steps, 1, TOP_K * tm), xn)


def _expert_kernel(blk_e_ref, used_ref, xs_ref, wg_ref, wu_ref, wd_ref, y_ref):
    del blk_e_ref

    @pl.when(pl.program_id(0) < used_ref[0])
    def _():
        xb = xs_ref[...].astype(BF16)
        g = _dot(xb, wg_ref[0])
        hid = g * _sigmoid(g) * _dot(xb, wu_ref[0])
        y_ref[...] = _dot(hid.astype(BF16), wd_ref[0])

    @pl.when(pl.program_id(0) >= used_ref[0])
    def _():
        y_ref[...] = jnp.zeros_like(y_ref)


def _expert_call(blk_e, used, xs, wg, wu, wd):
    m_pad, d = xs.shape
    f = wg.shape[2]
    rows = EXPERT_ROWS
    return pl.pallas_call(
        _expert_kernel,
        out_shape=jax.ShapeDtypeStruct((m_pad, d), F32),
        grid_spec=pltpu.PrefetchScalarGridSpec(
            num_scalar_prefetch=2,
            grid=(m_pad // rows,),
            in_specs=[pl.BlockSpec((rows, d), lambda i, be, us: (i, 0)),
                      pl.BlockSpec((1, d, f), lambda i, be, us: (be[i], 0, 0)),
                      pl.BlockSpec((1, d, f), lambda i, be, us: (be[i], 0, 0)),
                      pl.BlockSpec((1, f, d), lambda i, be, us: (be[i], 0, 0))],
            out_specs=pl.BlockSpec((rows, d), lambda i, be, us: (i, 0))),
        compiler_params=_cparams("arbitrary"),
        name="experts",
    )(blk_e, used, xs, wg, wu, wd)


def _combine_kernel(dest_ref, h_ref, route_ref, gtf_ref, gfin_ref, y_ref, o_ref, buf, sem, *, tm, final):
    def copies(j):
        return [pltpu.make_async_copy(y_ref.at[pl.ds(dest_ref[0, 0, TOP_K * j + k], 1)],
                                      buf.at[k, pl.ds(j, 1)], sem) for k in range(TOP_K)]

    def start(j, _):
        for cp in copies(j):
            cp.start()
        return 0

    def wait(j, _):
        for cp in copies(j):
            cp.wait()
        return 0

    lax.fori_loop(0, tm, start, 0, unroll=DMA_UNROLL)
    lax.fori_loop(0, tm, wait, 0, unroll=DMA_UNROLL)
    route = route_ref[...]
    y = route[:, 2:3] * buf[0] + route[:, 3:4] * buf[1]
    h = h_ref[...] + gtf_ref[0] * y
    o_ref[...] = _rms(h, gfin_ref[...]) if final else h


def _combine_call(dest, h, route, gtf, g_final, yb, seq, final):
    n_tok, d = h.shape
    tm = min(TOKEN_TILE, seq)
    per_b = seq // tm
    steps = n_tok // tm
    tok = lambda w: pl.BlockSpec((tm, w), lambda i: (i, 0))
    return pl.pallas_call(
        functools.partial(_combine_kernel, tm=tm, final=final),
        out_shape=jax.ShapeDtypeStruct((n_tok, d), F32),
        grid=(steps,),
        in_specs=[pl.BlockSpec((1, 1, TOP_K * tm), lambda i: (i, 0, 0), memory_space=pltpu.SMEM),
                  tok(d), tok(ROUTE_COLS),
                  pl.BlockSpec((1, 1, d), lambda i: (i // per_b, 0, 0)),
                  pl.BlockSpec((1, d), lambda i: (0, 0)),
                  pl.BlockSpec(memory_space=pl.ANY)],
        out_specs=tok(d),
        scratch_shapes=[pltpu.VMEM((TOP_K, tm, d), F32), pltpu.SemaphoreType.DMA(())],
        compiler_params=_cparams("arbitrary"),
        name="combine",
    )(dest.reshape(steps, 1, TOP_K * tm), h, route, gtf, g_final, yb)


def _routing_plan(route, counts, n_experts, m_pad):
    eid = route[:, 0:TOP_K].astype(jnp.int32)
    rank = route[:, 4:4 + TOP_K].astype(jnp.int32)
    cnt = counts.astype(jnp.int32)
    padded = (cnt + EXPERT_ROWS - 1) // EXPERT_ROWS * EXPERT_ROWS
    pend = jnp.cumsum(padded)
    pstart = pend - padded
    dest = jnp.take(pstart, eid) + rank
    nblk = m_pad // EXPERT_ROWS
    blk_row = jnp.arange(nblk, dtype=jnp.int32) * EXPERT_ROWS
    blk_e = jnp.minimum(jnp.sum(pend[None, :] <= blk_row[:, None], axis=1), n_experts - 1).astype(jnp.int32)
    used = (pend[-1:] // EXPERT_ROWS).astype(jnp.int32)
    last_blk = jnp.where(padded > 0, pend // EXPERT_ROWS - 1, -1)
    tail_blk = used + jnp.arange(n_experts, dtype=jnp.int32)
    tail_blk = jnp.where(tail_blk < nblk, tail_blk, -1)
    zero_blocks = jnp.concatenate([last_blk, tail_blk]).astype(jnp.int32)
    return dest, blk_e, used, zero_blocks


def kernel(x, c, g_mix, g_ffn, w_ada, b_ada, w_in, w_sba_out, g_sgu, w_spatial, b_spatial, w_sgu_out, w_out,
           w_router_group, b_router_group, w_router_expert, b_router_expert, w_expert_gate, w_expert_up,
           w_expert_down, g_final):
    bsz, seq, d = x.shape
    depth = w_in.shape[0]
    sba = w_sba_out.shape[1]
    sgu = g_sgu.shape[1]
    heads = sba // HEAD_DIM
    groups, chunk = w_spatial.shape[1], w_spatial.shape[2]
    n_groups = w_router_group.shape[2]
    n_experts = w_router_expert.shape[2]
    per_group = n_experts // n_groups
    n_tok = bsz * seq
    m_pad = n_tok * TOP_K + n_experts * EXPERT_ROWS
    widths = (sba, sba, sba, 2 * sgu, 2 * d)
    assert seq % chunk == 0 and LANES % (sgu // groups) == 0 and n_groups + n_experts <= LANES

    h = x.reshape(n_tok, d)
    for l in range(depth):
        mod = _mod_call(c, w_ada[l], b_ada[l])
        sh_m, sc_m, gt_m, sh_f, sc_f, gt_f = [mod[:, i * d:(i + 1) * d].reshape(bsz, 1, d) for i in range(6)]

        q, k, v, uv, gates = _proj_call(h, sh_m, sc_m, g_mix[l].reshape(1, d), w_in[l].astype(BF16), seq, widths)
        ya = _attn_call(q.reshape(bsz, seq, sba), k.reshape(bsz, seq, sba), v.reshape(bsz, seq, sba), heads)

        bias_full = jnp.repeat(b_spatial[l].T, sgu // groups, axis=1)
        wr = jnp.concatenate([w_router_group[l], w_router_expert[l]], axis=1)
        wr = jnp.pad(wr, ((0, 0), (0, LANES - wr.shape[1])))
        br = jnp.concatenate([b_router_group[l], b_router_expert[l]])
        br = jnp.pad(br, (0, LANES - br.shape[0])).reshape(1, LANES)
        h1, xn, route, counts = _mix_call(
            ya.reshape(n_tok, sba), uv, gates, h, gt_m, sh_f, sc_f, g_sgu[l].reshape(1, sgu), w_spatial[l],
            bias_full, w_sba_out[l].astype(BF16), w_sgu_out[l].astype(BF16), w_out[l].astype(BF16),
            g_ffn[l].reshape(1, d), wr, br, seq, n_groups, per_group)

        dest, blk_e, used, zero_blocks = _routing_plan(route, counts[0, n_groups:n_groups + n_experts],
                                                       n_experts, m_pad)
        xs = _dispatch_call(zero_blocks, dest, xn, m_pad, seq)
        yb = _expert_call(blk_e, used, xs, w_expert_gate[l].astype(BF16), w_expert_up[l].astype(BF16),
                          w_expert_down[l].astype(BF16))
        h = _combine_call(dest, h1, route, gt_f, g_final.reshape(1, d), yb, seq, final=l == depth - 1)
    return h.reshape(bsz, seq, d)
```

```python
import functools

import jax
import jax.numpy as jnp
from jax import lax
from jax.experimental import pallas as pl
from jax.experimental.pallas import tpu as pltpu

F32 = jnp.float32
BF16 = jnp.bfloat16

HEAD_DIM = 64
TOP_K = 2
EPS = 1e-6
LOG2E = 1.4426950408889634
ATTN_EXIT_LOG2 = 160.0
LANES = 128
ATTN_BLOCK = 256
EXPERT_ROWS = 256
TOKEN_TILE = 256
PROJ_TILE = 512
ROUTE_COLS = 8
VMEM_LIMIT = 56 * 1024 * 1024


def _cparams(*sem):
    return pltpu.CompilerParams(dimension_semantics=sem, vmem_limit_bytes=VMEM_LIMIT)


def _split_bf16(a):
    hi = a.astype(BF16)
    lo = (a - hi.astype(F32)).astype(BF16)
    return hi, lo


def _dot(a, b):
    return jnp.dot(a, b, preferred_element_type=F32)


def _dot3(a, b):
    ah, al = _split_bf16(a)
    bh, bl = _split_bf16(b)
    return _dot(ah, bh) + _dot(ah, bl) + _dot(al, bh)


def _rms(x, g):
    ms = jnp.mean(x * x, axis=-1, keepdims=True)
    return x * lax.rsqrt(ms + EPS) * g


def _mod_kernel(c_ref, w_ref, b_ref, o_ref):
    c = c_ref[...]
    ca = c * (1.0 / (1.0 + jnp.exp(-c)))
    o_ref[...] = _dot3(ca, w_ref[...]) + b_ref[...]


def _mod_call(c, w_ada, b_ada):
    bsz, d = c.shape
    n = w_ada.shape[1]
    tn = n // 4 if n % (4 * LANES) == 0 else n
    return pl.pallas_call(
        _mod_kernel,
        out_shape=jax.ShapeDtypeStruct((bsz, n), F32),
        grid=(n // tn,),
        in_specs=[pl.BlockSpec((bsz, d), lambda j: (0, 0)),
                  pl.BlockSpec((d, tn), lambda j: (0, j)),
                  pl.BlockSpec((1, tn), lambda j: (0, j))],
        out_specs=pl.BlockSpec((bsz, tn), lambda j: (0, j)),
        compiler_params=_cparams("arbitrary"),
        name="mod",
    )(c, w_ada, b_ada.reshape(1, n))


def _proj_kernel(x_ref, sh_ref, sc_ref, g_ref, w_ref, q_ref, k_ref, v_ref, uv_ref, gt_ref, *, widths):
    x = x_ref[...]
    n = _rms(x, g_ref[...]) * (1.0 + sc_ref[0]) + sh_ref[0]
    nb = n.astype(BF16)
    off = 0
    for ref, wd in zip((q_ref, k_ref, v_ref, uv_ref, gt_ref), widths):
        p = _dot(nb, w_ref[:, off:off + wd])
        if ref is q_ref:
            p = p * (HEAD_DIM ** -0.5 * LOG2E)
        ref[...] = p.astype(ref.dtype)
        off += wd


def _proj_call(x2, sh, sc, g, w_in_bf, seq, widths):
    n_tok, d = x2.shape
    tm = min(PROJ_TILE, seq)
    per_b = seq // tm
    cols = w_in_bf.shape[1]
    tok = lambda w: pl.BlockSpec((tm, w), lambda i: (i, 0))
    vec = pl.BlockSpec((1, 1, d), lambda i: (i // per_b, 0, 0))
    return pl.pallas_call(
        functools.partial(_proj_kernel, widths=widths),
        out_shape=[jax.ShapeDtypeStruct((n_tok, w), BF16) for w in widths],
        grid=(n_tok // tm,),
        in_specs=[tok(d), vec, vec,
                  pl.BlockSpec((1, d), lambda i: (0, 0)),
                  pl.BlockSpec((d, cols), lambda i: (0, 0))],
        out_specs=[tok(w) for w in widths],
        compiler_params=_cparams("arbitrary"),
        name="proj",
    )(x2, sh, sc, g, w_in_bf)


def _attn_kernel(q_ref, k_ref, v_ref, u_ref, o_ref, qm_ref, acc_ref, carry_ref, *, heads, blk):
    qi = pl.program_id(1)
    u_tri = u_ref[...]
    row = lax.broadcasted_iota(jnp.int32, (blk, blk), 0)
    col = lax.broadcasted_iota(jnp.int32, (blk, blk), 1)
    causal = col < row
    per_slab = LANES // HEAD_DIM
    lane = lax.broadcasted_iota(jnp.int32, (blk, LANES), 1)
    own = [(lane >= j * HEAD_DIM) & (lane < (j + 1) * HEAD_DIM) for j in range(per_slab)]
    for h in range(heads):
        slab = slice(h // per_slab * LANES, (h // per_slab + 1) * LANES)
        qm_ref[h] = jnp.where(own[h % per_slab], q_ref[0, :, slab], 0.0).astype(BF16)

    def key_tile(start, first):
        mask = causal if first else None
        slabs = [slice(p * LANES, (p + 1) * LANES) for p in range(heads // per_slab)]
        s_all = [lax.dot_general(qm_ref[h], k_ref[0, pl.ds(start, blk), slabs[h // per_slab]],
                                 (((1,), (1,)), ((), ())), preferred_element_type=F32) for h in range(heads)]
        sp_all, lb_all = [], []
        for s in s_all:
            sp = jnp.maximum(s, 0.0) + jnp.log2(1.0 + jnp.exp2(-jnp.abs(s)))
            lb_all.append(s - sp)
            sp_all.append(sp if mask is None else jnp.where(mask, sp, 0.0))
        cum_all = []
        for sp in sp_all:
            hi, lo = _split_bf16(sp)
            cum_all.append(_dot(hi, u_tri) + _dot(lo, u_tri))
        w_all = []
        cmin = None
        for h in range(heads):
            carry = jnp.zeros((blk, 1), F32) if first else carry_ref[h]
            w = jnp.exp2(lb_all[h] - cum_all[h] - carry)
            w_all.append((w if mask is None else jnp.where(mask, w, 0.0)).astype(BF16))
            carry = carry + jnp.sum(sp_all[h], axis=1, keepdims=True)
            carry_ref[h] = carry
            cmin = carry if cmin is None else jnp.minimum(cmin, carry)
        for p, slab in enumerate(slabs):
            vp = v_ref[0, pl.ds(start, blk), slab]
            upd = None
            for j in range(per_slab):
                pv = _dot(w_all[p * per_slab + j], vp)
                upd = pv if upd is None else jnp.where(own[j], pv, upd)
            acc_ref[:, slab] = upd if first else acc_ref[:, slab] + upd
        return jnp.min(cmin)

    cmin = key_tile(pl.multiple_of(qi * blk, blk), True)

    def cond(state):
        i, cmin = state
        return (i < qi) & (cmin < ATTN_EXIT_LOG2)

    def body(state):
        i, _ = state
        return i + 1, key_tile(pl.multiple_of((qi - 1 - i) * blk, blk), False)

    lax.while_loop(cond, body, (jnp.int32(0), cmin))
    o_ref[0] = acc_ref[...].astype(o_ref.dtype)


def _attn_call(q, k, v, heads):
    bsz, seq, width = q.shape
    blk = min(ATTN_BLOCK, seq)
    r = lax.broadcasted_iota(jnp.int32, (blk, blk), 0)
    c = lax.broadcasted_iota(jnp.int32, (blk, blk), 1)
    u_tri = (r > c).astype(BF16)
    full = pl.BlockSpec((1, seq, width), lambda b, i: (b, 0, 0))
    return pl.pallas_call(
        functools.partial(_attn_kernel, heads=heads, blk=blk),
        out_shape=jax.ShapeDtypeStruct((bsz, seq, width), BF16),
        grid=(bsz, seq // blk),
        in_specs=[pl.BlockSpec((1, blk, width), lambda b, i: (b, i, 0)), full, full,
                  pl.BlockSpec((blk, blk), lambda b, i: (0, 0))],
        out_specs=pl.BlockSpec((1, blk, width), lambda b, i: (b, i, 0)),
        scratch_shapes=[pltpu.VMEM((heads, blk, LANES), BF16), pltpu.VMEM((blk, width), F32),
                        pltpu.VMEM((heads, blk, 1), F32)],
        compiler_params=_cparams("arbitrary", "arbitrary"),
        name="attn",
    )(q, k, v, u_tri)


def _gelu_tanh(x):
    return 0.5 * x * (1.0 + jnp.tanh(0.7978845608028654 * (x + 0.044715 * (x * x * x))))


def _sigmoid(x):
    return 1.0 / (1.0 + jnp.exp(-x))


def _mix_kernel(ya_ref, uv_ref, gt_ref, x_ref, gtm_ref, shf_ref, scf_ref, gsgu_ref, ws_ref, bs_ref,
                wa_ref, wb_ref, wo_ref, gffn_ref, wr_ref, br_ref, ltri_ref,
                h_ref, xn_ref, route_ref, route_t_ref, cnt_ref, yb_scr, cnt_scr,
                *, chunk, groups, n_groups, per_group):
    step = pl.program_id(0)
    tm = x_ref.shape[0]
    sgu = gsgu_ref.shape[1]
    gdim = sgu // groups

    @pl.when(step == 0)
    def _():
        cnt_scr[...] = jnp.zeros_like(cnt_scr)

    ya_proj = _dot(ya_ref[...], wa_ref[...])

    act = _gelu_tanh(uv_ref[...].astype(F32))
    u = act[:, :sgu]
    v = act[:, sgu:]
    mu = jnp.mean(v, axis=-1, keepdims=True)
    vc = v - mu
    var = jnp.mean(vc * vc, axis=-1, keepdims=True)
    vn = vc * lax.rsqrt(var + EPS) * gsgu_ref[...]
    r = lax.broadcasted_iota(jnp.int32, (chunk, chunk), 0)
    c = lax.broadcasted_iota(jnp.int32, (chunk, chunk), 1)
    wcs = [jnp.where(r >= c, ws_ref[g], 0.0).astype(BF16) for g in range(groups)]
    pair = LANES // gdim
    lane = lax.broadcasted_iota(jnp.int32, (chunk, LANES), 1)
    for ci in range(tm // chunk):
        rows = slice(ci * chunk, (ci + 1) * chunk)
        for p in range(groups // pair):
            lanes = slice(p * LANES, (p + 1) * LANES)
            slab = vn[rows, lanes]
            mix = bs_ref[:, lanes]
            for j in range(pair):
                sel = (lane >= j * gdim) & (lane < (j + 1) * gdim)
                mix = mix + _dot(wcs[p * pair + j], jnp.where(sel, slab, 0.0).astype(BF16))
            yb_scr[rows, lanes] = (u[rows, lanes] * mix).astype(BF16)

    d = x_ref.shape[1]
    gates = _sigmoid(gt_ref[...].astype(F32))
    merged = gates[:, :d] * ya_proj + gates[:, d:] * _dot(yb_scr[...], wb_ref[...])
    h = x_ref[...] + gtm_ref[0] * _dot(merged.astype(BF16), wo_ref[...])
    h_ref[...] = h

    xn = _rms(h, gffn_ref[...]) * (1.0 + scf_ref[0]) + shf_ref[0]
    xn_ref[...] = xn
    logits = _dot3(xn, wr_ref[...]) + br_ref[...]
    ln = lax.broadcasted_iota(jnp.int32, (tm, LANES), 1)
    neg = -jnp.inf
    gl = jnp.where(ln < n_groups, logits, neg)
    gmax = jnp.max(gl, axis=1, keepdims=True)
    g_sel = jnp.min(jnp.where(gl == gmax, ln, LANES), axis=1, keepdims=True)
    g_w = 1.0 / jnp.sum(jnp.where(ln < n_groups, jnp.exp(logits - gmax), 0.0), axis=1, keepdims=True)
    lo_lane = n_groups + per_group * g_sel
    el = jnp.where((ln >= lo_lane) & (ln < lo_lane + per_group), logits, neg)
    m1 = jnp.max(el, axis=1, keepdims=True)
    i1 = jnp.min(jnp.where(el == m1, ln, LANES), axis=1, keepdims=True)
    el2 = jnp.where(ln == i1, neg, el)
    m2 = jnp.max(el2, axis=1, keepdims=True)
    i2 = jnp.min(jnp.where(el2 == m2, ln, LANES), axis=1, keepdims=True)
    t = jnp.exp(m2 - m1)
    w1 = g_w / (1.0 + t)
    w2 = g_w * t / (1.0 + t)

    oh1 = ln == i1
    oh2 = ln == i2
    both = jnp.where(oh1 | oh2, 1.0, 0.0)
    before = _dot(ltri_ref[...], both.astype(BF16)) + cnt_scr[...]
    rank1 = jnp.sum(jnp.where(oh1, before, 0.0), axis=1, keepdims=True)
    rank2 = jnp.sum(jnp.where(oh2, before, 0.0), axis=1, keepdims=True)
    cnt_scr[...] = cnt_scr[...] + jnp.sum(both, axis=0, keepdims=True)
    cnt_ref[...] = cnt_scr[...]

    e1 = (i1 - n_groups).astype(F32)
    e2 = (i2 - n_groups).astype(F32)
    out = jnp.zeros((tm, LANES), F32)
    for idx, val in enumerate((e1, e2, w1, w2, rank1, rank2)):
        out = jnp.where(ln == idx, val, out)
    route_ref[...] = out[:, :ROUTE_COLS]
    route_t_ref[...] = out.T[:ROUTE_COLS, :]


def _mix_call(ya, uv, gates, x2, gtm, shf, scf, g_sgu, w_spatial, bias_full, wa, wb, wo, g_ffn, wr, br,
              seq, n_groups, per_group):
    n_tok, d = x2.shape
    tm = min(TOKEN_TILE, seq)
    per_b = seq // tm
    groups, chunk, _ = w_spatial.shape
    sgu = g_sgu.shape[1]
    r = lax.broadcasted_iota(jnp.int32, (tm, tm), 0)
    c = lax.broadcasted_iota(jnp.int32, (tm, tm), 1)
    ltri = (r > c).astype(BF16)
    tok = lambda w: pl.BlockSpec((tm, w), lambda i: (i, 0))
    vec = pl.BlockSpec((1, 1, d), lambda i: (i // per_b, 0, 0))
    const = lambda a: pl.BlockSpec(a.shape, lambda i: (0,) * a.ndim)
    ins = [ya, uv, gates, x2, gtm, shf, scf, g_sgu, w_spatial, bias_full, wa, wb, wo, g_ffn, wr, br, ltri]
    in_specs = [tok(ya.shape[1]), tok(uv.shape[1]), tok(gates.shape[1]), tok(d), vec, vec, vec]
    in_specs += [const(a) for a in ins[7:]]
    return pl.pallas_call(
        functools.partial(_mix_kernel, chunk=chunk, groups=groups, n_groups=n_groups, per_group=per_group),
        out_shape=[jax.ShapeDtypeStruct((n_tok, d), F32), jax.ShapeDtypeStruct((n_tok, d), F32),
                   jax.ShapeDtypeStruct((n_tok, ROUTE_COLS), F32), jax.ShapeDtypeStruct((ROUTE_COLS, n_tok), F32),
                   jax.ShapeDtypeStruct((1, LANES), F32)],
        grid=(n_tok // tm,),
        in_specs=in_specs,
        out_specs=[tok(d), tok(d), tok(ROUTE_COLS), pl.BlockSpec((ROUTE_COLS, tm), lambda i: (0, i)),
                   pl.BlockSpec((1, LANES), lambda i: (0, 0))],
        scratch_shapes=[pltpu.VMEM((tm, sgu), BF16), pltpu.VMEM((1, LANES), F32)],
        compiler_params=_cparams("arbitrary"),
        name="mix",
    )(*ins)


def _row_copy(src, s, dst, d, sem):
    return pltpu.make_async_copy(src.at[pl.ds(s, 1)], dst.at[pl.ds(d, 1)], sem)


def _dispatch_kernel(zblk_ref, *refs, tm, rows):
    dest_refs, (xn_ref, xs_ref, zero_scr, sem, zsem) = refs[:TOP_K], refs[TOP_K:]

    @pl.when(pl.program_id(0) == 0)
    def _():
        zero_scr[...] = jnp.zeros_like(zero_scr)

        def zero_copy(j):
            start = pl.multiple_of(zblk_ref[j] * rows, rows)
            return pltpu.make_async_copy(zero_scr, xs_ref.at[pl.ds(start, rows)], zsem)

        for j in range(zblk_ref.shape[0]):
            pl.when(zblk_ref[j] >= 0)(lambda j=j: zero_copy(j).start())
        for j in range(zblk_ref.shape[0]):
            pl.when(zblk_ref[j] >= 0)(lambda j=j: zero_copy(j).wait())

    for j in range(tm):
        for dref in dest_refs:
            _row_copy(xn_ref, j, xs_ref, dref[0, 0, j], sem).start()
    for _ in range(TOP_K):
        pltpu.make_async_copy(xn_ref, xs_ref.at[pl.ds(0, tm)], sem).wait()


def _dest_blocks(dest, tm):
    return [dest[k].reshape(-1, 1, tm) for k in range(TOP_K)]


def _dispatch_call(zero_blocks, dest, xn, m_pad, seq):
    n_tok, d = xn.shape
    tm = min(TOKEN_TILE, seq)
    smem = pl.BlockSpec((1, 1, tm), lambda i, zb: (i, 0, 0), memory_space=pltpu.SMEM)
    return pl.pallas_call(
        functools.partial(_dispatch_kernel, tm=tm, rows=EXPERT_ROWS),
        out_shape=jax.ShapeDtypeStruct((m_pad, d), F32),
        grid_spec=pltpu.PrefetchScalarGridSpec(
            num_scalar_prefetch=1,
            grid=(n_tok // tm,),
            in_specs=[smem] * TOP_K + [pl.BlockSpec((tm, d), lambda i, zb: (i, 0))],
            out_specs=pl.BlockSpec(memory_space=pl.ANY),
            scratch_shapes=[pltpu.VMEM((EXPERT_ROWS, d), F32), pltpu.SemaphoreType.DMA(()),
                            pltpu.SemaphoreType.DMA(())]),
        compiler_params=_cparams("arbitrary"),
        name="dispatch",
    )(zero_blocks, *_dest_blocks(dest, tm), xn)


def _expert_kernel(blk_e_ref, used_ref, xs_ref, wg_ref, wu_ref, wd_ref, y_ref):
    del blk_e_ref

    @pl.when(pl.program_id(0) < used_ref[0])
    def _():
        xb = xs_ref[...].astype(BF16)
        g = _dot(xb, wg_ref[0])
        hid = g * _sigmoid(g) * _dot(xb, wu_ref[0])
        y_ref[...] = _dot(hid.astype(BF16), wd_ref[0])

    @pl.when(pl.program_id(0) >= used_ref[0])
    def _():
        y_ref[...] = jnp.zeros_like(y_ref)


def _expert_call(blk_e, used, xs, wg, wu, wd):
    m_pad, d = xs.shape
    f = wg.shape[2]
    rows = EXPERT_ROWS
    return pl.pallas_call(
        _expert_kernel,
        out_shape=jax.ShapeDtypeStruct((m_pad, d), F32),
        grid_spec=pltpu.PrefetchScalarGridSpec(
            num_scalar_prefetch=2,
            grid=(m_pad // rows,),
            in_specs=[pl.BlockSpec((rows, d), lambda i, be, us: (i, 0)),
                      pl.BlockSpec((1, d, f), lambda i, be, us: (be[i], 0, 0)),
                      pl.BlockSpec((1, d, f), lambda i, be, us: (be[i], 0, 0)),
                      pl.BlockSpec((1, f, d), lambda i, be, us: (be[i], 0, 0))],
            out_specs=pl.BlockSpec((rows, d), lambda i, be, us: (i, 0))),
        compiler_params=_cparams("arbitrary"),
        name="experts",
    )(blk_e, used, xs, wg, wu, wd)


def _combine_kernel(*refs, tm, final):
    cur_refs, nxt_refs = refs[:TOP_K], refs[TOP_K:2 * TOP_K]
    h_ref, route_ref, gtf_ref, gfin_ref, y_ref, o_ref, buf, sems = refs[2 * TOP_K:]
    step = pl.program_id(0)
    slot = step % 2

    def gather(dest_refs, s):
        for j in range(tm):
            for k, dref in enumerate(dest_refs):
                pltpu.make_async_copy(y_ref.at[pl.ds(dref[0, 0, j], 1)], buf.at[s, k, pl.ds(j, 1)],
                                      sems.at[s]).start()

    pl.when(step == 0)(lambda: gather(cur_refs, 0))
    pl.when(step + 1 < pl.num_programs(0))(lambda: gather(nxt_refs, 1 - slot))
    for k in range(TOP_K):
        pltpu.make_async_copy(y_ref.at[pl.ds(0, tm)], buf.at[slot, k], sems.at[slot]).wait()
    route = route_ref[...]
    y = route[:, 2:3] * buf[slot, 0] + route[:, 3:4] * buf[slot, 1]
    h = h_ref[...] + gtf_ref[0] * y
    o_ref[...] = _rms(h, gfin_ref[...]) if final else h


def _combine_call(dest, h, route, gtf, g_final, yb, seq, final):
    n_tok, d = h.shape
    tm = min(TOKEN_TILE, seq)
    per_b = seq // tm
    steps = n_tok // tm
    tok = lambda w: pl.BlockSpec((tm, w), lambda i: (i, 0))
    cur = pl.BlockSpec((1, 1, tm), lambda i: (i, 0, 0), memory_space=pltpu.SMEM)
    nxt = pl.BlockSpec((1, 1, tm), lambda i: (jnp.minimum(i + 1, steps - 1), 0, 0), memory_space=pltpu.SMEM)
    blocks = _dest_blocks(dest, tm)
    return pl.pallas_call(
        functools.partial(_combine_kernel, tm=tm, final=final),
        out_shape=jax.ShapeDtypeStruct((n_tok, d), F32),
        grid=(steps,),
        in_specs=[cur] * TOP_K + [nxt] * TOP_K + [
            tok(d), tok(ROUTE_COLS),
            pl.BlockSpec((1, 1, d), lambda i: (i // per_b, 0, 0)),
            pl.BlockSpec((1, d), lambda i: (0, 0)),
            pl.BlockSpec(memory_space=pl.ANY)],
        out_specs=tok(d),
        scratch_shapes=[pltpu.VMEM((2, TOP_K, tm, d), F32), pltpu.SemaphoreType.DMA((2,))],
        compiler_params=_cparams("arbitrary"),
        name="combine",
    )(*blocks, *blocks, h, route, gtf, g_final, yb)


def _routing_plan(route_t, counts, n_experts, m_pad):
    eid = route_t[0:TOP_K].astype(jnp.int32)
    rank = route_t[4:4 + TOP_K].astype(jnp.int32)
    cnt = counts.astype(jnp.int32)
    padded = (cnt + EXPERT_ROWS - 1) // EXPERT_ROWS * EXPERT_ROWS
    pend = jnp.cumsum(padded)
    pstart = pend - padded
    dest = jnp.take(pstart, eid) + rank
    nblk = m_pad // EXPERT_ROWS
    blk_row = jnp.arange(nblk, dtype=jnp.int32) * EXPERT_ROWS
    blk_e = jnp.minimum(jnp.sum(pend[None, :] <= blk_row[:, None], axis=1), n_experts - 1).astype(jnp.int32)
    used = (pend[-1:] // EXPERT_ROWS).astype(jnp.int32)
    last_blk = jnp.where(padded > 0, pend // EXPERT_ROWS - 1, -1)
    tail_blk = used + jnp.arange(n_experts, dtype=jnp.int32)
    tail_blk = jnp.where(tail_blk < nblk, tail_blk, -1)
    zero_blocks = jnp.concatenate([last_blk, tail_blk]).astype(jnp.int32)
    return dest, blk_e, used, zero_blocks


def kernel(x, c, g_mix, g_ffn, w_ada, b_ada, w_in, w_sba_out, g_sgu, w_spatial, b_spatial, w_sgu_out, w_out,
           w_router_group, b_router_group, w_router_expert, b_router_expert, w_expert_gate, w_expert_up,
           w_expert_down, g_final):
    bsz, seq, d = x.shape
    depth = w_in.shape[0]
    sba = w_sba_out.shape[1]
    sgu = g_sgu.shape[1]
    heads = sba // HEAD_DIM
    groups, chunk = w_spatial.shape[1], w_spatial.shape[2]
    n_groups = w_router_group.shape[2]
    n_experts = w_router_expert.shape[2]
    per_group = n_experts // n_groups
    n_tok = bsz * seq
    m_pad = n_tok * TOP_K + n_experts * EXPERT_ROWS
    widths = (sba, sba, sba, 2 * sgu, 2 * d)
    assert seq % chunk == 0 and LANES % (sgu // groups) == 0 and n_groups + n_experts <= LANES

    h = x.reshape(n_tok, d)
    for l in range(depth):
        mod = _mod_call(c, w_ada[l], b_ada[l])
        sh_m, sc_m, gt_m, sh_f, sc_f, gt_f = [mod[:, i * d:(i + 1) * d].reshape(bsz, 1, d) for i in range(6)]

        q, k, v, uv, gates = _proj_call(h, sh_m, sc_m, g_mix[l].reshape(1, d), w_in[l].astype(BF16), seq, widths)
        ya = _attn_call(q.reshape(bsz, seq, sba), k.reshape(bsz, seq, sba), v.reshape(bsz, seq, sba), heads)

        bias_full = jnp.repeat(b_spatial[l].T, sgu // groups, axis=1)
        wr = jnp.concatenate([w_router_group[l], w_router_expert[l]], axis=1)
        wr = jnp.pad(wr, ((0, 0), (0, LANES - wr.shape[1])))
        br = jnp.concatenate([b_router_group[l], b_router_expert[l]])
        br = jnp.pad(br, (0, LANES - br.shape[0])).reshape(1, LANES)
        h1, xn, route, route_t, counts = _mix_call(
            ya.reshape(n_tok, sba), uv, gates, h, gt_m, sh_f, sc_f, g_sgu[l].reshape(1, sgu), w_spatial[l],
            bias_full, w_sba_out[l].astype(BF16), w_sgu_out[l].astype(BF16), w_out[l].astype(BF16),
            g_ffn[l].reshape(1, d), wr, br, seq, n_groups, per_group)

        dest, blk_e, used, zero_blocks = _routing_plan(route_t, counts[0, n_groups:n_groups + n_experts],
                                                       n_experts, m_pad)
        xs = _dispatch_call(zero_blocks, dest, xn, m_pad, seq)
        yb = _expert_call(blk_e, used, xs, w_expert_gate[l].astype(BF16), w_expert_up[l].astype(BF16),
                          w_expert_down[l].astype(BF16))
        h = _combine_call(dest, h1, route, gt_f, g_final.reshape(1, d), yb, seq, final=l == depth - 1)
    return h.reshape(bsz, seq, d)
```

```python
import functools

import jax
import jax.numpy as jnp
from jax import lax
from jax.experimental import pallas as pl
from jax.experimental.pallas import tpu as pltpu

F32 = jnp.float32
BF16 = jnp.bfloat16

HEAD_DIM = 64
TOP_K = 2
EPS = 1e-6
LOG2E = 1.4426950408889634
ATTN_EXIT_LOG2 = 160.0
LANES = 128
ATTN_BLOCK = 256
EXPERT_ROWS = 256
TOKEN_TILE = 256
PROJ_TILE = 512
ROUTE_COLS = 8
VMEM_LIMIT = 56 * 1024 * 1024


def _cparams(*sem):
    return pltpu.CompilerParams(dimension_semantics=sem, vmem_limit_bytes=VMEM_LIMIT)


def _split_bf16(a):
    hi = a.astype(BF16)
    lo = (a - hi.astype(F32)).astype(BF16)
    return hi, lo


def _dot(a, b):
    return jnp.dot(a, b, preferred_element_type=F32)


def _dot3(a, b):
    ah, al = _split_bf16(a)
    bh, bl = _split_bf16(b)
    return _dot(ah, bh) + _dot(ah, bl) + _dot(al, bh)


def _rms(x, g):
    ms = jnp.mean(x * x, axis=-1, keepdims=True)
    return x * lax.rsqrt(ms + EPS) * g


def _mod_kernel(c_ref, w_ref, b_ref, o_ref):
    c = c_ref[...]
    ca = c * (1.0 / (1.0 + jnp.exp(-c)))
    o_ref[...] = _dot3(ca, w_ref[...]) + b_ref[...]


def _mod_call(c, w_ada, b_ada):
    bsz, d = c.shape
    n = w_ada.shape[1]
    tn = n // 4 if n % (4 * LANES) == 0 else n
    return pl.pallas_call(
        _mod_kernel,
        out_shape=jax.ShapeDtypeStruct((bsz, n), F32),
        grid=(n // tn,),
        in_specs=[pl.BlockSpec((bsz, d), lambda j: (0, 0)),
                  pl.BlockSpec((d, tn), lambda j: (0, j)),
                  pl.BlockSpec((1, tn), lambda j: (0, j))],
        out_specs=pl.BlockSpec((bsz, tn), lambda j: (0, j)),
        compiler_params=_cparams("arbitrary"),
        name="mod",
    )(c, w_ada, b_ada.reshape(1, n))


def _proj_kernel(x_ref, sh_ref, sc_ref, g_ref, w_ref, q_ref, k_ref, v_ref, uv_ref, gt_ref, *, widths):
    x = x_ref[...]
    n = _rms(x, g_ref[...]) * (1.0 + sc_ref[0]) + sh_ref[0]
    nb = n.astype(BF16)
    off = 0
    for ref, wd in zip((q_ref, k_ref, v_ref, uv_ref, gt_ref), widths):
        p = _dot(nb, w_ref[:, off:off + wd])
        if ref is q_ref:
            p = p * (HEAD_DIM ** -0.5 * LOG2E)
        ref[...] = p.astype(ref.dtype)
        off += wd


def _proj_call(x2, sh, sc, g, w_in_bf, seq, widths):
    n_tok, d = x2.shape
    tm = min(PROJ_TILE, seq)
    per_b = seq // tm
    cols = w_in_bf.shape[1]
    tok = lambda w: pl.BlockSpec((tm, w), lambda i: (i, 0))
    vec = pl.BlockSpec((1, 1, d), lambda i: (i // per_b, 0, 0))
    return pl.pallas_call(
        functools.partial(_proj_kernel, widths=widths),
        out_shape=[jax.ShapeDtypeStruct((n_tok, w), BF16) for w in widths],
        grid=(n_tok // tm,),
        in_specs=[tok(d), vec, vec,
                  pl.BlockSpec((1, d), lambda i: (0, 0)),
                  pl.BlockSpec((d, cols), lambda i: (0, 0))],
        out_specs=[tok(w) for w in widths],
        compiler_params=_cparams("arbitrary"),
        name="proj",
    )(x2, sh, sc, g, w_in_bf)


def _attn_kernel(q_ref, k_ref, v_ref, u_ref, o_ref, qm_ref, acc_ref, carry_ref, *, heads, blk):
    qi = pl.program_id(1)
    u_tri = u_ref[...]
    row = lax.broadcasted_iota(jnp.int32, (blk, blk), 0)
    col = lax.broadcasted_iota(jnp.int32, (blk, blk), 1)
    causal = col < row
    per_slab = LANES // HEAD_DIM
    lane = lax.broadcasted_iota(jnp.int32, (blk, LANES), 1)
    own = [(lane >= j * HEAD_DIM) & (lane < (j + 1) * HEAD_DIM) for j in range(per_slab)]
    for h in range(heads):
        slab = slice(h // per_slab * LANES, (h // per_slab + 1) * LANES)
        qm_ref[h] = jnp.where(own[h % per_slab], q_ref[0, :, slab], 0.0).astype(BF16)

    def key_tile(start, first):
        mask = causal if first else None
        slabs = [slice(p * LANES, (p + 1) * LANES) for p in range(heads // per_slab)]
        s_all = [lax.dot_general(qm_ref[h], k_ref[0, pl.ds(start, blk), slabs[h // per_slab]],
                                 (((1,), (1,)), ((), ())), preferred_element_type=F32) for h in range(heads)]
        sp_all, lb_all = [], []
        for s in s_all:
            sp = jnp.maximum(s, 0.0) + jnp.log2(1.0 + jnp.exp2(-jnp.abs(s)))
            lb_all.append(s - sp)
            sp_all.append(sp if mask is None else jnp.where(mask, sp, 0.0))
        cum_all = []
        for sp in sp_all:
            hi, lo = _split_bf16(sp)
            cum_all.append(_dot(hi, u_tri) + _dot(lo, u_tri))
        w_all = []
        cmin = None
        for h in range(heads):
            carry = jnp.zeros((blk, 1), F32) if first else carry_ref[h]
            w = jnp.exp2(lb_all[h] - cum_all[h] - carry)
            w_all.append((w if mask is None else jnp.where(mask, w, 0.0)).astype(BF16))
            carry = carry + jnp.sum(sp_all[h], axis=1, keepdims=True)
            carry_ref[h] = carry
            cmin = carry if cmin is None else jnp.minimum(cmin, carry)
        for p, slab in enumerate(slabs):
            vp = v_ref[0, pl.ds(start, blk), slab]
            upd = None
            for j in range(per_slab):
                pv = _dot(w_all[p * per_slab + j], vp)
                upd = pv if upd is None else jnp.where(own[j], pv, upd)
            acc_ref[:, slab] = upd if first else acc_ref[:, slab] + upd
        return jnp.min(cmin)

    cmin = key_tile(pl.multiple_of(qi * blk, blk), True)

    def cond(state):
        i, cmin = state
        return (i < qi) & (cmin < ATTN_EXIT_LOG2)

    def body(state):
        i, _ = state
        return i + 1, key_tile(pl.multiple_of((qi - 1 - i) * blk, blk), False)

    lax.while_loop(cond, body, (jnp.int32(0), cmin))
    o_ref[0] = acc_ref[...].astype(o_ref.dtype)


def _attn_call(q, k, v, heads):
    bsz, seq, width = q.shape
    blk = min(ATTN_BLOCK, seq)
    r = lax.broadcasted_iota(jnp.int32, (blk, blk), 0)
    c = lax.broadcasted_iota(jnp.int32, (blk, blk), 1)
    u_tri = (r > c).astype(BF16)
    full = pl.BlockSpec((1, seq, width), lambda b, i: (b, 0, 0))
    return pl.pallas_call(
        functools.partial(_attn_kernel, heads=heads, blk=blk),
        out_shape=jax.ShapeDtypeStruct((bsz, seq, width), BF16),
        grid=(bsz, seq // blk),
        in_specs=[pl.BlockSpec((1, blk, width), lambda b, i: (b, i, 0)), full, full,
                  pl.BlockSpec((blk, blk), lambda b, i: (0, 0))],
        out_specs=pl.BlockSpec((1, blk, width), lambda b, i: (b, i, 0)),
        scratch_shapes=[pltpu.VMEM((heads, blk, LANES), BF16), pltpu.VMEM((blk, width), F32),
                        pltpu.VMEM((heads, blk, 1), F32)],
        compiler_params=_cparams("arbitrary", "arbitrary"),
        name="attn",
    )(q, k, v, u_tri)


def _gelu_tanh(x):
    return 0.5 * x * (1.0 + jnp.tanh(0.7978845608028654 * (x + 0.044715 * (x * x * x))))


def _sigmoid(x):
    return 1.0 / (1.0 + jnp.exp(-x))


def _mix_kernel(ya_ref, uv_ref, gt_ref, x_ref, gtm_ref, shf_ref, scf_ref, gsgu_ref, ws_ref, bs_ref,
                wa_ref, wb_ref, wo_ref, gffn_ref, wr_ref, br_ref, ltri_ref,
                h_ref, xn_ref, route_ref, route_t_ref, cnt_ref, yb_scr, cnt_scr,
                *, chunk, groups, n_groups, per_group):
    step = pl.program_id(0)
    tm = x_ref.shape[0]
    sgu = gsgu_ref.shape[1]
    gdim = sgu // groups

    @pl.when(step == 0)
    def _():
        cnt_scr[...] = jnp.zeros_like(cnt_scr)

    ya_proj = _dot(ya_ref[...], wa_ref[...])

    act = _gelu_tanh(uv_ref[...].astype(F32))
    u = act[:, :sgu]
    v = act[:, sgu:]
    mu = jnp.mean(v, axis=-1, keepdims=True)
    vc = v - mu
    var = jnp.mean(vc * vc, axis=-1, keepdims=True)
    vn = vc * lax.rsqrt(var + EPS) * gsgu_ref[...]
    r = lax.broadcasted_iota(jnp.int32, (chunk, chunk), 0)
    c = lax.broadcasted_iota(jnp.int32, (chunk, chunk), 1)
    wcs = [jnp.where(r >= c, ws_ref[g], 0.0).astype(BF16) for g in range(groups)]
    pair = LANES // gdim
    lane = lax.broadcasted_iota(jnp.int32, (chunk, LANES), 1)
    for ci in range(tm // chunk):
        rows = slice(ci * chunk, (ci + 1) * chunk)
        for p in range(groups // pair):
            lanes = slice(p * LANES, (p + 1) * LANES)
            slab = vn[rows, lanes]
            mix = bs_ref[:, lanes]
            for j in range(pair):
                sel = (lane >= j * gdim) & (lane < (j + 1) * gdim)
                mix = mix + _dot(wcs[p * pair + j], jnp.where(sel, slab, 0.0).astype(BF16))
            yb_scr[rows, lanes] = (u[rows, lanes] * mix).astype(BF16)

    d = x_ref.shape[1]
    gates = _sigmoid(gt_ref[...].astype(F32))
    merged = gates[:, :d] * ya_proj + gates[:, d:] * _dot(yb_scr[...], wb_ref[...])
    h = x_ref[...] + gtm_ref[0] * _dot(merged.astype(BF16), wo_ref[...])
    h_ref[...] = h

    xn = _rms(h, gffn_ref[...]) * (1.0 + scf_ref[0]) + shf_ref[0]
    xn_ref[...] = xn
    logits = _dot3(xn, wr_ref[...]) + br_ref[...]
    ln = lax.broadcasted_iota(jnp.int32, (tm, LANES), 1)
    neg = -jnp.inf
    gl = jnp.where(ln < n_groups, logits, neg)
    gmax = jnp.max(gl, axis=1, keepdims=True)
    g_sel = jnp.min(jnp.where(gl == gmax, ln, LANES), axis=1, keepdims=True)
    g_w = 1.0 / jnp.sum(jnp.where(ln < n_groups, jnp.exp(logits - gmax), 0.0), axis=1, keepdims=True)
    lo_lane = n_groups + per_group * g_sel
    el = jnp.where((ln >= lo_lane) & (ln < lo_lane + per_group), logits, neg)
    m1 = jnp.max(el, axis=1, keepdims=True)
    i1 = jnp.min(jnp.where(el == m1, ln, LANES), axis=1, keepdims=True)
    el2 = jnp.where(ln == i1, neg, el)
    m2 = jnp.max(el2, axis=1, keepdims=True)
    i2 = jnp.min(jnp.where(el2 == m2, ln, LANES), axis=1, keepdims=True)
    t = jnp.exp(m2 - m1)
    w1 = g_w / (1.0 + t)
    w2 = g_w * t / (1.0 + t)

    oh1 = ln == i1
    oh2 = ln == i2
    both = jnp.where(oh1 | oh2, 1.0, 0.0)
    before = _dot(ltri_ref[...], both.astype(BF16)) + cnt_scr[...]
    rank1 = jnp.sum(jnp.where(oh1, before, 0.0), axis=1, keepdims=True)
    rank2 = jnp.sum(jnp.where(oh2, before, 0.0), axis=1, keepdims=True)
    cnt_scr[...] = cnt_scr[...] + jnp.sum(both, axis=0, keepdims=True)
    cnt_ref[...] = cnt_scr[...]

    e1 = (i1 - n_groups).astype(F32)
    e2 = (i2 - n_groups).astype(F32)
    out = jnp.zeros((tm, LANES), F32)
    for idx, val in enumerate((e1, e2, w1, w2, rank1, rank2)):
        out = jnp.where(ln == idx, val, out)
    route_ref[...] = out[:, :ROUTE_COLS]
    route_t_ref[...] = out.T[:ROUTE_COLS, :]


def _mix_call(ya, uv, gates, x2, gtm, shf, scf, g_sgu, w_spatial, bias_full, wa, wb, wo, g_ffn, wr, br,
              seq, n_groups, per_group):
    n_tok, d = x2.shape
    tm = min(TOKEN_TILE, seq)
    per_b = seq // tm
    groups, chunk, _ = w_spatial.shape
    sgu = g_sgu.shape[1]
    r = lax.broadcasted_iota(jnp.int32, (tm, tm), 0)
    c = lax.broadcasted_iota(jnp.int32, (tm, tm), 1)
    ltri = (r > c).astype(BF16)
    tok = lambda w: pl.BlockSpec((tm, w), lambda i: (i, 0))
    vec = pl.BlockSpec((1, 1, d), lambda i: (i // per_b, 0, 0))
    const = lambda a: pl.BlockSpec(a.shape, lambda i: (0,) * a.ndim)
    ins = [ya, uv, gates, x2, gtm, shf, scf, g_sgu, w_spatial, bias_full, wa, wb, wo, g_ffn, wr, br, ltri]
    in_specs = [tok(ya.shape[1]), tok(uv.shape[1]), tok(gates.shape[1]), tok(d), vec, vec, vec]
    in_specs += [const(a) for a in ins[7:]]
    return pl.pallas_call(
        functools.partial(_mix_kernel, chunk=chunk, groups=groups, n_groups=n_groups, per_group=per_group),
        out_shape=[jax.ShapeDtypeStruct((n_tok, d), F32), jax.ShapeDtypeStruct((n_tok, d), F32),
                   jax.ShapeDtypeStruct((n_tok, ROUTE_COLS), F32), jax.ShapeDtypeStruct((ROUTE_COLS, n_tok), F32),
                   jax.ShapeDtypeStruct((1, LANES), F32)],
        grid=(n_tok // tm,),
        in_specs=in_specs,
        out_specs=[tok(d), tok(d), tok(ROUTE_COLS), pl.BlockSpec((ROUTE_COLS, tm), lambda i: (0, i)),
                   pl.BlockSpec((1, LANES), lambda i: (0, 0))],
        scratch_shapes=[pltpu.VMEM((tm, sgu), BF16), pltpu.VMEM((1, LANES), F32)],
        compiler_params=_cparams("arbitrary"),
        name="mix",
    )(*ins)


def _row_copy(src, s, dst, d, sem):
    return pltpu.make_async_copy(src.at[pl.ds(s, 1)], dst.at[pl.ds(d, 1)], sem)


def _dispatch_kernel(zblk_ref, *refs, tm, rows):
    dest_refs, (xn_ref, xs_ref, zero_scr, sem, zsem) = refs[:TOP_K], refs[TOP_K:]

    @pl.when(pl.program_id(0) == 0)
    def _():
        zero_scr[...] = jnp.zeros_like(zero_scr)

        def zero_copy(j):
            start = pl.multiple_of(zblk_ref[j] * rows, rows)
            return pltpu.make_async_copy(zero_scr, xs_ref.at[pl.ds(start, rows)], zsem)

        for j in range(zblk_ref.shape[0]):
            pl.when(zblk_ref[j] >= 0)(lambda j=j: zero_copy(j).start())
        for j in range(zblk_ref.shape[0]):
            pl.when(zblk_ref[j] >= 0)(lambda j=j: zero_copy(j).wait())

    for j in range(tm):
        for k, dref in enumerate(dest_refs):
            _row_copy(xn_ref, j, xs_ref, dref[0, 0, j], sem).start(priority=k % 2)
    for _ in range(TOP_K):
        pltpu.make_async_copy(xn_ref, xs_ref.at[pl.ds(0, tm)], sem).wait()


def _dest_blocks(dest, tm):
    return [dest[k].reshape(-1, 1, tm) for k in range(TOP_K)]


def _dispatch_call(zero_blocks, dest, xn, m_pad, seq):
    n_tok, d = xn.shape
    tm = min(TOKEN_TILE, seq)
    smem = pl.BlockSpec((1, 1, tm), lambda i, zb: (i, 0, 0), memory_space=pltpu.SMEM)
    return pl.pallas_call(
        functools.partial(_dispatch_kernel, tm=tm, rows=EXPERT_ROWS),
        out_shape=jax.ShapeDtypeStruct((m_pad, d), F32),
        grid_spec=pltpu.PrefetchScalarGridSpec(
            num_scalar_prefetch=1,
            grid=(n_tok // tm,),
            in_specs=[smem] * TOP_K + [pl.BlockSpec((tm, d), lambda i, zb: (i, 0))],
            out_specs=pl.BlockSpec(memory_space=pl.ANY),
            scratch_shapes=[pltpu.VMEM((EXPERT_ROWS, d), F32), pltpu.SemaphoreType.DMA(()),
                            pltpu.SemaphoreType.DMA(())]),
        compiler_params=_cparams("arbitrary"),
        name="dispatch",
    )(zero_blocks, *_dest_blocks(dest, tm), xn)


def _expert_kernel(blk_e_ref, used_ref, xs_ref, wg_ref, wu_ref, wd_ref, y_ref):
    del blk_e_ref

    @pl.when(pl.program_id(0) < used_ref[0])
    def _():
        xb = xs_ref[...].astype(BF16)
        g = _dot(xb, wg_ref[0])
        hid = g * _sigmoid(g) * _dot(xb, wu_ref[0])
        y_ref[...] = _dot(hid.astype(BF16), wd_ref[0])

    @pl.when(pl.program_id(0) >= used_ref[0])
    def _():
        y_ref[...] = jnp.zeros_like(y_ref)


def _expert_call(blk_e, used, xs, wg, wu, wd):
    m_pad, d = xs.shape
    f = wg.shape[2]
    rows = EXPERT_ROWS
    return pl.pallas_call(
        _expert_kernel,
        out_shape=jax.ShapeDtypeStruct((m_pad, d), F32),
        grid_spec=pltpu.PrefetchScalarGridSpec(
            num_scalar_prefetch=2,
            grid=(m_pad // rows,),
            in_specs=[pl.BlockSpec((rows, d), lambda i, be, us: (i, 0)),
                      pl.BlockSpec((1, d, f), lambda i, be, us: (be[i], 0, 0)),
                      pl.BlockSpec((1, d, f), lambda i, be, us: (be[i], 0, 0)),
                      pl.BlockSpec((1, f, d), lambda i, be, us: (be[i], 0, 0))],
            out_specs=pl.BlockSpec((rows, d), lambda i, be, us: (i, 0))),
        compiler_params=_cparams("arbitrary"),
        name="experts",
    )(blk_e, used, xs, wg, wu, wd)


def _combine_kernel(*refs, tm, final):
    cur_refs, nxt_refs = refs[:TOP_K], refs[TOP_K:2 * TOP_K]
    h_ref, route_ref, gtf_ref, gfin_ref, y_ref, o_ref, buf, sems = refs[2 * TOP_K:]
    step = pl.program_id(0)
    slot = step % 2

    def gather(dest_refs, s):
        for j in range(tm):
            for k, dref in enumerate(dest_refs):
                pltpu.make_async_copy(y_ref.at[pl.ds(dref[0, 0, j], 1)], buf.at[s, k, pl.ds(j, 1)],
                                      sems.at[s]).start(priority=k % 2)

    pl.when(step == 0)(lambda: gather(cur_refs, 0))
    pl.when(step + 1 < pl.num_programs(0))(lambda: gather(nxt_refs, 1 - slot))
    for k in range(TOP_K):
        pltpu.make_async_copy(y_ref.at[pl.ds(0, tm)], buf.at[slot, k], sems.at[slot]).wait()
    route = route_ref[...]
    y = route[:, 2:3] * buf[slot, 0] + route[:, 3:4] * buf[slot, 1]
    h = h_ref[...] + gtf_ref[0] * y
    o_ref[...] = _rms(h, gfin_ref[...]) if final else h


def _combine_call(dest, h, route, gtf, g_final, yb, seq, final):
    n_tok, d = h.shape
    tm = min(TOKEN_TILE, seq)
    per_b = seq // tm
    steps = n_tok // tm
    tok = lambda w: pl.BlockSpec((tm, w), lambda i: (i, 0))
    cur = pl.BlockSpec((1, 1, tm), lambda i: (i, 0, 0), memory_space=pltpu.SMEM)
    nxt = pl.BlockSpec((1, 1, tm), lambda i: (jnp.minimum(i + 1, steps - 1), 0, 0), memory_space=pltpu.SMEM)
    blocks = _dest_blocks(dest, tm)
    return pl.pallas_call(
        functools.partial(_combine_kernel, tm=tm, final=final),
        out_shape=jax.ShapeDtypeStruct((n_tok, d), F32),
        grid=(steps,),
        in_specs=[cur] * TOP_K + [nxt] * TOP_K + [
            tok(d), tok(ROUTE_COLS),
            pl.BlockSpec((1, 1, d), lambda i: (i // per_b, 0, 0)),
            pl.BlockSpec((1, d), lambda i: (0, 0)),
            pl.BlockSpec(memory_space=pl.ANY)],
        out_specs=tok(d),
        scratch_shapes=[pltpu.VMEM((2, TOP_K, tm, d), F32), pltpu.SemaphoreType.DMA((2,))],
        compiler_params=_cparams("arbitrary"),
        name="combine",
    )(*blocks, *blocks, h, route, gtf, g_final, yb)


def _routing_plan(route_t, counts, n_experts, m_pad):
    eid = route_t[0:TOP_K].astype(jnp.int32)
    rank = route_t[4:4 + TOP_K].astype(jnp.int32)
    cnt = counts.astype(jnp.int32)
    padded = (cnt + EXPERT_ROWS - 1) // EXPERT_ROWS * EXPERT_ROWS
    pend = jnp.cumsum(padded)
    pstart = pend - padded
    dest = rank
    for e in range(n_experts):
        dest = dest + jnp.where(eid == e, pstart[e], 0)
    nblk = m_pad // EXPERT_ROWS
    blk_row = jnp.arange(nblk, dtype=jnp.int32) * EXPERT_ROWS
    blk_e = jnp.minimum(jnp.sum(pend[None, :] <= blk_row[:, None], axis=1), n_experts - 1).astype(jnp.int32)
    used = (pend[-1:] // EXPERT_ROWS).astype(jnp.int32)
    last_blk = jnp.where(padded > 0, pend // EXPERT_ROWS - 1, -1)
    tail_blk = used + jnp.arange(n_experts, dtype=jnp.int32)
    tail_blk = jnp.where(tail_blk < nblk, tail_blk, -1)
    zero_blocks = jnp.concatenate([last_blk, tail_blk]).astype(jnp.int32)
    return dest, blk_e, used, zero_blocks


def kernel(x, c, g_mix, g_ffn, w_ada, b_ada, w_in, w_sba_out, g_sgu, w_spatial, b_spatial, w_sgu_out, w_out,
           w_router_group, b_router_group, w_router_expert, b_router_expert, w_expert_gate, w_expert_up,
           w_expert_down, g_final):
    bsz, seq, d = x.shape
    depth = w_in.shape[0]
    sba = w_sba_out.shape[1]
    sgu = g_sgu.shape[1]
    heads = sba // HEAD_DIM
    groups, chunk = w_spatial.shape[1], w_spatial.shape[2]
    n_groups = w_router_group.shape[2]
    n_experts = w_router_expert.shape[2]
    per_group = n_experts // n_groups
    n_tok = bsz * seq
    m_pad = n_tok * TOP_K + n_experts * EXPERT_ROWS
    widths = (sba, sba, sba, 2 * sgu, 2 * d)
    assert seq % chunk == 0 and LANES % (sgu // groups) == 0 and n_groups + n_experts <= LANES

    h = x.reshape(n_tok, d)
    for l in range(depth):
        mod = _mod_call(c, w_ada[l], b_ada[l])
        sh_m, sc_m, gt_m, sh_f, sc_f, gt_f = [mod[:, i * d:(i + 1) * d].reshape(bsz, 1, d) for i in range(6)]

        q, k, v, uv, gates = _proj_call(h, sh_m, sc_m, g_mix[l].reshape(1, d), w_in[l].astype(BF16), seq, widths)
        ya = _attn_call(q.reshape(bsz, seq, sba), k.reshape(bsz, seq, sba), v.reshape(bsz, seq, sba), heads)

        bias_full = jnp.repeat(b_spatial[l].T, sgu // groups, axis=1)
        wr = jnp.concatenate([w_router_group[l], w_router_expert[l]], axis=1)
        wr = jnp.pad(wr, ((0, 0), (0, LANES - wr.shape[1])))
        br = jnp.concatenate([b_router_group[l], b_router_expert[l]])
        br = jnp.pad(br, (0, LANES - br.shape[0])).reshape(1, LANES)
        h1, xn, route, route_t, counts = _mix_call(
            ya.reshape(n_tok, sba), uv, gates, h, gt_m, sh_f, sc_f, g_sgu[l].reshape(1, sgu), w_spatial[l],
            bias_full, w_sba_out[l].astype(BF16), w_sgu_out[l].astype(BF16), w_out[l].astype(BF16),
            g_ffn[l].reshape(1, d), wr, br, seq, n_groups, per_group)

        dest, blk_e, used, zero_blocks = _routing_plan(route_t, counts[0, n_groups:n_groups + n_experts],
                                                       n_experts, m_pad)
        xs = _dispatch_call(zero_blocks, dest, xn, m_pad, seq)
        yb = _expert_call(blk_e, used, xs, w_expert_gate[l].astype(BF16), w_expert_up[l].astype(BF16),
                          w_expert_down[l].astype(BF16))
        h = _combine_call(dest, h1, route, gt_f, g_final.reshape(1, d), yb, seq, final=l == depth - 1)
    return h.reshape(bsz, seq, d)
```

```python
import functools

import jax
import jax.numpy as jnp
from jax import lax
from jax.experimental import pallas as pl
from jax.experimental.pallas import tpu as pltpu

F32 = jnp.float32
BF16 = jnp.bfloat16

HEAD_DIM = 64
TOP_K = 2
EPS = 1e-6
LOG2E = 1.4426950408889634
SIGN_BIT = 0x80000000
BF16_BITS = 0xFFFF0000
ATTN_EXIT_LOG2 = 160.0
LANES = 128
ATTN_BLOCK = 256
EXPERT_ROWS = 256
TOKEN_TILE = 256
PROJ_TILE = 512
ROUTE_COLS = 8
VMEM_LIMIT = 56 * 1024 * 1024


def _cparams(*sem):
    return pltpu.CompilerParams(dimension_semantics=sem, vmem_limit_bytes=VMEM_LIMIT)


def _split_bf16(a):
    hi = a.astype(BF16)
    lo = (a - hi.astype(F32)).astype(BF16)
    return hi, lo


def _dot(a, b):
    return jnp.dot(a, b, preferred_element_type=F32)


def _dot3(a, b):
    ah, al = _split_bf16(a)
    bh, bl = _split_bf16(b)
    return _dot(ah, bh) + _dot(ah, bl) + _dot(al, bh)


def _pack_halves(x):
    half = x.shape[1] // 2
    hi = lax.bitcast_convert_type(x[:, :half].astype(BF16).astype(F32), jnp.uint32)
    lo = lax.bitcast_convert_type(x[:, half:].astype(BF16).astype(F32), jnp.uint32)
    return hi | (lo >> 16)


def _unpack_halves(u):
    hi = lax.bitcast_convert_type(u & jnp.uint32(BF16_BITS), F32)
    lo = lax.bitcast_convert_type(u << 16, F32)
    return jnp.concatenate([hi, lo], axis=1)


def _rms(x, g):
    ms = jnp.mean(x * x, axis=-1, keepdims=True)
    return x * lax.rsqrt(ms + EPS) * g


def _mod_kernel(c_ref, w_ref, b_ref, o_ref):
    c = c_ref[...]
    ca = c * (1.0 / (1.0 + jnp.exp(-c)))
    o_ref[...] = _dot3(ca, w_ref[...]) + b_ref[...]


def _mod_call(c, w_ada, b_ada):
    bsz, d = c.shape
    n = w_ada.shape[1]
    tn = n // 4 if n % (4 * LANES) == 0 else n
    return pl.pallas_call(
        _mod_kernel,
        out_shape=jax.ShapeDtypeStruct((bsz, n), F32),
        grid=(n // tn,),
        in_specs=[pl.BlockSpec((bsz, d), lambda j: (0, 0)),
                  pl.BlockSpec((d, tn), lambda j: (0, j)),
                  pl.BlockSpec((1, tn), lambda j: (0, j))],
        out_specs=pl.BlockSpec((bsz, tn), lambda j: (0, j)),
        compiler_params=_cparams("arbitrary"),
        name="mod",
    )(c, w_ada, b_ada.reshape(1, n))


def _proj_kernel(x_ref, sh_ref, sc_ref, g_ref, w_ref, q_ref, k_ref, v_ref, uv_ref, gt_ref, *, widths):
    x = x_ref[...]
    n = _rms(x, g_ref[...]) * (1.0 + sc_ref[0]) + sh_ref[0]
    nb = n.astype(BF16)
    off = 0
    for ref, wd in zip((q_ref, k_ref, v_ref, uv_ref, gt_ref), widths):
        p = _dot(nb, w_ref[:, off:off + wd])
        if ref is q_ref:
            p = p * (HEAD_DIM ** -0.5 * LOG2E)
        ref[...] = p.astype(ref.dtype)
        off += wd


def _proj_call(x2, sh, sc, g, w_in_bf, seq, widths):
    n_tok, d = x2.shape
    tm = min(PROJ_TILE, seq)
    per_b = seq // tm
    cols = w_in_bf.shape[1]
    tok = lambda w: pl.BlockSpec((tm, w), lambda i: (i, 0))
    vec = pl.BlockSpec((1, 1, d), lambda i: (i // per_b, 0, 0))
    return pl.pallas_call(
        functools.partial(_proj_kernel, widths=widths),
        out_shape=[jax.ShapeDtypeStruct((n_tok, w), BF16) for w in widths],
        grid=(n_tok // tm,),
        in_specs=[tok(d), vec, vec,
                  pl.BlockSpec((1, d), lambda i: (0, 0)),
                  pl.BlockSpec((d, cols), lambda i: (0, 0))],
        out_specs=[tok(w) for w in widths],
        compiler_params=_cparams("arbitrary"),
        name="proj",
    )(x2, sh, sc, g, w_in_bf)


def _attn_kernel(q_ref, k_ref, v_ref, u_ref, o_ref, qm_ref, acc_ref, carry_ref, *, heads, blk):
    qi = pl.program_id(1)
    u_tri = u_ref[...]
    row = lax.broadcasted_iota(jnp.int32, (blk, blk), 0)
    col = lax.broadcasted_iota(jnp.int32, (blk, blk), 1)
    causal = col < row
    per_slab = LANES // HEAD_DIM
    lane = lax.broadcasted_iota(jnp.int32, (blk, LANES), 1)
    own = [(lane >= j * HEAD_DIM) & (lane < (j + 1) * HEAD_DIM) for j in range(per_slab)]
    for h in range(heads):
        slab = slice(h // per_slab * LANES, (h // per_slab + 1) * LANES)
        qm_ref[h] = jnp.where(own[h % per_slab], q_ref[0, :, slab], 0.0).astype(BF16)

    def key_tile(start, first):
        mask = causal if first else None
        slabs = [slice(p * LANES, (p + 1) * LANES) for p in range(heads // per_slab)]
        s_all = [lax.dot_general(qm_ref[h], k_ref[0, pl.ds(start, blk), slabs[h // per_slab]],
                                 (((1,), (1,)), ((), ())), preferred_element_type=F32) for h in range(heads)]
        cum_all = []
        for s in s_all:
            neg_abs = lax.bitcast_convert_type(lax.bitcast_convert_type(s, jnp.uint32) | jnp.uint32(SIGN_BIT), F32)
            sp = jnp.maximum(s, 0.0) + jnp.log(1.0 + jnp.exp2(neg_abs)) * LOG2E
            if mask is not None:
                sp = jnp.where(mask, sp, 0.0)
            hi = lax.bitcast_convert_type(lax.bitcast_convert_type(sp, jnp.uint32) & jnp.uint32(BF16_BITS), F32)
            cum_all.append(_dot(hi.astype(BF16), u_tri) + _dot((sp - hi).astype(BF16), u_tri))
        w_all = []
        cmin = None
        for h in range(heads):
            cum = cum_all[h]
            if first:
                w = jnp.exp2(s_all[h] - cum)
                carry = cum[:, 0:1]
            else:
                carry = carry_ref[h]
                w = jnp.exp2((s_all[h] - carry) - cum)
                carry = carry + cum[:, 0:1]
            w_all.append((w if mask is None else jnp.where(mask, w, 0.0)).astype(BF16))
            carry_ref[h] = carry
            cmin = carry if cmin is None else jnp.minimum(cmin, carry)
        for p, slab in enumerate(slabs):
            vp = v_ref[0, pl.ds(start, blk), slab]
            upd = None
            for j in range(per_slab):
                pv = _dot(w_all[p * per_slab + j], vp)
                upd = pv if upd is None else jnp.where(own[j], pv, upd)
            acc_ref[:, slab] = upd if first else acc_ref[:, slab] + upd
        return jnp.min(cmin)

    cmin = key_tile(pl.multiple_of(qi * blk, blk), True)

    def cond(state):
        i, cmin = state
        return (i < qi) & (cmin < ATTN_EXIT_LOG2)

    def body(state):
        i, _ = state
        return i + 1, key_tile(pl.multiple_of((qi - 1 - i) * blk, blk), False)

    lax.while_loop(cond, body, (jnp.int32(0), cmin))
    o_ref[0] = acc_ref[...].astype(o_ref.dtype)


def _attn_call(q, k, v, heads):
    bsz, seq, width = q.shape
    blk = min(ATTN_BLOCK, seq)
    r = lax.broadcasted_iota(jnp.int32, (blk, blk), 0)
    c = lax.broadcasted_iota(jnp.int32, (blk, blk), 1)
    u_tri = (r >= c).astype(BF16)
    full = pl.BlockSpec((1, seq, width), lambda b, i: (b, 0, 0))
    return pl.pallas_call(
        functools.partial(_attn_kernel, heads=heads, blk=blk),
        out_shape=jax.ShapeDtypeStruct((bsz, seq, width), BF16),
        grid=(bsz, seq // blk),
        in_specs=[pl.BlockSpec((1, blk, width), lambda b, i: (b, i, 0)), full, full,
                  pl.BlockSpec((blk, blk), lambda b, i: (0, 0))],
        out_specs=pl.BlockSpec((1, blk, width), lambda b, i: (b, i, 0)),
        scratch_shapes=[pltpu.VMEM((heads, blk, LANES), BF16), pltpu.VMEM((blk, width), F32),
                        pltpu.VMEM((heads, blk, 1), F32)],
        compiler_params=_cparams("arbitrary", "arbitrary"),
        name="attn",
    )(q, k, v, u_tri)


def _gelu_tanh(x):
    return 0.5 * x * (1.0 + jnp.tanh(0.7978845608028654 * (x + 0.044715 * (x * x * x))))


def _sigmoid(x):
    return 1.0 / (1.0 + jnp.exp(-x))


def _mix_kernel(ya_ref, uv_ref, gt_ref, x_ref, gtm_ref, shf_ref, scf_ref, gsgu_ref, ws_ref, bs_ref,
                wa_ref, wb_ref, wo_ref, gffn_ref, wr_ref, br_ref, ltri_ref,
                h_ref, xn_ref, route_ref, route_t_ref, cnt_ref, yb_scr, cnt_scr,
                *, chunk, groups, n_groups, per_group):
    step = pl.program_id(0)
    tm = x_ref.shape[0]
    sgu = gsgu_ref.shape[1]
    gdim = sgu // groups

    @pl.when(step == 0)
    def _():
        cnt_scr[...] = jnp.zeros_like(cnt_scr)

    ya_proj = _dot(ya_ref[...], wa_ref[...])

    act = _gelu_tanh(uv_ref[...].astype(F32))
    u = act[:, :sgu]
    v = act[:, sgu:]
    mu = jnp.mean(v, axis=-1, keepdims=True)
    vc = v - mu
    var = jnp.mean(vc * vc, axis=-1, keepdims=True)
    vn = vc * lax.rsqrt(var + EPS) * gsgu_ref[...]
    r = lax.broadcasted_iota(jnp.int32, (chunk, chunk), 0)
    c = lax.broadcasted_iota(jnp.int32, (chunk, chunk), 1)
    wcs = [jnp.where(r >= c, ws_ref[g], 0.0).astype(BF16) for g in range(groups)]
    pair = LANES // gdim
    lane = lax.broadcasted_iota(jnp.int32, (chunk, LANES), 1)
    for ci in range(tm // chunk):
        rows = slice(ci * chunk, (ci + 1) * chunk)
        for p in range(groups // pair):
            lanes = slice(p * LANES, (p + 1) * LANES)
            slab = vn[rows, lanes]
            mix = bs_ref[:, lanes]
            for j in range(pair):
                sel = (lane >= j * gdim) & (lane < (j + 1) * gdim)
                mix = mix + _dot(wcs[p * pair + j], jnp.where(sel, slab, 0.0).astype(BF16))
            yb_scr[rows, lanes] = (u[rows, lanes] * mix).astype(BF16)

    d = x_ref.shape[1]
    gates = _sigmoid(gt_ref[...].astype(F32))
    merged = gates[:, :d] * ya_proj + gates[:, d:] * _dot(yb_scr[...], wb_ref[...])
    h = x_ref[...] + gtm_ref[0] * _dot(merged.astype(BF16), wo_ref[...])
    h_ref[...] = h

    xn = _rms(h, gffn_ref[...]) * (1.0 + scf_ref[0]) + shf_ref[0]
    xn_ref[...] = _pack_halves(xn)
    logits = _dot3(xn, wr_ref[...]) + br_ref[...]
    ln = lax.broadcasted_iota(jnp.int32, (tm, LANES), 1)
    neg = -jnp.inf
    gl = jnp.where(ln < n_groups, logits, neg)
    gmax = jnp.max(gl, axis=1, keepdims=True)
    g_sel = jnp.min(jnp.where(gl == gmax, ln, LANES), axis=1, keepdims=True)
    g_w = 1.0 / jnp.sum(jnp.where(ln < n_groups, jnp.exp(logits - gmax), 0.0), axis=1, keepdims=True)
    lo_lane = n_groups + per_group * g_sel
    el = jnp.where((ln >= lo_lane) & (ln < lo_lane + per_group), logits, neg)
    m1 = jnp.max(el, axis=1, keepdims=True)
    i1 = jnp.min(jnp.where(el == m1, ln, LANES), axis=1, keepdims=True)
    el2 = jnp.where(ln == i1, neg, el)
    m2 = jnp.max(el2, axis=1, keepdims=True)
    i2 = jnp.min(jnp.where(el2 == m2, ln, LANES), axis=1, keepdims=True)
    t = jnp.exp(m2 - m1)
    w1 = g_w / (1.0 + t)
    w2 = g_w * t / (1.0 + t)

    oh1 = ln == i1
    oh2 = ln == i2
    both = jnp.where(oh1 | oh2, 1.0, 0.0)
    before = _dot(ltri_ref[...], both.astype(BF16)) + cnt_scr[...]
    rank1 = jnp.sum(jnp.where(oh1, before, 0.0), axis=1, keepdims=True)
    rank2 = jnp.sum(jnp.where(oh2, before, 0.0), axis=1, keepdims=True)
    cnt_scr[...] = cnt_scr[...] + jnp.sum(both, axis=0, keepdims=True)
    cnt_ref[...] = cnt_scr[...]

    e1 = (i1 - n_groups).astype(F32)
    e2 = (i2 - n_groups).astype(F32)
    out = jnp.zeros((tm, LANES), F32)
    for idx, val in enumerate((e1, e2, w1, w2, rank1, rank2)):
        out = jnp.where(ln == idx, val, out)
    route_ref[...] = out[:, :ROUTE_COLS]
    route_t_ref[...] = out.T[:ROUTE_COLS, :]


def _mix_call(ya, uv, gates, x2, gtm, shf, scf, g_sgu, w_spatial, bias_full, wa, wb, wo, g_ffn, wr, br,
              seq, n_groups, per_group):
    n_tok, d = x2.shape
    tm = min(TOKEN_TILE, seq)
    per_b = seq // tm
    groups, chunk, _ = w_spatial.shape
    sgu = g_sgu.shape[1]
    r = lax.broadcasted_iota(jnp.int32, (tm, tm), 0)
    c = lax.broadcasted_iota(jnp.int32, (tm, tm), 1)
    ltri = (r > c).astype(BF16)
    tok = lambda w: pl.BlockSpec((tm, w), lambda i: (i, 0))
    vec = pl.BlockSpec((1, 1, d), lambda i: (i // per_b, 0, 0))
    const = lambda a: pl.BlockSpec(a.shape, lambda i: (0,) * a.ndim)
    ins = [ya, uv, gates, x2, gtm, shf, scf, g_sgu, w_spatial, bias_full, wa, wb, wo, g_ffn, wr, br, ltri]
    in_specs = [tok(ya.shape[1]), tok(uv.shape[1]), tok(gates.shape[1]), tok(d), vec, vec, vec]
    in_specs += [const(a) for a in ins[7:]]
    return pl.pallas_call(
        functools.partial(_mix_kernel, chunk=chunk, groups=groups, n_groups=n_groups, per_group=per_group),
        out_shape=[jax.ShapeDtypeStruct((n_tok, d), F32), jax.ShapeDtypeStruct((n_tok, d // 2), jnp.uint32),
                   jax.ShapeDtypeStruct((n_tok, ROUTE_COLS), F32), jax.ShapeDtypeStruct((ROUTE_COLS, n_tok), F32),
                   jax.ShapeDtypeStruct((1, LANES), F32)],
        grid=(n_tok // tm,),
        in_specs=in_specs,
        out_specs=[tok(d), tok(d // 2), tok(ROUTE_COLS), pl.BlockSpec((ROUTE_COLS, tm), lambda i: (0, i)),
                   pl.BlockSpec((1, LANES), lambda i: (0, 0))],
        scratch_shapes=[pltpu.VMEM((tm, sgu), BF16), pltpu.VMEM((1, LANES), F32)],
        compiler_params=_cparams("arbitrary"),
        name="mix",
    )(*ins)


def _row_copy(src, s, dst, d, sem):
    return pltpu.make_async_copy(src.at[pl.ds(s, 1)], dst.at[pl.ds(d, 1)], sem)


def _dispatch_kernel(zblk_ref, *refs, tm, rows):
    dest_refs, (xn_ref, xs_ref, zero_scr, sem, zsem) = refs[:TOP_K], refs[TOP_K:]

    @pl.when(pl.program_id(0) == 0)
    def _():
        zero_scr[...] = jnp.zeros_like(zero_scr)

        def zero_copy(j):
            start = pl.multiple_of(zblk_ref[j] * rows, rows)
            return pltpu.make_async_copy(zero_scr, xs_ref.at[pl.ds(start, rows)], zsem)

        for j in range(zblk_ref.shape[0]):
            pl.when(zblk_ref[j] >= 0)(lambda j=j: zero_copy(j).start())
        for j in range(zblk_ref.shape[0]):
            pl.when(zblk_ref[j] >= 0)(lambda j=j: zero_copy(j).wait())

    for j in range(tm):
        for k, dref in enumerate(dest_refs):
            _row_copy(xn_ref, j, xs_ref, dref[0, 0, j], sem).start(priority=k % 2)
    for _ in range(TOP_K):
        pltpu.make_async_copy(xn_ref, xs_ref.at[pl.ds(0, tm)], sem).wait()


def _dest_blocks(dest, tm):
    return [dest[k].reshape(-1, 1, tm) for k in range(TOP_K)]


def _dispatch_call(zero_blocks, dest, xn, m_pad, seq):
    n_tok, dw = xn.shape
    tm = min(TOKEN_TILE, seq)
    smem = pl.BlockSpec((1, 1, tm), lambda i, zb: (i, 0, 0), memory_space=pltpu.SMEM)
    return pl.pallas_call(
        functools.partial(_dispatch_kernel, tm=tm, rows=EXPERT_ROWS),
        out_shape=jax.ShapeDtypeStruct((m_pad, dw), xn.dtype),
        grid_spec=pltpu.PrefetchScalarGridSpec(
            num_scalar_prefetch=1,
            grid=(n_tok // tm,),
            in_specs=[smem] * TOP_K + [pl.BlockSpec((tm, dw), lambda i, zb: (i, 0))],
            out_specs=pl.BlockSpec(memory_space=pl.ANY),
            scratch_shapes=[pltpu.VMEM((EXPERT_ROWS, dw), xn.dtype), pltpu.SemaphoreType.DMA(()),
                            pltpu.SemaphoreType.DMA(())]),
        compiler_params=_cparams("arbitrary"),
        name="dispatch",
    )(zero_blocks, *_dest_blocks(dest, tm), xn)


def _expert_kernel(blk_e_ref, used_ref, xs_ref, wg_ref, wu_ref, wd_ref, y_ref):
    del blk_e_ref

    @pl.when(pl.program_id(0) < used_ref[0])
    def _():
        xb = _unpack_halves(xs_ref[...]).astype(BF16)
        g = _dot(xb, wg_ref[0])
        hid = g * _sigmoid(g) * _dot(xb, wu_ref[0])
        y_ref[...] = _pack_halves(_dot(hid.astype(BF16), wd_ref[0]))

    @pl.when(pl.program_id(0) >= used_ref[0])
    def _():
        y_ref[...] = jnp.zeros_like(y_ref)


def _expert_call(blk_e, used, xs, wg, wu, wd):
    m_pad, dw = xs.shape
    _, d, f = wg.shape
    rows = EXPERT_ROWS
    return pl.pallas_call(
        _expert_kernel,
        out_shape=jax.ShapeDtypeStruct((m_pad, dw), xs.dtype),
        grid_spec=pltpu.PrefetchScalarGridSpec(
            num_scalar_prefetch=2,
            grid=(m_pad // rows,),
            in_specs=[pl.BlockSpec((rows, dw), lambda i, be, us: (i, 0)),
                      pl.BlockSpec((1, d, f), lambda i, be, us: (be[i], 0, 0)),
                      pl.BlockSpec((1, d, f), lambda i, be, us: (be[i], 0, 0)),
                      pl.BlockSpec((1, f, d), lambda i, be, us: (be[i], 0, 0))],
            out_specs=pl.BlockSpec((rows, dw), lambda i, be, us: (i, 0))),
        compiler_params=_cparams("arbitrary"),
        name="experts",
    )(blk_e, used, xs, wg, wu, wd)


def _combine_kernel(*refs, tm, final):
    cur_refs, nxt_refs = refs[:TOP_K], refs[TOP_K:2 * TOP_K]
    h_ref, route_ref, gtf_ref, gfin_ref, y_ref, o_ref, buf, sems = refs[2 * TOP_K:]
    step = pl.program_id(0)
    slot = step % 2

    def gather(dest_refs, s):
        for j in range(tm):
            for k, dref in enumerate(dest_refs):
                pltpu.make_async_copy(y_ref.at[pl.ds(dref[0, 0, j], 1)], buf.at[s, k, pl.ds(j, 1)],
                                      sems.at[s]).start(priority=k % 2)

    pl.when(step == 0)(lambda: gather(cur_refs, 0))
    pl.when(step + 1 < pl.num_programs(0))(lambda: gather(nxt_refs, 1 - slot))
    for k in range(TOP_K):
        pltpu.make_async_copy(y_ref.at[pl.ds(0, tm)], buf.at[slot, k], sems.at[slot]).wait()
    route = route_ref[...]
    y = route[:, 2:3] * _unpack_halves(buf[slot, 0]) + route[:, 3:4] * _unpack_halves(buf[slot, 1])
    h = h_ref[...] + gtf_ref[0] * y
    o_ref[...] = _rms(h, gfin_ref[...]) if final else h


def _combine_call(dest, h, route, gtf, g_final, yb, seq, final):
    n_tok, d = h.shape
    tm = min(TOKEN_TILE, seq)
    per_b = seq // tm
    steps = n_tok // tm
    tok = lambda w: pl.BlockSpec((tm, w), lambda i: (i, 0))
    cur = pl.BlockSpec((1, 1, tm), lambda i: (i, 0, 0), memory_space=pltpu.SMEM)
    nxt = pl.BlockSpec((1, 1, tm), lambda i: (jnp.minimum(i + 1, steps - 1), 0, 0), memory_space=pltpu.SMEM)
    blocks = _dest_blocks(dest, tm)
    return pl.pallas_call(
        functools.partial(_combine_kernel, tm=tm, final=final),
        out_shape=jax.ShapeDtypeStruct((n_tok, d), F32),
        grid=(steps,),
        in_specs=[cur] * TOP_K + [nxt] * TOP_K + [
            tok(d), tok(ROUTE_COLS),
            pl.BlockSpec((1, 1, d), lambda i: (i // per_b, 0, 0)),
            pl.BlockSpec((1, d), lambda i: (0, 0)),
            pl.BlockSpec(memory_space=pl.ANY)],
        out_specs=tok(d),
        scratch_shapes=[pltpu.VMEM((2, TOP_K, tm, yb.shape[1]), yb.dtype), pltpu.SemaphoreType.DMA((2,))],
        compiler_params=_cparams("arbitrary"),
        name="combine",
    )(*blocks, *blocks, h, route, gtf, g_final, yb)


def _routing_plan(route_t, counts, n_experts, m_pad):
    eid = route_t[0:TOP_K].astype(jnp.int32)
    rank = route_t[4:4 + TOP_K].astype(jnp.int32)
    cnt = counts.astype(jnp.int32)
    padded = (cnt + EXPERT_ROWS - 1) // EXPERT_ROWS * EXPERT_ROWS
    pend = jnp.cumsum(padded)
    pstart = pend - padded
    dest = rank
    for e in range(n_experts):
        dest = dest + jnp.where(eid == e, pstart[e], 0)
    nblk = m_pad // EXPERT_ROWS
    blk_row = jnp.arange(nblk, dtype=jnp.int32) * EXPERT_ROWS
    blk_e = jnp.minimum(jnp.sum(pend[None, :] <= blk_row[:, None], axis=1), n_experts - 1).astype(jnp.int32)
    used = (pend[-1:] // EXPERT_ROWS).astype(jnp.int32)
    last_blk = jnp.where(padded > 0, pend // EXPERT_ROWS - 1, -1)
    tail_blk = used + jnp.arange(n_experts, dtype=jnp.int32)
    tail_blk = jnp.where(tail_blk < nblk, tail_blk, -1)
    zero_blocks = jnp.concatenate([last_blk, tail_blk]).astype(jnp.int32)
    return dest, blk_e, used, zero_blocks


def kernel(x, c, g_mix, g_ffn, w_ada, b_ada, w_in, w_sba_out, g_sgu, w_spatial, b_spatial, w_sgu_out, w_out,
           w_router_group, b_router_group, w_router_expert, b_router_expert, w_expert_gate, w_expert_up,
           w_expert_down, g_final):
    bsz, seq, d = x.shape
    depth = w_in.shape[0]
    sba = w_sba_out.shape[1]
    sgu = g_sgu.shape[1]
    heads = sba // HEAD_DIM
    groups, chunk = w_spatial.shape[1], w_spatial.shape[2]
    n_groups = w_router_group.shape[2]
    n_experts = w_router_expert.shape[2]
    per_group = n_experts // n_groups
    n_tok = bsz * seq
    m_pad = n_tok * TOP_K + n_experts * EXPERT_ROWS
    widths = (sba, sba, sba, 2 * sgu, 2 * d)
    assert seq % chunk == 0 and LANES % (sgu // groups) == 0 and n_groups + n_experts <= LANES
    assert d % (2 * LANES) == 0

    h = x.reshape(n_tok, d)
    for l in range(depth):
        mod = _mod_call(c, w_ada[l], b_ada[l])
        sh_m, sc_m, gt_m, sh_f, sc_f, gt_f = [mod[:, i * d:(i + 1) * d].reshape(bsz, 1, d) for i in range(6)]

        q, k, v, uv, gates = _proj_call(h, sh_m, sc_m, g_mix[l].reshape(1, d), w_in[l].astype(BF16), seq, widths)
        ya = _attn_call(q.reshape(bsz, seq, sba), k.reshape(bsz, seq, sba), v.reshape(bsz, seq, sba), heads)

        bias_full = jnp.repeat(b_spatial[l].T, sgu // groups, axis=1)
        wr = jnp.concatenate([w_router_group[l], w_router_expert[l]], axis=1)
        wr = jnp.pad(wr, ((0, 0), (0, LANES - wr.shape[1])))
        br = jnp.concatenate([b_router_group[l], b_router_expert[l]])
        br = jnp.pad(br, (0, LANES - br.shape[0])).reshape(1, LANES)
        h1, xn, route, route_t, counts = _mix_call(
            ya.reshape(n_tok, sba), uv, gates, h, gt_m, sh_f, sc_f, g_sgu[l].reshape(1, sgu), w_spatial[l],
            bias_full, w_sba_out[l].astype(BF16), w_sgu_out[l].astype(BF16), w_out[l].astype(BF16),
            g_ffn[l].reshape(1, d), wr, br, seq, n_groups, per_group)

        dest, blk_e, used, zero_blocks = _routing_plan(route_t, counts[0, n_groups:n_groups + n_experts],
                                                       n_experts, m_pad)
        xs = _dispatch_call(zero_blocks, dest, xn, m_pad, seq)
        yb = _expert_call(blk_e, used, xs, w_expert_gate[l].astype(BF16), w_expert_up[l].astype(BF16),
                          w_expert_down[l].astype(BF16))
        h = _combine_call(dest, h1, route, gt_f, g_final.reshape(1, d), yb, seq, final=l == depth - 1)
    return h.reshape(bsz, seq, d)
```

```python
import functools

import jax
import jax.numpy as jnp
from jax import lax
from jax.experimental import pallas as pl
from jax.experimental.pallas import tpu as pltpu

F32 = jnp.float32
BF16 = jnp.bfloat16

HEAD_DIM = 64
TOP_K = 2
EPS = 1e-6
LOG2E = 1.4426950408889634
SIGN_BIT = 0x80000000
BF16_BITS = 0xFFFF0000
ATTN_EXIT_LOG2 = 160.0
LANES = 128
SUBLANES = 8
ATTN_BLOCK = 256
EXPERT_ROWS = 256
TOKEN_TILE = 256
PROJ_TILE = 512
ROUTE_COLS = 8
VMEM_LIMIT = 56 * 1024 * 1024


def _cparams(*sem):
    return pltpu.CompilerParams(dimension_semantics=sem, vmem_limit_bytes=VMEM_LIMIT)


def _split_bf16(a):
    hi = a.astype(BF16)
    lo = (a - hi.astype(F32)).astype(BF16)
    return hi, lo


def _dot(a, b):
    return jnp.dot(a, b, preferred_element_type=F32)


def _dot3(a, b):
    ah, al = _split_bf16(a)
    bh, bl = _split_bf16(b)
    return _dot(ah, bh) + _dot(ah, bl) + _dot(al, bh)


def _pack_halves(x):
    half = x.shape[1] // 2
    hi = lax.bitcast_convert_type(x[:, :half].astype(BF16).astype(F32), jnp.uint32)
    lo = lax.bitcast_convert_type(x[:, half:].astype(BF16).astype(F32), jnp.uint32)
    return hi | (lo >> 16)


def _unpack_halves(u):
    hi = lax.bitcast_convert_type(u & jnp.uint32(BF16_BITS), F32)
    lo = lax.bitcast_convert_type(u << 16, F32)
    return jnp.concatenate([hi, lo], axis=1)


def _rms(x, g):
    ms = jnp.mean(x * x, axis=-1, keepdims=True)
    return x * lax.rsqrt(ms + EPS) * g


def _mod_kernel(c_ref, w_ref, b_ref, o_ref):
    c = c_ref[...]
    ca = c * (1.0 / (1.0 + jnp.exp(-c)))
    o_ref[...] = _dot3(ca, w_ref[...]) + b_ref[...]


def _mod_call(c, w_ada, b_ada):
    bsz, d = c.shape
    n = w_ada.shape[1]
    tn = n // 4 if n % (4 * LANES) == 0 else n
    return pl.pallas_call(
        _mod_kernel,
        out_shape=jax.ShapeDtypeStruct((bsz, n), F32),
        grid=(n // tn,),
        in_specs=[pl.BlockSpec((bsz, d), lambda j: (0, 0)),
                  pl.BlockSpec((d, tn), lambda j: (0, j)),
                  pl.BlockSpec((1, tn), lambda j: (0, j))],
        out_specs=pl.BlockSpec((bsz, tn), lambda j: (0, j)),
        compiler_params=_cparams("arbitrary"),
        name="mod",
    )(c, w_ada, b_ada.reshape(1, n))


def _proj_kernel(x_ref, sh_ref, sc_ref, g_ref, w_ref, q_ref, k_ref, v_ref, uv_ref, gt_ref, *, widths):
    x = x_ref[...]
    n = _rms(x, g_ref[...]) * (1.0 + sc_ref[0]) + sh_ref[0]
    nb = n.astype(BF16)
    off = 0
    for ref, wd in zip((q_ref, k_ref, v_ref, uv_ref, gt_ref), widths):
        p = _dot(nb, w_ref[:, off:off + wd])
        if ref is q_ref:
            p = p * (HEAD_DIM ** -0.5 * LOG2E)
        ref[...] = p.astype(ref.dtype)
        off += wd


def _proj_call(x2, sh, sc, g, w_in_bf, seq, widths):
    n_tok, d = x2.shape
    tm = min(PROJ_TILE, seq)
    per_b = seq // tm
    cols = w_in_bf.shape[1]
    tok = lambda w: pl.BlockSpec((tm, w), lambda i: (i, 0))
    vec = pl.BlockSpec((1, 1, d), lambda i: (i // per_b, 0, 0))
    return pl.pallas_call(
        functools.partial(_proj_kernel, widths=widths),
        out_shape=[jax.ShapeDtypeStruct((n_tok, w), BF16) for w in widths],
        grid=(n_tok // tm,),
        in_specs=[tok(d), vec, vec,
                  pl.BlockSpec((1, d), lambda i: (0, 0)),
                  pl.BlockSpec((d, cols), lambda i: (0, 0))],
        out_specs=[tok(w) for w in widths],
        compiler_params=_cparams("arbitrary"),
        name="proj",
    )(x2, sh, sc, g, w_in_bf)


def _attn_kernel(q_ref, k_ref, v_ref, u_ref, o_ref, qm_ref, acc_ref, carry_ref, *, heads, blk):
    qi = pl.program_id(1)
    u_tri = u_ref[...]
    row = lax.broadcasted_iota(jnp.int32, (blk, blk), 0)
    col = lax.broadcasted_iota(jnp.int32, (blk, blk), 1)
    causal = col < row
    per_slab = LANES // HEAD_DIM
    lane = lax.broadcasted_iota(jnp.int32, (blk, LANES), 1)
    own = [(lane >= j * HEAD_DIM) & (lane < (j + 1) * HEAD_DIM) for j in range(per_slab)]
    for h in range(heads):
        slab = slice(h // per_slab * LANES, (h // per_slab + 1) * LANES)
        qm_ref[h] = jnp.where(own[h % per_slab], q_ref[0, :, slab], 0.0).astype(BF16)

    def key_tile(start, first):
        mask = causal if first else None
        slabs = [slice(p * LANES, (p + 1) * LANES) for p in range(heads // per_slab)]
        s_all = [lax.dot_general(qm_ref[h], k_ref[0, pl.ds(start, blk), slabs[h // per_slab]],
                                 (((1,), (1,)), ((), ())), preferred_element_type=F32) for h in range(heads)]
        cum_all = []
        for s in s_all:
            neg_abs = lax.bitcast_convert_type(lax.bitcast_convert_type(s, jnp.uint32) | jnp.uint32(SIGN_BIT), F32)
            sp = jnp.maximum(s, 0.0) + jnp.log(1.0 + jnp.exp2(neg_abs)) * LOG2E
            if mask is not None:
                sp = jnp.where(mask, sp, 0.0)
            hi = lax.bitcast_convert_type(lax.bitcast_convert_type(sp, jnp.uint32) & jnp.uint32(BF16_BITS), F32)
            cum_all.append(_dot(hi.astype(BF16), u_tri) + _dot((sp - hi).astype(BF16), u_tri))
        w_all = []
        cmin = None
        for h in range(heads):
            cum = cum_all[h]
            if first:
                w = jnp.exp2(s_all[h] - cum)
                carry = cum[:, 0:1]
            else:
                carry = carry_ref[h]
                w = jnp.exp2((s_all[h] - carry) - cum)
                carry = carry + cum[:, 0:1]
            w_all.append((w if mask is None else jnp.where(mask, w, 0.0)).astype(BF16))
            carry_ref[h] = carry
            cmin = carry if cmin is None else jnp.minimum(cmin, carry)
        for p, slab in enumerate(slabs):
            vp = v_ref[0, pl.ds(start, blk), slab]
            upd = None
            for j in range(per_slab):
                pv = _dot(w_all[p * per_slab + j], vp)
                upd = pv if upd is None else jnp.where(own[j], pv, upd)
            acc_ref[:, slab] = upd if first else acc_ref[:, slab] + upd
        return jnp.min(cmin)

    cmin = key_tile(pl.multiple_of(qi * blk, blk), True)

    def cond(state):
        i, cmin = state
        return (i < qi) & (cmin < ATTN_EXIT_LOG2)

    def body(state):
        i, _ = state
        return i + 1, key_tile(pl.multiple_of((qi - 1 - i) * blk, blk), False)

    lax.while_loop(cond, body, (jnp.int32(0), cmin))
    o_ref[0] = acc_ref[...].astype(o_ref.dtype)


def _attn_call(q, k, v, heads):
    bsz, seq, width = q.shape
    blk = min(ATTN_BLOCK, seq)
    r = lax.broadcasted_iota(jnp.int32, (blk, blk), 0)
    c = lax.broadcasted_iota(jnp.int32, (blk, blk), 1)
    u_tri = (r >= c).astype(BF16)
    full = pl.BlockSpec((1, seq, width), lambda b, i: (b, 0, 0))
    return pl.pallas_call(
        functools.partial(_attn_kernel, heads=heads, blk=blk),
        out_shape=jax.ShapeDtypeStruct((bsz, seq, width), BF16),
        grid=(bsz, seq // blk),
        in_specs=[pl.BlockSpec((1, blk, width), lambda b, i: (b, i, 0)), full, full,
                  pl.BlockSpec((blk, blk), lambda b, i: (0, 0))],
        out_specs=pl.BlockSpec((1, blk, width), lambda b, i: (b, i, 0)),
        scratch_shapes=[pltpu.VMEM((heads, blk, LANES), BF16), pltpu.VMEM((blk, width), F32),
                        pltpu.VMEM((heads, blk, 1), F32)],
        compiler_params=_cparams("arbitrary", "arbitrary"),
        name="attn",
    )(q, k, v, u_tri)


def _gelu_tanh(x):
    c = 0.7978845608028654
    hx = 0.5 * x
    return hx + hx * jnp.tanh(x * (c + (0.044715 * c) * (x * x)))


def _sigmoid(x):
    return 1.0 / (1.0 + jnp.exp(-x))


def _mix_kernel(ya_ref, uv_ref, gt_ref, x_ref, gtm_ref, shf_ref, scf_ref, gsgu_ref, ws_ref, bs_ref,
                wa_ref, wb_ref, wo_ref, gffn_ref, wr_ref, br_ref, utri_ref,
                h_ref, xn_ref, route_ref, route_t_ref, cnt_ref, yb_scr, cnt_scr,
                *, chunk, groups, n_groups, per_group):
    step = pl.program_id(0)
    tm = x_ref.shape[0]
    sgu = gsgu_ref.shape[1]
    gdim = sgu // groups

    @pl.when(step == 0)
    def _():
        cnt_scr[...] = jnp.zeros_like(cnt_scr)

    ya_proj = _dot(ya_ref[...], wa_ref[...])

    act = _gelu_tanh(uv_ref[...].astype(F32))
    u = act[:, :sgu]
    v = act[:, sgu:]
    mu = jnp.mean(v, axis=-1, keepdims=True)
    vc = v - mu
    var = jnp.mean(vc * vc, axis=-1, keepdims=True)
    vn = vc * lax.rsqrt(var + EPS) * gsgu_ref[...]
    r = lax.broadcasted_iota(jnp.int32, (chunk, chunk), 0)
    c = lax.broadcasted_iota(jnp.int32, (chunk, chunk), 1)
    wcs = [jnp.where(r >= c, ws_ref[g], 0.0).astype(BF16) for g in range(groups)]
    pair = LANES // gdim
    lane = lax.broadcasted_iota(jnp.int32, (chunk, LANES), 1)
    for ci in range(tm // chunk):
        rows = slice(ci * chunk, (ci + 1) * chunk)
        for p in range(groups // pair):
            lanes = slice(p * LANES, (p + 1) * LANES)
            slab = vn[rows, lanes]
            mix = bs_ref[:, lanes]
            for j in range(pair):
                sel = (lane >= j * gdim) & (lane < (j + 1) * gdim)
                mix = mix + _dot(wcs[p * pair + j], jnp.where(sel, slab, 0.0).astype(BF16))
            yb_scr[rows, lanes] = (u[rows, lanes] * mix).astype(BF16)

    d = x_ref.shape[1]
    gates2 = 1.0 + jnp.tanh(0.5 * gt_ref[...].astype(F32))
    merged2 = gates2[:, :d] * ya_proj + gates2[:, d:] * _dot(yb_scr[...], wb_ref[...])
    h = x_ref[...] + (0.5 * gtm_ref[0]) * _dot(merged2.astype(BF16), wo_ref[...])
    h_ref[...] = h

    xn = _rms(h, gffn_ref[...]) * (1.0 + scf_ref[0]) + shf_ref[0]
    xn_ref[...] = _pack_halves(xn)
    logits = _dot3(xn, wr_ref[...]) + br_ref[...]

    lt = logits.T
    sub = lax.broadcasted_iota(jnp.int32, (SUBLANES, tm), 0)
    neg = -jnp.inf
    gl = jnp.where(sub < n_groups, lt[0:SUBLANES], neg)
    gmax = jnp.max(gl, axis=0, keepdims=True)
    g_sel = jnp.min(jnp.where(gl == gmax, sub, SUBLANES), axis=0, keepdims=True)
    g_w = 1.0 / jnp.sum(jnp.exp(gl - gmax), axis=0, keepdims=True)
    el = lt[SUBLANES:2 * SUBLANES]
    for g in range(1, n_groups):
        el = jnp.where(g_sel == g, lt[(g + 1) * SUBLANES:(g + 2) * SUBLANES], el)
    m1 = jnp.max(el, axis=0, keepdims=True)
    j1 = jnp.min(jnp.where(el == m1, sub, SUBLANES), axis=0, keepdims=True)
    el2 = jnp.where(sub == j1, neg, el)
    m2 = jnp.max(el2, axis=0, keepdims=True)
    j2 = jnp.min(jnp.where(el2 == m2, sub, SUBLANES), axis=0, keepdims=True)
    t = jnp.exp(m2 - m1)
    w1 = g_w / (1.0 + t)
    w2 = g_w * t / (1.0 + t)
    e1 = g_sel * per_group + j1
    e2 = g_sel * per_group + j2

    row = lax.broadcasted_iota(jnp.int32, (LANES, tm), 0)
    oh1 = row == e1 + SUBLANES
    oh2 = row == e2 + SUBLANES
    both = jnp.where(oh1 | oh2, 1.0, 0.0)
    before = _dot(both.astype(BF16), utri_ref[...]) + cnt_scr[...]
    rank1 = jnp.sum(jnp.where(oh1, before, 0.0), axis=0, keepdims=True)
    rank2 = jnp.sum(jnp.where(oh2, before, 0.0), axis=0, keepdims=True)
    cnt_scr[...] = cnt_scr[...] + jnp.sum(both, axis=1, keepdims=True)
    cnt_ref[...] = cnt_scr[...]

    rec = jnp.zeros((LANES, tm), F32)
    for idx, val in enumerate((e1.astype(F32), e2.astype(F32), w1, w2, rank1, rank2)):
        rec = jnp.where(row == idx, val, rec)
    route_t_ref[...] = rec[:ROUTE_COLS]
    route_ref[...] = rec.T[:, :ROUTE_COLS]


def _mix_call(ya, uv, gates, x2, gtm, shf, scf, g_sgu, w_spatial, bias_full, wa, wb, wo, g_ffn, wr, br,
              seq, n_groups, per_group):
    n_tok, d = x2.shape
    tm = min(TOKEN_TILE, seq)
    per_b = seq // tm
    groups, chunk, _ = w_spatial.shape
    sgu = g_sgu.shape[1]
    r = lax.broadcasted_iota(jnp.int32, (tm, tm), 0)
    c = lax.broadcasted_iota(jnp.int32, (tm, tm), 1)
    utri = (r < c).astype(BF16)
    tok = lambda w: pl.BlockSpec((tm, w), lambda i: (i, 0))
    vec = pl.BlockSpec((1, 1, d), lambda i: (i // per_b, 0, 0))
    const = lambda a: pl.BlockSpec(a.shape, lambda i: (0,) * a.ndim)
    ins = [ya, uv, gates, x2, gtm, shf, scf, g_sgu, w_spatial, bias_full, wa, wb, wo, g_ffn, wr, br, utri]
    in_specs = [tok(ya.shape[1]), tok(uv.shape[1]), tok(gates.shape[1]), tok(d), vec, vec, vec]
    in_specs += [const(a) for a in ins[7:]]
    return pl.pallas_call(
        functools.partial(_mix_kernel, chunk=chunk, groups=groups, n_groups=n_groups, per_group=per_group),
        out_shape=[jax.ShapeDtypeStruct((n_tok, d), F32), jax.ShapeDtypeStruct((n_tok, d // 2), jnp.uint32),
                   jax.ShapeDtypeStruct((n_tok, ROUTE_COLS), F32), jax.ShapeDtypeStruct((ROUTE_COLS, n_tok), F32),
                   jax.ShapeDtypeStruct((LANES, 1), F32)],
        grid=(n_tok // tm,),
        in_specs=in_specs,
        out_specs=[tok(d), tok(d // 2), tok(ROUTE_COLS), pl.BlockSpec((ROUTE_COLS, tm), lambda i: (0, i)),
                   pl.BlockSpec((LANES, 1), lambda i: (0, 0))],
        scratch_shapes=[pltpu.VMEM((tm, sgu), BF16), pltpu.VMEM((LANES, 1), F32)],
        compiler_params=_cparams("arbitrary"),
        name="mix",
    )(*ins)


def _row_copy(src, s, dst, d, sem):
    return pltpu.make_async_copy(src.at[pl.ds(s, 1)], dst.at[pl.ds(d, 1)], sem)


def _dispatch_kernel(zblk_ref, *refs, tm, rows):
    dest_refs, (xn_ref, xs_ref, zero_scr, sem, zsem) = refs[:TOP_K], refs[TOP_K:]

    @pl.when(pl.program_id(0) == 0)
    def _():
        zero_scr[...] = jnp.zeros_like(zero_scr)

        def zero_copy(j):
            start = pl.multiple_of(zblk_ref[j] * rows, rows)
            return pltpu.make_async_copy(zero_scr, xs_ref.at[pl.ds(start, rows)], zsem)

        for j in range(zblk_ref.shape[0]):
            pl.when(zblk_ref[j] >= 0)(lambda j=j: zero_copy(j).start())
        for j in range(zblk_ref.shape[0]):
            pl.when(zblk_ref[j] >= 0)(lambda j=j: zero_copy(j).wait())

    for j in range(tm):
        for k, dref in enumerate(dest_refs):
            _row_copy(xn_ref, j, xs_ref, dref[0, 0, j], sem).start(priority=k % 2)
    for _ in range(TOP_K):
        pltpu.make_async_copy(xn_ref, xs_ref.at[pl.ds(0, tm)], sem).wait()


def _dest_blocks(dest, tm):
    return [dest[k].reshape(-1, 1, tm) for k in range(TOP_K)]


def _dispatch_call(zero_blocks, dest, xn, m_pad, seq):
    n_tok, dw = xn.shape
    tm = min(TOKEN_TILE, seq)
    smem = pl.BlockSpec((1, 1, tm), lambda i, zb: (i, 0, 0), memory_space=pltpu.SMEM)
    return pl.pallas_call(
        functools.partial(_dispatch_kernel, tm=tm, rows=EXPERT_ROWS),
        out_shape=jax.ShapeDtypeStruct((m_pad, dw), xn.dtype),
        grid_spec=pltpu.PrefetchScalarGridSpec(
            num_scalar_prefetch=1,
            grid=(n_tok // tm,),
            in_specs=[smem] * TOP_K + [pl.BlockSpec((tm, dw), lambda i, zb: (i, 0))],
            out_specs=pl.BlockSpec(memory_space=pl.ANY),
            scratch_shapes=[pltpu.VMEM((EXPERT_ROWS, dw), xn.dtype), pltpu.SemaphoreType.DMA(()),
                            pltpu.SemaphoreType.DMA(())]),
        compiler_params=_cparams("arbitrary"),
        name="dispatch",
    )(zero_blocks, *_dest_blocks(dest, tm), xn)


def _expert_kernel(blk_e_ref, used_ref, xs_ref, wg_ref, wu_ref, wd_ref, y_ref):
    del blk_e_ref

    @pl.when(pl.program_id(0) < used_ref[0])
    def _():
        xb = _unpack_halves(xs_ref[...]).astype(BF16)
        g = _dot(xb, wg_ref[0])
        hid = g * _sigmoid(g) * _dot(xb, wu_ref[0])
        y_ref[...] = _pack_halves(_dot(hid.astype(BF16), wd_ref[0]))

    @pl.when(pl.program_id(0) >= used_ref[0])
    def _():
        y_ref[...] = jnp.zeros_like(y_ref)


def _expert_call(blk_e, used, xs, wg, wu, wd):
    m_pad, dw = xs.shape
    _, d, f = wg.shape
    rows = EXPERT_ROWS
    return pl.pallas_call(
        _expert_kernel,
        out_shape=jax.ShapeDtypeStruct((m_pad, dw), xs.dtype),
        grid_spec=pltpu.PrefetchScalarGridSpec(
            num_scalar_prefetch=2,
            grid=(m_pad // rows,),
            in_specs=[pl.BlockSpec((rows, dw), lambda i, be, us: (i, 0)),
                      pl.BlockSpec((1, d, f), lambda i, be, us: (be[i], 0, 0)),
                      pl.BlockSpec((1, d, f), lambda i, be, us: (be[i], 0, 0)),
                      pl.BlockSpec((1, f, d), lambda i, be, us: (be[i], 0, 0))],
            out_specs=pl.BlockSpec((rows, dw), lambda i, be, us: (i, 0))),
        compiler_params=_cparams("arbitrary"),
        name="experts",
    )(blk_e, used, xs, wg, wu, wd)


def _combine_kernel(*refs, tm, final):
    cur_refs, nxt_refs = refs[:TOP_K], refs[TOP_K:2 * TOP_K]
    h_ref, route_ref, gtf_ref, gfin_ref, y_ref, o_ref, buf, sems = refs[2 * TOP_K:]
    step = pl.program_id(0)
    slot = step % 2

    def gather(dest_refs, s):
        for j in range(tm):
            for k, dref in enumerate(dest_refs):
                pltpu.make_async_copy(y_ref.at[pl.ds(dref[0, 0, j], 1)], buf.at[s, k, pl.ds(j, 1)],
                                      sems.at[s]).start(priority=k % 2)

    pl.when(step == 0)(lambda: gather(cur_refs, 0))
    pl.when(step + 1 < pl.num_programs(0))(lambda: gather(nxt_refs, 1 - slot))
    for k in range(TOP_K):
        pltpu.make_async_copy(y_ref.at[pl.ds(0, tm)], buf.at[slot, k], sems.at[slot]).wait()
    route = route_ref[...]
    y = route[:, 2:3] * _unpack_halves(buf[slot, 0]) + route[:, 3:4] * _unpack_halves(buf[slot, 1])
    h = h_ref[...] + gtf_ref[0] * y
    o_ref[...] = _rms(h, gfin_ref[...]) if final else h


def _combine_call(dest, h, route, gtf, g_final, yb, seq, final):
    n_tok, d = h.shape
    tm = min(TOKEN_TILE, seq)
    per_b = seq // tm
    steps = n_tok // tm
    tok = lambda w: pl.BlockSpec((tm, w), lambda i: (i, 0))
    cur = pl.BlockSpec((1, 1, tm), lambda i: (i, 0, 0), memory_space=pltpu.SMEM)
    nxt = pl.BlockSpec((1, 1, tm), lambda i: (jnp.minimum(i + 1, steps - 1), 0, 0), memory_space=pltpu.SMEM)
    blocks = _dest_blocks(dest, tm)
    return pl.pallas_call(
        functools.partial(_combine_kernel, tm=tm, final=final),
        out_shape=jax.ShapeDtypeStruct((n_tok, d), F32),
        grid=(steps,),
        in_specs=[cur] * TOP_K + [nxt] * TOP_K + [
            tok(d), tok(ROUTE_COLS),
            pl.BlockSpec((1, 1, d), lambda i: (i // per_b, 0, 0)),
            pl.BlockSpec((1, d), lambda i: (0, 0)),
            pl.BlockSpec(memory_space=pl.ANY)],
        out_specs=tok(d),
        scratch_shapes=[pltpu.VMEM((2, TOP_K, tm, yb.shape[1]), yb.dtype), pltpu.SemaphoreType.DMA((2,))],
        compiler_params=_cparams("arbitrary"),
        name="combine",
    )(*blocks, *blocks, h, route, gtf, g_final, yb)


def _routing_plan(route_t, counts, n_experts, m_pad):
    eid = route_t[0:TOP_K].astype(jnp.int32)
    rank = route_t[4:4 + TOP_K].astype(jnp.int32)
    cnt = counts.astype(jnp.int32)
    padded = (cnt + EXPERT_ROWS - 1) // EXPERT_ROWS * EXPERT_ROWS
    pend = jnp.cumsum(padded)
    pstart = pend - padded
    dest = rank
    for e in range(n_experts):
        dest = dest + jnp.where(eid == e, pstart[e], 0)
    nblk = m_pad // EXPERT_ROWS
    blk_row = jnp.arange(nblk, dtype=jnp.int32) * EXPERT_ROWS
    blk_e = jnp.minimum(jnp.sum(pend[None, :] <= blk_row[:, None], axis=1), n_experts - 1).astype(jnp.int32)
    used = (pend[-1:] // EXPERT_ROWS).astype(jnp.int32)
    last_blk = jnp.where(padded > 0, pend // EXPERT_ROWS - 1, -1)
    tail_blk = used + jnp.arange(n_experts, dtype=jnp.int32)
    tail_blk = jnp.where(tail_blk < nblk, tail_blk, -1)
    zero_blocks = jnp.concatenate([last_blk, tail_blk]).astype(jnp.int32)
    return dest, blk_e, used, zero_blocks


def kernel(x, c, g_mix, g_ffn, w_ada, b_ada, w_in, w_sba_out, g_sgu, w_spatial, b_spatial, w_sgu_out, w_out,
           w_router_group, b_router_group, w_router_expert, b_router_expert, w_expert_gate, w_expert_up,
           w_expert_down, g_final):
    bsz, seq, d = x.shape
    depth = w_in.shape[0]
    sba = w_sba_out.shape[1]
    sgu = g_sgu.shape[1]
    heads = sba // HEAD_DIM
    groups, chunk = w_spatial.shape[1], w_spatial.shape[2]
    n_groups = w_router_group.shape[2]
    n_experts = w_router_expert.shape[2]
    per_group = n_experts // n_groups
    n_tok = bsz * seq
    m_pad = n_tok * TOP_K + n_experts * EXPERT_ROWS
    widths = (sba, sba, sba, 2 * sgu, 2 * d)
    assert seq % chunk == 0 and LANES % (sgu // groups) == 0
    assert d % (2 * LANES) == 0
    assert per_group == SUBLANES and n_groups <= SUBLANES and (n_groups + 1) * SUBLANES <= LANES

    h = x.reshape(n_tok, d)
    for l in range(depth):
        mod = _mod_call(c, w_ada[l], b_ada[l])
        sh_m, sc_m, gt_m, sh_f, sc_f, gt_f = [mod[:, i * d:(i + 1) * d].reshape(bsz, 1, d) for i in range(6)]

        q, k, v, uv, gates = _proj_call(h, sh_m, sc_m, g_mix[l].reshape(1, d), w_in[l].astype(BF16), seq, widths)
        ya = _attn_call(q.reshape(bsz, seq, sba), k.reshape(bsz, seq, sba), v.reshape(bsz, seq, sba), heads)

        bias_full = jnp.repeat(b_spatial[l].T, sgu // groups, axis=1)
        gpad = SUBLANES - n_groups
        wr = jnp.concatenate([jnp.pad(w_router_group[l], ((0, 0), (0, gpad))), w_router_expert[l]], axis=1)
        wr = jnp.pad(wr, ((0, 0), (0, LANES - wr.shape[1])))
        br = jnp.concatenate([jnp.pad(b_router_group[l], (0, gpad)), b_router_expert[l]])
        br = jnp.pad(br, (0, LANES - br.shape[0])).reshape(1, LANES)
        h1, xn, route, route_t, counts = _mix_call(
            ya.reshape(n_tok, sba), uv, gates, h, gt_m, sh_f, sc_f, g_sgu[l].reshape(1, sgu), w_spatial[l],
            bias_full, w_sba_out[l].astype(BF16), w_sgu_out[l].astype(BF16), w_out[l].astype(BF16),
            g_ffn[l].reshape(1, d), wr, br, seq, n_groups, per_group)

        dest, blk_e, used, zero_blocks = _routing_plan(route_t, counts[SUBLANES:SUBLANES + n_experts, 0],
                                                       n_experts, m_pad)
        xs = _dispatch_call(zero_blocks, dest, xn, m_pad, seq)
        yb = _expert_call(blk_e, used, xs, w_expert_gate[l].astype(BF16), w_expert_up[l].astype(BF16),
                          w_expert_down[l].astype(BF16))
        h = _combine_call(dest, h1, route, gt_f, g_final.reshape(1, d), yb, seq, final=l == depth - 1)
    return h.reshape(bsz, seq, d)
```

```python
import functools

import jax
import jax.numpy as jnp
from jax import lax
from jax.experimental import pallas as pl
from jax.experimental.pallas import tpu as pltpu

F32 = jnp.float32
BF16 = jnp.bfloat16

HEAD_DIM = 64
TOP_K = 2
EPS = 1e-6
LOG2E = 1.4426950408889634
SIGN_BIT = 0x80000000
BF16_BITS = 0xFFFF0000
ATTN_EXIT_LOG2 = 160.0
LANES = 128
SUBLANES = 8
ATTN_BLOCK = 256
EXPERT_ROWS = 512
TOKEN_TILE = 256
ROW_COPY_TILE = 512
PROJ_TILE = 512
ROUTE_COLS = 8
VMEM_LIMIT = 56 * 1024 * 1024


def _cparams(*sem):
    return pltpu.CompilerParams(dimension_semantics=sem, vmem_limit_bytes=VMEM_LIMIT)


def _split_bf16(a):
    hi = a.astype(BF16)
    lo = (a - hi.astype(F32)).astype(BF16)
    return hi, lo


def _dot(a, b):
    return jnp.dot(a, b, preferred_element_type=F32)


def _dot3(a, b):
    ah, al = _split_bf16(a)
    bh, bl = _split_bf16(b)
    return _dot(ah, bh) + _dot(ah, bl) + _dot(al, bh)


def _pack_halves(x):
    half = x.shape[1] // 2
    hi = lax.bitcast_convert_type(x[:, :half].astype(BF16).astype(F32), jnp.uint32)
    lo = lax.bitcast_convert_type(x[:, half:].astype(BF16).astype(F32), jnp.uint32)
    return hi | (lo >> 16)


def _unpack_halves(u):
    hi = lax.bitcast_convert_type(u & jnp.uint32(BF16_BITS), F32)
    lo = lax.bitcast_convert_type(u << 16, F32)
    return jnp.concatenate([hi, lo], axis=1)


def _rms(x, g):
    ms = jnp.mean(x * x, axis=-1, keepdims=True)
    return x * lax.rsqrt(ms + EPS) * g


def _mod_kernel(c_ref, w_ref, b_ref, o_ref):
    c = c_ref[...]
    ca = c * (1.0 / (1.0 + jnp.exp(-c)))
    o_ref[...] = _dot3(ca, w_ref[...]) + b_ref[...]


def _mod_call(c, w_ada, b_ada):
    bsz, d = c.shape
    n = w_ada.shape[1]
    tn = n // 4 if n % (4 * LANES) == 0 else n
    return pl.pallas_call(
        _mod_kernel,
        out_shape=jax.ShapeDtypeStruct((bsz, n), F32),
        grid=(n // tn,),
        in_specs=[pl.BlockSpec((bsz, d), lambda j: (0, 0)),
                  pl.BlockSpec((d, tn), lambda j: (0, j)),
                  pl.BlockSpec((1, tn), lambda j: (0, j))],
        out_specs=pl.BlockSpec((bsz, tn), lambda j: (0, j)),
        compiler_params=_cparams("arbitrary"),
        name="mod",
    )(c, w_ada, b_ada.reshape(1, n))


def _proj_kernel(x_ref, sh_ref, sc_ref, g_ref, w_ref, q_ref, k_ref, v_ref, uv_ref, gt_ref, *, widths):
    x = x_ref[...]
    n = _rms(x, g_ref[...]) * (1.0 + sc_ref[0]) + sh_ref[0]
    nb = n.astype(BF16)
    off = 0
    for ref, wd in zip((q_ref, k_ref, v_ref, uv_ref, gt_ref), widths):
        p = _dot(nb, w_ref[:, off:off + wd])
        if ref is q_ref:
            p = p * (HEAD_DIM ** -0.5 * LOG2E)
        ref[...] = p.astype(ref.dtype)
        off += wd


def _proj_call(x2, sh, sc, g, w_in_bf, seq, widths):
    n_tok, d = x2.shape
    tm = min(PROJ_TILE, seq)
    per_b = seq // tm
    cols = w_in_bf.shape[1]
    tok = lambda w: pl.BlockSpec((tm, w), lambda i: (i, 0))
    vec = pl.BlockSpec((1, 1, d), lambda i: (i // per_b, 0, 0))
    return pl.pallas_call(
        functools.partial(_proj_kernel, widths=widths),
        out_shape=[jax.ShapeDtypeStruct((n_tok, w), BF16) for w in widths],
        grid=(n_tok // tm,),
        in_specs=[tok(d), vec, vec,
                  pl.BlockSpec((1, d), lambda i: (0, 0)),
                  pl.BlockSpec((d, cols), lambda i: (0, 0))],
        out_specs=[tok(w) for w in widths],
        compiler_params=_cparams("arbitrary"),
        name="proj",
    )(x2, sh, sc, g, w_in_bf)


def _attn_kernel(q_ref, k_ref, v_ref, u_ref, o_ref, qm_ref, acc_ref, carry_ref, *, heads, blk):
    qi = pl.program_id(1)
    u_tri = u_ref[...]
    row = lax.broadcasted_iota(jnp.int32, (blk, blk), 0)
    col = lax.broadcasted_iota(jnp.int32, (blk, blk), 1)
    causal = col < row
    per_slab = LANES // HEAD_DIM
    lane = lax.broadcasted_iota(jnp.int32, (blk, LANES), 1)
    own = [(lane >= j * HEAD_DIM) & (lane < (j + 1) * HEAD_DIM) for j in range(per_slab)]
    for h in range(heads):
        slab = slice(h // per_slab * LANES, (h // per_slab + 1) * LANES)
        qm_ref[h] = jnp.where(own[h % per_slab], q_ref[0, :, slab], 0.0).astype(BF16)

    def key_tile(start, first):
        mask = causal if first else None
        slabs = [slice(p * LANES, (p + 1) * LANES) for p in range(heads // per_slab)]
        s_all = [lax.dot_general(qm_ref[h], k_ref[0, pl.ds(start, blk), slabs[h // per_slab]],
                                 (((1,), (1,)), ((), ())), preferred_element_type=F32) for h in range(heads)]
        cum_all = []
        for s in s_all:
            neg_abs = lax.bitcast_convert_type(lax.bitcast_convert_type(s, jnp.uint32) | jnp.uint32(SIGN_BIT), F32)
            sp = jnp.maximum(s, 0.0) + jnp.log(1.0 + jnp.exp2(neg_abs)) * LOG2E
            if mask is not None:
                sp = jnp.where(mask, sp, 0.0)
            hi = lax.bitcast_convert_type(lax.bitcast_convert_type(sp, jnp.uint32) & jnp.uint32(BF16_BITS), F32)
            cum_all.append(_dot(hi.astype(BF16), u_tri) + _dot((sp - hi).astype(BF16), u_tri))
        w_all = []
        cmin = None
        for h in range(heads):
            cum = cum_all[h]
            if first:
                w = jnp.exp2(s_all[h] - cum)
                carry = cum[:, 0:1]
            else:
                carry = carry_ref[h]
                w = jnp.exp2((s_all[h] - carry) - cum)
                carry = carry + cum[:, 0:1]
            w_all.append((w if mask is None else jnp.where(mask, w, 0.0)).astype(BF16))
            carry_ref[h] = carry
            cmin = carry if cmin is None else jnp.minimum(cmin, carry)
        for p, slab in enumerate(slabs):
            vp = v_ref[0, pl.ds(start, blk), slab]
            upd = None
            for j in range(per_slab):
                pv = _dot(w_all[p * per_slab + j], vp)
                upd = pv if upd is None else jnp.where(own[j], pv, upd)
            acc_ref[:, slab] = upd if first else acc_ref[:, slab] + upd
        return jnp.min(cmin)

    cmin = key_tile(pl.multiple_of(qi * blk, blk), True)

    def cond(state):
        i, cmin = state
        return (i < qi) & (cmin < ATTN_EXIT_LOG2)

    def body(state):
        i, _ = state
        return i + 1, key_tile(pl.multiple_of((qi - 1 - i) * blk, blk), False)

    lax.while_loop(cond, body, (jnp.int32(0), cmin))
    o_ref[0] = acc_ref[...].astype(o_ref.dtype)


def _attn_call(q, k, v, heads):
    bsz, seq, width = q.shape
    blk = min(ATTN_BLOCK, seq)
    r = lax.broadcasted_iota(jnp.int32, (blk, blk), 0)
    c = lax.broadcasted_iota(jnp.int32, (blk, blk), 1)
    u_tri = (r >= c).astype(BF16)
    full = pl.BlockSpec((1, seq, width), lambda b, i: (b, 0, 0))
    return pl.pallas_call(
        functools.partial(_attn_kernel, heads=heads, blk=blk),
        out_shape=jax.ShapeDtypeStruct((bsz, seq, width), BF16),
        grid=(bsz, seq // blk),
        in_specs=[pl.BlockSpec((1, blk, width), lambda b, i: (b, i, 0)), full, full,
                  pl.BlockSpec((blk, blk), lambda b, i: (0, 0))],
        out_specs=pl.BlockSpec((1, blk, width), lambda b, i: (b, i, 0)),
        scratch_shapes=[pltpu.VMEM((heads, blk, LANES), BF16), pltpu.VMEM((blk, width), F32),
                        pltpu.VMEM((heads, blk, 1), F32)],
        compiler_params=_cparams("arbitrary", "arbitrary"),
        name="attn",
    )(q, k, v, u_tri)


def _gelu_tanh(x):
    c = 0.7978845608028654
    hx = 0.5 * x
    return hx + hx * jnp.tanh(x * (c + (0.044715 * c) * (x * x)))


def _sigmoid(x):
    return 1.0 / (1.0 + jnp.exp(-x))


def _mix_kernel(ya_ref, uv_ref, gt_ref, x_ref, gtm_ref, shf_ref, scf_ref, gsgu_ref, ws_ref, bs_ref,
                wa_ref, wb_ref, wo_ref, gffn_ref, wr_ref, br_ref, utri_ref,
                h_ref, xn_ref, route_ref, route_t_ref, cnt_ref, yb_scr, cnt_scr,
                *, chunk, groups, n_groups, per_group):
    step = pl.program_id(0)
    tm = x_ref.shape[0]
    sgu = gsgu_ref.shape[1]
    gdim = sgu // groups

    @pl.when(step == 0)
    def _():
        cnt_scr[...] = jnp.zeros_like(cnt_scr)

    ya_proj = _dot(ya_ref[...], wa_ref[...])

    act = _gelu_tanh(uv_ref[...].astype(F32))
    u = act[:, :sgu]
    v = act[:, sgu:]
    mu = jnp.mean(v, axis=-1, keepdims=True)
    vc = v - mu
    var = jnp.mean(vc * vc, axis=-1, keepdims=True)
    vn = vc * lax.rsqrt(var + EPS) * gsgu_ref[...]
    r = lax.broadcasted_iota(jnp.int32, (chunk, chunk), 0)
    c = lax.broadcasted_iota(jnp.int32, (chunk, chunk), 1)
    wcs = [jnp.where(r >= c, ws_ref[g], 0.0).astype(BF16) for g in range(groups)]
    pair = LANES // gdim
    lane = lax.broadcasted_iota(jnp.int32, (chunk, LANES), 1)
    for ci in range(tm // chunk):
        rows = slice(ci * chunk, (ci + 1) * chunk)
        for p in range(groups // pair):
            lanes = slice(p * LANES, (p + 1) * LANES)
            slab = vn[rows, lanes]
            mix = bs_ref[:, lanes]
            for j in range(pair):
                sel = (lane >= j * gdim) & (lane < (j + 1) * gdim)
                mix = mix + _dot(wcs[p * pair + j], jnp.where(sel, slab, 0.0).astype(BF16))
            yb_scr[rows, lanes] = (u[rows, lanes] * mix).astype(BF16)

    d = x_ref.shape[1]
    gates2 = 1.0 + jnp.tanh(0.5 * gt_ref[...].astype(F32))
    merged2 = gates2[:, :d] * ya_proj + gates2[:, d:] * _dot(yb_scr[...], wb_ref[...])
    h = x_ref[...] + (0.5 * gtm_ref[0]) * _dot(merged2.astype(BF16), wo_ref[...])
    h_ref[...] = h

    xn = _rms(h, gffn_ref[...]) * (1.0 + scf_ref[0]) + shf_ref[0]
    xn_ref[...] = _pack_halves(xn)
    logits = _dot3(xn, wr_ref[...]) + br_ref[...]

    lt = logits.T
    sub = lax.broadcasted_iota(jnp.int32, (SUBLANES, tm), 0)
    neg = -jnp.inf
    gl = jnp.where(sub < n_groups, lt[0:SUBLANES], neg)
    gmax = jnp.max(gl, axis=0, keepdims=True)
    g_sel = jnp.min(jnp.where(gl == gmax, sub, SUBLANES), axis=0, keepdims=True)
    g_w = 1.0 / jnp.sum(jnp.exp(gl - gmax), axis=0, keepdims=True)
    el = lt[SUBLANES:2 * SUBLANES]
    for g in range(1, n_groups):
        el = jnp.where(g_sel == g, lt[(g + 1) * SUBLANES:(g + 2) * SUBLANES], el)
    m1 = jnp.max(el, axis=0, keepdims=True)
    j1 = jnp.min(jnp.where(el == m1, sub, SUBLANES), axis=0, keepdims=True)
    el2 = jnp.where(sub == j1, neg, el)
    m2 = jnp.max(el2, axis=0, keepdims=True)
    j2 = jnp.min(jnp.where(el2 == m2, sub, SUBLANES), axis=0, keepdims=True)
    t = jnp.exp(m2 - m1)
    w1 = g_w / (1.0 + t)
    w2 = g_w * t / (1.0 + t)
    e1 = g_sel * per_group + j1
    e2 = g_sel * per_group + j2

    row = lax.broadcasted_iota(jnp.int32, (LANES, tm), 0)
    oh1 = row == e1 + SUBLANES
    oh2 = row == e2 + SUBLANES
    both = jnp.where(oh1 | oh2, 1.0, 0.0)
    before = _dot(both.astype(BF16), utri_ref[...]) + cnt_scr[...]
    rank1 = jnp.sum(jnp.where(oh1, before, 0.0), axis=0, keepdims=True)
    rank2 = jnp.sum(jnp.where(oh2, before, 0.0), axis=0, keepdims=True)
    cnt_scr[...] = cnt_scr[...] + jnp.sum(both, axis=1, keepdims=True)
    cnt_ref[...] = cnt_scr[...]

    rec = jnp.zeros((LANES, tm), F32)
    for idx, val in enumerate((e1.astype(F32), e2.astype(F32), w1, w2, rank1, rank2)):
        rec = jnp.where(row == idx, val, rec)
    route_t_ref[...] = rec[:ROUTE_COLS]
    route_ref[...] = rec.T[:, :ROUTE_COLS]


def _mix_call(ya, uv, gates, x2, gtm, shf, scf, g_sgu, w_spatial, bias_full, wa, wb, wo, g_ffn, wr, br,
              seq, n_groups, per_group):
    n_tok, d = x2.shape
    tm = min(TOKEN_TILE, seq)
    per_b = seq // tm
    groups, chunk, _ = w_spatial.shape
    sgu = g_sgu.shape[1]
    r = lax.broadcasted_iota(jnp.int32, (tm, tm), 0)
    c = lax.broadcasted_iota(jnp.int32, (tm, tm), 1)
    utri = (r < c).astype(BF16)
    tok = lambda w: pl.BlockSpec((tm, w), lambda i: (i, 0))
    vec = pl.BlockSpec((1, 1, d), lambda i: (i // per_b, 0, 0))
    const = lambda a: pl.BlockSpec(a.shape, lambda i: (0,) * a.ndim)
    ins = [ya, uv, gates, x2, gtm, shf, scf, g_sgu, w_spatial, bias_full, wa, wb, wo, g_ffn, wr, br, utri]
    in_specs = [tok(ya.shape[1]), tok(uv.shape[1]), tok(gates.shape[1]), tok(d), vec, vec, vec]
    in_specs += [const(a) for a in ins[7:]]
    return pl.pallas_call(
        functools.partial(_mix_kernel, chunk=chunk, groups=groups, n_groups=n_groups, per_group=per_group),
        out_shape=[jax.ShapeDtypeStruct((n_tok, d), F32), jax.ShapeDtypeStruct((n_tok, d // 2), jnp.uint32),
                   jax.ShapeDtypeStruct((n_tok, ROUTE_COLS), F32), jax.ShapeDtypeStruct((ROUTE_COLS, n_tok), F32),
                   jax.ShapeDtypeStruct((LANES, 1), F32)],
        grid=(n_tok // tm,),
        in_specs=in_specs,
        out_specs=[tok(d), tok(d // 2), tok(ROUTE_COLS), pl.BlockSpec((ROUTE_COLS, tm), lambda i: (0, i)),
                   pl.BlockSpec((LANES, 1), lambda i: (0, 0))],
        scratch_shapes=[pltpu.VMEM((tm, sgu), BF16), pltpu.VMEM((LANES, 1), F32)],
        compiler_params=_cparams("arbitrary"),
        name="mix",
    )(*ins)


def _row_copy(src, s, dst, d, sem):
    return pltpu.make_async_copy(src.at[pl.ds(s, 1)], dst.at[pl.ds(d, 1)], sem)


def _dispatch_kernel(zblk_ref, *refs, tm, rows):
    dest_refs, (xn_ref, xs_ref, zero_scr, sem, zsem) = refs[:TOP_K], refs[TOP_K:]

    @pl.when(pl.program_id(0) == 0)
    def _():
        zero_scr[...] = jnp.zeros_like(zero_scr)

        def zero_copy(j):
            start = pl.multiple_of(zblk_ref[j] * rows, rows)
            return pltpu.make_async_copy(zero_scr, xs_ref.at[pl.ds(start, rows)], zsem)

        for j in range(zblk_ref.shape[0]):
            pl.when(zblk_ref[j] >= 0)(lambda j=j: zero_copy(j).start())
        for j in range(zblk_ref.shape[0]):
            pl.when(zblk_ref[j] >= 0)(lambda j=j: zero_copy(j).wait())

    for j in range(tm):
        for k, dref in enumerate(dest_refs):
            _row_copy(xn_ref, j, xs_ref, dref[0, 0, j], sem).start(priority=k % 2)
    for _ in range(TOP_K):
        pltpu.make_async_copy(xn_ref, xs_ref.at[pl.ds(0, tm)], sem).wait()


def _dest_blocks(dest, tm):
    return [dest[k].reshape(-1, 1, tm) for k in range(TOP_K)]


def _dispatch_call(zero_blocks, dest, xn, m_pad, seq):
    n_tok, dw = xn.shape
    tm = min(ROW_COPY_TILE, seq)
    smem = pl.BlockSpec((1, 1, tm), lambda i, zb: (i, 0, 0), memory_space=pltpu.SMEM)
    return pl.pallas_call(
        functools.partial(_dispatch_kernel, tm=tm, rows=EXPERT_ROWS),
        out_shape=jax.ShapeDtypeStruct((m_pad, dw), xn.dtype),
        grid_spec=pltpu.PrefetchScalarGridSpec(
            num_scalar_prefetch=1,
            grid=(n_tok // tm,),
            in_specs=[smem] * TOP_K + [pl.BlockSpec((tm, dw), lambda i, zb: (i, 0))],
            out_specs=pl.BlockSpec(memory_space=pl.ANY),
            scratch_shapes=[pltpu.VMEM((EXPERT_ROWS, dw), xn.dtype), pltpu.SemaphoreType.DMA(()),
                            pltpu.SemaphoreType.DMA(())]),
        compiler_params=_cparams("arbitrary"),
        name="dispatch",
    )(zero_blocks, *_dest_blocks(dest, tm), xn)


def _expert_kernel(blk_e_ref, used_ref, xs_ref, wg_ref, wu_ref, wd_ref, y_ref, wg_bf, wu_bf, wd_bf):
    i = pl.program_id(0)
    live = i < used_ref[0]
    new_expert = (i == 0) | (blk_e_ref[i] != blk_e_ref[jnp.maximum(i - 1, 0)])

    @pl.when(live & new_expert)
    def _():
        wg_bf[...] = wg_ref[0].astype(BF16)
        wu_bf[...] = wu_ref[0].astype(BF16)
        wd_bf[...] = wd_ref[0].astype(BF16)

    @pl.when(live)
    def _():
        xb = _unpack_halves(xs_ref[...]).astype(BF16)
        g = _dot(xb, wg_bf[...])
        hid = g * _sigmoid(g) * _dot(xb, wu_bf[...])
        y_ref[...] = _pack_halves(_dot(hid.astype(BF16), wd_bf[...]))

    @pl.when(jnp.logical_not(live))
    def _():
        y_ref[...] = jnp.zeros_like(y_ref)


def _expert_call(blk_e, used, xs, wg, wu, wd):
    m_pad, dw = xs.shape
    _, d, f = wg.shape
    rows = EXPERT_ROWS
    return pl.pallas_call(
        _expert_kernel,
        out_shape=jax.ShapeDtypeStruct((m_pad, dw), xs.dtype),
        grid_spec=pltpu.PrefetchScalarGridSpec(
            num_scalar_prefetch=2,
            grid=(m_pad // rows,),
            in_specs=[pl.BlockSpec((rows, dw), lambda i, be, us: (i, 0)),
                      pl.BlockSpec((1, d, f), lambda i, be, us: (be[i], 0, 0)),
                      pl.BlockSpec((1, d, f), lambda i, be, us: (be[i], 0, 0)),
                      pl.BlockSpec((1, f, d), lambda i, be, us: (be[i], 0, 0))],
            out_specs=pl.BlockSpec((rows, dw), lambda i, be, us: (i, 0)),
            scratch_shapes=[pltpu.VMEM((d, f), BF16), pltpu.VMEM((d, f), BF16), pltpu.VMEM((f, d), BF16)]),
        compiler_params=_cparams("arbitrary"),
        name="experts",
    )(blk_e, used, xs, wg, wu, wd)


def _combine_kernel(*refs, tm, final):
    cur_refs, nxt_refs = refs[:TOP_K], refs[TOP_K:2 * TOP_K]
    h_ref, route_ref, gtf_ref, gfin_ref, y_ref, o_ref, buf, sems = refs[2 * TOP_K:]
    step = pl.program_id(0)
    slot = step % 2

    def gather(dest_refs, s):
        for j in range(tm):
            for k, dref in enumerate(dest_refs):
                pltpu.make_async_copy(y_ref.at[pl.ds(dref[0, 0, j], 1)], buf.at[s, k, pl.ds(j, 1)],
                                      sems.at[s]).start(priority=k % 2)

    pl.when(step == 0)(lambda: gather(cur_refs, 0))
    pl.when(step + 1 < pl.num_programs(0))(lambda: gather(nxt_refs, 1 - slot))
    for k in range(TOP_K):
        pltpu.make_async_copy(y_ref.at[pl.ds(0, tm)], buf.at[slot, k], sems.at[slot]).wait()
    route = route_ref[...]
    y = route[:, 2:3] * _unpack_halves(buf[slot, 0]) + route[:, 3:4] * _unpack_halves(buf[slot, 1])
    h = h_ref[...] + gtf_ref[0] * y
    o_ref[...] = _rms(h, gfin_ref[...]) if final else h


def _combine_call(dest, h, route, gtf, g_final, yb, seq, final):
    n_tok, d = h.shape
    tm = min(ROW_COPY_TILE, seq)
    per_b = seq // tm
    steps = n_tok // tm
    tok = lambda w: pl.BlockSpec((tm, w), lambda i: (i, 0))
    cur = pl.BlockSpec((1, 1, tm), lambda i: (i, 0, 0), memory_space=pltpu.SMEM)
    nxt = pl.BlockSpec((1, 1, tm), lambda i: (jnp.minimum(i + 1, steps - 1), 0, 0), memory_space=pltpu.SMEM)
    blocks = _dest_blocks(dest, tm)
    return pl.pallas_call(
        functools.partial(_combine_kernel, tm=tm, final=final),
        out_shape=jax.ShapeDtypeStruct((n_tok, d), F32),
        grid=(steps,),
        in_specs=[cur] * TOP_K + [nxt] * TOP_K + [
            tok(d), tok(ROUTE_COLS),
            pl.BlockSpec((1, 1, d), lambda i: (i // per_b, 0, 0)),
            pl.BlockSpec((1, d), lambda i: (0, 0)),
            pl.BlockSpec(memory_space=pl.ANY)],
        out_specs=tok(d),
        scratch_shapes=[pltpu.VMEM((2, TOP_K, tm, yb.shape[1]), yb.dtype), pltpu.SemaphoreType.DMA((2,))],
        compiler_params=_cparams("arbitrary"),
        name="combine",
    )(*blocks, *blocks, h, route, gtf, g_final, yb)


def _routing_plan(route_t, counts, n_experts, m_pad):
    eid = route_t[0:TOP_K].astype(jnp.int32)
    rank = route_t[4:4 + TOP_K].astype(jnp.int32)
    cnt = counts.astype(jnp.int32)
    padded = (cnt + EXPERT_ROWS - 1) // EXPERT_ROWS * EXPERT_ROWS
    pend = jnp.cumsum(padded)
    pstart = pend - padded
    dest = rank
    for e in range(n_experts):
        dest = dest + jnp.where(eid == e, pstart[e], 0)
    nblk = m_pad // EXPERT_ROWS
    blk_row = jnp.arange(nblk, dtype=jnp.int32) * EXPERT_ROWS
    blk_e = jnp.minimum(jnp.sum(pend[None, :] <= blk_row[:, None], axis=1), n_experts - 1).astype(jnp.int32)
    used = (pend[-1:] // EXPERT_ROWS).astype(jnp.int32)
    last_blk = jnp.where(padded > 0, pend // EXPERT_ROWS - 1, -1)
    tail_blk = used + jnp.arange(n_experts, dtype=jnp.int32)
    tail_blk = jnp.where(tail_blk < nblk, tail_blk, -1)
    zero_blocks = jnp.concatenate([last_blk, tail_blk]).astype(jnp.int32)
    return dest, blk_e, used, zero_blocks


def kernel(x, c, g_mix, g_ffn, w_ada, b_ada, w_in, w_sba_out, g_sgu, w_spatial, b_spatial, w_sgu_out, w_out,
           w_router_group, b_router_group, w_router_expert, b_router_expert, w_expert_gate, w_expert_up,
           w_expert_down, g_final):
    bsz, seq, d = x.shape
    depth = w_in.shape[0]
    sba = w_sba_out.shape[1]
    sgu = g_sgu.shape[1]
    heads = sba // HEAD_DIM
    groups, chunk = w_spatial.shape[1], w_spatial.shape[2]
    n_groups = w_router_group.shape[2]
    n_experts = w_router_expert.shape[2]
    per_group = n_experts // n_groups
    n_tok = bsz * seq
    m_pad = n_tok * TOP_K + n_experts * EXPERT_ROWS
    widths = (sba, sba, sba, 2 * sgu, 2 * d)
    assert seq % chunk == 0 and LANES % (sgu // groups) == 0
    assert d % (2 * LANES) == 0
    assert per_group == SUBLANES and n_groups <= SUBLANES and (n_groups + 1) * SUBLANES <= LANES

    h = x.reshape(n_tok, d)
    for l in range(depth):
        mod = _mod_call(c, w_ada[l], b_ada[l])
        sh_m, sc_m, gt_m, sh_f, sc_f, gt_f = [mod[:, i * d:(i + 1) * d].reshape(bsz, 1, d) for i in range(6)]

        q, k, v, uv, gates = _proj_call(h, sh_m, sc_m, g_mix[l].reshape(1, d), w_in[l].astype(BF16), seq, widths)
        ya = _attn_call(q.reshape(bsz, seq, sba), k.reshape(bsz, seq, sba), v.reshape(bsz, seq, sba), heads)

        bias_full = jnp.repeat(b_spatial[l].T, sgu // groups, axis=1)
        gpad = SUBLANES - n_groups
        wr = jnp.concatenate([jnp.pad(w_router_group[l], ((0, 0), (0, gpad))), w_router_expert[l]], axis=1)
        wr = jnp.pad(wr, ((0, 0), (0, LANES - wr.shape[1])))
        br = jnp.concatenate([jnp.pad(b_router_group[l], (0, gpad)), b_router_expert[l]])
        br = jnp.pad(br, (0, LANES - br.shape[0])).reshape(1, LANES)
        h1, xn, route, route_t, counts = _mix_call(
            ya.reshape(n_tok, sba), uv, gates, h, gt_m, sh_f, sc_f, g_sgu[l].reshape(1, sgu), w_spatial[l],
            bias_full, w_sba_out[l].astype(BF16), w_sgu_out[l].astype(BF16), w_out[l].astype(BF16),
            g_ffn[l].reshape(1, d), wr, br, seq, n_groups, per_group)

        dest, blk_e, used, zero_blocks = _routing_plan(route_t, counts[SUBLANES:SUBLANES + n_experts, 0],
                                                       n_experts, m_pad)
        xs = _dispatch_call(zero_blocks, dest, xn, m_pad, seq)
        yb = _expert_call(blk_e, used, xs, w_expert_gate[l], w_expert_up[l], w_expert_down[l])
        h = _combine_call(dest, h1, route, gt_f, g_final.reshape(1, d), yb, seq, final=l == depth - 1)
    return h.reshape(bsz, seq, d)
```

```python
import functools

import jax
import jax.numpy as jnp
from jax import lax
from jax.experimental import pallas as pl
from jax.experimental.pallas import tpu as pltpu

F32 = jnp.float32
BF16 = jnp.bfloat16

HEAD_DIM = 64
TOP_K = 2
EPS = 1e-6
LOG2E = 1.4426950408889634
SIGN_BIT = 0x80000000
BF16_BITS = 0xFFFF0000
ATTN_EXIT_LOG2 = 160.0
DEAD_CARRY = 1e30
LANES = 128
SUBLANES = 8
ATTN_BLOCK = 256
EXPERT_ROWS = 512
TOKEN_TILE = 256
ROW_COPY_TILE = 512
PROJ_TILE = 512
ROUTE_COLS = 8
VMEM_LIMIT = 56 * 1024 * 1024


def _cparams(*sem):
    return pltpu.CompilerParams(dimension_semantics=sem, vmem_limit_bytes=VMEM_LIMIT)


def _split_bf16(a):
    hi = a.astype(BF16)
    lo = (a - hi.astype(F32)).astype(BF16)
    return hi, lo


def _dot(a, b):
    return jnp.dot(a, b, preferred_element_type=F32)


def _dot3(a, b):
    ah, al = _split_bf16(a)
    bh, bl = _split_bf16(b)
    return _dot(ah, bh) + _dot(ah, bl) + _dot(al, bh)


def _pack_halves(x):
    half = x.shape[1] // 2
    hi = lax.bitcast_convert_type(x[:, :half].astype(BF16).astype(F32), jnp.uint32)
    lo = lax.bitcast_convert_type(x[:, half:].astype(BF16).astype(F32), jnp.uint32)
    return hi | (lo >> 16)


def _unpack_halves(u):
    hi = lax.bitcast_convert_type(u & jnp.uint32(BF16_BITS), F32)
    lo = lax.bitcast_convert_type(u << 16, F32)
    return jnp.concatenate([hi, lo], axis=1)


def _rms(x, g):
    ms = jnp.mean(x * x, axis=-1, keepdims=True)
    return x * lax.rsqrt(ms + EPS) * g


def _mod_kernel(c_ref, w_ref, b_ref, o_ref):
    c = c_ref[...]
    ca = c * (1.0 / (1.0 + jnp.exp(-c)))
    o_ref[...] = _dot3(ca, w_ref[...]) + b_ref[...]


def _mod_call(c, w_ada, b_ada):
    bsz, d = c.shape
    n = w_ada.shape[1]
    tn = n // 4 if n % (4 * LANES) == 0 else n
    return pl.pallas_call(
        _mod_kernel,
        out_shape=jax.ShapeDtypeStruct((bsz, n), F32),
        grid=(n // tn,),
        in_specs=[pl.BlockSpec((bsz, d), lambda j: (0, 0)),
                  pl.BlockSpec((d, tn), lambda j: (0, j)),
                  pl.BlockSpec((1, tn), lambda j: (0, j))],
        out_specs=pl.BlockSpec((bsz, tn), lambda j: (0, j)),
        compiler_params=_cparams("arbitrary"),
        name="mod",
    )(c, w_ada, b_ada.reshape(1, n))


def _proj_kernel(x_ref, sh_ref, sc_ref, g_ref, w_ref, q_ref, k_ref, v_ref, uv_ref, gt_ref, *, widths):
    x = x_ref[...]
    n = _rms(x, g_ref[...]) * (1.0 + sc_ref[0]) + sh_ref[0]
    nb = n.astype(BF16)
    off = 0
    for ref, wd in zip((q_ref, k_ref, v_ref, uv_ref, gt_ref), widths):
        p = _dot(nb, w_ref[:, off:off + wd])
        if ref is q_ref:
            p = p * (HEAD_DIM ** -0.5 * LOG2E)
        ref[...] = p.astype(ref.dtype)
        off += wd


def _proj_call(x2, sh, sc, g, w_in_bf, seq, widths):
    n_tok, d = x2.shape
    tm = min(PROJ_TILE, seq)
    per_b = seq // tm
    cols = w_in_bf.shape[1]
    tok = lambda w: pl.BlockSpec((tm, w), lambda i: (i, 0))
    vec = pl.BlockSpec((1, 1, d), lambda i: (i // per_b, 0, 0))
    return pl.pallas_call(
        functools.partial(_proj_kernel, widths=widths),
        out_shape=[jax.ShapeDtypeStruct((n_tok, w), BF16) for w in widths],
        grid=(n_tok // tm,),
        in_specs=[tok(d), vec, vec,
                  pl.BlockSpec((1, d), lambda i: (0, 0)),
                  pl.BlockSpec((d, cols), lambda i: (0, 0))],
        out_specs=[tok(w) for w in widths],
        compiler_params=_cparams("arbitrary"),
        name="proj",
    )(x2, sh, sc, g, w_in_bf)


def _attn_kernel(q_ref, k_ref, v_ref, u_ref, o_ref, qm_ref, acc_ref, carry_ref, *, heads, blk):
    qi = pl.program_id(1)
    u_tri = u_ref[...]
    row = lax.broadcasted_iota(jnp.int32, (blk, blk), 0)
    col = lax.broadcasted_iota(jnp.int32, (blk, blk), 1)
    causal = col < row
    per_slab = LANES // HEAD_DIM
    lane = lax.broadcasted_iota(jnp.int32, (blk, LANES), 1)
    own = [(lane >= j * HEAD_DIM) & (lane < (j + 1) * HEAD_DIM) for j in range(per_slab)]
    for h in range(heads):
        slab = slice(h // per_slab * LANES, (h // per_slab + 1) * LANES)
        qm_ref[h] = jnp.where(own[h % per_slab], q_ref[0, :, slab], 0.0).astype(BF16)

    def key_tile(start, first, dead=None):
        mask = causal if first else None
        slabs = [slice(p * LANES, (p + 1) * LANES) for p in range(heads // per_slab)]
        s_all = [lax.dot_general(qm_ref[h], k_ref[0, pl.ds(start, blk), slabs[h // per_slab]],
                                 (((1,), (1,)), ((), ())), preferred_element_type=F32) for h in range(heads)]
        cum_all = []
        for s in s_all:
            neg_abs = lax.bitcast_convert_type(lax.bitcast_convert_type(s, jnp.uint32) | jnp.uint32(SIGN_BIT), F32)
            sp = jnp.maximum(s, 0.0) + jnp.log(1.0 + jnp.exp2(neg_abs)) * LOG2E
            if mask is not None:
                sp = jnp.where(mask, sp, 0.0)
            hi = lax.bitcast_convert_type(lax.bitcast_convert_type(sp, jnp.uint32) & jnp.uint32(BF16_BITS), F32)
            cum_all.append(_dot(hi.astype(BF16), u_tri) + _dot((sp - hi).astype(BF16), u_tri))
        w_all = []
        cmin = None
        for h in range(heads):
            cum = cum_all[h]
            if first:
                w = jnp.exp2(s_all[h] - cum)
                carry = cum[:, 0:1]
            else:
                carry = carry_ref[h]
                if dead is not None:
                    carry = jnp.where(dead, DEAD_CARRY, carry)
                w = jnp.exp2((s_all[h] - carry) - cum)
                carry = carry + cum[:, 0:1]
            w_all.append((w if mask is None else jnp.where(mask, w, 0.0)).astype(BF16))
            carry_ref[h] = carry
            cmin = carry if cmin is None else jnp.minimum(cmin, carry)
        for p, slab in enumerate(slabs):
            vp = v_ref[0, pl.ds(start, blk), slab]
            upd = None
            for j in range(per_slab):
                pv = _dot(w_all[p * per_slab + j], vp)
                upd = pv if upd is None else jnp.where(own[j], pv, upd)
            acc_ref[:, slab] = upd if first else acc_ref[:, slab] + upd
        return jnp.min(cmin)

    key_tile(pl.multiple_of(qi * blk, blk), True)
    cmin = key_tile(pl.multiple_of(jnp.maximum(qi - 1, 0) * blk, blk), False, dead=qi == 0)

    def cond(state):
        i, cmin = state
        return (i < qi) & (cmin < ATTN_EXIT_LOG2)

    def body(state):
        i, _ = state
        return i + 1, key_tile(pl.multiple_of((qi - 1 - i) * blk, blk), False)

    lax.while_loop(cond, body, (jnp.int32(1), cmin))
    o_ref[0] = acc_ref[...].astype(o_ref.dtype)


def _attn_call(q, k, v, heads):
    bsz, seq, width = q.shape
    blk = min(ATTN_BLOCK, seq)
    r = lax.broadcasted_iota(jnp.int32, (blk, blk), 0)
    c = lax.broadcasted_iota(jnp.int32, (blk, blk), 1)
    u_tri = (r >= c).astype(BF16)
    full = pl.BlockSpec((1, seq, width), lambda b, i: (b, 0, 0))
    return pl.pallas_call(
        functools.partial(_attn_kernel, heads=heads, blk=blk),
        out_shape=jax.ShapeDtypeStruct((bsz, seq, width), BF16),
        grid=(bsz, seq // blk),
        in_specs=[pl.BlockSpec((1, blk, width), lambda b, i: (b, i, 0)), full, full,
                  pl.BlockSpec((blk, blk), lambda b, i: (0, 0))],
        out_specs=pl.BlockSpec((1, blk, width), lambda b, i: (b, i, 0)),
        scratch_shapes=[pltpu.VMEM((heads, blk, LANES), BF16), pltpu.VMEM((blk, width), F32),
                        pltpu.VMEM((heads, blk, 1), F32)],
        compiler_params=_cparams("arbitrary", "arbitrary"),
        name="attn",
    )(q, k, v, u_tri)


def _gelu_tanh(x):
    c = 0.7978845608028654
    hx = 0.5 * x
    return hx + hx * jnp.tanh(x * (c + (0.044715 * c) * (x * x)))


def _sigmoid(x):
    return 1.0 / (1.0 + jnp.exp(-x))


def _mix_kernel(ya_ref, uv_ref, gt_ref, x_ref, gtm_ref, shf_ref, scf_ref, gsgu_ref, ws_ref, bs_ref,
                wa_ref, wb_ref, wo_ref, gffn_ref, wr_ref, br_ref, utri_ref,
                h_ref, xn_ref, route_ref, route_t_ref, cnt_ref, yb_scr, cnt_scr,
                *, chunk, groups, n_groups, per_group):
    step = pl.program_id(0)
    tm = x_ref.shape[0]
    sgu = gsgu_ref.shape[1]
    gdim = sgu // groups

    @pl.when(step == 0)
    def _():
        cnt_scr[...] = jnp.zeros_like(cnt_scr)

    ya_proj = _dot(ya_ref[...], wa_ref[...])

    act = _gelu_tanh(uv_ref[...].astype(F32))
    u = act[:, :sgu]
    v = act[:, sgu:]
    mu = jnp.mean(v, axis=-1, keepdims=True)
    vc = v - mu
    var = jnp.mean(vc * vc, axis=-1, keepdims=True)
    vn = vc * lax.rsqrt(var + EPS) * gsgu_ref[...]
    r = lax.broadcasted_iota(jnp.int32, (chunk, chunk), 0)
    c = lax.broadcasted_iota(jnp.int32, (chunk, chunk), 1)
    wcs = [jnp.where(r >= c, ws_ref[g], 0.0).astype(BF16) for g in range(groups)]
    pair = LANES // gdim
    lane = lax.broadcasted_iota(jnp.int32, (chunk, LANES), 1)
    for ci in range(tm // chunk):
        rows = slice(ci * chunk, (ci + 1) * chunk)
        for p in range(groups // pair):
            lanes = slice(p * LANES, (p + 1) * LANES)
            slab = vn[rows, lanes]
            mix = bs_ref[:, lanes]
            for j in range(pair):
                sel = (lane >= j * gdim) & (lane < (j + 1) * gdim)
                mix = mix + _dot(wcs[p * pair + j], jnp.where(sel, slab, 0.0).astype(BF16))
            yb_scr[rows, lanes] = (u[rows, lanes] * mix).astype(BF16)

    d = x_ref.shape[1]
    gates2 = 1.0 + jnp.tanh(0.5 * gt_ref[...].astype(F32))
    merged2 = gates2[:, :d] * ya_proj + gates2[:, d:] * _dot(yb_scr[...], wb_ref[...])
    h = x_ref[...] + (0.5 * gtm_ref[0]) * _dot(merged2.astype(BF16), wo_ref[...])
    h_ref[...] = h

    xn = _rms(h, gffn_ref[...]) * (1.0 + scf_ref[0]) + shf_ref[0]
    xn_ref[...] = _pack_halves(xn)
    logits = _dot3(xn, wr_ref[...]) + br_ref[...]

    lt = logits.T
    sub = lax.broadcasted_iota(jnp.int32, (SUBLANES, tm), 0)
    neg = -jnp.inf
    gl = jnp.where(sub < n_groups, lt[0:SUBLANES], neg)
    gmax = jnp.max(gl, axis=0, keepdims=True)
    g_sel = jnp.min(jnp.where(gl == gmax, sub, SUBLANES), axis=0, keepdims=True)
    g_w = 1.0 / jnp.sum(jnp.exp(gl - gmax), axis=0, keepdims=True)
    el = lt[SUBLANES:2 * SUBLANES]
    for g in range(1, n_groups):
        el = jnp.where(g_sel == g, lt[(g + 1) * SUBLANES:(g + 2) * SUBLANES], el)
    m1 = jnp.max(el, axis=0, keepdims=True)
    j1 = jnp.min(jnp.where(el == m1, sub, SUBLANES), axis=0, keepdims=True)
    el2 = jnp.where(sub == j1, neg, el)
    m2 = jnp.max(el2, axis=0, keepdims=True)
    j2 = jnp.min(jnp.where(el2 == m2, sub, SUBLANES), axis=0, keepdims=True)
    t = jnp.exp(m2 - m1)
    w1 = g_w / (1.0 + t)
    w2 = g_w * t / (1.0 + t)
    e1 = g_sel * per_group + j1
    e2 = g_sel * per_group + j2

    row = lax.broadcasted_iota(jnp.int32, (LANES, tm), 0)
    oh1 = row == e1 + SUBLANES
    oh2 = row == e2 + SUBLANES
    both = jnp.where(oh1 | oh2, 1.0, 0.0)
    before = _dot(both.astype(BF16), utri_ref[...]) + cnt_scr[...]
    rank1 = jnp.sum(jnp.where(oh1, before, 0.0), axis=0, keepdims=True)
    rank2 = jnp.sum(jnp.where(oh2, before, 0.0), axis=0, keepdims=True)
    cnt_scr[...] = cnt_scr[...] + jnp.sum(both, axis=1, keepdims=True)
    cnt_ref[...] = cnt_scr[...]

    rec = jnp.zeros((LANES, tm), F32)
    for idx, val in enumerate((e1.astype(F32), e2.astype(F32), w1, w2, rank1, rank2)):
        rec = jnp.where(row == idx, val, rec)
    route_t_ref[...] = rec[:ROUTE_COLS]
    route_ref[...] = rec.T[:, :ROUTE_COLS]


def _mix_call(ya, uv, gates, x2, gtm, shf, scf, g_sgu, w_spatial, bias_full, wa, wb, wo, g_ffn, wr, br,
              seq, n_groups, per_group):
    n_tok, d = x2.shape
    tm = min(TOKEN_TILE, seq)
    per_b = seq // tm
    groups, chunk, _ = w_spatial.shape
    sgu = g_sgu.shape[1]
    r = lax.broadcasted_iota(jnp.int32, (tm, tm), 0)
    c = lax.broadcasted_iota(jnp.int32, (tm, tm), 1)
    utri = (r < c).astype(BF16)
    tok = lambda w: pl.BlockSpec((tm, w), lambda i: (i, 0))
    vec = pl.BlockSpec((1, 1, d), lambda i: (i // per_b, 0, 0))
    const = lambda a: pl.BlockSpec(a.shape, lambda i: (0,) * a.ndim)
    ins = [ya, uv, gates, x2, gtm, shf, scf, g_sgu, w_spatial, bias_full, wa, wb, wo, g_ffn, wr, br, utri]
    in_specs = [tok(ya.shape[1]), tok(uv.shape[1]), tok(gates.shape[1]), tok(d), vec, vec, vec]
    in_specs += [const(a) for a in ins[7:]]
    return pl.pallas_call(
        functools.partial(_mix_kernel, chunk=chunk, groups=groups, n_groups=n_groups, per_group=per_group),
        out_shape=[jax.ShapeDtypeStruct((n_tok, d), F32), jax.ShapeDtypeStruct((n_tok, d // 2), jnp.uint32),
                   jax.ShapeDtypeStruct((n_tok, ROUTE_COLS), F32), jax.ShapeDtypeStruct((ROUTE_COLS, n_tok), F32),
                   jax.ShapeDtypeStruct((LANES, 1), F32)],
        grid=(n_tok // tm,),
        in_specs=in_specs,
        out_specs=[tok(d), tok(d // 2), tok(ROUTE_COLS), pl.BlockSpec((ROUTE_COLS, tm), lambda i: (0, i)),
                   pl.BlockSpec((LANES, 1), lambda i: (0, 0))],
        scratch_shapes=[pltpu.VMEM((tm, sgu), BF16), pltpu.VMEM((LANES, 1), F32)],
        compiler_params=_cparams("arbitrary"),
        name="mix",
    )(*ins)


def _row_copy(src, s, dst, d, sem):
    return pltpu.make_async_copy(src.at[pl.ds(s, 1)], dst.at[pl.ds(d, 1)], sem)


def _dispatch_kernel(zblk_ref, *refs, tm, rows):
    dest_refs, (xn_ref, xs_ref, zero_scr, sem, zsem) = refs[:TOP_K], refs[TOP_K:]

    @pl.when(pl.program_id(0) == 0)
    def _():
        zero_scr[...] = jnp.zeros_like(zero_scr)

        def zero_copy(j):
            start = pl.multiple_of(zblk_ref[j] * rows, rows)
            return pltpu.make_async_copy(zero_scr, xs_ref.at[pl.ds(start, rows)], zsem)

        for j in range(zblk_ref.shape[0]):
            pl.when(zblk_ref[j] >= 0)(lambda j=j: zero_copy(j).start())
        for j in range(zblk_ref.shape[0]):
            pl.when(zblk_ref[j] >= 0)(lambda j=j: zero_copy(j).wait())

    for j in range(tm):
        for k, dref in enumerate(dest_refs):
            _row_copy(xn_ref, j, xs_ref, dref[0, 0, j], sem).start(priority=k % 2)
    for _ in range(TOP_K):
        pltpu.make_async_copy(xn_ref, xs_ref.at[pl.ds(0, tm)], sem).wait()


def _dest_blocks(dest, tm):
    return [dest[k].reshape(-1, 1, tm) for k in range(TOP_K)]


def _dispatch_call(zero_blocks, dest, xn, m_pad, seq):
    n_tok, dw = xn.shape
    tm = min(ROW_COPY_TILE, seq)
    smem = pl.BlockSpec((1, 1, tm), lambda i, zb: (i, 0, 0), memory_space=pltpu.SMEM)
    return pl.pallas_call(
        functools.partial(_dispatch_kernel, tm=tm, rows=EXPERT_ROWS),
        out_shape=jax.ShapeDtypeStruct((m_pad, dw), xn.dtype),
        grid_spec=pltpu.PrefetchScalarGridSpec(
            num_scalar_prefetch=1,
            grid=(n_tok // tm,),
            in_specs=[smem] * TOP_K + [pl.BlockSpec((tm, dw), lambda i, zb: (i, 0))],
            out_specs=pl.BlockSpec(memory_space=pl.ANY),
            scratch_shapes=[pltpu.VMEM((EXPERT_ROWS, dw), xn.dtype), pltpu.SemaphoreType.DMA(()),
                            pltpu.SemaphoreType.DMA(())]),
        compiler_params=_cparams("arbitrary"),
        name="dispatch",
    )(zero_blocks, *_dest_blocks(dest, tm), xn)


def _expert_kernel(blk_e_ref, used_ref, xs_ref, wg_ref, wu_ref, wd_ref, y_ref, wg_bf, wu_bf, wd_bf):
    i = pl.program_id(0)
    live = i < used_ref[0]
    new_expert = (i == 0) | (blk_e_ref[i] != blk_e_ref[jnp.maximum(i - 1, 0)])

    @pl.when(live & new_expert)
    def _():
        wg_bf[...] = wg_ref[0].astype(BF16)
        wu_bf[...] = wu_ref[0].astype(BF16)
        wd_bf[...] = wd_ref[0].astype(BF16)

    @pl.when(live)
    def _():
        xb = _unpack_halves(xs_ref[...]).astype(BF16)
        g = _dot(xb, wg_bf[...])
        hid = g * _sigmoid(g) * _dot(xb, wu_bf[...])
        y_ref[...] = _pack_halves(_dot(hid.astype(BF16), wd_bf[...]))

    @pl.when(jnp.logical_not(live))
    def _():
        y_ref[...] = jnp.zeros_like(y_ref)


def _expert_call(blk_e, used, xs, wg, wu, wd):
    m_pad, dw = xs.shape
    _, d, f = wg.shape
    rows = EXPERT_ROWS
    return pl.pallas_call(
        _expert_kernel,
        out_shape=jax.ShapeDtypeStruct((m_pad, dw), xs.dtype),
        grid_spec=pltpu.PrefetchScalarGridSpec(
            num_scalar_prefetch=2,
            grid=(m_pad // rows,),
            in_specs=[pl.BlockSpec((rows, dw), lambda i, be, us: (i, 0)),
                      pl.BlockSpec((1, d, f), lambda i, be, us: (be[i], 0, 0)),
                      pl.BlockSpec((1, d, f), lambda i, be, us: (be[i], 0, 0)),
                      pl.BlockSpec((1, f, d), lambda i, be, us: (be[i], 0, 0))],
            out_specs=pl.BlockSpec((rows, dw), lambda i, be, us: (i, 0)),
            scratch_shapes=[pltpu.VMEM((d, f), BF16), pltpu.VMEM((d, f), BF16), pltpu.VMEM((f, d), BF16)]),
        compiler_params=_cparams("arbitrary"),
        name="experts",
    )(blk_e, used, xs, wg, wu, wd)


def _combine_kernel(*refs, tm, final):
    cur_refs, nxt_refs = refs[:TOP_K], refs[TOP_K:2 * TOP_K]
    h_ref, route_ref, gtf_ref, gfin_ref, y_ref, o_ref, buf, sems = refs[2 * TOP_K:]
    step = pl.program_id(0)
    slot = step % 2

    def gather(dest_refs, s):
        for j in range(tm):
            for k, dref in enumerate(dest_refs):
                pltpu.make_async_copy(y_ref.at[pl.ds(dref[0, 0, j], 1)], buf.at[s, k, pl.ds(j, 1)],
                                      sems.at[s]).start(priority=k % 2)

    pl.when(step == 0)(lambda: gather(cur_refs, 0))
    pl.when(step + 1 < pl.num_programs(0))(lambda: gather(nxt_refs, 1 - slot))
    for k in range(TOP_K):
        pltpu.make_async_copy(y_ref.at[pl.ds(0, tm)], buf.at[slot, k], sems.at[slot]).wait()
    route = route_ref[...]
    y = route[:, 2:3] * _unpack_halves(buf[slot, 0]) + route[:, 3:4] * _unpack_halves(buf[slot, 1])
    h = h_ref[...] + gtf_ref[0] * y
    o_ref[...] = _rms(h, gfin_ref[...]) if final else h


def _combine_call(dest, h, route, gtf, g_final, yb, seq, final):
    n_tok, d = h.shape
    tm = min(ROW_COPY_TILE, seq)
    per_b = seq // tm
    steps = n_tok // tm
    tok = lambda w: pl.BlockSpec((tm, w), lambda i: (i, 0))
    cur = pl.BlockSpec((1, 1, tm), lambda i: (i, 0, 0), memory_space=pltpu.SMEM)
    nxt = pl.BlockSpec((1, 1, tm), lambda i: (jnp.minimum(i + 1, steps - 1), 0, 0), memory_space=pltpu.SMEM)
    blocks = _dest_blocks(dest, tm)
    return pl.pallas_call(
        functools.partial(_combine_kernel, tm=tm, final=final),
        out_shape=jax.ShapeDtypeStruct((n_tok, d), F32),
        grid=(steps,),
        in_specs=[cur] * TOP_K + [nxt] * TOP_K + [
            tok(d), tok(ROUTE_COLS),
            pl.BlockSpec((1, 1, d), lambda i: (i // per_b, 0, 0)),
            pl.BlockSpec((1, d), lambda i: (0, 0)),
            pl.BlockSpec(memory_space=pl.ANY)],
        out_specs=tok(d),
        scratch_shapes=[pltpu.VMEM((2, TOP_K, tm, yb.shape[1]), yb.dtype), pltpu.SemaphoreType.DMA((2,))],
        compiler_params=_cparams("arbitrary"),
        name="combine",
    )(*blocks, *blocks, h, route, gtf, g_final, yb)


def _routing_plan(route_t, counts, n_experts, m_pad):
    eid = route_t[0:TOP_K].astype(jnp.int32)
    rank = route_t[4:4 + TOP_K].astype(jnp.int32)
    cnt = counts.astype(jnp.int32)
    padded = (cnt + EXPERT_ROWS - 1) // EXPERT_ROWS * EXPERT_ROWS
    pend = jnp.cumsum(padded)
    pstart = pend - padded
    dest = rank
    for e in range(n_experts):
        dest = dest + jnp.where(eid == e, pstart[e], 0)
    nblk = m_pad // EXPERT_ROWS
    blk_row = jnp.arange(nblk, dtype=jnp.int32) * EXPERT_ROWS
    blk_e = jnp.minimum(jnp.sum(pend[None, :] <= blk_row[:, None], axis=1), n_experts - 1).astype(jnp.int32)
    used = (pend[-1:] // EXPERT_ROWS).astype(jnp.int32)
    last_blk = jnp.where(padded > 0, pend // EXPERT_ROWS - 1, -1)
    tail_blk = used + jnp.arange(n_experts, dtype=jnp.int32)
    tail_blk = jnp.where(tail_blk < nblk, tail_blk, -1)
    zero_blocks = jnp.concatenate([last_blk, tail_blk]).astype(jnp.int32)
    return dest, blk_e, used, zero_blocks


def kernel(x, c, g_mix, g_ffn, w_ada, b_ada, w_in, w_sba_out, g_sgu, w_spatial, b_spatial, w_sgu_out, w_out,
           w_router_group, b_router_group, w_router_expert, b_router_expert, w_expert_gate, w_expert_up,
           w_expert_down, g_final):
    bsz, seq, d = x.shape
    depth = w_in.shape[0]
    sba = w_sba_out.shape[1]
    sgu = g_sgu.shape[1]
    heads = sba // HEAD_DIM
    groups, chunk = w_spatial.shape[1], w_spatial.shape[2]
    n_groups = w_router_group.shape[2]
    n_experts = w_router_expert.shape[2]
    per_group = n_experts // n_groups
    n_tok = bsz * seq
    m_pad = n_tok * TOP_K + n_experts * EXPERT_ROWS
    widths = (sba, sba, sba, 2 * sgu, 2 * d)
    assert seq % chunk == 0 and LANES % (sgu // groups) == 0
    assert d % (2 * LANES) == 0
    assert per_group == SUBLANES and n_groups <= SUBLANES and (n_groups + 1) * SUBLANES <= LANES

    h = x.reshape(n_tok, d)
    for l in range(depth):
        mod = _mod_call(c, w_ada[l], b_ada[l])
        sh_m, sc_m, gt_m, sh_f, sc_f, gt_f = [mod[:, i * d:(i + 1) * d].reshape(bsz, 1, d) for i in range(6)]

        q, k, v, uv, gates = _proj_call(h, sh_m, sc_m, g_mix[l].reshape(1, d), w_in[l].astype(BF16), seq, widths)
        ya = _attn_call(q.reshape(bsz, seq, sba), k.reshape(bsz, seq, sba), v.reshape(bsz, seq, sba), heads)

        bias_full = jnp.repeat(b_spatial[l].T, sgu // groups, axis=1)
        gpad = SUBLANES - n_groups
        wr = jnp.concatenate([jnp.pad(w_router_group[l], ((0, 0), (0, gpad))), w_router_expert[l]], axis=1)
        wr = jnp.pad(wr, ((0, 0), (0, LANES - wr.shape[1])))
        br = jnp.concatenate([jnp.pad(b_router_group[l], (0, gpad)), b_router_expert[l]])
        br = jnp.pad(br, (0, LANES - br.shape[0])).reshape(1, LANES)
        h1, xn, route, route_t, counts = _mix_call(
            ya.reshape(n_tok, sba), uv, gates, h, gt_m, sh_f, sc_f, g_sgu[l].reshape(1, sgu), w_spatial[l],
            bias_full, w_sba_out[l].astype(BF16), w_sgu_out[l].astype(BF16), w_out[l].astype(BF16),
            g_ffn[l].reshape(1, d), wr, br, seq, n_groups, per_group)

        dest, blk_e, used, zero_blocks = _routing_plan(route_t, counts[SUBLANES:SUBLANES + n_experts, 0],
                                                       n_experts, m_pad)
        xs = _dispatch_call(zero_blocks, dest, xn, m_pad, seq)
        yb = _expert_call(blk_e, used, xs, w_expert_gate[l], w_expert_up[l], w_expert_down[l])
        h = _combine_call(dest, h1, route, gt_f, g_final.reshape(1, d), yb, seq, final=l == depth - 1)
    return h.reshape(bsz, seq, d)
```

```python
import functools

import jax
import jax.numpy as jnp
from jax import lax
from jax.experimental import pallas as pl
from jax.experimental.pallas import tpu as pltpu

F32 = jnp.float32
BF16 = jnp.bfloat16

HEAD_DIM = 64
TOP_K = 2
EPS = 1e-6
LOG2E = 1.4426950408889634
SIGN_BIT = 0x80000000
BF16_BITS = 0xFFFF0000
ATTN_EXIT_LOG2 = 160.0
DEAD_CARRY = 1e30
LANES = 128
SUBLANES = 8
ATTN_BLOCK = 256
EXPERT_ROWS = 512
TOKEN_TILE = 512
ROW_COPY_TILE = 512
PROJ_TILE = 512
ROUTE_COLS = 8
VMEM_LIMIT = 56 * 1024 * 1024


def _cparams(*sem):
    return pltpu.CompilerParams(dimension_semantics=sem, vmem_limit_bytes=VMEM_LIMIT)


def _split_bf16(a):
    hi = a.astype(BF16)
    lo = (a - hi.astype(F32)).astype(BF16)
    return hi, lo


def _dot(a, b):
    return jnp.dot(a, b, preferred_element_type=F32)


def _dot3(a, b):
    ah, al = _split_bf16(a)
    bh, bl = _split_bf16(b)
    return _dot(ah, bh) + _dot(ah, bl) + _dot(al, bh)


def _pack_halves(x):
    half = x.shape[1] // 2
    hi = lax.bitcast_convert_type(x[:, :half].astype(BF16).astype(F32), jnp.uint32)
    lo = lax.bitcast_convert_type(x[:, half:].astype(BF16).astype(F32), jnp.uint32)
    return hi | (lo >> 16)


def _unpack_halves(u):
    hi = lax.bitcast_convert_type(u & jnp.uint32(BF16_BITS), F32)
    lo = lax.bitcast_convert_type(u << 16, F32)
    return jnp.concatenate([hi, lo], axis=1)


def _rms(x, g):
    ms = jnp.mean(x * x, axis=-1, keepdims=True)
    return x * lax.rsqrt(ms + EPS) * g


def _mod_kernel(c_ref, w_ref, b_ref, o_ref):
    c = c_ref[...]
    ca = c * (1.0 / (1.0 + jnp.exp(-c)))
    o_ref[...] = _dot3(ca, w_ref[...]) + b_ref[...]


def _mod_call(c, w_ada, b_ada):
    bsz, d = c.shape
    n = w_ada.shape[1]
    tn = n // 4 if n % (4 * LANES) == 0 else n
    return pl.pallas_call(
        _mod_kernel,
        out_shape=jax.ShapeDtypeStruct((bsz, n), F32),
        grid=(n // tn,),
        in_specs=[pl.BlockSpec((bsz, d), lambda j: (0, 0)),
                  pl.BlockSpec((d, tn), lambda j: (0, j)),
                  pl.BlockSpec((1, tn), lambda j: (0, j))],
        out_specs=pl.BlockSpec((bsz, tn), lambda j: (0, j)),
        compiler_params=_cparams("arbitrary"),
        name="mod",
    )(c, w_ada, b_ada.reshape(1, n))


def _proj_kernel(x_ref, sh_ref, sc_ref, g_ref, w_ref, q_ref, k_ref, v_ref, uv_ref, gt_ref, *, widths):
    x = x_ref[...]
    n = _rms(x, g_ref[...]) * (1.0 + sc_ref[0]) + sh_ref[0]
    nb = n.astype(BF16)
    off = 0
    for ref, wd in zip((q_ref, k_ref, v_ref, uv_ref, gt_ref), widths):
        p = _dot(nb, w_ref[:, off:off + wd])
        if ref is q_ref:
            p = p * (HEAD_DIM ** -0.5 * LOG2E)
        ref[...] = p.astype(ref.dtype)
        off += wd


def _proj_call(x2, sh, sc, g, w_in_bf, seq, widths):
    n_tok, d = x2.shape
    tm = min(PROJ_TILE, seq)
    per_b = seq // tm
    cols = w_in_bf.shape[1]
    tok = lambda w: pl.BlockSpec((tm, w), lambda i: (i, 0))
    vec = pl.BlockSpec((1, 1, d), lambda i: (i // per_b, 0, 0))
    return pl.pallas_call(
        functools.partial(_proj_kernel, widths=widths),
        out_shape=[jax.ShapeDtypeStruct((n_tok, w), BF16) for w in widths],
        grid=(n_tok // tm,),
        in_specs=[tok(d), vec, vec,
                  pl.BlockSpec((1, d), lambda i: (0, 0)),
                  pl.BlockSpec((d, cols), lambda i: (0, 0))],
        out_specs=[tok(w) for w in widths],
        compiler_params=_cparams("arbitrary"),
        name="proj",
    )(x2, sh, sc, g, w_in_bf)


def _attn_kernel(q_ref, k_ref, v_ref, u_ref, o_ref, qm_ref, acc_ref, carry_ref, *, heads, blk):
    qi = pl.program_id(1)
    u_tri = u_ref[...]
    row = lax.broadcasted_iota(jnp.int32, (blk, blk), 0)
    col = lax.broadcasted_iota(jnp.int32, (blk, blk), 1)
    causal = col < row
    per_slab = LANES // HEAD_DIM
    lane = lax.broadcasted_iota(jnp.int32, (blk, LANES), 1)
    own = [(lane >= j * HEAD_DIM) & (lane < (j + 1) * HEAD_DIM) for j in range(per_slab)]
    for h in range(heads):
        slab = slice(h // per_slab * LANES, (h // per_slab + 1) * LANES)
        qm_ref[h] = jnp.where(own[h % per_slab], q_ref[0, :, slab], 0.0).astype(BF16)

    def key_tile(start, first, dead=None):
        mask = causal if first else None
        slabs = [slice(p * LANES, (p + 1) * LANES) for p in range(heads // per_slab)]
        s_all = [lax.dot_general(qm_ref[h], k_ref[0, pl.ds(start, blk), slabs[h // per_slab]],
                                 (((1,), (1,)), ((), ())), preferred_element_type=F32) for h in range(heads)]
        cum_all = []
        for s in s_all:
            neg_abs = lax.bitcast_convert_type(lax.bitcast_convert_type(s, jnp.uint32) | jnp.uint32(SIGN_BIT), F32)
            sp = jnp.maximum(s, 0.0) + jnp.log(1.0 + jnp.exp2(neg_abs)) * LOG2E
            if mask is not None:
                sp = jnp.where(mask, sp, 0.0)
            hi = lax.bitcast_convert_type(lax.bitcast_convert_type(sp, jnp.uint32) & jnp.uint32(BF16_BITS), F32)
            cum_all.append(_dot(hi.astype(BF16), u_tri) + _dot((sp - hi).astype(BF16), u_tri))
        w_all = []
        cmin = None
        for h in range(heads):
            cum = cum_all[h]
            if first:
                w = jnp.exp2(s_all[h] - cum)
                carry = cum[:, 0:1]
            else:
                carry = carry_ref[h]
                if dead is not None:
                    carry = jnp.where(dead, DEAD_CARRY, carry)
                w = jnp.exp2((s_all[h] - carry) - cum)
                carry = carry + cum[:, 0:1]
            w_all.append((w if mask is None else jnp.where(mask, w, 0.0)).astype(BF16))
            carry_ref[h] = carry
            cmin = carry if cmin is None else jnp.minimum(cmin, carry)
        for p, slab in enumerate(slabs):
            vp = v_ref[0, pl.ds(start, blk), slab]
            upd = None
            for j in range(per_slab):
                pv = _dot(w_all[p * per_slab + j], vp)
                upd = pv if upd is None else jnp.where(own[j], pv, upd)
            acc_ref[:, slab] = upd if first else acc_ref[:, slab] + upd
        return jnp.min(cmin)

    key_tile(pl.multiple_of(qi * blk, blk), True)
    cmin = key_tile(pl.multiple_of(jnp.maximum(qi - 1, 0) * blk, blk), False, dead=qi == 0)

    def cond(state):
        i, cmin = state
        return (i < qi) & (cmin < ATTN_EXIT_LOG2)

    def body(state):
        i, _ = state
        return i + 1, key_tile(pl.multiple_of((qi - 1 - i) * blk, blk), False)

    lax.while_loop(cond, body, (jnp.int32(1), cmin))
    o_ref[0] = acc_ref[...].astype(o_ref.dtype)


def _attn_call(q, k, v, heads):
    bsz, seq, width = q.shape
    blk = min(ATTN_BLOCK, seq)
    r = lax.broadcasted_iota(jnp.int32, (blk, blk), 0)
    c = lax.broadcasted_iota(jnp.int32, (blk, blk), 1)
    u_tri = (r >= c).astype(BF16)
    full = pl.BlockSpec((1, seq, width), lambda b, i: (b, 0, 0))
    return pl.pallas_call(
        functools.partial(_attn_kernel, heads=heads, blk=blk),
        out_shape=jax.ShapeDtypeStruct((bsz, seq, width), BF16),
        grid=(bsz, seq // blk),
        in_specs=[pl.BlockSpec((1, blk, width), lambda b, i: (b, i, 0)), full, full,
                  pl.BlockSpec((blk, blk), lambda b, i: (0, 0))],
        out_specs=pl.BlockSpec((1, blk, width), lambda b, i: (b, i, 0)),
        scratch_shapes=[pltpu.VMEM((heads, blk, LANES), BF16), pltpu.VMEM((blk, width), F32),
                        pltpu.VMEM((heads, blk, 1), F32)],
        compiler_params=_cparams("arbitrary", "arbitrary"),
        name="attn",
    )(q, k, v, u_tri)


def _gelu_tanh(x):
    c = 0.7978845608028654
    hx = 0.5 * x
    return hx + hx * jnp.tanh(x * (c + (0.044715 * c) * (x * x)))


def _sigmoid(x):
    return 1.0 / (1.0 + jnp.exp(-x))


def _mix_kernel(ya_ref, uv_ref, gt_ref, x_ref, gtm_ref, shf_ref, scf_ref, gsgu_ref, ws_ref, bs_ref,
                wa_ref, wb_ref, wo_ref, gffn_ref, wr_ref, br_ref, utri_ref,
                h_ref, xn_ref, route_ref, route_t_ref, cnt_ref, yb_scr, cnt_scr,
                *, chunk, groups, n_groups, per_group):
    step = pl.program_id(0)
    tm = x_ref.shape[0]
    sgu = gsgu_ref.shape[1]
    gdim = sgu // groups

    @pl.when(step == 0)
    def _():
        cnt_scr[...] = jnp.zeros_like(cnt_scr)

    ya_proj = _dot(ya_ref[...], wa_ref[...])

    act = _gelu_tanh(uv_ref[...].astype(F32))
    u = act[:, :sgu]
    v = act[:, sgu:]
    mu = jnp.mean(v, axis=-1, keepdims=True)
    vc = v - mu
    var = jnp.mean(vc * vc, axis=-1, keepdims=True)
    vn = vc * lax.rsqrt(var + EPS) * gsgu_ref[...]
    r = lax.broadcasted_iota(jnp.int32, (chunk, chunk), 0)
    c = lax.broadcasted_iota(jnp.int32, (chunk, chunk), 1)
    wcs = [jnp.where(r >= c, ws_ref[g], 0.0).astype(BF16) for g in range(groups)]
    pair = LANES // gdim
    lane = lax.broadcasted_iota(jnp.int32, (chunk, LANES), 1)
    for ci in range(tm // chunk):
        rows = slice(ci * chunk, (ci + 1) * chunk)
        for p in range(groups // pair):
            lanes = slice(p * LANES, (p + 1) * LANES)
            slab = vn[rows, lanes]
            mix = bs_ref[:, lanes]
            for j in range(pair):
                sel = (lane >= j * gdim) & (lane < (j + 1) * gdim)
                mix = mix + _dot(wcs[p * pair + j], jnp.where(sel, slab, 0.0).astype(BF16))
            yb_scr[rows, lanes] = (u[rows, lanes] * mix).astype(BF16)

    d = x_ref.shape[1]
    gates2 = 1.0 + jnp.tanh(0.5 * gt_ref[...].astype(F32))
    merged2 = gates2[:, :d] * ya_proj + gates2[:, d:] * _dot(yb_scr[...], wb_ref[...])
    h = x_ref[...] + (0.5 * gtm_ref[0]) * _dot(merged2.astype(BF16), wo_ref[...])
    h_ref[...] = h

    xn = _rms(h, gffn_ref[...]) * (1.0 + scf_ref[0]) + shf_ref[0]
    xn_ref[...] = _pack_halves(xn)
    logits = _dot3(xn, wr_ref[...]) + br_ref[...]

    lt = logits.T
    sub = lax.broadcasted_iota(jnp.int32, (SUBLANES, tm), 0)
    neg = -jnp.inf
    gl = jnp.where(sub < n_groups, lt[0:SUBLANES], neg)
    gmax = jnp.max(gl, axis=0, keepdims=True)
    g_sel = jnp.min(jnp.where(gl == gmax, sub, SUBLANES), axis=0, keepdims=True)
    g_w = 1.0 / jnp.sum(jnp.exp(gl - gmax), axis=0, keepdims=True)
    el = lt[SUBLANES:2 * SUBLANES]
    for g in range(1, n_groups):
        el = jnp.where(g_sel == g, lt[(g + 1) * SUBLANES:(g + 2) * SUBLANES], el)
    m1 = jnp.max(el, axis=0, keepdims=True)
    j1 = jnp.min(jnp.where(el == m1, sub, SUBLANES), axis=0, keepdims=True)
    el2 = jnp.where(sub == j1, neg, el)
    m2 = jnp.max(el2, axis=0, keepdims=True)
    j2 = jnp.min(jnp.where(el2 == m2, sub, SUBLANES), axis=0, keepdims=True)
    t = jnp.exp(m2 - m1)
    w1 = g_w / (1.0 + t)
    w2 = g_w * t / (1.0 + t)
    e1 = g_sel * per_group + j1
    e2 = g_sel * per_group + j2

    row = lax.broadcasted_iota(jnp.int32, (LANES, tm), 0)
    oh1 = row == e1 + SUBLANES
    oh2 = row == e2 + SUBLANES
    both = jnp.where(oh1 | oh2, 1.0, 0.0)
    before = _dot(both.astype(BF16), utri_ref[...]) + cnt_scr[...]
    rank1 = jnp.sum(jnp.where(oh1, before, 0.0), axis=0, keepdims=True)
    rank2 = jnp.sum(jnp.where(oh2, before, 0.0), axis=0, keepdims=True)
    cnt_scr[...] = cnt_scr[...] + jnp.sum(both, axis=1, keepdims=True)
    cnt_ref[...] = cnt_scr[...]

    rec = jnp.zeros((LANES, tm), F32)
    for idx, val in enumerate((e1.astype(F32), e2.astype(F32), w1, w2, rank1, rank2)):
        rec = jnp.where(row == idx, val, rec)
    route_t_ref[...] = rec[:ROUTE_COLS]
    route_ref[...] = rec.T[:, :ROUTE_COLS]


def _mix_call(ya, uv, gates, x2, gtm, shf, scf, g_sgu, w_spatial, bias_full, wa, wb, wo, g_ffn, wr, br,
              seq, n_groups, per_group):
    n_tok, d = x2.shape
    tm = min(TOKEN_TILE, seq)
    per_b = seq // tm
    groups, chunk, _ = w_spatial.shape
    sgu = g_sgu.shape[1]
    r = lax.broadcasted_iota(jnp.int32, (tm, tm), 0)
    c = lax.broadcasted_iota(jnp.int32, (tm, tm), 1)
    utri = (r < c).astype(BF16)
    tok = lambda w: pl.BlockSpec((tm, w), lambda i: (i, 0))
    vec = pl.BlockSpec((1, 1, d), lambda i: (i // per_b, 0, 0))
    const = lambda a: pl.BlockSpec(a.shape, lambda i: (0,) * a.ndim)
    ins = [ya, uv, gates, x2, gtm, shf, scf, g_sgu, w_spatial, bias_full, wa, wb, wo, g_ffn, wr, br, utri]
    in_specs = [tok(ya.shape[1]), tok(uv.shape[1]), tok(gates.shape[1]), tok(d), vec, vec, vec]
    in_specs += [const(a) for a in ins[7:]]
    return pl.pallas_call(
        functools.partial(_mix_kernel, chunk=chunk, groups=groups, n_groups=n_groups, per_group=per_group),
        out_shape=[jax.ShapeDtypeStruct((n_tok, d), F32), jax.ShapeDtypeStruct((n_tok, d // 2), jnp.uint32),
                   jax.ShapeDtypeStruct((n_tok, ROUTE_COLS), F32), jax.ShapeDtypeStruct((ROUTE_COLS, n_tok), F32),
                   jax.ShapeDtypeStruct((LANES, 1), F32)],
        grid=(n_tok // tm,),
        in_specs=in_specs,
        out_specs=[tok(d), tok(d // 2), tok(ROUTE_COLS), pl.BlockSpec((ROUTE_COLS, tm), lambda i: (0, i)),
                   pl.BlockSpec((LANES, 1), lambda i: (0, 0))],
        scratch_shapes=[pltpu.VMEM((tm, sgu), BF16), pltpu.VMEM((LANES, 1), F32)],
        compiler_params=_cparams("arbitrary"),
        name="mix",
    )(*ins)


def _row_copy(src, s, dst, d, sem):
    return pltpu.make_async_copy(src.at[pl.ds(s, 1)], dst.at[pl.ds(d, 1)], sem)


def _dispatch_kernel(zblk_ref, *refs, tm, rows):
    dest_refs, (xn_ref, xs_ref, zero_scr, sem, zsem) = refs[:TOP_K], refs[TOP_K:]

    @pl.when(pl.program_id(0) == 0)
    def _():
        zero_scr[...] = jnp.zeros_like(zero_scr)

        def zero_copy(j):
            start = pl.multiple_of(zblk_ref[j] * rows, rows)
            return pltpu.make_async_copy(zero_scr, xs_ref.at[pl.ds(start, rows)], zsem)

        for j in range(zblk_ref.shape[0]):
            pl.when(zblk_ref[j] >= 0)(lambda j=j: zero_copy(j).start())
        for j in range(zblk_ref.shape[0]):
            pl.when(zblk_ref[j] >= 0)(lambda j=j: zero_copy(j).wait())

    for j in range(tm):
        for k, dref in enumerate(dest_refs):
            _row_copy(xn_ref, j, xs_ref, dref[0, 0, j], sem).start(priority=k % 2)
    for _ in range(TOP_K):
        pltpu.make_async_copy(xn_ref, xs_ref.at[pl.ds(0, tm)], sem).wait()


def _dest_blocks(dest, tm):
    return [dest[k].reshape(-1, 1, tm) for k in range(TOP_K)]


def _dispatch_call(zero_blocks, dest, xn, m_pad, seq):
    n_tok, dw = xn.shape
    tm = min(ROW_COPY_TILE, seq)
    smem = pl.BlockSpec((1, 1, tm), lambda i, zb: (i, 0, 0), memory_space=pltpu.SMEM)
    return pl.pallas_call(
        functools.partial(_dispatch_kernel, tm=tm, rows=EXPERT_ROWS),
        out_shape=jax.ShapeDtypeStruct((m_pad, dw), xn.dtype),
        grid_spec=pltpu.PrefetchScalarGridSpec(
            num_scalar_prefetch=1,
            grid=(n_tok // tm,),
            in_specs=[smem] * TOP_K + [pl.BlockSpec((tm, dw), lambda i, zb: (i, 0))],
            out_specs=pl.BlockSpec(memory_space=pl.ANY),
            scratch_shapes=[pltpu.VMEM((EXPERT_ROWS, dw), xn.dtype), pltpu.SemaphoreType.DMA(()),
                            pltpu.SemaphoreType.DMA(())]),
        compiler_params=_cparams("arbitrary"),
        name="dispatch",
    )(zero_blocks, *_dest_blocks(dest, tm), xn)


def _expert_kernel(blk_e_ref, used_ref, xs_ref, wg_ref, wu_ref, wd_ref, y_ref, wg_bf, wu_bf, wd_bf):
    i = pl.program_id(0)
    live = i < used_ref[0]
    new_expert = (i == 0) | (blk_e_ref[i] != blk_e_ref[jnp.maximum(i - 1, 0)])

    @pl.when(live & new_expert)
    def _():
        wg_bf[...] = wg_ref[0].astype(BF16)
        wu_bf[...] = wu_ref[0].astype(BF16)
        wd_bf[...] = wd_ref[0].astype(BF16)

    @pl.when(live)
    def _():
        xb = _unpack_halves(xs_ref[...]).astype(BF16)
        g = _dot(xb, wg_bf[...])
        hid = g * _sigmoid(g) * _dot(xb, wu_bf[...])
        y_ref[...] = _pack_halves(_dot(hid.astype(BF16), wd_bf[...]))

    @pl.when(jnp.logical_not(live))
    def _():
        y_ref[...] = jnp.zeros_like(y_ref)


def _expert_call(blk_e, used, xs, wg, wu, wd):
    m_pad, dw = xs.shape
    _, d, f = wg.shape
    rows = EXPERT_ROWS
    return pl.pallas_call(
        _expert_kernel,
        out_shape=jax.ShapeDtypeStruct((m_pad, dw), xs.dtype),
        grid_spec=pltpu.PrefetchScalarGridSpec(
            num_scalar_prefetch=2,
            grid=(m_pad // rows,),
            in_specs=[pl.BlockSpec((rows, dw), lambda i, be, us: (i, 0)),
                      pl.BlockSpec((1, d, f), lambda i, be, us: (be[i], 0, 0)),
                      pl.BlockSpec((1, d, f), lambda i, be, us: (be[i], 0, 0)),
                      pl.BlockSpec((1, f, d), lambda i, be, us: (be[i], 0, 0))],
            out_specs=pl.BlockSpec((rows, dw), lambda i, be, us: (i, 0)),
            scratch_shapes=[pltpu.VMEM((d, f), BF16), pltpu.VMEM((d, f), BF16), pltpu.VMEM((f, d), BF16)]),
        compiler_params=_cparams("arbitrary"),
        name="experts",
    )(blk_e, used, xs, wg, wu, wd)


def _combine_kernel(*refs, tm, final):
    cur_refs, nxt_refs = refs[:TOP_K], refs[TOP_K:2 * TOP_K]
    h_ref, route_ref, gtf_ref, gfin_ref, y_ref, o_ref, buf, sems = refs[2 * TOP_K:]
    step = pl.program_id(0)
    slot = step % 2

    def gather(dest_refs, s):
        for j in range(tm):
            for k, dref in enumerate(dest_refs):
                pltpu.make_async_copy(y_ref.at[pl.ds(dref[0, 0, j], 1)], buf.at[s, k, pl.ds(j, 1)],
                                      sems.at[s]).start(priority=k % 2)

    pl.when(step == 0)(lambda: gather(cur_refs, 0))
    pl.when(step + 1 < pl.num_programs(0))(lambda: gather(nxt_refs, 1 - slot))
    for k in range(TOP_K):
        pltpu.make_async_copy(y_ref.at[pl.ds(0, tm)], buf.at[slot, k], sems.at[slot]).wait()
    route = route_ref[...]
    y = route[:, 2:3] * _unpack_halves(buf[slot, 0]) + route[:, 3:4] * _unpack_halves(buf[slot, 1])
    h = h_ref[...] + gtf_ref[0] * y
    o_ref[...] = _rms(h, gfin_ref[...]) if final else h


def _combine_call(dest, h, route, gtf, g_final, yb, seq, final):
    n_tok, d = h.shape
    tm = min(ROW_COPY_TILE, seq)
    per_b = seq // tm
    steps = n_tok // tm
    tok = lambda w: pl.BlockSpec((tm, w), lambda i: (i, 0))
    cur = pl.BlockSpec((1, 1, tm), lambda i: (i, 0, 0), memory_space=pltpu.SMEM)
    nxt = pl.BlockSpec((1, 1, tm), lambda i: (jnp.minimum(i + 1, steps - 1), 0, 0), memory_space=pltpu.SMEM)
    blocks = _dest_blocks(dest, tm)
    return pl.pallas_call(
        functools.partial(_combine_kernel, tm=tm, final=final),
        out_shape=jax.ShapeDtypeStruct((n_tok, d), F32),
        grid=(steps,),
        in_specs=[cur] * TOP_K + [nxt] * TOP_K + [
            tok(d), tok(ROUTE_COLS),
            pl.BlockSpec((1, 1, d), lambda i: (i // per_b, 0, 0)),
            pl.BlockSpec((1, d), lambda i: (0, 0)),
            pl.BlockSpec(memory_space=pl.ANY)],
        out_specs=tok(d),
        scratch_shapes=[pltpu.VMEM((2, TOP_K, tm, yb.shape[1]), yb.dtype), pltpu.SemaphoreType.DMA((2,))],
        compiler_params=_cparams("arbitrary"),
        name="combine",
    )(*blocks, *blocks, h, route, gtf, g_final, yb)


def _routing_plan(route_t, counts, n_experts, m_pad):
    eid = route_t[0:TOP_K].astype(jnp.int32)
    rank = route_t[4:4 + TOP_K].astype(jnp.int32)
    cnt = counts.astype(jnp.int32)
    padded = (cnt + EXPERT_ROWS - 1) // EXPERT_ROWS * EXPERT_ROWS
    pend = jnp.cumsum(padded)
    pstart = pend - padded
    dest = rank
    for e in range(n_experts):
        dest = dest + jnp.where(eid == e, pstart[e], 0)
    nblk = m_pad // EXPERT_ROWS
    blk_row = jnp.arange(nblk, dtype=jnp.int32) * EXPERT_ROWS
    blk_e = jnp.minimum(jnp.sum(pend[None, :] <= blk_row[:, None], axis=1), n_experts - 1).astype(jnp.int32)
    used = (pend[-1:] // EXPERT_ROWS).astype(jnp.int32)
    last_blk = jnp.where(padded > 0, pend // EXPERT_ROWS - 1, -1)
    tail_blk = used + jnp.arange(n_experts, dtype=jnp.int32)
    tail_blk = jnp.where(tail_blk < nblk, tail_blk, -1)
    zero_blocks = jnp.concatenate([last_blk, tail_blk]).astype(jnp.int32)
    return dest, blk_e, used, zero_blocks


def kernel(x, c, g_mix, g_ffn, w_ada, b_ada, w_in, w_sba_out, g_sgu, w_spatial, b_spatial, w_sgu_out, w_out,
           w_router_group, b_router_group, w_router_expert, b_router_expert, w_expert_gate, w_expert_up,
           w_expert_down, g_final):
    bsz, seq, d = x.shape
    depth = w_in.shape[0]
    sba = w_sba_out.shape[1]
    sgu = g_sgu.shape[1]
    heads = sba // HEAD_DIM
    groups, chunk = w_spatial.shape[1], w_spatial.shape[2]
    n_groups = w_router_group.shape[2]
    n_experts = w_router_expert.shape[2]
    per_group = n_experts // n_groups
    n_tok = bsz * seq
    m_pad = n_tok * TOP_K + n_experts * EXPERT_ROWS
    widths = (sba, sba, sba, 2 * sgu, 2 * d)
    assert seq % chunk == 0 and LANES % (sgu // groups) == 0
    assert d % (2 * LANES) == 0
    assert per_group == SUBLANES and n_groups <= SUBLANES and (n_groups + 1) * SUBLANES <= LANES

    h = x.reshape(n_tok, d)
    for l in range(depth):
        mod = _mod_call(c, w_ada[l], b_ada[l])
        sh_m, sc_m, gt_m, sh_f, sc_f, gt_f = [mod[:, i * d:(i + 1) * d].reshape(bsz, 1, d) for i in range(6)]

        q, k, v, uv, gates = _proj_call(h, sh_m, sc_m, g_mix[l].reshape(1, d), w_in[l].astype(BF16), seq, widths)
        ya = _attn_call(q.reshape(bsz, seq, sba), k.reshape(bsz, seq, sba), v.reshape(bsz, seq, sba), heads)

        bias_full = jnp.repeat(b_spatial[l].T, sgu // groups, axis=1)
        gpad = SUBLANES - n_groups
        wr = jnp.concatenate([jnp.pad(w_router_group[l], ((0, 0), (0, gpad))), w_router_expert[l]], axis=1)
        wr = jnp.pad(wr, ((0, 0), (0, LANES - wr.shape[1])))
        br = jnp.concatenate([jnp.pad(b_router_group[l], (0, gpad)), b_router_expert[l]])
        br = jnp.pad(br, (0, LANES - br.shape[0])).reshape(1, LANES)
        h1, xn, route, route_t, counts = _mix_call(
            ya.reshape(n_tok, sba), uv, gates, h, gt_m, sh_f, sc_f, g_sgu[l].reshape(1, sgu), w_spatial[l],
            bias_full, w_sba_out[l].astype(BF16), w_sgu_out[l].astype(BF16), w_out[l].astype(BF16),
            g_ffn[l].reshape(1, d), wr, br, seq, n_groups, per_group)

        dest, blk_e, used, zero_blocks = _routing_plan(route_t, counts[SUBLANES:SUBLANES + n_experts, 0],
                                                       n_experts, m_pad)
        xs = _dispatch_call(zero_blocks, dest, xn, m_pad, seq)
        yb = _expert_call(blk_e, used, xs, w_expert_gate[l], w_expert_up[l], w_expert_down[l])
        h = _combine_call(dest, h1, route, gt_f, g_final.reshape(1, d), yb, seq, final=l == depth - 1)
    return h.reshape(bsz, seq, d)
```

```python
import functools

import jax
import jax.numpy as jnp
from jax import lax
from jax.experimental import pallas as pl
from jax.experimental.pallas import tpu as pltpu

F32 = jnp.float32
BF16 = jnp.bfloat16

HEAD_DIM = 64
TOP_K = 2
EPS = 1e-6
LOG2E = 1.4426950408889634
SIGN_BIT = 0x80000000
BF16_BITS = 0xFFFF0000
ATTN_EXIT_LOG2 = 160.0
DEAD_CARRY = 1e30
LANES = 128
SUBLANES = 8
ATTN_BLOCK = 256
EXPERT_ROWS = 512
TOKEN_TILE = 512
ROW_COPY_TILE = 512
PROJ_TILE = 512
ROUTE_COLS = 8
VMEM_LIMIT = 56 * 1024 * 1024


def _cparams(*sem):
    return pltpu.CompilerParams(dimension_semantics=sem, vmem_limit_bytes=VMEM_LIMIT)


def _split_bf16(a):
    hi = a.astype(BF16)
    lo = (a - hi.astype(F32)).astype(BF16)
    return hi, lo


def _dot(a, b):
    return jnp.dot(a, b, preferred_element_type=F32)


def _dot3(a, b):
    ah, al = _split_bf16(a)
    bh, bl = _split_bf16(b)
    return _dot(ah, bh) + _dot(ah, bl) + _dot(al, bh)


def _pack_halves(x):
    half = x.shape[1] // 2
    hi = lax.bitcast_convert_type(x[:, :half].astype(BF16).astype(F32), jnp.uint32)
    lo = lax.bitcast_convert_type(x[:, half:].astype(BF16).astype(F32), jnp.uint32)
    return hi | (lo >> 16)


def _unpack_halves(u):
    hi = lax.bitcast_convert_type(u & jnp.uint32(BF16_BITS), F32)
    lo = lax.bitcast_convert_type(u << 16, F32)
    return jnp.concatenate([hi, lo], axis=1)


def _rms(x, g):
    ms = jnp.mean(x * x, axis=-1, keepdims=True)
    return x * lax.rsqrt(ms + EPS) * g


def _mod_kernel(c_ref, w_ref, b_ref, o_ref):
    c = c_ref[...]
    ca = c * (1.0 / (1.0 + jnp.exp(-c)))
    o_ref[...] = _dot3(ca, w_ref[...]) + b_ref[...]


def _mod_call(c, w_ada, b_ada):
    bsz, d = c.shape
    n = w_ada.shape[1]
    tn = n // 4 if n % (4 * LANES) == 0 else n
    return pl.pallas_call(
        _mod_kernel,
        out_shape=jax.ShapeDtypeStruct((bsz, n), F32),
        grid=(n // tn,),
        in_specs=[pl.BlockSpec((bsz, d), lambda j: (0, 0)),
                  pl.BlockSpec((d, tn), lambda j: (0, j)),
                  pl.BlockSpec((1, tn), lambda j: (0, j))],
        out_specs=pl.BlockSpec((bsz, tn), lambda j: (0, j)),
        compiler_params=_cparams("arbitrary"),
        name="mod",
    )(c, w_ada, b_ada.reshape(1, n))


def _proj_kernel(x_ref, sh_ref, sc_ref, g_ref, w_ref, q_ref, k_ref, v_ref, uv_ref, gt_ref, *, widths):
    x = x_ref[...]
    n = _rms(x, g_ref[...]) * (1.0 + sc_ref[0]) + sh_ref[0]
    nb = n.astype(BF16)
    off = 0
    for ref, wd in zip((q_ref, k_ref, v_ref, uv_ref, gt_ref), widths):
        p = _dot(nb, w_ref[:, off:off + wd])
        if ref is q_ref:
            p = p * (HEAD_DIM ** -0.5 * LOG2E)
        ref[...] = p.astype(ref.dtype)
        off += wd


def _proj_call(x2, sh, sc, g, w_in_bf, seq, widths):
    n_tok, d = x2.shape
    tm = min(PROJ_TILE, seq)
    per_b = seq // tm
    cols = w_in_bf.shape[1]
    tok = lambda w: pl.BlockSpec((tm, w), lambda i: (i, 0))
    vec = pl.BlockSpec((1, 1, d), lambda i: (i // per_b, 0, 0))
    return pl.pallas_call(
        functools.partial(_proj_kernel, widths=widths),
        out_shape=[jax.ShapeDtypeStruct((n_tok, w), BF16) for w in widths],
        grid=(n_tok // tm,),
        in_specs=[tok(d), vec, vec,
                  pl.BlockSpec((1, d), lambda i: (0, 0)),
                  pl.BlockSpec((d, cols), lambda i: (0, 0))],
        out_specs=[tok(w) for w in widths],
        compiler_params=_cparams("arbitrary"),
        name="proj",
    )(x2, sh, sc, g, w_in_bf)


def _attn_kernel(q_ref, k_ref, v_ref, u_ref, o_ref, qm_ref, acc_ref, carry_ref, *, heads, blk):
    qi = pl.program_id(1)
    u_tri = u_ref[...]
    row = lax.broadcasted_iota(jnp.int32, (blk, blk), 0)
    col = lax.broadcasted_iota(jnp.int32, (blk, blk), 1)
    causal = col < row
    per_slab = LANES // HEAD_DIM
    lane = lax.broadcasted_iota(jnp.int32, (blk, LANES), 1)
    own = [(lane >= j * HEAD_DIM) & (lane < (j + 1) * HEAD_DIM) for j in range(per_slab)]
    for h in range(heads):
        slab = slice(h // per_slab * LANES, (h // per_slab + 1) * LANES)
        qm_ref[h] = jnp.where(own[h % per_slab], q_ref[0, :, slab], 0.0).astype(BF16)

    def key_tile(start, first, dead=None):
        mask = causal if first else None
        slabs = [slice(p * LANES, (p + 1) * LANES) for p in range(heads // per_slab)]
        s_all, cum_all, w_all, carries = {}, {}, {}, []

        def scores(h):
            s_all[h] = lax.dot_general(qm_ref[h], k_ref[0, pl.ds(start, blk), slabs[h // per_slab]],
                                       (((1,), (1,)), ((), ())), preferred_element_type=F32)

        def suffix_sums(h):
            s = s_all[h]
            neg_abs = lax.bitcast_convert_type(lax.bitcast_convert_type(s, jnp.uint32) | jnp.uint32(SIGN_BIT), F32)
            sp = jnp.maximum(s, 0.0) + jnp.log(1.0 + jnp.exp2(neg_abs)) * LOG2E
            if mask is not None:
                sp = jnp.where(mask, sp, 0.0)
            hi = lax.bitcast_convert_type(lax.bitcast_convert_type(sp, jnp.uint32) & jnp.uint32(BF16_BITS), F32)
            cum_all[h] = _dot(hi.astype(BF16), u_tri) + _dot((sp - hi).astype(BF16), u_tri)

        def weights(h):
            cum = cum_all[h]
            if first:
                w = jnp.exp2(s_all[h] - cum)
                carry = cum[:, 0:1]
            else:
                carry = carry_ref[h]
                if dead is not None:
                    carry = jnp.where(dead, DEAD_CARRY, carry)
                w = jnp.exp2((s_all[h] - carry) - cum)
                carry = carry + cum[:, 0:1]
            w_all[h] = (w if mask is None else jnp.where(mask, w, 0.0)).astype(BF16)
            carry_ref[h] = carry
            carries.append(carry)

        def values(p):
            vp = v_ref[0, pl.ds(start, blk), slabs[p]]
            upd = None
            for j in range(per_slab):
                pv = _dot(w_all[p * per_slab + j], vp)
                upd = pv if upd is None else jnp.where(own[j], pv, upd)
            acc_ref[:, slabs[p]] = upd if first else acc_ref[:, slabs[p]] + upd

        for t in range(heads + 2):
            if t < heads:
                scores(t)
            if 1 <= t <= heads:
                suffix_sums(t - 1)
            if t >= 2:
                weights(t - 2)
                if (t - 2) % per_slab == per_slab - 1:
                    values((t - 2) // per_slab)
        return jnp.min(functools.reduce(jnp.minimum, carries))

    key_tile(pl.multiple_of(qi * blk, blk), True)
    cmin = key_tile(pl.multiple_of(jnp.maximum(qi - 1, 0) * blk, blk), False, dead=qi == 0)

    def cond(state):
        i, cmin = state
        return (i < qi) & (cmin < ATTN_EXIT_LOG2)

    def body(state):
        i, _ = state
        return i + 1, key_tile(pl.multiple_of((qi - 1 - i) * blk, blk), False)

    lax.while_loop(cond, body, (jnp.int32(1), cmin))
    o_ref[0] = acc_ref[...].astype(o_ref.dtype)


def _attn_call(q, k, v, heads):
    bsz, seq, width = q.shape
    blk = min(ATTN_BLOCK, seq)
    r = lax.broadcasted_iota(jnp.int32, (blk, blk), 0)
    c = lax.broadcasted_iota(jnp.int32, (blk, blk), 1)
    u_tri = (r >= c).astype(BF16)
    full = pl.BlockSpec((1, seq, width), lambda b, i: (b, 0, 0))
    return pl.pallas_call(
        functools.partial(_attn_kernel, heads=heads, blk=blk),
        out_shape=jax.ShapeDtypeStruct((bsz, seq, width), BF16),
        grid=(bsz, seq // blk),
        in_specs=[pl.BlockSpec((1, blk, width), lambda b, i: (b, i, 0)), full, full,
                  pl.BlockSpec((blk, blk), lambda b, i: (0, 0))],
        out_specs=pl.BlockSpec((1, blk, width), lambda b, i: (b, i, 0)),
        scratch_shapes=[pltpu.VMEM((heads, blk, LANES), BF16), pltpu.VMEM((blk, width), F32),
                        pltpu.VMEM((heads, blk, 1), F32)],
        compiler_params=_cparams("arbitrary", "arbitrary"),
        name="attn",
    )(q, k, v, u_tri)


def _gelu_tanh(x):
    c = 0.7978845608028654
    hx = 0.5 * x
    return hx + hx * jnp.tanh(x * (c + (0.044715 * c) * (x * x)))


def _sigmoid(x):
    return 1.0 / (1.0 + jnp.exp(-x))


def _mix_kernel(ya_ref, uv_ref, gt_ref, x_ref, gtm_ref, shf_ref, scf_ref, gsgu_ref, ws_ref, bs_ref,
                wa_ref, wb_ref, wo_ref, gffn_ref, wr_ref, br_ref, utri_ref,
                h_ref, xn_ref, route_ref, route_t_ref, cnt_ref, yb_scr, cnt_scr,
                *, chunk, groups, n_groups, per_group):
    step = pl.program_id(0)
    tm = x_ref.shape[0]
    sgu = gsgu_ref.shape[1]
    gdim = sgu // groups

    @pl.when(step == 0)
    def _():
        cnt_scr[...] = jnp.zeros_like(cnt_scr)

    ya_proj = _dot(ya_ref[...], wa_ref[...])

    act = _gelu_tanh(uv_ref[...].astype(F32))
    u = act[:, :sgu]
    v = act[:, sgu:]
    mu = jnp.mean(v, axis=-1, keepdims=True)
    vc = v - mu
    var = jnp.mean(vc * vc, axis=-1, keepdims=True)
    vn = vc * lax.rsqrt(var + EPS) * gsgu_ref[...]
    r = lax.broadcasted_iota(jnp.int32, (chunk, chunk), 0)
    c = lax.broadcasted_iota(jnp.int32, (chunk, chunk), 1)
    wcs = [jnp.where(r >= c, ws_ref[g], 0.0).astype(BF16) for g in range(groups)]
    pair = LANES // gdim
    lane = lax.broadcasted_iota(jnp.int32, (chunk, LANES), 1)
    for ci in range(tm // chunk):
        rows = slice(ci * chunk, (ci + 1) * chunk)
        for p in range(groups // pair):
            lanes = slice(p * LANES, (p + 1) * LANES)
            slab = vn[rows, lanes]
            mix = bs_ref[:, lanes]
            for j in range(pair):
                sel = (lane >= j * gdim) & (lane < (j + 1) * gdim)
                mix = mix + _dot(wcs[p * pair + j], jnp.where(sel, slab, 0.0).astype(BF16))
            yb_scr[rows, lanes] = (u[rows, lanes] * mix).astype(BF16)

    d = x_ref.shape[1]
    gates2 = 1.0 + jnp.tanh(0.5 * gt_ref[...].astype(F32))
    merged2 = gates2[:, :d] * ya_proj + gates2[:, d:] * _dot(yb_scr[...], wb_ref[...])
    h = x_ref[...] + (0.5 * gtm_ref[0]) * _dot(merged2.astype(BF16), wo_ref[...])
    h_ref[...] = h

    xn = _rms(h, gffn_ref[...]) * (1.0 + scf_ref[0]) + shf_ref[0]
    xn_ref[...] = _pack_halves(xn)
    logits = _dot3(xn, wr_ref[...]) + br_ref[...]

    lt = logits.T
    sub = lax.broadcasted_iota(jnp.int32, (SUBLANES, tm), 0)
    neg = -jnp.inf
    gl = jnp.where(sub < n_groups, lt[0:SUBLANES], neg)
    gmax = jnp.max(gl, axis=0, keepdims=True)
    g_sel = jnp.min(jnp.where(gl == gmax, sub, SUBLANES), axis=0, keepdims=True)
    g_w = 1.0 / jnp.sum(jnp.exp(gl - gmax), axis=0, keepdims=True)
    el = lt[SUBLANES:2 * SUBLANES]
    for g in range(1, n_groups):
        el = jnp.where(g_sel == g, lt[(g + 1) * SUBLANES:(g + 2) * SUBLANES], el)
    m1 = jnp.max(el, axis=0, keepdims=True)
    j1 = jnp.min(jnp.where(el == m1, sub, SUBLANES), axis=0, keepdims=True)
    el2 = jnp.where(sub == j1, neg, el)
    m2 = jnp.max(el2, axis=0, keepdims=True)
    j2 = jnp.min(jnp.where(el2 == m2, sub, SUBLANES), axis=0, keepdims=True)
    t = jnp.exp(m2 - m1)
    w1 = g_w / (1.0 + t)
    w2 = g_w * t / (1.0 + t)
    e1 = g_sel * per_group + j1
    e2 = g_sel * per_group + j2

    row = lax.broadcasted_iota(jnp.int32, (LANES, tm), 0)
    oh1 = row == e1 + SUBLANES
    oh2 = row == e2 + SUBLANES
    both = jnp.where(oh1 | oh2, 1.0, 0.0)
    before = _dot(both.astype(BF16), utri_ref[...]) + cnt_scr[...]
    rank1 = jnp.sum(jnp.where(oh1, before, 0.0), axis=0, keepdims=True)
    rank2 = jnp.sum(jnp.where(oh2, before, 0.0), axis=0, keepdims=True)
    cnt_scr[...] = cnt_scr[...] + jnp.sum(both, axis=1, keepdims=True)
    cnt_ref[...] = cnt_scr[...]

    rec = jnp.zeros((LANES, tm), F32)
    for idx, val in enumerate((e1.astype(F32), e2.astype(F32), w1, w2, rank1, rank2)):
        rec = jnp.where(row == idx, val, rec)
    route_t_ref[...] = rec[:ROUTE_COLS]
    route_ref[...] = rec.T[:, :ROUTE_COLS]


def _mix_call(ya, uv, gates, x2, gtm, shf, scf, g_sgu, w_spatial, bias_full, wa, wb, wo, g_ffn, wr, br,
              seq, n_groups, per_group):
    n_tok, d = x2.shape
    tm = min(TOKEN_TILE, seq)
    per_b = seq // tm
    groups, chunk, _ = w_spatial.shape
    sgu = g_sgu.shape[1]
    r = lax.broadcasted_iota(jnp.int32, (tm, tm), 0)
    c = lax.broadcasted_iota(jnp.int32, (tm, tm), 1)
    utri = (r < c).astype(BF16)
    tok = lambda w: pl.BlockSpec((tm, w), lambda i: (i, 0))
    vec = pl.BlockSpec((1, 1, d), lambda i: (i // per_b, 0, 0))
    const = lambda a: pl.BlockSpec(a.shape, lambda i: (0,) * a.ndim)
    ins = [ya, uv, gates, x2, gtm, shf, scf, g_sgu, w_spatial, bias_full, wa, wb, wo, g_ffn, wr, br, utri]
    in_specs = [tok(ya.shape[1]), tok(uv.shape[1]), tok(gates.shape[1]), tok(d), vec, vec, vec]
    in_specs += [const(a) for a in ins[7:]]
    return pl.pallas_call(
        functools.partial(_mix_kernel, chunk=chunk, groups=groups, n_groups=n_groups, per_group=per_group),
        out_shape=[jax.ShapeDtypeStruct((n_tok, d), F32), jax.ShapeDtypeStruct((n_tok, d // 2), jnp.uint32),
                   jax.ShapeDtypeStruct((n_tok, ROUTE_COLS), F32), jax.ShapeDtypeStruct((ROUTE_COLS, n_tok), F32),
                   jax.ShapeDtypeStruct((LANES, 1), F32)],
        grid=(n_tok // tm,),
        in_specs=in_specs,
        out_specs=[tok(d), tok(d // 2), tok(ROUTE_COLS), pl.BlockSpec((ROUTE_COLS, tm), lambda i: (0, i)),
                   pl.BlockSpec((LANES, 1), lambda i: (0, 0))],
        scratch_shapes=[pltpu.VMEM((tm, sgu), BF16), pltpu.VMEM((LANES, 1), F32)],
        compiler_params=_cparams("arbitrary"),
        name="mix",
    )(*ins)


def _row_copy(src, s, dst, d, sem):
    return pltpu.make_async_copy(src.at[pl.ds(s, 1)], dst.at[pl.ds(d, 1)], sem)


def _dispatch_kernel(zblk_ref, *refs, tm, rows):
    dest_refs, (xn_ref, xs_ref, zero_scr, sem, zsem) = refs[:TOP_K], refs[TOP_K:]

    @pl.when(pl.program_id(0) == 0)
    def _():
        zero_scr[...] = jnp.zeros_like(zero_scr)

        def zero_copy(j):
            start = pl.multiple_of(zblk_ref[j] * rows, rows)
            return pltpu.make_async_copy(zero_scr, xs_ref.at[pl.ds(start, rows)], zsem)

        for j in range(zblk_ref.shape[0]):
            pl.when(zblk_ref[j] >= 0)(lambda j=j: zero_copy(j).start())
        for j in range(zblk_ref.shape[0]):
            pl.when(zblk_ref[j] >= 0)(lambda j=j: zero_copy(j).wait())

    for j in range(tm):
        for k, dref in enumerate(dest_refs):
            _row_copy(xn_ref, j, xs_ref, dref[0, 0, j], sem).start(priority=k % 2)
    for _ in range(TOP_K):
        pltpu.make_async_copy(xn_ref, xs_ref.at[pl.ds(0, tm)], sem).wait()


def _dest_blocks(dest, tm):
    return [dest[k].reshape(-1, 1, tm) for k in range(TOP_K)]


def _dispatch_call(zero_blocks, dest, xn, m_pad, seq):
    n_tok, dw = xn.shape
    tm = min(ROW_COPY_TILE, seq)
    smem = pl.BlockSpec((1, 1, tm), lambda i, zb: (i, 0, 0), memory_space=pltpu.SMEM)
    return pl.pallas_call(
        functools.partial(_dispatch_kernel, tm=tm, rows=EXPERT_ROWS),
        out_shape=jax.ShapeDtypeStruct((m_pad, dw), xn.dtype),
        grid_spec=pltpu.PrefetchScalarGridSpec(
            num_scalar_prefetch=1,
            grid=(n_tok // tm,),
            in_specs=[smem] * TOP_K + [pl.BlockSpec((tm, dw), lambda i, zb: (i, 0))],
            out_specs=pl.BlockSpec(memory_space=pl.ANY),
            scratch_shapes=[pltpu.VMEM((EXPERT_ROWS, dw), xn.dtype), pltpu.SemaphoreType.DMA(()),
                            pltpu.SemaphoreType.DMA(())]),
        compiler_params=_cparams("arbitrary"),
        name="dispatch",
    )(zero_blocks, *_dest_blocks(dest, tm), xn)


def _expert_kernel(blk_e_ref, used_ref, xs_ref, wg_ref, wu_ref, wd_ref, y_ref, wg_bf, wu_bf, wd_bf):
    i = pl.program_id(0)
    live = i < used_ref[0]
    new_expert = (i == 0) | (blk_e_ref[i] != blk_e_ref[jnp.maximum(i - 1, 0)])

    @pl.when(live & new_expert)
    def _():
        wg_bf[...] = wg_ref[0].astype(BF16)
        wu_bf[...] = wu_ref[0].astype(BF16)
        wd_bf[...] = wd_ref[0].astype(BF16)

    @pl.when(live)
    def _():
        xb = _unpack_halves(xs_ref[...]).astype(BF16)
        g = _dot(xb, wg_bf[...])
        hid = g * _sigmoid(g) * _dot(xb, wu_bf[...])
        y_ref[...] = _pack_halves(_dot(hid.astype(BF16), wd_bf[...]))

    @pl.when(jnp.logical_not(live))
    def _():
        y_ref[...] = jnp.zeros_like(y_ref)


def _expert_call(blk_e, used, xs, wg, wu, wd):
    m_pad, dw = xs.shape
    _, d, f = wg.shape
    rows = EXPERT_ROWS
    return pl.pallas_call(
        _expert_kernel,
        out_shape=jax.ShapeDtypeStruct((m_pad, dw), xs.dtype),
        grid_spec=pltpu.PrefetchScalarGridSpec(
            num_scalar_prefetch=2,
            grid=(m_pad // rows,),
            in_specs=[pl.BlockSpec((rows, dw), lambda i, be, us: (i, 0)),
                      pl.BlockSpec((1, d, f), lambda i, be, us: (be[i], 0, 0)),
                      pl.BlockSpec((1, d, f), lambda i, be, us: (be[i], 0, 0)),
                      pl.BlockSpec((1, f, d), lambda i, be, us: (be[i], 0, 0))],
            out_specs=pl.BlockSpec((rows, dw), lambda i, be, us: (i, 0)),
            scratch_shapes=[pltpu.VMEM((d, f), BF16), pltpu.VMEM((d, f), BF16), pltpu.VMEM((f, d), BF16)]),
        compiler_params=_cparams("arbitrary"),
        name="experts",
    )(blk_e, used, xs, wg, wu, wd)


def _combine_kernel(*refs, tm, final):
    cur_refs, nxt_refs = refs[:TOP_K], refs[TOP_K:2 * TOP_K]
    h_ref, route_ref, gtf_ref, gfin_ref, y_ref, o_ref, buf, sems = refs[2 * TOP_K:]
    step = pl.program_id(0)
    slot = step % 2

    def gather(dest_refs, s):
        for j in range(tm):
            for k, dref in enumerate(dest_refs):
                pltpu.make_async_copy(y_ref.at[pl.ds(dref[0, 0, j], 1)], buf.at[s, k, pl.ds(j, 1)],
                                      sems.at[s]).start(priority=k % 2)

    pl.when(step == 0)(lambda: gather(cur_refs, 0))
    pl.when(step + 1 < pl.num_programs(0))(lambda: gather(nxt_refs, 1 - slot))
    for k in range(TOP_K):
        pltpu.make_async_copy(y_ref.at[pl.ds(0, tm)], buf.at[slot, k], sems.at[slot]).wait()
    route = route_ref[...]
    y = route[:, 2:3] * _unpack_halves(buf[slot, 0]) + route[:, 3:4] * _unpack_halves(buf[slot, 1])
    h = h_ref[...] + gtf_ref[0] * y
    o_ref[...] = _rms(h, gfin_ref[...]) if final else h


def _combine_call(dest, h, route, gtf, g_final, yb, seq, final):
    n_tok, d = h.shape
    tm = min(ROW_COPY_TILE, seq)
    per_b = seq // tm
    steps = n_tok // tm
    tok = lambda w: pl.BlockSpec((tm, w), lambda i: (i, 0))
    cur = pl.BlockSpec((1, 1, tm), lambda i: (i, 0, 0), memory_space=pltpu.SMEM)
    nxt = pl.BlockSpec((1, 1, tm), lambda i: (jnp.minimum(i + 1, steps - 1), 0, 0), memory_space=pltpu.SMEM)
    blocks = _dest_blocks(dest, tm)
    return pl.pallas_call(
        functools.partial(_combine_kernel, tm=tm, final=final),
        out_shape=jax.ShapeDtypeStruct((n_tok, d), F32),
        grid=(steps,),
        in_specs=[cur] * TOP_K + [nxt] * TOP_K + [
            tok(d), tok(ROUTE_COLS),
            pl.BlockSpec((1, 1, d), lambda i: (i // per_b, 0, 0)),
            pl.BlockSpec((1, d), lambda i: (0, 0)),
            pl.BlockSpec(memory_space=pl.ANY)],
        out_specs=tok(d),
        scratch_shapes=[pltpu.VMEM((2, TOP_K, tm, yb.shape[1]), yb.dtype), pltpu.SemaphoreType.DMA((2,))],
        compiler_params=_cparams("arbitrary"),
        name="combine",
    )(*blocks, *blocks, h, route, gtf, g_final, yb)


def _routing_plan(route_t, counts, n_experts, m_pad):
    eid = route_t[0:TOP_K].astype(jnp.int32)
    rank = route_t[4:4 + TOP_K].astype(jnp.int32)
    cnt = counts.astype(jnp.int32)
    padded = (cnt + EXPERT_ROWS - 1) // EXPERT_ROWS * EXPERT_ROWS
    pend = jnp.cumsum(padded)
    pstart = pend - padded
    dest = rank
    for e in range(n_experts):
        dest = dest + jnp.where(eid == e, pstart[e], 0)
    nblk = m_pad // EXPERT_ROWS
    blk_row = jnp.arange(nblk, dtype=jnp.int32) * EXPERT_ROWS
    blk_e = jnp.minimum(jnp.sum(pend[None, :] <= blk_row[:, None], axis=1), n_experts - 1).astype(jnp.int32)
    used = (pend[-1:] // EXPERT_ROWS).astype(jnp.int32)
    last_blk = jnp.where(padded > 0, pend // EXPERT_ROWS - 1, -1)
    tail_blk = used + jnp.arange(n_experts, dtype=jnp.int32)
    tail_blk = jnp.where(tail_blk < nblk, tail_blk, -1)
    zero_blocks = jnp.concatenate([last_blk, tail_blk]).astype(jnp.int32)
    return dest, blk_e, used, zero_blocks


def kernel(x, c, g_mix, g_ffn, w_ada, b_ada, w_in, w_sba_out, g_sgu, w_spatial, b_spatial, w_sgu_out, w_out,
           w_router_group, b_router_group, w_router_expert, b_router_expert, w_expert_gate, w_expert_up,
           w_expert_down, g_final):
    bsz, seq, d = x.shape
    depth = w_in.shape[0]
    sba = w_sba_out.shape[1]
    sgu = g_sgu.shape[1]
    heads = sba // HEAD_DIM
    groups, chunk = w_spatial.shape[1], w_spatial.shape[2]
    n_groups = w_router_group.shape[2]
    n_experts = w_router_expert.shape[2]
    per_group = n_experts // n_groups
    n_tok = bsz * seq
    m_pad = n_tok * TOP_K + n_experts * EXPERT_ROWS
    widths = (sba, sba, sba, 2 * sgu, 2 * d)
    assert seq % chunk == 0 and LANES % (sgu // groups) == 0
    assert d % (2 * LANES) == 0
    assert per_group == SUBLANES and n_groups <= SUBLANES and (n_groups + 1) * SUBLANES <= LANES

    h = x.reshape(n_tok, d)
    for l in range(depth):
        mod = _mod_call(c, w_ada[l], b_ada[l])
        sh_m, sc_m, gt_m, sh_f, sc_f, gt_f = [mod[:, i * d:(i + 1) * d].reshape(bsz, 1, d) for i in range(6)]

        q, k, v, uv, gates = _proj_call(h, sh_m, sc_m, g_mix[l].reshape(1, d), w_in[l].astype(BF16), seq, widths)
        ya = _attn_call(q.reshape(bsz, seq, sba), k.reshape(bsz, seq, sba), v.reshape(bsz, seq, sba), heads)

        bias_full = jnp.repeat(b_spatial[l].T, sgu // groups, axis=1)
        gpad = SUBLANES - n_groups
        wr = jnp.concatenate([jnp.pad(w_router_group[l], ((0, 0), (0, gpad))), w_router_expert[l]], axis=1)
        wr = jnp.pad(wr, ((0, 0), (0, LANES - wr.shape[1])))
        br = jnp.concatenate([jnp.pad(b_router_group[l], (0, gpad)), b_router_expert[l]])
        br = jnp.pad(br, (0, LANES - br.shape[0])).reshape(1, LANES)
        h1, xn, route, route_t, counts = _mix_call(
            ya.reshape(n_tok, sba), uv, gates, h, gt_m, sh_f, sc_f, g_sgu[l].reshape(1, sgu), w_spatial[l],
            bias_full, w_sba_out[l].astype(BF16), w_sgu_out[l].astype(BF16), w_out[l].astype(BF16),
            g_ffn[l].reshape(1, d), wr, br, seq, n_groups, per_group)

        dest, blk_e, used, zero_blocks = _routing_plan(route_t, counts[SUBLANES:SUBLANES + n_experts, 0],
                                                       n_experts, m_pad)
        xs = _dispatch_call(zero_blocks, dest, xn, m_pad, seq)
        yb = _expert_call(blk_e, used, xs, w_expert_gate[l], w_expert_up[l], w_expert_down[l])
        h = _combine_call(dest, h1, route, gt_f, g_final.reshape(1, d), yb, seq, final=l == depth - 1)
    return h.reshape(bsz, seq, d)
```

```python
import functools

import jax
import jax.numpy as jnp
from jax import lax
from jax.experimental import pallas as pl
from jax.experimental.pallas import tpu as pltpu

F32 = jnp.float32
BF16 = jnp.bfloat16

HEAD_DIM = 64
TOP_K = 2
EPS = 1e-6
LOG2E = 1.4426950408889634
SIGN_BIT = 0x80000000
BF16_BITS = 0xFFFF0000
ATTN_EXIT_LOG2 = 160.0
DEAD_CARRY = 1e30
LANES = 128
SUBLANES = 8
ATTN_BLOCK = 256
EXPERT_ROWS = 512
TOKEN_TILE = 512
ROW_COPY_TILE = 512
PROJ_TILE = 512
ROUTE_COLS = 8
VMEM_LIMIT = 56 * 1024 * 1024


def _cparams(*sem):
    return pltpu.CompilerParams(dimension_semantics=sem, vmem_limit_bytes=VMEM_LIMIT)


def _split_bf16(a):
    hi = a.astype(BF16)
    lo = (a - hi.astype(F32)).astype(BF16)
    return hi, lo


def _dot(a, b):
    return jnp.dot(a, b, preferred_element_type=F32)


def _dot3(a, b):
    ah, al = _split_bf16(a)
    bh, bl = _split_bf16(b)
    return _dot(ah, bh) + _dot(ah, bl) + _dot(al, bh)


def _pack_halves(x):
    half = x.shape[1] // 2
    hi = lax.bitcast_convert_type(x[:, :half].astype(BF16).astype(F32), jnp.uint32)
    lo = lax.bitcast_convert_type(x[:, half:].astype(BF16).astype(F32), jnp.uint32)
    return hi | (lo >> 16)


def _unpack_halves(u):
    hi = lax.bitcast_convert_type(u & jnp.uint32(BF16_BITS), F32)
    lo = lax.bitcast_convert_type(u << 16, F32)
    return jnp.concatenate([hi, lo], axis=1)


def _rms(x, g):
    ms = jnp.mean(x * x, axis=-1, keepdims=True)
    return x * lax.rsqrt(ms + EPS) * g


def _mod_kernel(c_ref, w_ref, b_ref, o_ref):
    c = c_ref[...]
    ca = c * (1.0 / (1.0 + jnp.exp(-c)))
    o_ref[...] = _dot3(ca, w_ref[...]) + b_ref[...]


def _mod_call(c, w_ada, b_ada):
    bsz, d = c.shape
    n = w_ada.shape[1]
    tn = n // 4 if n % (4 * LANES) == 0 else n
    return pl.pallas_call(
        _mod_kernel,
        out_shape=jax.ShapeDtypeStruct((bsz, n), F32),
        grid=(n // tn,),
        in_specs=[pl.BlockSpec((bsz, d), lambda j: (0, 0)),
                  pl.BlockSpec((d, tn), lambda j: (0, j)),
                  pl.BlockSpec((1, tn), lambda j: (0, j))],
        out_specs=pl.BlockSpec((bsz, tn), lambda j: (0, j)),
        compiler_params=_cparams("arbitrary"),
        name="mod",
    )(c, w_ada, b_ada.reshape(1, n))


def _proj_kernel(x_ref, sh_ref, sc_ref, g_ref, w_ref, q_ref, k_ref, v_ref, uv_ref, gt_ref, *, widths):
    x = x_ref[...]
    n = _rms(x, g_ref[...]) * (1.0 + sc_ref[0]) + sh_ref[0]
    nb = n.astype(BF16)
    off = 0
    for ref, wd in zip((q_ref, k_ref, v_ref, uv_ref, gt_ref), widths):
        p = _dot(nb, w_ref[:, off:off + wd])
        if ref is q_ref:
            p = p * (HEAD_DIM ** -0.5 * LOG2E)
        ref[...] = p.astype(ref.dtype)
        off += wd


def _proj_call(x2, sh, sc, g, w_in_bf, seq, widths):
    n_tok, d = x2.shape
    tm = min(PROJ_TILE, seq)
    per_b = seq // tm
    cols = w_in_bf.shape[1]
    tok = lambda w: pl.BlockSpec((tm, w), lambda i: (i, 0))
    vec = pl.BlockSpec((1, 1, d), lambda i: (i // per_b, 0, 0))
    return pl.pallas_call(
        functools.partial(_proj_kernel, widths=widths),
        out_shape=[jax.ShapeDtypeStruct((n_tok, w), BF16) for w in widths],
        grid=(n_tok // tm,),
        in_specs=[tok(d), vec, vec,
                  pl.BlockSpec((1, d), lambda i: (0, 0)),
                  pl.BlockSpec((d, cols), lambda i: (0, 0))],
        out_specs=[tok(w) for w in widths],
        compiler_params=_cparams("arbitrary"),
        name="proj",
    )(x2, sh, sc, g, w_in_bf)


def _attn_kernel(q_ref, k_ref, v_ref, u_ref, o_ref, qm_ref, acc_ref, carry_ref, *, heads, blk):
    qi = pl.program_id(1)
    u_tri = u_ref[...]
    row = lax.broadcasted_iota(jnp.int32, (blk, blk), 0)
    col = lax.broadcasted_iota(jnp.int32, (blk, blk), 1)
    causal = col < row
    per_slab = LANES // HEAD_DIM
    lane = lax.broadcasted_iota(jnp.int32, (blk, LANES), 1)
    own = [(lane >= j * HEAD_DIM) & (lane < (j + 1) * HEAD_DIM) for j in range(per_slab)]
    for h in range(heads):
        slab = slice(h // per_slab * LANES, (h // per_slab + 1) * LANES)
        qm_ref[h] = jnp.where(own[h % per_slab], q_ref[0, :, slab], 0.0).astype(BF16)

    def key_tile(start, first, dead=None):
        mask = causal if first else None
        slabs = [slice(p * LANES, (p + 1) * LANES) for p in range(heads // per_slab)]
        s_all, cum_all, w_all, carries = {}, {}, {}, []

        def scores(h):
            s_all[h] = lax.dot_general(qm_ref[h], k_ref[0, pl.ds(start, blk), slabs[h // per_slab]],
                                       (((1,), (1,)), ((), ())), preferred_element_type=F32)

        def suffix_sums(h):
            s = s_all[h]
            neg_abs = lax.bitcast_convert_type(lax.bitcast_convert_type(s, jnp.uint32) | jnp.uint32(SIGN_BIT), F32)
            sp = jnp.maximum(s, 0.0) + jnp.log(1.0 + jnp.exp2(neg_abs)) * LOG2E
            if mask is not None:
                sp = jnp.where(mask, sp, 0.0)
            hi = lax.bitcast_convert_type(lax.bitcast_convert_type(sp, jnp.uint32) & jnp.uint32(BF16_BITS), F32)
            cum_all[h] = _dot(hi.astype(BF16), u_tri) + _dot((sp - hi).astype(BF16), u_tri)

        def weights(h):
            cum = cum_all[h]
            if first:
                w = jnp.exp2(s_all[h] - cum)
                carry = cum[:, 0:1]
            else:
                carry = carry_ref[h]
                if dead is not None:
                    carry = jnp.where(dead, DEAD_CARRY, carry)
                w = jnp.exp2((s_all[h] - carry) - cum)
                carry = carry + cum[:, 0:1]
            w_all[h] = (w if mask is None else jnp.where(mask, w, 0.0)).astype(BF16)
            carry_ref[h] = carry
            carries.append(carry)

        def values(p):
            vp = v_ref[0, pl.ds(start, blk), slabs[p]]
            upd = None
            for j in range(per_slab):
                pv = _dot(w_all[p * per_slab + j], vp)
                upd = pv if upd is None else jnp.where(own[j], pv, upd)
            acc_ref[:, slabs[p]] = upd if first else acc_ref[:, slabs[p]] + upd

        for t in range(heads + 2):
            if t < heads:
                scores(t)
            if 1 <= t <= heads:
                suffix_sums(t - 1)
            if t >= 2:
                weights(t - 2)
                if (t - 2) % per_slab == per_slab - 1:
                    values((t - 2) // per_slab)
        return jnp.min(functools.reduce(jnp.minimum, carries))

    key_tile(pl.multiple_of(qi * blk, blk), True)
    cmin = key_tile(pl.multiple_of(jnp.maximum(qi - 1, 0) * blk, blk), False, dead=qi == 0)

    def cond(state):
        i, cmin = state
        return (i < qi) & (cmin < ATTN_EXIT_LOG2)

    def body(state):
        i, _ = state
        return i + 1, key_tile(pl.multiple_of((qi - 1 - i) * blk, blk), False)

    lax.while_loop(cond, body, (jnp.int32(1), cmin))
    o_ref[0] = acc_ref[...].astype(o_ref.dtype)


def _attn_call(q, k, v, heads):
    bsz, seq, width = q.shape
    blk = min(ATTN_BLOCK, seq)
    r = lax.broadcasted_iota(jnp.int32, (blk, blk), 0)
    c = lax.broadcasted_iota(jnp.int32, (blk, blk), 1)
    u_tri = (r >= c).astype(BF16)
    full = pl.BlockSpec((1, seq, width), lambda b, i: (b, 0, 0))
    return pl.pallas_call(
        functools.partial(_attn_kernel, heads=heads, blk=blk),
        out_shape=jax.ShapeDtypeStruct((bsz, seq, width), BF16),
        grid=(bsz, seq // blk),
        in_specs=[pl.BlockSpec((1, blk, width), lambda b, i: (b, i, 0)), full, full,
                  pl.BlockSpec((blk, blk), lambda b, i: (0, 0))],
        out_specs=pl.BlockSpec((1, blk, width), lambda b, i: (b, i, 0)),
        scratch_shapes=[pltpu.VMEM((heads, blk, LANES), BF16), pltpu.VMEM((blk, width), F32),
                        pltpu.VMEM((heads, blk, 1), F32)],
        compiler_params=_cparams("arbitrary", "arbitrary"),
        name="attn",
    )(q, k, v, u_tri)


def _gelu_tanh(x):
    c = 0.7978845608028654
    hx = 0.5 * x
    return hx + hx * jnp.tanh(x * (c + (0.044715 * c) * (x * x)))


def _sigmoid(x):
    return 1.0 / (1.0 + jnp.exp(-x))


def _mix_kernel(ya_ref, uv_ref, gt_ref, x_ref, gtm_ref, shf_ref, scf_ref, gsgu_ref, ws_ref, bs_ref,
                wa_ref, wb_ref, wo_ref, gffn_ref, wr_ref, br_ref, utri_ref,
                h_ref, xn_ref, route_ref, route_t_ref, cnt_ref, yb_scr, cnt_scr,
                *, chunk, groups, n_groups, per_group):
    step = pl.program_id(0)
    tm = x_ref.shape[0]
    sgu = gsgu_ref.shape[1]
    gdim = sgu // groups

    @pl.when(step == 0)
    def _():
        cnt_scr[...] = jnp.zeros_like(cnt_scr)

    ya_proj = _dot(ya_ref[...], wa_ref[...])

    act = _gelu_tanh(uv_ref[...].astype(F32))
    u = act[:, :sgu]
    v = act[:, sgu:]
    mu = jnp.mean(v, axis=-1, keepdims=True)
    vc = v - mu
    var = jnp.mean(vc * vc, axis=-1, keepdims=True)
    vn = vc * lax.rsqrt(var + EPS) * gsgu_ref[...]
    r = lax.broadcasted_iota(jnp.int32, (chunk, chunk), 0)
    c = lax.broadcasted_iota(jnp.int32, (chunk, chunk), 1)
    wcs = [jnp.where(r >= c, ws_ref[g], 0.0).astype(BF16) for g in range(groups)]
    pair = LANES // gdim
    lane = lax.broadcasted_iota(jnp.int32, (chunk, LANES), 1)
    for ci in range(tm // chunk):
        rows = slice(ci * chunk, (ci + 1) * chunk)
        for p in range(groups // pair):
            lanes = slice(p * LANES, (p + 1) * LANES)
            slab = vn[rows, lanes]
            mix = bs_ref[:, lanes]
            for j in range(pair):
                sel = (lane >= j * gdim) & (lane < (j + 1) * gdim)
                mix = mix + _dot(wcs[p * pair + j], jnp.where(sel, slab, 0.0).astype(BF16))
            yb_scr[rows, lanes] = (u[rows, lanes] * mix).astype(BF16)

    d = x_ref.shape[1]
    gates2 = 1.0 + jnp.tanh(0.5 * gt_ref[...].astype(F32))
    merged2 = gates2[:, :d] * ya_proj + gates2[:, d:] * _dot(yb_scr[...], wb_ref[...])
    h = x_ref[...] + (0.5 * gtm_ref[0]) * _dot(merged2.astype(BF16), wo_ref[...])
    h_ref[...] = h

    xn = _rms(h, gffn_ref[...]) * (1.0 + scf_ref[0]) + shf_ref[0]
    xn_ref[...] = _pack_halves(xn)
    logits = _dot3(xn, wr_ref[...]) + br_ref[...]

    lt = logits.T
    sub = lax.broadcasted_iota(jnp.int32, (SUBLANES, tm), 0)
    neg = -jnp.inf
    gl = jnp.where(sub < n_groups, lt[0:SUBLANES], neg)
    gmax = jnp.max(gl, axis=0, keepdims=True)
    g_sel = jnp.min(jnp.where(gl == gmax, sub, SUBLANES), axis=0, keepdims=True)
    g_w = 1.0 / jnp.sum(jnp.exp(gl - gmax), axis=0, keepdims=True)
    el = lt[SUBLANES:2 * SUBLANES]
    for g in range(1, n_groups):
        el = jnp.where(g_sel == g, lt[(g + 1) * SUBLANES:(g + 2) * SUBLANES], el)
    m1 = jnp.max(el, axis=0, keepdims=True)
    j1 = jnp.min(jnp.where(el == m1, sub, SUBLANES), axis=0, keepdims=True)
    el2 = jnp.where(sub == j1, neg, el)
    m2 = jnp.max(el2, axis=0, keepdims=True)
    j2 = jnp.min(jnp.where(el2 == m2, sub, SUBLANES), axis=0, keepdims=True)
    t = jnp.exp(m2 - m1)
    w1 = g_w / (1.0 + t)
    w2 = g_w * t / (1.0 + t)
    e1 = g_sel * per_group + j1
    e2 = g_sel * per_group + j2

    row = lax.broadcasted_iota(jnp.int32, (LANES, tm), 0)
    oh1 = row == e1 + SUBLANES
    oh2 = row == e2 + SUBLANES
    both = jnp.where(oh1 | oh2, 1.0, 0.0)
    before = _dot(both.astype(BF16), utri_ref[...]) + cnt_scr[...]
    rank1 = jnp.sum(jnp.where(oh1, before, 0.0), axis=0, keepdims=True)
    rank2 = jnp.sum(jnp.where(oh2, before, 0.0), axis=0, keepdims=True)
    cnt_scr[...] = cnt_scr[...] + jnp.sum(both, axis=1, keepdims=True)
    cnt_ref[...] = cnt_scr[...]

    rec = jnp.zeros((LANES, tm), F32)
    for idx, val in enumerate((e1.astype(F32), e2.astype(F32), w1, w2, rank1, rank2)):
        rec = jnp.where(row == idx, val, rec)
    route_t_ref[...] = rec[:ROUTE_COLS]
    route_ref[...] = rec.T[:, :ROUTE_COLS]


def _mix_call(ya, uv, gates, x2, gtm, shf, scf, g_sgu, w_spatial, bias_full, wa, wb, wo, g_ffn, wr, br,
              seq, n_groups, per_group):
    n_tok, d = x2.shape
    tm = min(TOKEN_TILE, seq)
    per_b = seq // tm
    groups, chunk, _ = w_spatial.shape
    sgu = g_sgu.shape[1]
    r = lax.broadcasted_iota(jnp.int32, (tm, tm), 0)
    c = lax.broadcasted_iota(jnp.int32, (tm, tm), 1)
    utri = (r < c).astype(BF16)
    tok = lambda w: pl.BlockSpec((tm, w), lambda i: (i, 0))
    vec = pl.BlockSpec((1, 1, d), lambda i: (i // per_b, 0, 0))
    const = lambda a: pl.BlockSpec(a.shape, lambda i: (0,) * a.ndim)
    ins = [ya, uv, gates, x2, gtm, shf, scf, g_sgu, w_spatial, bias_full, wa, wb, wo, g_ffn, wr, br, utri]
    in_specs = [tok(ya.shape[1]), tok(uv.shape[1]), tok(gates.shape[1]), tok(d), vec, vec, vec]
    in_specs += [const(a) for a in ins[7:]]
    return pl.pallas_call(
        functools.partial(_mix_kernel, chunk=chunk, groups=groups, n_groups=n_groups, per_group=per_group),
        out_shape=[jax.ShapeDtypeStruct((n_tok, d), F32), jax.ShapeDtypeStruct((n_tok, d // 2), jnp.uint32),
                   jax.ShapeDtypeStruct((n_tok, ROUTE_COLS), F32), jax.ShapeDtypeStruct((ROUTE_COLS, n_tok), F32),
                   jax.ShapeDtypeStruct((LANES, 1), F32)],
        grid=(n_tok // tm,),
        in_specs=in_specs,
        out_specs=[tok(d), tok(d // 2), tok(ROUTE_COLS), pl.BlockSpec((ROUTE_COLS, tm), lambda i: (0, i)),
                   pl.BlockSpec((LANES, 1), lambda i: (0, 0))],
        scratch_shapes=[pltpu.VMEM((tm, sgu), BF16), pltpu.VMEM((LANES, 1), F32)],
        compiler_params=_cparams("arbitrary"),
        name="mix",
    )(*ins)


def _row_copy(src, s, dst, d, sem):
    return pltpu.make_async_copy(src.at[pl.ds(s, 1)], dst.at[pl.ds(d, 1)], sem)


def _dispatch_kernel(zblk_ref, *refs, tm, rows):
    dest_refs, (xn_ref, xs_ref, zero_scr, src_scr, sem, zsem) = refs[:TOP_K], refs[TOP_K:]
    step = pl.program_id(0)

    def wait_rows():
        for _ in range(TOP_K):
            pltpu.make_async_copy(src_scr, xs_ref.at[pl.ds(0, tm)], sem).wait()

    @pl.when(step == 0)
    def _():
        zero_scr[...] = jnp.zeros_like(zero_scr)

        def zero_copy(j):
            start = pl.multiple_of(zblk_ref[j] * rows, rows)
            return pltpu.make_async_copy(zero_scr, xs_ref.at[pl.ds(start, rows)], zsem)

        for j in range(zblk_ref.shape[0]):
            pl.when(zblk_ref[j] >= 0)(lambda j=j: zero_copy(j).start())
        for j in range(zblk_ref.shape[0]):
            pl.when(zblk_ref[j] >= 0)(lambda j=j: zero_copy(j).wait())

    pl.when(step > 0)(wait_rows)
    src_scr[...] = xn_ref[...]
    for j in range(tm):
        for k, dref in enumerate(dest_refs):
            _row_copy(src_scr, j, xs_ref, dref[0, 0, j], sem).start(priority=k % 2)
    pl.when(step == pl.num_programs(0) - 1)(wait_rows)


def _dest_blocks(dest, tm):
    return [dest[k].reshape(-1, 1, tm) for k in range(TOP_K)]


def _dispatch_call(zero_blocks, dest, xn, m_pad, seq):
    n_tok, dw = xn.shape
    tm = min(ROW_COPY_TILE, seq)
    smem = pl.BlockSpec((1, 1, tm), lambda i, zb: (i, 0, 0), memory_space=pltpu.SMEM)
    return pl.pallas_call(
        functools.partial(_dispatch_kernel, tm=tm, rows=EXPERT_ROWS),
        out_shape=jax.ShapeDtypeStruct((m_pad, dw), xn.dtype),
        grid_spec=pltpu.PrefetchScalarGridSpec(
            num_scalar_prefetch=1,
            grid=(n_tok // tm,),
            in_specs=[smem] * TOP_K + [pl.BlockSpec((tm, dw), lambda i, zb: (i, 0))],
            out_specs=pl.BlockSpec(memory_space=pl.ANY),
            scratch_shapes=[pltpu.VMEM((EXPERT_ROWS, dw), xn.dtype), pltpu.VMEM((tm, dw), xn.dtype),
                            pltpu.SemaphoreType.DMA(()), pltpu.SemaphoreType.DMA(())]),
        compiler_params=_cparams("arbitrary"),
        name="dispatch",
    )(zero_blocks, *_dest_blocks(dest, tm), xn)


def _expert_kernel(blk_e_ref, used_ref, xs_ref, wg_ref, wu_ref, wd_ref, y_ref, wg_bf, wu_bf, wd_bf):
    i = pl.program_id(0)
    live = i < used_ref[0]
    new_expert = (i == 0) | (blk_e_ref[i] != blk_e_ref[jnp.maximum(i - 1, 0)])

    @pl.when(live & new_expert)
    def _():
        wg_bf[...] = wg_ref[0].astype(BF16)
        wu_bf[...] = wu_ref[0].astype(BF16)
        wd_bf[...] = wd_ref[0].astype(BF16)

    @pl.when(live)
    def _():
        xb = _unpack_halves(xs_ref[...]).astype(BF16)
        g = _dot(xb, wg_bf[...])
        hid = g * _sigmoid(g) * _dot(xb, wu_bf[...])
        y_ref[...] = _pack_halves(_dot(hid.astype(BF16), wd_bf[...]))

    @pl.when(jnp.logical_not(live))
    def _():
        y_ref[...] = jnp.zeros_like(y_ref)


def _expert_call(blk_e, used, xs, wg, wu, wd):
    m_pad, dw = xs.shape
    _, d, f = wg.shape
    rows = EXPERT_ROWS
    return pl.pallas_call(
        _expert_kernel,
        out_shape=jax.ShapeDtypeStruct((m_pad, dw), xs.dtype),
        grid_spec=pltpu.PrefetchScalarGridSpec(
            num_scalar_prefetch=2,
            grid=(m_pad // rows,),
            in_specs=[pl.BlockSpec((rows, dw), lambda i, be, us: (i, 0)),
                      pl.BlockSpec((1, d, f), lambda i, be, us: (be[i], 0, 0)),
                      pl.BlockSpec((1, d, f), lambda i, be, us: (be[i], 0, 0)),
                      pl.BlockSpec((1, f, d), lambda i, be, us: (be[i], 0, 0))],
            out_specs=pl.BlockSpec((rows, dw), lambda i, be, us: (i, 0)),
            scratch_shapes=[pltpu.VMEM((d, f), BF16), pltpu.VMEM((d, f), BF16), pltpu.VMEM((f, d), BF16)]),
        compiler_params=_cparams("arbitrary"),
        name="experts",
    )(blk_e, used, xs, wg, wu, wd)


def _combine_kernel(*refs, tm, final):
    cur_refs, nxt_refs = refs[:TOP_K], refs[TOP_K:2 * TOP_K]
    h_ref, route_ref, gtf_ref, gfin_ref, y_ref, o_ref, land, rows, sem = refs[2 * TOP_K:]
    step = pl.program_id(0)

    def gather(dest_refs):
        for j in range(tm):
            for k, dref in enumerate(dest_refs):
                pltpu.make_async_copy(y_ref.at[pl.ds(dref[0, 0, j], 1)], land.at[k, pl.ds(j, 1)],
                                      sem).start(priority=k % 2)

    def wait_rows():
        for k in range(TOP_K):
            pltpu.make_async_copy(y_ref.at[pl.ds(0, tm)], land.at[k], sem).wait()

    pl.when(step == 0)(lambda: gather(cur_refs))
    wait_rows()
    rows[...] = land[...]
    gather(nxt_refs)
    route = route_ref[...]
    y = route[:, 2:3] * _unpack_halves(rows[0]) + route[:, 3:4] * _unpack_halves(rows[1])
    h = h_ref[...] + gtf_ref[0] * y
    o_ref[...] = _rms(h, gfin_ref[...]) if final else h
    pl.when(step == pl.num_programs(0) - 1)(wait_rows)


def _combine_call(dest, h, route, gtf, g_final, yb, seq, final):
    n_tok, d = h.shape
    tm = min(ROW_COPY_TILE, seq)
    per_b = seq // tm
    steps = n_tok // tm
    tok = lambda w: pl.BlockSpec((tm, w), lambda i: (i, 0))
    cur = pl.BlockSpec((1, 1, tm), lambda i: (i, 0, 0), memory_space=pltpu.SMEM)
    nxt = pl.BlockSpec((1, 1, tm), lambda i: (jnp.minimum(i + 1, steps - 1), 0, 0), memory_space=pltpu.SMEM)
    blocks = _dest_blocks(dest, tm)
    return pl.pallas_call(
        functools.partial(_combine_kernel, tm=tm, final=final),
        out_shape=jax.ShapeDtypeStruct((n_tok, d), F32),
        grid=(steps,),
        in_specs=[cur] * TOP_K + [nxt] * TOP_K + [
            tok(d), tok(ROUTE_COLS),
            pl.BlockSpec((1, 1, d), lambda i: (i // per_b, 0, 0)),
            pl.BlockSpec((1, d), lambda i: (0, 0)),
            pl.BlockSpec(memory_space=pl.ANY)],
        out_specs=tok(d),
        scratch_shapes=[pltpu.VMEM((TOP_K, tm, yb.shape[1]), yb.dtype),
                        pltpu.VMEM((TOP_K, tm, yb.shape[1]), yb.dtype), pltpu.SemaphoreType.DMA(())],
        compiler_params=_cparams("arbitrary"),
        name="combine",
    )(*blocks, *blocks, h, route, gtf, g_final, yb)


def _routing_plan(route_t, counts, n_experts, m_pad):
    eid = route_t[0:TOP_K].astype(jnp.int32)
    rank = route_t[4:4 + TOP_K].astype(jnp.int32)
    cnt = counts.astype(jnp.int32)
    padded = (cnt + EXPERT_ROWS - 1) // EXPERT_ROWS * EXPERT_ROWS
    pend = jnp.cumsum(padded)
    pstart = pend - padded
    dest = rank
    for e in range(n_experts):
        dest = dest + jnp.where(eid == e, pstart[e], 0)
    nblk = m_pad // EXPERT_ROWS
    blk_row = jnp.arange(nblk, dtype=jnp.int32) * EXPERT_ROWS
    blk_e = jnp.minimum(jnp.sum(pend[None, :] <= blk_row[:, None], axis=1), n_experts - 1).astype(jnp.int32)
    used = (pend[-1:] // EXPERT_ROWS).astype(jnp.int32)
    last_blk = jnp.where(padded > 0, pend // EXPERT_ROWS - 1, -1)
    tail_blk = used + jnp.arange(n_experts, dtype=jnp.int32)
    tail_blk = jnp.where(tail_blk < nblk, tail_blk, -1)
    zero_blocks = jnp.concatenate([last_blk, tail_blk]).astype(jnp.int32)
    return dest, blk_e, used, zero_blocks


def kernel(x, c, g_mix, g_ffn, w_ada, b_ada, w_in, w_sba_out, g_sgu, w_spatial, b_spatial, w_sgu_out, w_out,
           w_router_group, b_router_group, w_router_expert, b_router_expert, w_expert_gate, w_expert_up,
           w_expert_down, g_final):
    bsz, seq, d = x.shape
    depth = w_in.shape[0]
    sba = w_sba_out.shape[1]
    sgu = g_sgu.shape[1]
    heads = sba // HEAD_DIM
    groups, chunk = w_spatial.shape[1], w_spatial.shape[2]
    n_groups = w_router_group.shape[2]
    n_experts = w_router_expert.shape[2]
    per_group = n_experts // n_groups
    n_tok = bsz * seq
    m_pad = n_tok * TOP_K + n_experts * EXPERT_ROWS
    widths = (sba, sba, sba, 2 * sgu, 2 * d)
    assert seq % chunk == 0 and LANES % (sgu // groups) == 0
    assert d % (2 * LANES) == 0
    assert per_group == SUBLANES and n_groups <= SUBLANES and (n_groups + 1) * SUBLANES <= LANES

    h = x.reshape(n_tok, d)
    for l in range(depth):
        mod = _mod_call(c, w_ada[l], b_ada[l])
        sh_m, sc_m, gt_m, sh_f, sc_f, gt_f = [mod[:, i * d:(i + 1) * d].reshape(bsz, 1, d) for i in range(6)]

        q, k, v, uv, gates = _proj_call(h, sh_m, sc_m, g_mix[l].reshape(1, d), w_in[l].astype(BF16), seq, widths)
        ya = _attn_call(q.reshape(bsz, seq, sba), k.reshape(bsz, seq, sba), v.reshape(bsz, seq, sba), heads)

        bias_full = jnp.repeat(b_spatial[l].T, sgu // groups, axis=1)
        gpad = SUBLANES - n_groups
        wr = jnp.concatenate([jnp.pad(w_router_group[l], ((0, 0), (0, gpad))), w_router_expert[l]], axis=1)
        wr = jnp.pad(wr, ((0, 0), (0, LANES - wr.shape[1])))
        br = jnp.concatenate([jnp.pad(b_router_group[l], (0, gpad)), b_router_expert[l]])
        br = jnp.pad(br, (0, LANES - br.shape[0])).reshape(1, LANES)
        h1, xn, route, route_t, counts = _mix_call(
            ya.reshape(n_tok, sba), uv, gates, h, gt_m, sh_f, sc_f, g_sgu[l].reshape(1, sgu), w_spatial[l],
            bias_full, w_sba_out[l].astype(BF16), w_sgu_out[l].astype(BF16), w_out[l].astype(BF16),
            g_ffn[l].reshape(1, d), wr, br, seq, n_groups, per_group)

        dest, blk_e, used, zero_blocks = _routing_plan(route_t, counts[SUBLANES:SUBLANES + n_experts, 0],
                                                       n_experts, m_pad)
        xs = _dispatch_call(zero_blocks, dest, xn, m_pad, seq)
        yb = _expert_call(blk_e, used, xs, w_expert_gate[l], w_expert_up[l], w_expert_down[l])
        h = _combine_call(dest, h1, route, gt_f, g_final.reshape(1, d), yb, seq, final=l == depth - 1)
    return h.reshape(bsz, seq, d)
```

```python
import functools

import jax
import jax.numpy as jnp
from jax import lax
from jax.experimental import pallas as pl
from jax.experimental.pallas import tpu as pltpu

F32 = jnp.float32
BF16 = jnp.bfloat16

HEAD_DIM = 64
TOP_K = 2
EPS = 1e-6
LOG2E = 1.4426950408889634
SIGN_BIT = 0x80000000
BF16_BITS = 0xFFFF0000
ATTN_EXIT_LOG2 = 160.0
DEAD_CARRY = 1e30
LANES = 128
SUBLANES = 8
ATTN_BLOCK = 256
EXPERT_ROWS = 512
TOKEN_TILE = 1024
ROW_COPY_TILE = 512
PROJ_TILE = 512
ROUTE_COLS = 8
VMEM_LIMIT = 56 * 1024 * 1024


def _cparams(*sem):
    return pltpu.CompilerParams(dimension_semantics=sem, vmem_limit_bytes=VMEM_LIMIT)


def _split_bf16(a):
    hi = a.astype(BF16)
    lo = (a - hi.astype(F32)).astype(BF16)
    return hi, lo


def _dot(a, b):
    return jnp.dot(a, b, preferred_element_type=F32)


def _dot3(a, b):
    ah, al = _split_bf16(a)
    bh, bl = _split_bf16(b)
    return _dot(ah, bh) + _dot(ah, bl) + _dot(al, bh)


def _pack_halves(x):
    half = x.shape[1] // 2
    hi = lax.bitcast_convert_type(x[:, :half].astype(BF16).astype(F32), jnp.uint32)
    lo = lax.bitcast_convert_type(x[:, half:].astype(BF16).astype(F32), jnp.uint32)
    return hi | (lo >> 16)


def _unpack_halves(u):
    hi = lax.bitcast_convert_type(u & jnp.uint32(BF16_BITS), F32)
    lo = lax.bitcast_convert_type(u << 16, F32)
    return jnp.concatenate([hi, lo], axis=1)


def _rms(x, g):
    ms = jnp.mean(x * x, axis=-1, keepdims=True)
    return x * lax.rsqrt(ms + EPS) * g


def _mod_kernel(c_ref, w_ref, b_ref, o_ref):
    c = c_ref[...]
    ca = c * (1.0 / (1.0 + jnp.exp(-c)))
    o_ref[...] = _dot3(ca, w_ref[...]) + b_ref[...]


def _mod_call(c, w_ada, b_ada):
    bsz, d = c.shape
    n = w_ada.shape[1]
    tn = n // 4 if n % (4 * LANES) == 0 else n
    return pl.pallas_call(
        _mod_kernel,
        out_shape=jax.ShapeDtypeStruct((bsz, n), F32),
        grid=(n // tn,),
        in_specs=[pl.BlockSpec((bsz, d), lambda j: (0, 0)),
                  pl.BlockSpec((d, tn), lambda j: (0, j)),
                  pl.BlockSpec((1, tn), lambda j: (0, j))],
        out_specs=pl.BlockSpec((bsz, tn), lambda j: (0, j)),
        compiler_params=_cparams("arbitrary"),
        name="mod",
    )(c, w_ada, b_ada.reshape(1, n))


def _proj_kernel(x_ref, sh_ref, sc_ref, g_ref, w_ref, q_ref, k_ref, v_ref, uv_ref, gt_ref, *, widths):
    x = x_ref[...]
    n = _rms(x, g_ref[...]) * (1.0 + sc_ref[0]) + sh_ref[0]
    nb = n.astype(BF16)
    off = 0
    for ref, wd in zip((q_ref, k_ref, v_ref, uv_ref, gt_ref), widths):
        p = _dot(nb, w_ref[:, off:off + wd])
        if ref is q_ref:
            p = p * (HEAD_DIM ** -0.5 * LOG2E)
        ref[...] = p.astype(ref.dtype)
        off += wd


def _proj_call(x2, sh, sc, g, w_in_bf, seq, widths):
    n_tok, d = x2.shape
    tm = min(PROJ_TILE, seq)
    per_b = seq // tm
    cols = w_in_bf.shape[1]
    tok = lambda w: pl.BlockSpec((tm, w), lambda i: (i, 0))
    vec = pl.BlockSpec((1, 1, d), lambda i: (i // per_b, 0, 0))
    return pl.pallas_call(
        functools.partial(_proj_kernel, widths=widths),
        out_shape=[jax.ShapeDtypeStruct((n_tok, w), BF16) for w in widths],
        grid=(n_tok // tm,),
        in_specs=[tok(d), vec, vec,
                  pl.BlockSpec((1, d), lambda i: (0, 0)),
                  pl.BlockSpec((d, cols), lambda i: (0, 0))],
        out_specs=[tok(w) for w in widths],
        compiler_params=_cparams("arbitrary"),
        name="proj",
    )(x2, sh, sc, g, w_in_bf)


def _attn_kernel(q_ref, k_ref, v_ref, u_ref, o_ref, qm_ref, acc_ref, carry_ref, *, heads, blk):
    qi = pl.program_id(1)
    u_tri = u_ref[...]
    row = lax.broadcasted_iota(jnp.int32, (blk, blk), 0)
    col = lax.broadcasted_iota(jnp.int32, (blk, blk), 1)
    causal = col < row
    per_slab = LANES // HEAD_DIM
    lane = lax.broadcasted_iota(jnp.int32, (blk, LANES), 1)
    own = [(lane >= j * HEAD_DIM) & (lane < (j + 1) * HEAD_DIM) for j in range(per_slab)]
    for h in range(heads):
        slab = slice(h // per_slab * LANES, (h // per_slab + 1) * LANES)
        qm_ref[h] = jnp.where(own[h % per_slab], q_ref[0, :, slab], 0.0).astype(BF16)

    def key_tile(start, first, dead=None):
        mask = causal if first else None
        slabs = [slice(p * LANES, (p + 1) * LANES) for p in range(heads // per_slab)]
        s_all, cum_all, w_all, carries = {}, {}, {}, []

        def scores(h):
            s_all[h] = lax.dot_general(qm_ref[h], k_ref[0, pl.ds(start, blk), slabs[h // per_slab]],
                                       (((1,), (1,)), ((), ())), preferred_element_type=F32)

        def suffix_sums(h):
            s = s_all[h]
            neg_abs = lax.bitcast_convert_type(lax.bitcast_convert_type(s, jnp.uint32) | jnp.uint32(SIGN_BIT), F32)
            sp = jnp.maximum(s, 0.0) + jnp.log(1.0 + jnp.exp2(neg_abs)) * LOG2E
            if mask is not None:
                sp = jnp.where(mask, sp, 0.0)
            hi = lax.bitcast_convert_type(lax.bitcast_convert_type(sp, jnp.uint32) & jnp.uint32(BF16_BITS), F32)
            cum_all[h] = _dot(hi.astype(BF16), u_tri) + _dot((sp - hi).astype(BF16), u_tri)

        def weights(h):
            cum = cum_all[h]
            if first:
                w = jnp.exp2(s_all[h] - cum)
                carry = cum[:, 0:1]
            else:
                carry = carry_ref[h]
                if dead is not None:
                    carry = jnp.where(dead, DEAD_CARRY, carry)
                w = jnp.exp2((s_all[h] - carry) - cum)
                carry = carry + cum[:, 0:1]
            w_all[h] = (w if mask is None else jnp.where(mask, w, 0.0)).astype(BF16)
            carry_ref[h] = carry
            carries.append(carry)

        def values(p):
            vp = v_ref[0, pl.ds(start, blk), slabs[p]]
            upd = None
            for j in range(per_slab):
                pv = _dot(w_all[p * per_slab + j], vp)
                upd = pv if upd is None else jnp.where(own[j], pv, upd)
            acc_ref[:, slabs[p]] = upd if first else acc_ref[:, slabs[p]] + upd

        for t in range(heads + 2):
            if t < heads:
                scores(t)
            if 1 <= t <= heads:
                suffix_sums(t - 1)
            if t >= 2:
                weights(t - 2)
                if (t - 2) % per_slab == per_slab - 1:
                    values((t - 2) // per_slab)
        return jnp.min(functools.reduce(jnp.minimum, carries))

    key_tile(pl.multiple_of(qi * blk, blk), True)
    cmin = key_tile(pl.multiple_of(jnp.maximum(qi - 1, 0) * blk, blk), False, dead=qi == 0)

    def cond(state):
        i, cmin = state
        return (i < qi) & (cmin < ATTN_EXIT_LOG2)

    def body(state):
        i, _ = state
        return i + 1, key_tile(pl.multiple_of((qi - 1 - i) * blk, blk), False)

    lax.while_loop(cond, body, (jnp.int32(1), cmin))
    o_ref[0] = acc_ref[...].astype(o_ref.dtype)


def _attn_call(q, k, v, heads):
    bsz, seq, width = q.shape
    blk = min(ATTN_BLOCK, seq)
    r = lax.broadcasted_iota(jnp.int32, (blk, blk), 0)
    c = lax.broadcasted_iota(jnp.int32, (blk, blk), 1)
    u_tri = (r >= c).astype(BF16)
    full = pl.BlockSpec((1, seq, width), lambda b, i: (b, 0, 0))
    return pl.pallas_call(
        functools.partial(_attn_kernel, heads=heads, blk=blk),
        out_shape=jax.ShapeDtypeStruct((bsz, seq, width), BF16),
        grid=(bsz, seq // blk),
        in_specs=[pl.BlockSpec((1, blk, width), lambda b, i: (b, i, 0)), full, full,
                  pl.BlockSpec((blk, blk), lambda b, i: (0, 0))],
        out_specs=pl.BlockSpec((1, blk, width), lambda b, i: (b, i, 0)),
        scratch_shapes=[pltpu.VMEM((heads, blk, LANES), BF16), pltpu.VMEM((blk, width), F32),
                        pltpu.VMEM((heads, blk, 1), F32)],
        compiler_params=_cparams("arbitrary", "arbitrary"),
        name="attn",
    )(q, k, v, u_tri)


def _gelu_tanh(x):
    c = 0.7978845608028654
    hx = 0.5 * x
    return hx + hx * jnp.tanh(x * (c + (0.044715 * c) * (x * x)))


def _sigmoid(x):
    return 1.0 / (1.0 + jnp.exp(-x))


def _mix_kernel(ya_ref, uv_ref, gt_ref, x_ref, gtm_ref, shf_ref, scf_ref, gsgu_ref, ws_ref, bs_ref,
                wa_ref, wb_ref, wo_ref, gffn_ref, wr_ref, br_ref, utri_ref,
                h_ref, xn_ref, route_ref, route_t_ref, cnt_ref, yb_scr, cnt_scr,
                *, chunk, groups, n_groups, per_group):
    step = pl.program_id(0)
    tm = x_ref.shape[0]
    sgu = gsgu_ref.shape[1]
    gdim = sgu // groups

    @pl.when(step == 0)
    def _():
        cnt_scr[...] = jnp.zeros_like(cnt_scr)

    ya_proj = _dot(ya_ref[...], wa_ref[...])

    act = _gelu_tanh(uv_ref[...].astype(F32))
    u = act[:, :sgu]
    v = act[:, sgu:]
    mu = jnp.mean(v, axis=-1, keepdims=True)
    vc = v - mu
    var = jnp.mean(vc * vc, axis=-1, keepdims=True)
    vn = vc * lax.rsqrt(var + EPS) * gsgu_ref[...]
    r = lax.broadcasted_iota(jnp.int32, (chunk, chunk), 0)
    c = lax.broadcasted_iota(jnp.int32, (chunk, chunk), 1)
    wcs = [jnp.where(r >= c, ws_ref[g], 0.0).astype(BF16) for g in range(groups)]
    pair = LANES // gdim
    lane = lax.broadcasted_iota(jnp.int32, (chunk, LANES), 1)
    for ci in range(tm // chunk):
        rows = slice(ci * chunk, (ci + 1) * chunk)
        for p in range(groups // pair):
            lanes = slice(p * LANES, (p + 1) * LANES)
            slab = vn[rows, lanes]
            mix = bs_ref[:, lanes]
            for j in range(pair):
                sel = (lane >= j * gdim) & (lane < (j + 1) * gdim)
                mix = mix + _dot(wcs[p * pair + j], jnp.where(sel, slab, 0.0).astype(BF16))
            yb_scr[rows, lanes] = (u[rows, lanes] * mix).astype(BF16)

    d = x_ref.shape[1]
    gates2 = 1.0 + jnp.tanh(0.5 * gt_ref[...].astype(F32))
    merged2 = gates2[:, :d] * ya_proj + gates2[:, d:] * _dot(yb_scr[...], wb_ref[...])
    h = x_ref[...] + (0.5 * gtm_ref[0]) * _dot(merged2.astype(BF16), wo_ref[...])
    h_ref[...] = h

    xn = _rms(h, gffn_ref[...]) * (1.0 + scf_ref[0]) + shf_ref[0]
    xn_ref[...] = _pack_halves(xn)
    logits = _dot3(xn, wr_ref[...]) + br_ref[...]

    lt = logits.T
    sub = lax.broadcasted_iota(jnp.int32, (SUBLANES, tm), 0)
    neg = -jnp.inf
    gl = jnp.where(sub < n_groups, lt[0:SUBLANES], neg)
    gmax = jnp.max(gl, axis=0, keepdims=True)
    g_sel = jnp.min(jnp.where(gl == gmax, sub, SUBLANES), axis=0, keepdims=True)
    g_w = 1.0 / jnp.sum(jnp.exp(gl - gmax), axis=0, keepdims=True)
    el = lt[SUBLANES:2 * SUBLANES]
    for g in range(1, n_groups):
        el = jnp.where(g_sel == g, lt[(g + 1) * SUBLANES:(g + 2) * SUBLANES], el)
    m1 = jnp.max(el, axis=0, keepdims=True)
    j1 = jnp.min(jnp.where(el == m1, sub, SUBLANES), axis=0, keepdims=True)
    el2 = jnp.where(sub == j1, neg, el)
    m2 = jnp.max(el2, axis=0, keepdims=True)
    j2 = jnp.min(jnp.where(el2 == m2, sub, SUBLANES), axis=0, keepdims=True)
    t = jnp.exp(m2 - m1)
    w1 = g_w / (1.0 + t)
    w2 = g_w * t / (1.0 + t)
    e1 = g_sel * per_group + j1
    e2 = g_sel * per_group + j2

    row = lax.broadcasted_iota(jnp.int32, (LANES, tm), 0)
    oh1 = row == e1 + SUBLANES
    oh2 = row == e2 + SUBLANES
    both = jnp.where(oh1 | oh2, 1.0, 0.0)
    before = _dot(both.astype(BF16), utri_ref[...]) + cnt_scr[...]
    rank1 = jnp.sum(jnp.where(oh1, before, 0.0), axis=0, keepdims=True)
    rank2 = jnp.sum(jnp.where(oh2, before, 0.0), axis=0, keepdims=True)
    cnt_scr[...] = cnt_scr[...] + jnp.sum(both, axis=1, keepdims=True)
    cnt_ref[...] = cnt_scr[...]

    rec = jnp.zeros((LANES, tm), F32)
    for idx, val in enumerate((e1.astype(F32), e2.astype(F32), w1, w2, rank1, rank2)):
        rec = jnp.where(row == idx, val, rec)
    route_t_ref[...] = rec[:ROUTE_COLS]
    route_ref[...] = rec.T[:, :ROUTE_COLS]


def _mix_call(ya, uv, gates, x2, gtm, shf, scf, g_sgu, w_spatial, bias_full, wa, wb, wo, g_ffn, wr, br,
              seq, n_groups, per_group):
    n_tok, d = x2.shape
    tm = min(TOKEN_TILE, seq)
    per_b = seq // tm
    groups, chunk, _ = w_spatial.shape
    sgu = g_sgu.shape[1]
    r = lax.broadcasted_iota(jnp.int32, (tm, tm), 0)
    c = lax.broadcasted_iota(jnp.int32, (tm, tm), 1)
    utri = (r < c).astype(BF16)
    tok = lambda w: pl.BlockSpec((tm, w), lambda i: (i, 0))
    vec = pl.BlockSpec((1, 1, d), lambda i: (i // per_b, 0, 0))
    const = lambda a: pl.BlockSpec(a.shape, lambda i: (0,) * a.ndim)
    ins = [ya, uv, gates, x2, gtm, shf, scf, g_sgu, w_spatial, bias_full, wa, wb, wo, g_ffn, wr, br, utri]
    in_specs = [tok(ya.shape[1]), tok(uv.shape[1]), tok(gates.shape[1]), tok(d), vec, vec, vec]
    in_specs += [const(a) for a in ins[7:]]
    return pl.pallas_call(
        functools.partial(_mix_kernel, chunk=chunk, groups=groups, n_groups=n_groups, per_group=per_group),
        out_shape=[jax.ShapeDtypeStruct((n_tok, d), F32), jax.ShapeDtypeStruct((n_tok, d // 2), jnp.uint32),
                   jax.ShapeDtypeStruct((n_tok, ROUTE_COLS), F32), jax.ShapeDtypeStruct((ROUTE_COLS, n_tok), F32),
                   jax.ShapeDtypeStruct((LANES, 1), F32)],
        grid=(n_tok // tm,),
        in_specs=in_specs,
        out_specs=[tok(d), tok(d // 2), tok(ROUTE_COLS), pl.BlockSpec((ROUTE_COLS, tm), lambda i: (0, i)),
                   pl.BlockSpec((LANES, 1), lambda i: (0, 0))],
        scratch_shapes=[pltpu.VMEM((tm, sgu), BF16), pltpu.VMEM((LANES, 1), F32)],
        compiler_params=_cparams("arbitrary"),
        name="mix",
    )(*ins)


def _row_copy(src, s, dst, d, sem):
    return pltpu.make_async_copy(src.at[pl.ds(s, 1)], dst.at[pl.ds(d, 1)], sem)


def _dispatch_kernel(zblk_ref, *refs, tm, rows):
    dest_refs, (xn_ref, xs_ref, zero_scr, src_scr, sem, zsem) = refs[:TOP_K], refs[TOP_K:]
    step = pl.program_id(0)

    def wait_rows():
        for _ in range(TOP_K):
            pltpu.make_async_copy(src_scr, xs_ref.at[pl.ds(0, tm)], sem).wait()

    @pl.when(step == 0)
    def _():
        zero_scr[...] = jnp.zeros_like(zero_scr)

        def zero_copy(j):
            start = pl.multiple_of(zblk_ref[j] * rows, rows)
            return pltpu.make_async_copy(zero_scr, xs_ref.at[pl.ds(start, rows)], zsem)

        for j in range(zblk_ref.shape[0]):
            pl.when(zblk_ref[j] >= 0)(lambda j=j: zero_copy(j).start())
        for j in range(zblk_ref.shape[0]):
            pl.when(zblk_ref[j] >= 0)(lambda j=j: zero_copy(j).wait())

    pl.when(step > 0)(wait_rows)
    src_scr[...] = xn_ref[...]
    for j in range(tm):
        for k, dref in enumerate(dest_refs):
            _row_copy(src_scr, j, xs_ref, dref[0, 0, j], sem).start(priority=k % 2)
    pl.when(step == pl.num_programs(0) - 1)(wait_rows)


def _dest_blocks(dest, tm):
    return [dest[k].reshape(-1, 1, tm) for k in range(TOP_K)]


def _dispatch_call(zero_blocks, dest, xn, m_pad, seq):
    n_tok, dw = xn.shape
    tm = min(ROW_COPY_TILE, seq)
    smem = pl.BlockSpec((1, 1, tm), lambda i, zb: (i, 0, 0), memory_space=pltpu.SMEM)
    return pl.pallas_call(
        functools.partial(_dispatch_kernel, tm=tm, rows=EXPERT_ROWS),
        out_shape=jax.ShapeDtypeStruct((m_pad, dw), xn.dtype),
        grid_spec=pltpu.PrefetchScalarGridSpec(
            num_scalar_prefetch=1,
            grid=(n_tok // tm,),
            in_specs=[smem] * TOP_K + [pl.BlockSpec((tm, dw), lambda i, zb: (i, 0))],
            out_specs=pl.BlockSpec(memory_space=pl.ANY),
            scratch_shapes=[pltpu.VMEM((EXPERT_ROWS, dw), xn.dtype), pltpu.VMEM((tm, dw), xn.dtype),
                            pltpu.SemaphoreType.DMA(()), pltpu.SemaphoreType.DMA(())]),
        compiler_params=_cparams("arbitrary"),
        name="dispatch",
    )(zero_blocks, *_dest_blocks(dest, tm), xn)


def _expert_kernel(blk_e_ref, used_ref, xs_ref, wg_ref, wu_ref, wd_ref, y_ref, wg_bf, wu_bf, wd_bf):
    i = pl.program_id(0)
    live = i < used_ref[0]
    new_expert = (i == 0) | (blk_e_ref[i] != blk_e_ref[jnp.maximum(i - 1, 0)])

    @pl.when(live & new_expert)
    def _():
        wg_bf[...] = wg_ref[0].astype(BF16)
        wu_bf[...] = wu_ref[0].astype(BF16)
        wd_bf[...] = wd_ref[0].astype(BF16)

    @pl.when(live)
    def _():
        xb = _unpack_halves(xs_ref[...]).astype(BF16)
        g = _dot(xb, wg_bf[...])
        hid = g * _sigmoid(g) * _dot(xb, wu_bf[...])
        y_ref[...] = _pack_halves(_dot(hid.astype(BF16), wd_bf[...]))

    @pl.when(jnp.logical_not(live))
    def _():
        y_ref[...] = jnp.zeros_like(y_ref)


def _expert_call(blk_e, used, xs, wg, wu, wd):
    m_pad, dw = xs.shape
    _, d, f = wg.shape
    rows = EXPERT_ROWS
    return pl.pallas_call(
        _expert_kernel,
        out_shape=jax.ShapeDtypeStruct((m_pad, dw), xs.dtype),
        grid_spec=pltpu.PrefetchScalarGridSpec(
            num_scalar_prefetch=2,
            grid=(m_pad // rows,),
            in_specs=[pl.BlockSpec((rows, dw), lambda i, be, us: (i, 0)),
                      pl.BlockSpec((1, d, f), lambda i, be, us: (be[i], 0, 0)),
                      pl.BlockSpec((1, d, f), lambda i, be, us: (be[i], 0, 0)),
                      pl.BlockSpec((1, f, d), lambda i, be, us: (be[i], 0, 0))],
            out_specs=pl.BlockSpec((rows, dw), lambda i, be, us: (i, 0)),
            scratch_shapes=[pltpu.VMEM((d, f), BF16), pltpu.VMEM((d, f), BF16), pltpu.VMEM((f, d), BF16)]),
        compiler_params=_cparams("arbitrary"),
        name="experts",
    )(blk_e, used, xs, wg, wu, wd)


def _combine_kernel(*refs, tm, final):
    cur_refs, nxt_refs = refs[:TOP_K], refs[TOP_K:2 * TOP_K]
    h_ref, route_ref, gtf_ref, gfin_ref, y_ref, o_ref, land, rows, sem = refs[2 * TOP_K:]
    step = pl.program_id(0)

    def gather(dest_refs):
        for j in range(tm):
            for k, dref in enumerate(dest_refs):
                pltpu.make_async_copy(y_ref.at[pl.ds(dref[0, 0, j], 1)], land.at[k, pl.ds(j, 1)],
                                      sem).start(priority=k % 2)

    def wait_rows():
        for k in range(TOP_K):
            pltpu.make_async_copy(y_ref.at[pl.ds(0, tm)], land.at[k], sem).wait()

    pl.when(step == 0)(lambda: gather(cur_refs))
    wait_rows()
    rows[...] = land[...]
    gather(nxt_refs)
    route = route_ref[...]
    y = route[:, 2:3] * _unpack_halves(rows[0]) + route[:, 3:4] * _unpack_halves(rows[1])
    h = h_ref[...] + gtf_ref[0] * y
    o_ref[...] = _rms(h, gfin_ref[...]) if final else h
    pl.when(step == pl.num_programs(0) - 1)(wait_rows)


def _combine_call(dest, h, route, gtf, g_final, yb, seq, final):
    n_tok, d = h.shape
    tm = min(ROW_COPY_TILE, seq)
    per_b = seq // tm
    steps = n_tok // tm
    tok = lambda w: pl.BlockSpec((tm, w), lambda i: (i, 0))
    cur = pl.BlockSpec((1, 1, tm), lambda i: (i, 0, 0), memory_space=pltpu.SMEM)
    nxt = pl.BlockSpec((1, 1, tm), lambda i: (jnp.minimum(i + 1, steps - 1), 0, 0), memory_space=pltpu.SMEM)
    blocks = _dest_blocks(dest, tm)
    return pl.pallas_call(
        functools.partial(_combine_kernel, tm=tm, final=final),
        out_shape=jax.ShapeDtypeStruct((n_tok, d), F32),
        grid=(steps,),
        in_specs=[cur] * TOP_K + [nxt] * TOP_K + [
            tok(d), tok(ROUTE_COLS),
            pl.BlockSpec((1, 1, d), lambda i: (i // per_b, 0, 0)),
            pl.BlockSpec((1, d), lambda i: (0, 0)),
            pl.BlockSpec(memory_space=pl.ANY)],
        out_specs=tok(d),
        scratch_shapes=[pltpu.VMEM((TOP_K, tm, yb.shape[1]), yb.dtype),
                        pltpu.VMEM((TOP_K, tm, yb.shape[1]), yb.dtype), pltpu.SemaphoreType.DMA(())],
        compiler_params=_cparams("arbitrary"),
        name="combine",
    )(*blocks, *blocks, h, route, gtf, g_final, yb)


def _routing_plan(route_t, counts, n_experts, m_pad):
    eid = route_t[0:TOP_K].astype(jnp.int32)
    rank = route_t[4:4 + TOP_K].astype(jnp.int32)
    cnt = counts.astype(jnp.int32)
    padded = (cnt + EXPERT_ROWS - 1) // EXPERT_ROWS * EXPERT_ROWS
    pend = jnp.cumsum(padded)
    pstart = pend - padded
    dest = rank
    for e in range(n_experts):
        dest = dest + jnp.where(eid == e, pstart[e], 0)
    nblk = m_pad // EXPERT_ROWS
    blk_row = jnp.arange(nblk, dtype=jnp.int32) * EXPERT_ROWS
    blk_e = jnp.minimum(jnp.sum(pend[None, :] <= blk_row[:, None], axis=1), n_experts - 1).astype(jnp.int32)
    used = (pend[-1:] // EXPERT_ROWS).astype(jnp.int32)
    last_blk = jnp.where(padded > 0, pend // EXPERT_ROWS - 1, -1)
    tail_blk = used + jnp.arange(n_experts, dtype=jnp.int32)
    tail_blk = jnp.where(tail_blk < nblk, tail_blk, -1)
    zero_blocks = jnp.concatenate([last_blk, tail_blk]).astype(jnp.int32)
    return dest, blk_e, used, zero_blocks


def kernel(x, c, g_mix, g_ffn, w_ada, b_ada, w_in, w_sba_out, g_sgu, w_spatial, b_spatial, w_sgu_out, w_out,
           w_router_group, b_router_group, w_router_expert, b_router_expert, w_expert_gate, w_expert_up,
           w_expert_down, g_final):
    bsz, seq, d = x.shape
    depth = w_in.shape[0]
    sba = w_sba_out.shape[1]
    sgu = g_sgu.shape[1]
    heads = sba // HEAD_DIM
    groups, chunk = w_spatial.shape[1], w_spatial.shape[2]
    n_groups = w_router_group.shape[2]
    n_experts = w_router_expert.shape[2]
    per_group = n_experts // n_groups
    n_tok = bsz * seq
    m_pad = n_tok * TOP_K + n_experts * EXPERT_ROWS
    widths = (sba, sba, sba, 2 * sgu, 2 * d)
    assert seq % chunk == 0 and LANES % (sgu // groups) == 0
    assert d % (2 * LANES) == 0
    assert per_group == SUBLANES and n_groups <= SUBLANES and (n_groups + 1) * SUBLANES <= LANES

    h = x.reshape(n_tok, d)
    for l in range(depth):
        mod = _mod_call(c, w_ada[l], b_ada[l])
        sh_m, sc_m, gt_m, sh_f, sc_f, gt_f = [mod[:, i * d:(i + 1) * d].reshape(bsz, 1, d) for i in range(6)]

        q, k, v, uv, gates = _proj_call(h, sh_m, sc_m, g_mix[l].reshape(1, d), w_in[l].astype(BF16), seq, widths)
        ya = _attn_call(q.reshape(bsz, seq, sba), k.reshape(bsz, seq, sba), v.reshape(bsz, seq, sba), heads)

        bias_full = jnp.repeat(b_spatial[l].T, sgu // groups, axis=1)
        gpad = SUBLANES - n_groups
        wr = jnp.concatenate([jnp.pad(w_router_group[l], ((0, 0), (0, gpad))), w_router_expert[l]], axis=1)
        wr = jnp.pad(wr, ((0, 0), (0, LANES - wr.shape[1])))
        br = jnp.concatenate([jnp.pad(b_router_group[l], (0, gpad)), b_router_expert[l]])
        br = jnp.pad(br, (0, LANES - br.shape[0])).reshape(1, LANES)
        h1, xn, route, route_t, counts = _mix_call(
            ya.reshape(n_tok, sba), uv, gates, h, gt_m, sh_f, sc_f, g_sgu[l].reshape(1, sgu), w_spatial[l],
            bias_full, w_sba_out[l].astype(BF16), w_sgu_out[l].astype(BF16), w_out[l].astype(BF16),
            g_ffn[l].reshape(1, d), wr, br, seq, n_groups, per_group)

        dest, blk_e, used, zero_blocks = _routing_plan(route_t, counts[SUBLANES:SUBLANES + n_experts, 0],
                                                       n_experts, m_pad)
        xs = _dispatch_call(zero_blocks, dest, xn, m_pad, seq)
        yb = _expert_call(blk_e, used, xs, w_expert_gate[l], w_expert_up[l], w_expert_down[l])
        h = _combine_call(dest, h1, route, gt_f, g_final.reshape(1, d), yb, seq, final=l == depth - 1)
    return h.reshape(bsz, seq, d)
```

```python
import functools

import jax
import jax.numpy as jnp
from jax import lax
from jax.experimental import pallas as pl
from jax.experimental.pallas import tpu as pltpu

F32 = jnp.float32
BF16 = jnp.bfloat16

HEAD_DIM = 64
TOP_K = 2
EPS = 1e-6
LOG2E = 1.4426950408889634
SIGN_BIT = 0x80000000
BF16_BITS = 0xFFFF0000
ATTN_EXIT_LOG2 = 160.0
DEAD_CARRY = 1e30
LANES = 128
SUBLANES = 8
ATTN_BLOCK = 256
EXPERT_ROWS = 512
TOKEN_TILE = 512
TOKEN_SUB_TILE = 256
ROW_COPY_TILE = 512
PROJ_TILE = 512
ROUTE_COLS = 8
VMEM_LIMIT = 56 * 1024 * 1024


def _cparams(*sem):
    return pltpu.CompilerParams(dimension_semantics=sem, vmem_limit_bytes=VMEM_LIMIT)


def _split_bf16(a):
    hi = a.astype(BF16)
    lo = (a - hi.astype(F32)).astype(BF16)
    return hi, lo


def _dot(a, b):
    return jnp.dot(a, b, preferred_element_type=F32)


def _dot3(a, b):
    ah, al = _split_bf16(a)
    bh, bl = _split_bf16(b)
    return _dot(ah, bh) + _dot(ah, bl) + _dot(al, bh)


def _pack_halves(x):
    half = x.shape[1] // 2
    hi = lax.bitcast_convert_type(x[:, :half].astype(BF16).astype(F32), jnp.uint32)
    lo = lax.bitcast_convert_type(x[:, half:].astype(BF16).astype(F32), jnp.uint32)
    return hi | (lo >> 16)


def _unpack_halves(u):
    hi = lax.bitcast_convert_type(u & jnp.uint32(BF16_BITS), F32)
    lo = lax.bitcast_convert_type(u << 16, F32)
    return jnp.concatenate([hi, lo], axis=1)


def _rms(x, g):
    ms = jnp.mean(x * x, axis=-1, keepdims=True)
    return x * lax.rsqrt(ms + EPS) * g


def _mod_kernel(c_ref, w_ref, b_ref, o_ref):
    c = c_ref[...]
    ca = c * (1.0 / (1.0 + jnp.exp(-c)))
    o_ref[...] = _dot3(ca, w_ref[...]) + b_ref[...]


def _mod_call(c, w_ada, b_ada):
    bsz, d = c.shape
    n = w_ada.shape[1]
    tn = n // 4 if n % (4 * LANES) == 0 else n
    return pl.pallas_call(
        _mod_kernel,
        out_shape=jax.ShapeDtypeStruct((bsz, n), F32),
        grid=(n // tn,),
        in_specs=[pl.BlockSpec((bsz, d), lambda j: (0, 0)),
                  pl.BlockSpec((d, tn), lambda j: (0, j)),
                  pl.BlockSpec((1, tn), lambda j: (0, j))],
        out_specs=pl.BlockSpec((bsz, tn), lambda j: (0, j)),
        compiler_params=_cparams("arbitrary"),
        name="mod",
    )(c, w_ada, b_ada.reshape(1, n))


def _proj_kernel(x_ref, sh_ref, sc_ref, g_ref, w_ref, n_ref, q_ref, k_ref, v_ref, *, width):
    x = x_ref[...]
    n = _rms(x, g_ref[...]) * (1.0 + sc_ref[0]) + sh_ref[0]
    nb = n.astype(BF16)
    n_ref[...] = nb
    for i, ref in enumerate((q_ref, k_ref, v_ref)):
        p = _dot(nb, w_ref[:, i * width:(i + 1) * width])
        if ref is q_ref:
            p = p * (HEAD_DIM ** -0.5 * LOG2E)
        ref[...] = p.astype(ref.dtype)


def _proj_call(x2, sh, sc, g, w_qkv, seq):
    n_tok, d = x2.shape
    tm = min(PROJ_TILE, seq)
    per_b = seq // tm
    width = w_qkv.shape[1] // 3
    tok = lambda w: pl.BlockSpec((tm, w), lambda i: (i, 0))
    vec = pl.BlockSpec((1, 1, d), lambda i: (i // per_b, 0, 0))
    return pl.pallas_call(
        functools.partial(_proj_kernel, width=width),
        out_shape=[jax.ShapeDtypeStruct((n_tok, w), BF16) for w in (d, width, width, width)],
        grid=(n_tok // tm,),
        in_specs=[tok(d), vec, vec,
                  pl.BlockSpec((1, d), lambda i: (0, 0)),
                  pl.BlockSpec(w_qkv.shape, lambda i: (0, 0))],
        out_specs=[tok(w) for w in (d, width, width, width)],
        compiler_params=_cparams("arbitrary"),
        name="proj",
    )(x2, sh, sc, g, w_qkv)


def _attn_kernel(q_ref, k_ref, v_ref, u_ref, o_ref, qm_ref, acc_ref, carry_ref, *, heads, blk):
    qi = pl.program_id(1)
    u_tri = u_ref[...]
    row = lax.broadcasted_iota(jnp.int32, (blk, blk), 0)
    col = lax.broadcasted_iota(jnp.int32, (blk, blk), 1)
    causal = col < row
    per_slab = LANES // HEAD_DIM
    lane = lax.broadcasted_iota(jnp.int32, (blk, LANES), 1)
    own = [(lane >= j * HEAD_DIM) & (lane < (j + 1) * HEAD_DIM) for j in range(per_slab)]
    for h in range(heads):
        slab = slice(h // per_slab * LANES, (h // per_slab + 1) * LANES)
        qm_ref[h] = jnp.where(own[h % per_slab], q_ref[0, :, slab], 0.0).astype(BF16)

    def key_tile(start, first, dead=None):
        mask = causal if first else None
        slabs = [slice(p * LANES, (p + 1) * LANES) for p in range(heads // per_slab)]
        s_all, cum_all, w_all, carries = {}, {}, {}, []

        def scores(h):
            s_all[h] = lax.dot_general(qm_ref[h], k_ref[0, pl.ds(start, blk), slabs[h // per_slab]],
                                       (((1,), (1,)), ((), ())), preferred_element_type=F32)

        def suffix_sums(h):
            s = s_all[h]
            neg_abs = lax.bitcast_convert_type(lax.bitcast_convert_type(s, jnp.uint32) | jnp.uint32(SIGN_BIT), F32)
            sp = jnp.maximum(s, 0.0) + jnp.log(1.0 + jnp.exp2(neg_abs)) * LOG2E
            if mask is not None:
                sp = jnp.where(mask, sp, 0.0)
            hi = lax.bitcast_convert_type(lax.bitcast_convert_type(sp, jnp.uint32) & jnp.uint32(BF16_BITS), F32)
            cum_all[h] = _dot(hi.astype(BF16), u_tri) + _dot((sp - hi).astype(BF16), u_tri)

        def weights(h):
            cum = cum_all[h]
            if first:
                w = jnp.exp2(s_all[h] - cum)
                carry = cum[:, 0:1]
            else:
                carry = carry_ref[h]
                if dead is not None:
                    carry = jnp.where(dead, DEAD_CARRY, carry)
                w = jnp.exp2((s_all[h] - carry) - cum)
                carry = carry + cum[:, 0:1]
            w_all[h] = (w if mask is None else jnp.where(mask, w, 0.0)).astype(BF16)
            carry_ref[h] = carry
            carries.append(carry)

        def values(p):
            vp = v_ref[0, pl.ds(start, blk), slabs[p]]
            upd = None
            for j in range(per_slab):
                pv = _dot(w_all[p * per_slab + j], vp)
                upd = pv if upd is None else jnp.where(own[j], pv, upd)
            acc_ref[:, slabs[p]] = upd if first else acc_ref[:, slabs[p]] + upd

        for t in range(heads + 2):
            if t < heads:
                scores(t)
            if 1 <= t <= heads:
                suffix_sums(t - 1)
            if t >= 2:
                weights(t - 2)
                if (t - 2) % per_slab == per_slab - 1:
                    values((t - 2) // per_slab)
        return jnp.min(functools.reduce(jnp.minimum, carries))

    key_tile(pl.multiple_of(qi * blk, blk), True)
    cmin = key_tile(pl.multiple_of(jnp.maximum(qi - 1, 0) * blk, blk), False, dead=qi == 0)

    def cond(state):
        i, cmin = state
        return (i < qi) & (cmin < ATTN_EXIT_LOG2)

    def body(state):
        i, _ = state
        return i + 1, key_tile(pl.multiple_of((qi - 1 - i) * blk, blk), False)

    lax.while_loop(cond, body, (jnp.int32(1), cmin))
    o_ref[0] = acc_ref[...].astype(o_ref.dtype)


def _attn_call(q, k, v, heads):
    bsz, seq, width = q.shape
    blk = min(ATTN_BLOCK, seq)
    r = lax.broadcasted_iota(jnp.int32, (blk, blk), 0)
    c = lax.broadcasted_iota(jnp.int32, (blk, blk), 1)
    u_tri = (r >= c).astype(BF16)
    full = pl.BlockSpec((1, seq, width), lambda b, i: (b, 0, 0))
    return pl.pallas_call(
        functools.partial(_attn_kernel, heads=heads, blk=blk),
        out_shape=jax.ShapeDtypeStruct((bsz, seq, width), BF16),
        grid=(bsz, seq // blk),
        in_specs=[pl.BlockSpec((1, blk, width), lambda b, i: (b, i, 0)), full, full,
                  pl.BlockSpec((blk, blk), lambda b, i: (0, 0))],
        out_specs=pl.BlockSpec((1, blk, width), lambda b, i: (b, i, 0)),
        scratch_shapes=[pltpu.VMEM((heads, blk, LANES), BF16), pltpu.VMEM((blk, width), F32),
                        pltpu.VMEM((heads, blk, 1), F32)],
        compiler_params=_cparams("arbitrary", "arbitrary"),
        name="attn",
    )(q, k, v, u_tri)


def _gelu_tanh(x):
    c = 0.7978845608028654
    hx = 0.5 * x
    return hx + hx * jnp.tanh(x * (c + (0.044715 * c) * (x * x)))


def _sigmoid(x):
    return 1.0 / (1.0 + jnp.exp(-x))


def _mix_kernel(ya_ref, n_ref, x_ref, gtm_ref, shf_ref, scf_ref, gsgu_ref, ws_ref, bs_ref,
                wuv_ref, wgt_ref, wa_ref, wb_ref, wo_ref, gffn_ref, wr_ref, br_ref, utri_ref,
                h_ref, xn_ref, route_ref, route_t_ref, cnt_ref, yb_scr, cnt_scr,
                *, chunk, groups, n_groups, per_group):
    step = pl.program_id(0)
    tm, d = x_ref.shape
    sgu = gsgu_ref.shape[1]
    gdim = sgu // groups
    sub = utri_ref.shape[0]
    pair = LANES // gdim

    @pl.when(step == 0)
    def _():
        cnt_scr[...] = jnp.zeros_like(cnt_scr)

    r = lax.broadcasted_iota(jnp.int32, (chunk, chunk), 0)
    c = lax.broadcasted_iota(jnp.int32, (chunk, chunk), 1)
    wcs = [jnp.where(r >= c, ws_ref[g], 0.0).astype(BF16) for g in range(groups)]
    lane = lax.broadcasted_iota(jnp.int32, (chunk, LANES), 1)
    state = {}

    def project(i):
        rows = pl.ds(i * sub, sub)
        nb = n_ref[rows, :]
        state[i] = dict(ya=_dot(ya_ref[rows, :], wa_ref[...]), uv=_dot(nb, wuv_ref[...]),
                        ga=_dot(nb, wgt_ref[:, :d]), gb=_dot(nb, wgt_ref[:, d:]))

    def gate_and_merge(i):
        st = state[i]
        act = _gelu_tanh(st['uv'])
        u = act[:, :sgu]
        v = act[:, sgu:]
        mu = jnp.mean(v, axis=-1, keepdims=True)
        vc = v - mu
        var = jnp.mean(vc * vc, axis=-1, keepdims=True)
        vn = vc * lax.rsqrt(var + EPS) * gsgu_ref[...]
        for ci in range(sub // chunk):
            rows = slice(ci * chunk, (ci + 1) * chunk)
            for p in range(groups // pair):
                lanes = slice(p * LANES, (p + 1) * LANES)
                slab = vn[rows, lanes]
                mix = bs_ref[:, lanes]
                for j in range(pair):
                    sel = (lane >= j * gdim) & (lane < (j + 1) * gdim)
                    mix = mix + _dot(wcs[p * pair + j], jnp.where(sel, slab, 0.0).astype(BF16))
                yb_scr[pl.ds(i * sub + ci * chunk, chunk), lanes] = (u[rows, lanes] * mix).astype(BF16)
        rows = pl.ds(i * sub, sub)
        merged2 = ((1.0 + jnp.tanh(0.5 * st['ga'])) * st['ya']
                   + (1.0 + jnp.tanh(0.5 * st['gb'])) * _dot(yb_scr[rows, :], wb_ref[...]))
        h = x_ref[rows, :] + (0.5 * gtm_ref[0]) * _dot(merged2.astype(BF16), wo_ref[...])
        h_ref[rows, :] = h
        st['h'] = h

    def route(i):
        rows = pl.ds(i * sub, sub)
        xn = _rms(state[i]['h'], gffn_ref[...]) * (1.0 + scf_ref[0]) + shf_ref[0]
        xn_ref[rows, :] = _pack_halves(xn)
        logits = _dot3(xn, wr_ref[...]) + br_ref[...]

        lt = logits.T
        sl = lax.broadcasted_iota(jnp.int32, (SUBLANES, sub), 0)
        neg = -jnp.inf
        gl = jnp.where(sl < n_groups, lt[0:SUBLANES], neg)
        gmax = jnp.max(gl, axis=0, keepdims=True)
        g_sel = jnp.min(jnp.where(gl == gmax, sl, SUBLANES), axis=0, keepdims=True)
        g_w = 1.0 / jnp.sum(jnp.exp(gl - gmax), axis=0, keepdims=True)
        el = lt[SUBLANES:2 * SUBLANES]
        for g in range(1, n_groups):
            el = jnp.where(g_sel == g, lt[(g + 1) * SUBLANES:(g + 2) * SUBLANES], el)
        m1 = jnp.max(el, axis=0, keepdims=True)
        j1 = jnp.min(jnp.where(el == m1, sl, SUBLANES), axis=0, keepdims=True)
        el2 = jnp.where(sl == j1, neg, el)
        m2 = jnp.max(el2, axis=0, keepdims=True)
        j2 = jnp.min(jnp.where(el2 == m2, sl, SUBLANES), axis=0, keepdims=True)
        t = jnp.exp(m2 - m1)
        w1 = g_w / (1.0 + t)
        w2 = g_w * t / (1.0 + t)
        e1 = g_sel * per_group + j1
        e2 = g_sel * per_group + j2

        row = lax.broadcasted_iota(jnp.int32, (LANES, sub), 0)
        oh1 = row == e1 + SUBLANES
        oh2 = row == e2 + SUBLANES
        both = jnp.where(oh1 | oh2, 1.0, 0.0)
        before = _dot(both.astype(BF16), utri_ref[...]) + cnt_scr[...]
        rank1 = jnp.sum(jnp.where(oh1, before, 0.0), axis=0, keepdims=True)
        rank2 = jnp.sum(jnp.where(oh2, before, 0.0), axis=0, keepdims=True)
        cnt_scr[...] = cnt_scr[...] + jnp.sum(both, axis=1, keepdims=True)

        rec = jnp.zeros((LANES, sub), F32)
        for idx, val in enumerate((e1.astype(F32), e2.astype(F32), w1, w2, rank1, rank2)):
            rec = jnp.where(row == idx, val, rec)
        route_t_ref[:, rows] = rec[:ROUTE_COLS]
        route_ref[rows, :] = rec.T[:, :ROUTE_COLS]

    nsub = tm // sub
    project(0)
    for i in range(nsub):
        if i + 1 < nsub:
            project(i + 1)
        gate_and_merge(i)
    for i in range(nsub):
        route(i)
    cnt_ref[...] = cnt_scr[...]


def _mix_call(ya, nb, x2, gtm, shf, scf, g_sgu, w_spatial, bias_full, wuv, wgt, wa, wb, wo, g_ffn, wr, br,
              seq, n_groups, per_group):
    n_tok, d = x2.shape
    tm = min(TOKEN_TILE, seq)
    per_b = seq // tm
    groups, chunk, _ = w_spatial.shape
    sgu = g_sgu.shape[1]
    sub = min(TOKEN_SUB_TILE, tm)
    r = lax.broadcasted_iota(jnp.int32, (sub, sub), 0)
    c = lax.broadcasted_iota(jnp.int32, (sub, sub), 1)
    utri = (r < c).astype(BF16)
    tok = lambda w: pl.BlockSpec((tm, w), lambda i: (i, 0))
    vec = pl.BlockSpec((1, 1, d), lambda i: (i // per_b, 0, 0))
    const = lambda a: pl.BlockSpec(a.shape, lambda i: (0,) * a.ndim)
    ins = [ya, nb, x2, gtm, shf, scf, g_sgu, w_spatial, bias_full, wuv, wgt, wa, wb, wo, g_ffn, wr, br, utri]
    in_specs = [tok(ya.shape[1]), tok(d), tok(d), vec, vec, vec]
    in_specs += [const(a) for a in ins[6:]]
    return pl.pallas_call(
        functools.partial(_mix_kernel, chunk=chunk, groups=groups, n_groups=n_groups, per_group=per_group),
        out_shape=[jax.ShapeDtypeStruct((n_tok, d), F32), jax.ShapeDtypeStruct((n_tok, d // 2), jnp.uint32),
                   jax.ShapeDtypeStruct((n_tok, ROUTE_COLS), F32), jax.ShapeDtypeStruct((ROUTE_COLS, n_tok), F32),
                   jax.ShapeDtypeStruct((LANES, 1), F32)],
        grid=(n_tok // tm,),
        in_specs=in_specs,
        out_specs=[tok(d), tok(d // 2), tok(ROUTE_COLS), pl.BlockSpec((ROUTE_COLS, tm), lambda i: (0, i)),
                   pl.BlockSpec((LANES, 1), lambda i: (0, 0))],
        scratch_shapes=[pltpu.VMEM((tm, sgu), BF16), pltpu.VMEM((LANES, 1), F32)],
        compiler_params=_cparams("arbitrary"),
        name="mix",
    )(*ins)


def _row_copy(src, s, dst, d, sem):
    return pltpu.make_async_copy(src.at[pl.ds(s, 1)], dst.at[pl.ds(d, 1)], sem)


def _dispatch_kernel(zblk_ref, *refs, tm, rows):
    dest_refs, (xn_ref, xs_ref, zero_scr, src_scr, sem, zsem) = refs[:TOP_K], refs[TOP_K:]
    step = pl.program_id(0)

    def wait_rows():
        for _ in range(TOP_K):
            pltpu.make_async_copy(src_scr, xs_ref.at[pl.ds(0, tm)], sem).wait()

    @pl.when(step == 0)
    def _():
        zero_scr[...] = jnp.zeros_like(zero_scr)

        def zero_copy(j):
            start = pl.multiple_of(zblk_ref[j] * rows, rows)
            return pltpu.make_async_copy(zero_scr, xs_ref.at[pl.ds(start, rows)], zsem)

        for j in range(zblk_ref.shape[0]):
            pl.when(zblk_ref[j] >= 0)(lambda j=j: zero_copy(j).start())
        for j in range(zblk_ref.shape[0]):
            pl.when(zblk_ref[j] >= 0)(lambda j=j: zero_copy(j).wait())

    pl.when(step > 0)(wait_rows)
    src_scr[...] = xn_ref[...]
    for j in range(tm):
        for k, dref in enumerate(dest_refs):
            _row_copy(src_scr, j, xs_ref, dref[0, 0, j], sem).start(priority=k % 2)
    pl.when(step == pl.num_programs(0) - 1)(wait_rows)


def _dest_blocks(dest, tm):
    return [dest[k].reshape(-1, 1, tm) for k in range(TOP_K)]


def _dispatch_call(zero_blocks, dest, xn, m_pad, seq):
    n_tok, dw = xn.shape
    tm = min(ROW_COPY_TILE, seq)
    smem = pl.BlockSpec((1, 1, tm), lambda i, zb: (i, 0, 0), memory_space=pltpu.SMEM)
    return pl.pallas_call(
        functools.partial(_dispatch_kernel, tm=tm, rows=EXPERT_ROWS),
        out_shape=jax.ShapeDtypeStruct((m_pad, dw), xn.dtype),
        grid_spec=pltpu.PrefetchScalarGridSpec(
            num_scalar_prefetch=1,
            grid=(n_tok // tm,),
            in_specs=[smem] * TOP_K + [pl.BlockSpec((tm, dw), lambda i, zb: (i, 0))],
            out_specs=pl.BlockSpec(memory_space=pl.ANY),
            scratch_shapes=[pltpu.VMEM((EXPERT_ROWS, dw), xn.dtype), pltpu.VMEM((tm, dw), xn.dtype),
                            pltpu.SemaphoreType.DMA(()), pltpu.SemaphoreType.DMA(())]),
        compiler_params=_cparams("arbitrary"),
        name="dispatch",
    )(zero_blocks, *_dest_blocks(dest, tm), xn)


def _expert_kernel(blk_e_ref, used_ref, xs_ref, wg_ref, wu_ref, wd_ref, y_ref, wg_bf, wu_bf, wd_bf):
    i = pl.program_id(0)
    live = i < used_ref[0]
    new_expert = (i == 0) | (blk_e_ref[i] != blk_e_ref[jnp.maximum(i - 1, 0)])

    @pl.when(live & new_expert)
    def _():
        wg_bf[...] = wg_ref[0].astype(BF16)
        wu_bf[...] = wu_ref[0].astype(BF16)
        wd_bf[...] = wd_ref[0].astype(BF16)

    @pl.when(live)
    def _():
        xb = _unpack_halves(xs_ref[...]).astype(BF16)
        g = _dot(xb, wg_bf[...])
        hid = g * _sigmoid(g) * _dot(xb, wu_bf[...])
        y_ref[...] = _pack_halves(_dot(hid.astype(BF16), wd_bf[...]))

    @pl.when(jnp.logical_not(live))
    def _():
        y_ref[...] = jnp.zeros_like(y_ref)


def _expert_call(blk_e, used, xs, wg, wu, wd):
    m_pad, dw = xs.shape
    _, d, f = wg.shape
    rows = EXPERT_ROWS
    return pl.pallas_call(
        _expert_kernel,
        out_shape=jax.ShapeDtypeStruct((m_pad, dw), xs.dtype),
        grid_spec=pltpu.PrefetchScalarGridSpec(
            num_scalar_prefetch=2,
            grid=(m_pad // rows,),
            in_specs=[pl.BlockSpec((rows, dw), lambda i, be, us: (i, 0)),
                      pl.BlockSpec((1, d, f), lambda i, be, us: (be[i], 0, 0)),
                      pl.BlockSpec((1, d, f), lambda i, be, us: (be[i], 0, 0)),
                      pl.BlockSpec((1, f, d), lambda i, be, us: (be[i], 0, 0))],
            out_specs=pl.BlockSpec((rows, dw), lambda i, be, us: (i, 0)),
            scratch_shapes=[pltpu.VMEM((d, f), BF16), pltpu.VMEM((d, f), BF16), pltpu.VMEM((f, d), BF16)]),
        compiler_params=_cparams("arbitrary"),
        name="experts",
    )(blk_e, used, xs, wg, wu, wd)


def _combine_kernel(*refs, tm, final):
    cur_refs, nxt_refs = refs[:TOP_K], refs[TOP_K:2 * TOP_K]
    h_ref, route_ref, gtf_ref, gfin_ref, y_ref, o_ref, land, rows, sem = refs[2 * TOP_K:]
    step = pl.program_id(0)

    def gather(dest_refs):
        for j in range(tm):
            for k, dref in enumerate(dest_refs):
                pltpu.make_async_copy(y_ref.at[pl.ds(dref[0, 0, j], 1)], land.at[k, pl.ds(j, 1)],
                                      sem).start(priority=k % 2)

    def wait_rows():
        for k in range(TOP_K):
            pltpu.make_async_copy(y_ref.at[pl.ds(0, tm)], land.at[k], sem).wait()

    pl.when(step == 0)(lambda: gather(cur_refs))
    wait_rows()
    rows[...] = land[...]
    gather(nxt_refs)
    route = route_ref[...]
    y = route[:, 2:3] * _unpack_halves(rows[0]) + route[:, 3:4] * _unpack_halves(rows[1])
    h = h_ref[...] + gtf_ref[0] * y
    o_ref[...] = _rms(h, gfin_ref[...]) if final else h
    pl.when(step == pl.num_programs(0) - 1)(wait_rows)


def _combine_call(dest, h, route, gtf, g_final, yb, seq, final):
    n_tok, d = h.shape
    tm = min(ROW_COPY_TILE, seq)
    per_b = seq // tm
    steps = n_tok // tm
    tok = lambda w: pl.BlockSpec((tm, w), lambda i: (i, 0))
    cur = pl.BlockSpec((1, 1, tm), lambda i: (i, 0, 0), memory_space=pltpu.SMEM)
    nxt = pl.BlockSpec((1, 1, tm), lambda i: (jnp.minimum(i + 1, steps - 1), 0, 0), memory_space=pltpu.SMEM)
    blocks = _dest_blocks(dest, tm)
    return pl.pallas_call(
        functools.partial(_combine_kernel, tm=tm, final=final),
        out_shape=jax.ShapeDtypeStruct((n_tok, d), F32),
        grid=(steps,),
        in_specs=[cur] * TOP_K + [nxt] * TOP_K + [
            tok(d), tok(ROUTE_COLS),
            pl.BlockSpec((1, 1, d), lambda i: (i // per_b, 0, 0)),
            pl.BlockSpec((1, d), lambda i: (0, 0)),
            pl.BlockSpec(memory_space=pl.ANY)],
        out_specs=tok(d),
        scratch_shapes=[pltpu.VMEM((TOP_K, tm, yb.shape[1]), yb.dtype),
                        pltpu.VMEM((TOP_K, tm, yb.shape[1]), yb.dtype), pltpu.SemaphoreType.DMA(())],
        compiler_params=_cparams("arbitrary"),
        name="combine",
    )(*blocks, *blocks, h, route, gtf, g_final, yb)


def _routing_plan(route_t, counts, n_experts, m_pad):
    eid = route_t[0:TOP_K].astype(jnp.int32)
    rank = route_t[4:4 + TOP_K].astype(jnp.int32)
    cnt = counts.astype(jnp.int32)
    padded = (cnt + EXPERT_ROWS - 1) // EXPERT_ROWS * EXPERT_ROWS
    pend = jnp.cumsum(padded)
    pstart = pend - padded
    dest = rank
    for e in range(n_experts):
        dest = dest + jnp.where(eid == e, pstart[e], 0)
    nblk = m_pad // EXPERT_ROWS
    blk_row = jnp.arange(nblk, dtype=jnp.int32) * EXPERT_ROWS
    blk_e = jnp.minimum(jnp.sum(pend[None, :] <= blk_row[:, None], axis=1), n_experts - 1).astype(jnp.int32)
    used = (pend[-1:] // EXPERT_ROWS).astype(jnp.int32)
    last_blk = jnp.where(padded > 0, pend // EXPERT_ROWS - 1, -1)
    tail_blk = used + jnp.arange(n_experts, dtype=jnp.int32)
    tail_blk = jnp.where(tail_blk < nblk, tail_blk, -1)
    zero_blocks = jnp.concatenate([last_blk, tail_blk]).astype(jnp.int32)
    return dest, blk_e, used, zero_blocks


def kernel(x, c, g_mix, g_ffn, w_ada, b_ada, w_in, w_sba_out, g_sgu, w_spatial, b_spatial, w_sgu_out, w_out,
           w_router_group, b_router_group, w_router_expert, b_router_expert, w_expert_gate, w_expert_up,
           w_expert_down, g_final):
    bsz, seq, d = x.shape
    depth = w_in.shape[0]
    sba = w_sba_out.shape[1]
    sgu = g_sgu.shape[1]
    heads = sba // HEAD_DIM
    groups, chunk = w_spatial.shape[1], w_spatial.shape[2]
    n_groups = w_router_group.shape[2]
    n_experts = w_router_expert.shape[2]
    per_group = n_experts // n_groups
    n_tok = bsz * seq
    m_pad = n_tok * TOP_K + n_experts * EXPERT_ROWS
    assert w_in.shape[2] == 3 * sba + 2 * sgu + 2 * d
    assert seq % chunk == 0 and LANES % (sgu // groups) == 0
    assert d % (2 * LANES) == 0
    assert per_group == SUBLANES and n_groups <= SUBLANES and (n_groups + 1) * SUBLANES <= LANES

    h = x.reshape(n_tok, d)
    for l in range(depth):
        mod = _mod_call(c, w_ada[l], b_ada[l])
        sh_m, sc_m, gt_m, sh_f, sc_f, gt_f = [mod[:, i * d:(i + 1) * d].reshape(bsz, 1, d) for i in range(6)]

        w_in_bf = w_in[l].astype(BF16)
        qkv, uv_end = 3 * sba, 3 * sba + 2 * sgu
        nb, q, k, v = _proj_call(h, sh_m, sc_m, g_mix[l].reshape(1, d), w_in_bf[:, :qkv], seq)
        ya = _attn_call(q.reshape(bsz, seq, sba), k.reshape(bsz, seq, sba), v.reshape(bsz, seq, sba), heads)

        bias_full = jnp.repeat(b_spatial[l].T, sgu // groups, axis=1)
        gpad = SUBLANES - n_groups
        wr = jnp.concatenate([jnp.pad(w_router_group[l], ((0, 0), (0, gpad))), w_router_expert[l]], axis=1)
        wr = jnp.pad(wr, ((0, 0), (0, LANES - wr.shape[1])))
        br = jnp.concatenate([jnp.pad(b_router_group[l], (0, gpad)), b_router_expert[l]])
        br = jnp.pad(br, (0, LANES - br.shape[0])).reshape(1, LANES)
        h1, xn, route, route_t, counts = _mix_call(
            ya.reshape(n_tok, sba), nb, h, gt_m, sh_f, sc_f, g_sgu[l].reshape(1, sgu), w_spatial[l],
            bias_full, w_in_bf[:, qkv:uv_end], w_in_bf[:, uv_end:], w_sba_out[l].astype(BF16),
            w_sgu_out[l].astype(BF16), w_out[l].astype(BF16), g_ffn[l].reshape(1, d), wr, br,
            seq, n_groups, per_group)

        dest, blk_e, used, zero_blocks = _routing_plan(route_t, counts[SUBLANES:SUBLANES + n_experts, 0],
                                                       n_experts, m_pad)
        xs = _dispatch_call(zero_blocks, dest, xn, m_pad, seq)
        yb = _expert_call(blk_e, used, xs, w_expert_gate[l], w_expert_up[l], w_expert_down[l])
        h = _combine_call(dest, h1, route, gt_f, g_final.reshape(1, d), yb, seq, final=l == depth - 1)
    return h.reshape(bsz, seq, d)
```

```python
import functools

import jax
import jax.numpy as jnp
from jax import lax
from jax.experimental import pallas as pl
from jax.experimental.pallas import tpu as pltpu

F32 = jnp.float32
BF16 = jnp.bfloat16

HEAD_DIM = 64
TOP_K = 2
EPS = 1e-6
LOG2E = 1.4426950408889634
SIGN_BIT = 0x80000000
BF16_BITS = 0xFFFF0000
ATTN_EXIT_LOG2 = 160.0
DEAD_CARRY = 1e30
LANES = 128
SUBLANES = 8
ATTN_BLOCK = 256
EXPERT_ROWS = 512
TOKEN_TILE = 1024
ROW_COPY_TILE = 512
PROJ_TILE = 512
ROUTE_COLS = 8
VMEM_LIMIT = 56 * 1024 * 1024


def _cparams(*sem):
    return pltpu.CompilerParams(dimension_semantics=sem, vmem_limit_bytes=VMEM_LIMIT)


def _split_bf16(a):
    hi = a.astype(BF16)
    lo = (a - hi.astype(F32)).astype(BF16)
    return hi, lo


def _split_bf16_trunc(a):
    hi = lax.bitcast_convert_type(lax.bitcast_convert_type(a, jnp.uint32) & jnp.uint32(BF16_BITS), F32)
    return hi.astype(BF16), (a - hi).astype(BF16)


def _dot(a, b):
    return jnp.dot(a, b, preferred_element_type=F32)


def _dot3(a, b):
    ah, al = _split_bf16(a)
    bh, bl = _split_bf16(b)
    return _dot(ah, bh) + _dot(ah, bl) + _dot(al, bh)


def _pack_halves(x):
    half = x.shape[1] // 2
    hi = lax.bitcast_convert_type(x[:, :half].astype(BF16).astype(F32), jnp.uint32)
    lo = lax.bitcast_convert_type(x[:, half:].astype(BF16).astype(F32), jnp.uint32)
    return hi | (lo >> 16)


def _unpack_halves(u):
    hi = lax.bitcast_convert_type(u & jnp.uint32(BF16_BITS), F32)
    lo = lax.bitcast_convert_type(u << 16, F32)
    return jnp.concatenate([hi, lo], axis=1)


def _rms(x, g):
    ms = jnp.mean(x * x, axis=-1, keepdims=True)
    return x * lax.rsqrt(ms + EPS) * g


def _mod_kernel(c_ref, w_ref, b_ref, o_ref):
    c = c_ref[...]
    ca = c * (1.0 / (1.0 + jnp.exp(-c)))
    o_ref[...] = _dot3(ca, w_ref[...]) + b_ref[...]


def _mod_call(c, w_ada, b_ada):
    bsz, d = c.shape
    n = w_ada.shape[1]
    tn = n // 4 if n % (4 * LANES) == 0 else n
    return pl.pallas_call(
        _mod_kernel,
        out_shape=jax.ShapeDtypeStruct((bsz, n), F32),
        grid=(n // tn,),
        in_specs=[pl.BlockSpec((bsz, d), lambda j: (0, 0)),
                  pl.BlockSpec((d, tn), lambda j: (0, j)),
                  pl.BlockSpec((1, tn), lambda j: (0, j))],
        out_specs=pl.BlockSpec((bsz, tn), lambda j: (0, j)),
        compiler_params=_cparams("arbitrary"),
        name="mod",
    )(c, w_ada, b_ada.reshape(1, n))


def _proj_kernel(x_ref, sh_ref, sc_ref, g_ref, w_ref, q_ref, k_ref, v_ref, uv_ref, gt_ref, *, widths):
    x = x_ref[...]
    n = _rms(x, g_ref[...]) * (1.0 + sc_ref[0]) + sh_ref[0]
    nb = n.astype(BF16)
    off = 0
    for ref, wd in zip((q_ref, k_ref, v_ref, uv_ref, gt_ref), widths):
        p = _dot(nb, w_ref[:, off:off + wd])
        if ref is q_ref:
            p = p * (HEAD_DIM ** -0.5 * LOG2E)
        ref[...] = p.astype(ref.dtype)
        off += wd


def _proj_call(x2, sh, sc, g, w_in_bf, seq, widths):
    n_tok, d = x2.shape
    tm = min(PROJ_TILE, seq)
    per_b = seq // tm
    cols = w_in_bf.shape[1]
    tok = lambda w: pl.BlockSpec((tm, w), lambda i: (i, 0))
    vec = pl.BlockSpec((1, 1, d), lambda i: (i // per_b, 0, 0))
    return pl.pallas_call(
        functools.partial(_proj_kernel, widths=widths),
        out_shape=[jax.ShapeDtypeStruct((n_tok, w), BF16) for w in widths],
        grid=(n_tok // tm,),
        in_specs=[tok(d), vec, vec,
                  pl.BlockSpec((1, d), lambda i: (0, 0)),
                  pl.BlockSpec((d, cols), lambda i: (0, 0))],
        out_specs=[tok(w) for w in widths],
        compiler_params=_cparams("arbitrary"),
        name="proj",
    )(x2, sh, sc, g, w_in_bf)


def _attn_kernel(q_ref, k_ref, v_ref, u_ref, o_ref, qm_ref, acc_ref, carry_ref, *, heads, blk):
    qi = pl.program_id(1)
    u_tri = u_ref[...]
    row = lax.broadcasted_iota(jnp.int32, (blk, blk), 0)
    col = lax.broadcasted_iota(jnp.int32, (blk, blk), 1)
    causal = col < row
    per_slab = LANES // HEAD_DIM
    lane = lax.broadcasted_iota(jnp.int32, (blk, LANES), 1)
    own = [(lane >= j * HEAD_DIM) & (lane < (j + 1) * HEAD_DIM) for j in range(per_slab)]
    for h in range(heads):
        slab = slice(h // per_slab * LANES, (h // per_slab + 1) * LANES)
        qm_ref[h] = jnp.where(own[h % per_slab], q_ref[0, :, slab], 0.0).astype(BF16)

    def key_tile(start, first, dead=None):
        mask = causal if first else None
        slabs = [slice(p * LANES, (p + 1) * LANES) for p in range(heads // per_slab)]
        s_all, cum_all, w_all, carries = {}, {}, {}, []

        def scores(h):
            s_all[h] = lax.dot_general(qm_ref[h], k_ref[0, pl.ds(start, blk), slabs[h // per_slab]],
                                       (((1,), (1,)), ((), ())), preferred_element_type=F32)

        def suffix_sums(h):
            s = s_all[h]
            neg_abs = lax.bitcast_convert_type(lax.bitcast_convert_type(s, jnp.uint32) | jnp.uint32(SIGN_BIT), F32)
            sp = jnp.maximum(s, 0.0) + jnp.log(1.0 + jnp.exp2(neg_abs)) * LOG2E
            if mask is not None:
                sp = jnp.where(mask, sp, 0.0)
            hi, lo = _split_bf16_trunc(sp)
            cum_all[h] = _dot(hi, u_tri) + _dot(lo, u_tri)

        def weights(h):
            cum = cum_all[h]
            if first:
                w = jnp.exp2(s_all[h] - cum)
                carry = cum[:, 0:1]
            else:
                carry = carry_ref[h]
                if dead is not None:
                    carry = jnp.where(dead, DEAD_CARRY, carry)
                w = jnp.exp2((s_all[h] - carry) - cum)
                carry = carry + cum[:, 0:1]
            w_all[h] = (w if mask is None else jnp.where(mask, w, 0.0)).astype(BF16)
            carry_ref[h] = carry
            carries.append(carry)

        def values(p):
            vp = v_ref[0, pl.ds(start, blk), slabs[p]]
            upd = None
            for j in range(per_slab):
                pv = _dot(w_all[p * per_slab + j], vp)
                upd = pv if upd is None else jnp.where(own[j], pv, upd)
            acc_ref[:, slabs[p]] = upd if first else acc_ref[:, slabs[p]] + upd

        for t in range(heads + 2):
            if t < heads:
                scores(t)
            if 1 <= t <= heads:
                suffix_sums(t - 1)
            if t >= 2:
                weights(t - 2)
                if (t - 2) % per_slab == per_slab - 1:
                    values((t - 2) // per_slab)
        return jnp.min(functools.reduce(jnp.minimum, carries))

    key_tile(pl.multiple_of(qi * blk, blk), True)
    cmin = key_tile(pl.multiple_of(jnp.maximum(qi - 1, 0) * blk, blk), False, dead=qi == 0)

    def cond(state):
        i, cmin = state
        return (i < qi) & (cmin < ATTN_EXIT_LOG2)

    def body(state):
        i, _ = state
        return i + 1, key_tile(pl.multiple_of((qi - 1 - i) * blk, blk), False)

    lax.while_loop(cond, body, (jnp.int32(1), cmin))
    o_ref[0] = acc_ref[...].astype(o_ref.dtype)


def _attn_call(q, k, v, heads):
    bsz, seq, width = q.shape
    blk = min(ATTN_BLOCK, seq)
    r = lax.broadcasted_iota(jnp.int32, (blk, blk), 0)
    c = lax.broadcasted_iota(jnp.int32, (blk, blk), 1)
    u_tri = (r >= c).astype(BF16)
    full = pl.BlockSpec((1, seq, width), lambda b, i: (b, 0, 0))
    return pl.pallas_call(
        functools.partial(_attn_kernel, heads=heads, blk=blk),
        out_shape=jax.ShapeDtypeStruct((bsz, seq, width), BF16),
        grid=(bsz, seq // blk),
        in_specs=[pl.BlockSpec((1, blk, width), lambda b, i: (b, i, 0)), full, full,
                  pl.BlockSpec((blk, blk), lambda b, i: (0, 0))],
        out_specs=pl.BlockSpec((1, blk, width), lambda b, i: (b, i, 0)),
        scratch_shapes=[pltpu.VMEM((heads, blk, LANES), BF16), pltpu.VMEM((blk, width), F32),
                        pltpu.VMEM((heads, blk, 1), F32)],
        compiler_params=_cparams("arbitrary", "arbitrary"),
        name="attn",
    )(q, k, v, u_tri)


def _gelu_tanh(x):
    c = 0.7978845608028654
    hx = 0.5 * x
    return hx + hx * jnp.tanh(x * (c + (0.044715 * c) * (x * x)))


def _sigmoid(x):
    return 1.0 / (1.0 + jnp.exp(-x))


def _mix_kernel(ya_ref, uv_ref, gt_ref, x_ref, gtm_ref, shf_ref, scf_ref, gsgu_ref, ws_ref, bs_ref,
                wa_ref, wb_ref, wo_ref, gffn_ref, wr_ref, br_ref, utri_ref,
                h_ref, xn_ref, route_ref, route_t_ref, cnt_ref, yb_scr, cnt_scr,
                *, chunk, groups, n_groups, per_group):
    step = pl.program_id(0)
    tm = x_ref.shape[0]
    sgu = gsgu_ref.shape[1]
    gdim = sgu // groups

    @pl.when(step == 0)
    def _():
        cnt_scr[...] = jnp.zeros_like(cnt_scr)

    ya_proj = _dot(ya_ref[...], wa_ref[...])

    act = _gelu_tanh(uv_ref[...].astype(F32))
    u = act[:, :sgu]
    v = act[:, sgu:]
    mu = jnp.mean(v, axis=-1, keepdims=True)
    vc = v - mu
    var = jnp.mean(vc * vc, axis=-1, keepdims=True)
    vn = vc * lax.rsqrt(var + EPS) * gsgu_ref[...]
    r = lax.broadcasted_iota(jnp.int32, (chunk, chunk), 0)
    c = lax.broadcasted_iota(jnp.int32, (chunk, chunk), 1)
    wcs = [jnp.where(r >= c, ws_ref[g], 0.0).astype(BF16) for g in range(groups)]
    pair = LANES // gdim
    lane = lax.broadcasted_iota(jnp.int32, (chunk, LANES), 1)
    for ci in range(tm // chunk):
        rows = slice(ci * chunk, (ci + 1) * chunk)
        for p in range(groups // pair):
            lanes = slice(p * LANES, (p + 1) * LANES)
            slab = vn[rows, lanes]
            mix = bs_ref[:, lanes]
            for j in range(pair):
                sel = (lane >= j * gdim) & (lane < (j + 1) * gdim)
                mix = mix + _dot(wcs[p * pair + j], jnp.where(sel, slab, 0.0).astype(BF16))
            yb_scr[rows, lanes] = (u[rows, lanes] * mix).astype(BF16)

    d = x_ref.shape[1]
    gates2 = 1.0 + jnp.tanh(gt_ref[...].astype(F32))
    merged2 = gates2[:, :d] * ya_proj + gates2[:, d:] * _dot(yb_scr[...], wb_ref[...])
    h = x_ref[...] + (0.5 * gtm_ref[0]) * _dot(merged2.astype(BF16), wo_ref[...])
    h_ref[...] = h

    xn = _rms(h, gffn_ref[...]) * (1.0 + scf_ref[0]) + shf_ref[0]
    xn_ref[...] = _pack_halves(xn)
    logits = _dot3(xn, wr_ref[...]) + br_ref[...]

    lt = logits.T
    sub = lax.broadcasted_iota(jnp.int32, (SUBLANES, tm), 0)
    neg = -jnp.inf
    gl = jnp.where(sub < n_groups, lt[0:SUBLANES], neg)
    gmax = jnp.max(gl, axis=0, keepdims=True)
    g_sel = jnp.min(jnp.where(gl == gmax, sub, SUBLANES), axis=0, keepdims=True)
    g_w = 1.0 / jnp.sum(jnp.exp(gl - gmax), axis=0, keepdims=True)
    el = lt[SUBLANES:2 * SUBLANES]
    for g in range(1, n_groups):
        el = jnp.where(g_sel == g, lt[(g + 1) * SUBLANES:(g + 2) * SUBLANES], el)
    m1 = jnp.max(el, axis=0, keepdims=True)
    j1 = jnp.min(jnp.where(el == m1, sub, SUBLANES), axis=0, keepdims=True)
    el2 = jnp.where(sub == j1, neg, el)
    m2 = jnp.max(el2, axis=0, keepdims=True)
    j2 = jnp.min(jnp.where(el2 == m2, sub, SUBLANES), axis=0, keepdims=True)
    t = jnp.exp(m2 - m1)
    w1 = g_w / (1.0 + t)
    w2 = g_w * t / (1.0 + t)
    e1 = g_sel * per_group + j1
    e2 = g_sel * per_group + j2

    row = lax.broadcasted_iota(jnp.int32, (LANES, tm), 0)
    oh1 = row == e1 + SUBLANES
    oh2 = row == e2 + SUBLANES
    both = jnp.where(oh1 | oh2, 1.0, 0.0)
    before = _dot(both.astype(BF16), utri_ref[...]) + cnt_scr[...]
    rank1 = jnp.sum(jnp.where(oh1, before, 0.0), axis=0, keepdims=True)
    rank2 = jnp.sum(jnp.where(oh2, before, 0.0), axis=0, keepdims=True)
    cnt_scr[...] = cnt_scr[...] + jnp.sum(both, axis=1, keepdims=True)
    cnt_ref[...] = cnt_scr[...]

    rec = jnp.zeros((LANES, tm), F32)
    for idx, val in enumerate((e1.astype(F32), e2.astype(F32), w1, w2, rank1, rank2)):
        rec = jnp.where(row == idx, val, rec)
    route_t_ref[...] = rec[:ROUTE_COLS]
    route_ref[...] = rec.T[:, :ROUTE_COLS]


def _mix_call(ya, uv, gates, x2, gtm, shf, scf, g_sgu, w_spatial, bias_full, wa, wb, wo, g_ffn, wr, br,
              seq, n_groups, per_group):
    n_tok, d = x2.shape
    tm = min(TOKEN_TILE, seq)
    per_b = seq // tm
    groups, chunk, _ = w_spatial.shape
    sgu = g_sgu.shape[1]
    r = lax.broadcasted_iota(jnp.int32, (tm, tm), 0)
    c = lax.broadcasted_iota(jnp.int32, (tm, tm), 1)
    utri = (r < c).astype(BF16)
    tok = lambda w: pl.BlockSpec((tm, w), lambda i: (i, 0))
    vec = pl.BlockSpec((1, 1, d), lambda i: (i // per_b, 0, 0))
    const = lambda a: pl.BlockSpec(a.shape, lambda i: (0,) * a.ndim)
    ins = [ya, uv, gates, x2, gtm, shf, scf, g_sgu, w_spatial, bias_full, wa, wb, wo, g_ffn, wr, br, utri]
    in_specs = [tok(ya.shape[1]), tok(uv.shape[1]), tok(gates.shape[1]), tok(d), vec, vec, vec]
    in_specs += [const(a) for a in ins[7:]]
    return pl.pallas_call(
        functools.partial(_mix_kernel, chunk=chunk, groups=groups, n_groups=n_groups, per_group=per_group),
        out_shape=[jax.ShapeDtypeStruct((n_tok, d), F32), jax.ShapeDtypeStruct((n_tok, d // 2), jnp.uint32),
                   jax.ShapeDtypeStruct((n_tok, ROUTE_COLS), F32), jax.ShapeDtypeStruct((ROUTE_COLS, n_tok), F32),
                   jax.ShapeDtypeStruct((LANES, 1), F32)],
        grid=(n_tok // tm,),
        in_specs=in_specs,
        out_specs=[tok(d), tok(d // 2), tok(ROUTE_COLS), pl.BlockSpec((ROUTE_COLS, tm), lambda i: (0, i)),
                   pl.BlockSpec((LANES, 1), lambda i: (0, 0))],
        scratch_shapes=[pltpu.VMEM((tm, sgu), BF16), pltpu.VMEM((LANES, 1), F32)],
        compiler_params=_cparams("arbitrary"),
        name="mix",
    )(*ins)


def _dispatch_kernel(zblk_ref, base_ref, len_ref, lpos_ref, xn_ref, xs_ref, dump_ref, zero_scr, srt_scr, sem, zsem,
                     *, tm, rows, n_experts):
    step = pl.program_id(0)
    srt_rows = srt_scr.shape[0]

    def wait_rows():
        pltpu.make_async_copy(srt_scr, dump_ref, sem).wait()

    def copy_pieces(n, src, dst_ref, dst, largest):
        size = largest
        while size >= SUBLANES:
            take = (n & size) != 0

            @pl.when(take)
            def _(src=src, dst=dst, size=size):
                pltpu.make_async_copy(srt_scr.at[pl.ds(pl.multiple_of(src, SUBLANES), size)],
                                      dst_ref.at[pl.ds(pl.multiple_of(dst, SUBLANES), size)], sem).start()

            step_by = jnp.where(take, size, 0)
            src, dst = src + step_by, dst + step_by
            size //= 2

    @pl.when(step == 0)
    def _():
        zero_scr[...] = jnp.zeros_like(zero_scr)
        srt_scr[...] = jnp.zeros_like(srt_scr)
        init_dump = pltpu.make_async_copy(srt_scr, dump_ref, zsem)
        init_dump.start()
        init_dump.wait()

        def zero_copy(j):
            start = pl.multiple_of(zblk_ref[j] * rows, rows)
            return pltpu.make_async_copy(zero_scr, xs_ref.at[pl.ds(start, rows)], zsem)

        for j in range(zblk_ref.shape[0]):
            pl.when(zblk_ref[j] >= 0)(lambda j=j: zero_copy(j).start())
        for j in range(zblk_ref.shape[0]):
            pl.when(zblk_ref[j] >= 0)(lambda j=j: zero_copy(j).wait())

    pl.when(step > 0)(wait_rows)
    xb = _unpack_halves(xn_ref[...]).astype(BF16)
    pos = lax.broadcasted_iota(jnp.int32, (srt_rows, tm), 0)
    hit = functools.reduce(jnp.logical_or, [lpos_ref[k:k + 1, :] == pos for k in range(TOP_K)])
    srt_scr[...] = _pack_halves(_dot(jnp.where(hit, 1.0, 0.0).astype(BF16), xb))

    loc = jnp.int32(0)
    for e in range(n_experts):
        n = len_ref[step * n_experts + e]
        copy_pieces(n, loc, xs_ref, base_ref[step * n_experts + e], tm)
        loc = loc + n
    copy_pieces(srt_rows - loc, loc, dump_ref, loc, n_experts * SUBLANES)
    pl.when(step == pl.num_programs(0) - 1)(wait_rows)


def _dest_blocks(dest, tm):
    return [dest[k].reshape(-1, 1, tm) for k in range(TOP_K)]


def _dispatch_call(zero_blocks, run_base, run_len, lpos, xn, m_pad, seq, n_experts):
    n_tok, dw = xn.shape
    tm = min(ROW_COPY_TILE, seq)
    assert tm & (tm - 1) == 0
    srt_rows = TOP_K * tm + n_experts * SUBLANES
    any_spec = pl.BlockSpec(memory_space=pl.ANY)
    xs, _ = pl.pallas_call(
        functools.partial(_dispatch_kernel, tm=tm, rows=EXPERT_ROWS, n_experts=n_experts),
        out_shape=[jax.ShapeDtypeStruct((m_pad, dw), xn.dtype), jax.ShapeDtypeStruct((srt_rows, dw), xn.dtype)],
        grid_spec=pltpu.PrefetchScalarGridSpec(
            num_scalar_prefetch=3,
            grid=(n_tok // tm,),
            in_specs=[pl.BlockSpec((TOP_K, tm), lambda i, *_: (0, i)),
                      pl.BlockSpec((tm, dw), lambda i, *_: (i, 0))],
            out_specs=[any_spec, any_spec],
            scratch_shapes=[pltpu.VMEM((EXPERT_ROWS, dw), xn.dtype), pltpu.VMEM((srt_rows, dw), xn.dtype),
                            pltpu.SemaphoreType.DMA(()), pltpu.SemaphoreType.DMA(())]),
        compiler_params=_cparams("arbitrary"),
        name="dispatch",
    )(zero_blocks, run_base, run_len, lpos, xn)
    return xs


def _expert_kernel(blk_e_ref, used_ref, xs_ref, wg_ref, wu_ref, wd_ref, y_ref, wg_bf, wu_bf, wd_bf):
    i = pl.program_id(0)
    live = i < used_ref[0]
    new_expert = (i == 0) | (blk_e_ref[i] != blk_e_ref[jnp.maximum(i - 1, 0)])

    @pl.when(live & new_expert)
    def _():
        wg_bf[...] = wg_ref[0].astype(BF16)
        wu_bf[...] = wu_ref[0].astype(BF16)
        wd_bf[...] = wd_ref[0].astype(BF16)

    @pl.when(live)
    def _():
        xb = _unpack_halves(xs_ref[...]).astype(BF16)
        g = _dot(xb, wg_bf[...])
        hid = g * _sigmoid(g) * _dot(xb, wu_bf[...])
        y_ref[...] = _pack_halves(_dot(hid.astype(BF16), wd_bf[...]))

    @pl.when(jnp.logical_not(live))
    def _():
        y_ref[...] = jnp.zeros_like(y_ref)


def _expert_call(blk_e, used, xs, wg, wu, wd):
    m_pad, dw = xs.shape
    _, d, f = wg.shape
    rows = EXPERT_ROWS
    return pl.pallas_call(
        _expert_kernel,
        out_shape=jax.ShapeDtypeStruct((m_pad, dw), xs.dtype),
        grid_spec=pltpu.PrefetchScalarGridSpec(
            num_scalar_prefetch=2,
            grid=(m_pad // rows,),
            in_specs=[pl.BlockSpec((rows, dw), lambda i, be, us: (i, 0)),
                      pl.BlockSpec((1, d, f), lambda i, be, us: (be[i], 0, 0)),
                      pl.BlockSpec((1, d, f), lambda i, be, us: (be[i], 0, 0)),
                      pl.BlockSpec((1, f, d), lambda i, be, us: (be[i], 0, 0))],
            out_specs=pl.BlockSpec((rows, dw), lambda i, be, us: (i, 0)),
            scratch_shapes=[pltpu.VMEM((d, f), BF16), pltpu.VMEM((d, f), BF16), pltpu.VMEM((f, d), BF16)]),
        compiler_params=_cparams("arbitrary"),
        name="experts",
    )(blk_e, used, xs, wg, wu, wd)


def _combine_kernel(*refs, tm, final):
    cur_refs, nxt_refs = refs[:TOP_K], refs[TOP_K:2 * TOP_K]
    h_ref, route_ref, gtf_ref, gfin_ref, y_ref, o_ref, land, rows, sem = refs[2 * TOP_K:]
    step = pl.program_id(0)

    def gather(dest_refs):
        for j in range(tm):
            for k, dref in enumerate(dest_refs):
                pltpu.make_async_copy(y_ref.at[pl.ds(dref[0, 0, j], 1)], land.at[k, pl.ds(j, 1)],
                                      sem).start(priority=k % 2)

    def wait_rows():
        for k in range(TOP_K):
            pltpu.make_async_copy(y_ref.at[pl.ds(0, tm)], land.at[k], sem).wait()

    pl.when(step == 0)(lambda: gather(cur_refs))
    wait_rows()
    rows[...] = land[...]
    gather(nxt_refs)
    route = route_ref[...]
    y = route[:, 2:3] * _unpack_halves(rows[0]) + route[:, 3:4] * _unpack_halves(rows[1])
    h = h_ref[...] + gtf_ref[0] * y
    o_ref[...] = _rms(h, gfin_ref[...]) if final else h
    pl.when(step == pl.num_programs(0) - 1)(wait_rows)


def _combine_call(dest, h, route, gtf, g_final, yb, seq, final):
    n_tok, d = h.shape
    tm = min(ROW_COPY_TILE, seq)
    per_b = seq // tm
    steps = n_tok // tm
    tok = lambda w: pl.BlockSpec((tm, w), lambda i: (i, 0))
    cur = pl.BlockSpec((1, 1, tm), lambda i: (i, 0, 0), memory_space=pltpu.SMEM)
    nxt = pl.BlockSpec((1, 1, tm), lambda i: (jnp.minimum(i + 1, steps - 1), 0, 0), memory_space=pltpu.SMEM)
    blocks = _dest_blocks(dest, tm)
    return pl.pallas_call(
        functools.partial(_combine_kernel, tm=tm, final=final),
        out_shape=jax.ShapeDtypeStruct((n_tok, d), F32),
        grid=(steps,),
        in_specs=[cur] * TOP_K + [nxt] * TOP_K + [
            tok(d), tok(ROUTE_COLS),
            pl.BlockSpec((1, 1, d), lambda i: (i // per_b, 0, 0)),
            pl.BlockSpec((1, d), lambda i: (0, 0)),
            pl.BlockSpec(memory_space=pl.ANY)],
        out_specs=tok(d),
        scratch_shapes=[pltpu.VMEM((TOP_K, tm, yb.shape[1]), yb.dtype),
                        pltpu.VMEM((TOP_K, tm, yb.shape[1]), yb.dtype), pltpu.SemaphoreType.DMA(())],
        compiler_params=_cparams("arbitrary"),
        name="combine",
    )(*blocks, *blocks, h, route, gtf, g_final, yb)


def _routing_plan(route_t, counts, n_experts, m_pad, tile):
    del counts
    eid = route_t[0:TOP_K].astype(jnp.int32)
    rank = route_t[4:4 + TOP_K].astype(jnp.int32)

    eid_t = eid.reshape(TOP_K, -1, tile)
    experts = jnp.arange(n_experts, dtype=jnp.int32)
    run_cnt = jnp.sum(eid_t[..., None] == experts, axis=(0, 2), dtype=jnp.int32)
    run_len = (run_cnt + SUBLANES - 1) // SUBLANES * SUBLANES
    rows_e = jnp.sum(run_len, axis=0)
    padded = (rows_e + EXPERT_ROWS - 1) // EXPERT_ROWS * EXPERT_ROWS
    pend = jnp.cumsum(padded)
    pstart = pend - padded
    run_base = pstart[None, :] + jnp.cumsum(run_len, axis=0) - run_len
    before = jnp.cumsum(run_cnt, axis=0) - run_cnt
    to_dest = run_base - before
    to_lpos = (jnp.cumsum(run_len, axis=1) - run_len) - before
    rank_t = rank.reshape(TOP_K, -1, tile)
    dest, lpos = rank_t, rank_t
    for e in range(n_experts):
        sel = eid_t == e
        dest = dest + jnp.where(sel, to_dest[None, :, e, None], 0)
        lpos = lpos + jnp.where(sel, to_lpos[None, :, e, None], 0)

    nblk = m_pad // EXPERT_ROWS
    blk_row = jnp.arange(nblk, dtype=jnp.int32) * EXPERT_ROWS
    blk_e = jnp.minimum(jnp.sum(pend[None, :] <= blk_row[:, None], axis=1), n_experts - 1).astype(jnp.int32)
    used = (pend[-1:] // EXPERT_ROWS).astype(jnp.int32)
    last_blk = jnp.where(padded > 0, pend // EXPERT_ROWS - 1, -1)
    tail_blk = used + jnp.arange(nblk - eid.size // EXPERT_ROWS, dtype=jnp.int32)
    tail_blk = jnp.where(tail_blk < nblk, tail_blk, -1)
    zero_blocks = jnp.concatenate([last_blk, tail_blk]).astype(jnp.int32)
    return (dest.reshape(TOP_K, -1), blk_e, used, zero_blocks, run_base.reshape(-1), run_len.reshape(-1),
            lpos.reshape(TOP_K, -1))


def kernel(x, c, g_mix, g_ffn, w_ada, b_ada, w_in, w_sba_out, g_sgu, w_spatial, b_spatial, w_sgu_out, w_out,
           w_router_group, b_router_group, w_router_expert, b_router_expert, w_expert_gate, w_expert_up,
           w_expert_down, g_final):
    bsz, seq, d = x.shape
    depth = w_in.shape[0]
    sba = w_sba_out.shape[1]
    sgu = g_sgu.shape[1]
    heads = sba // HEAD_DIM
    groups, chunk = w_spatial.shape[1], w_spatial.shape[2]
    n_groups = w_router_group.shape[2]
    n_experts = w_router_expert.shape[2]
    per_group = n_experts // n_groups
    n_tok = bsz * seq
    copy_tile = min(ROW_COPY_TILE, seq)
    m_pad = n_tok * TOP_K + n_experts * (n_tok // copy_tile) * (SUBLANES - 1) + n_experts * EXPERT_ROWS
    m_pad = -(-m_pad // EXPERT_ROWS) * EXPERT_ROWS
    widths = (sba, sba, sba, 2 * sgu, 2 * d)
    assert seq % chunk == 0 and LANES % (sgu // groups) == 0
    assert d % (2 * LANES) == 0
    assert per_group == SUBLANES and n_groups <= SUBLANES and (n_groups + 1) * SUBLANES <= LANES

    h = x.reshape(n_tok, d)
    for l in range(depth):
        mod = _mod_call(c, w_ada[l], b_ada[l])
        sh_m, sc_m, gt_m, sh_f, sc_f, gt_f = [mod[:, i * d:(i + 1) * d].reshape(bsz, 1, d) for i in range(6)]

        col_scale = jnp.concatenate([jnp.ones((w_in.shape[2] - 2 * d,), F32), jnp.full((2 * d,), 0.5, F32)])
        w_in_bf = (w_in[l] * col_scale).astype(BF16)
        q, k, v, uv, gates = _proj_call(h, sh_m, sc_m, g_mix[l].reshape(1, d), w_in_bf, seq, widths)
        ya = _attn_call(q.reshape(bsz, seq, sba), k.reshape(bsz, seq, sba), v.reshape(bsz, seq, sba), heads)

        bias_full = jnp.repeat(b_spatial[l].T, sgu // groups, axis=1)
        gpad = SUBLANES - n_groups
        wr = jnp.concatenate([jnp.pad(w_router_group[l], ((0, 0), (0, gpad))), w_router_expert[l]], axis=1)
        wr = jnp.pad(wr, ((0, 0), (0, LANES - wr.shape[1])))
        br = jnp.concatenate([jnp.pad(b_router_group[l], (0, gpad)), b_router_expert[l]])
        br = jnp.pad(br, (0, LANES - br.shape[0])).reshape(1, LANES)
        h1, xn, route, route_t, counts = _mix_call(
            ya.reshape(n_tok, sba), uv, gates, h, gt_m, sh_f, sc_f, g_sgu[l].reshape(1, sgu), w_spatial[l],
            bias_full, w_sba_out[l].astype(BF16), w_sgu_out[l].astype(BF16), w_out[l].astype(BF16),
            g_ffn[l].reshape(1, d), wr, br, seq, n_groups, per_group)

        dest, blk_e, used, zero_blocks, run_base, run_len, lpos = _routing_plan(
            route_t, counts[SUBLANES:SUBLANES + n_experts, 0], n_experts, m_pad, copy_tile)
        xs = _dispatch_call(zero_blocks, run_base, run_len, lpos, xn, m_pad, seq, n_experts)
        yb = _expert_call(blk_e, used, xs, w_expert_gate[l], w_expert_up[l], w_expert_down[l])
        h = _combine_call(dest, h1, route, gt_f, g_final.reshape(1, d), yb, seq, final=l == depth - 1)
    return h.reshape(bsz, seq, d)
```

```python
import functools

import jax
import jax.numpy as jnp
from jax import lax
from jax.experimental import pallas as pl
from jax.experimental.pallas import tpu as pltpu

F32 = jnp.float32
BF16 = jnp.bfloat16

HEAD_DIM = 64
TOP_K = 2
EPS = 1e-6
LOG2E = 1.4426950408889634
SIGN_BIT = 0x80000000
BF16_BITS = 0xFFFF0000
ATTN_EXIT_LOG2 = 160.0
DEAD_CARRY = 1e30
LANES = 128
SUBLANES = 8
ATTN_BLOCK = 256
EXPERT_ROWS = 512
TOKEN_TILE = 1024
ROW_COPY_TILE = 512
PROJ_TILE = 512
ROUTE_COLS = 8
VMEM_LIMIT = 56 * 1024 * 1024


def _cparams(*sem):
    return pltpu.CompilerParams(dimension_semantics=sem, vmem_limit_bytes=VMEM_LIMIT)


def _split_bf16(a):
    hi = a.astype(BF16)
    lo = (a - hi.astype(F32)).astype(BF16)
    return hi, lo


def _split_bf16_trunc(a):
    hi = lax.bitcast_convert_type(lax.bitcast_convert_type(a, jnp.uint32) & jnp.uint32(BF16_BITS), F32)
    return hi.astype(BF16), (a - hi).astype(BF16)


def _dot(a, b):
    return jnp.dot(a, b, preferred_element_type=F32)


def _dot3(a, b):
    ah, al = _split_bf16(a)
    bh, bl = _split_bf16(b)
    return _dot(ah, bh) + _dot(ah, bl) + _dot(al, bh)


def _pack_halves(x):
    half = x.shape[1] // 2
    hi = lax.bitcast_convert_type(x[:, :half].astype(BF16).astype(F32), jnp.uint32)
    lo = lax.bitcast_convert_type(x[:, half:].astype(BF16).astype(F32), jnp.uint32)
    return hi | (lo >> 16)


def _unpack_halves(u):
    hi = lax.bitcast_convert_type(u & jnp.uint32(BF16_BITS), F32)
    lo = lax.bitcast_convert_type(u << 16, F32)
    return jnp.concatenate([hi, lo], axis=1)


def _rms(x, g):
    ms = jnp.mean(x * x, axis=-1, keepdims=True)
    return x * lax.rsqrt(ms + EPS) * g


def _mod_kernel(c_ref, w_ref, b_ref, o_ref):
    c = c_ref[...]
    ca = c * (1.0 / (1.0 + jnp.exp(-c)))
    o_ref[...] = _dot3(ca, w_ref[...]) + b_ref[...]


def _mod_call(c, w_ada, b_ada):
    bsz, d = c.shape
    n = w_ada.shape[1]
    tn = n // 4 if n % (4 * LANES) == 0 else n
    return pl.pallas_call(
        _mod_kernel,
        out_shape=jax.ShapeDtypeStruct((bsz, n), F32),
        grid=(n // tn,),
        in_specs=[pl.BlockSpec((bsz, d), lambda j: (0, 0)),
                  pl.BlockSpec((d, tn), lambda j: (0, j)),
                  pl.BlockSpec((1, tn), lambda j: (0, j))],
        out_specs=pl.BlockSpec((bsz, tn), lambda j: (0, j)),
        compiler_params=_cparams("arbitrary"),
        name="mod",
    )(c, w_ada, b_ada.reshape(1, n))


def _proj_kernel(x_ref, sh_ref, sc_ref, g_ref, w_ref, q_ref, k_ref, v_ref, uv_ref, gt_ref, *, widths):
    x = x_ref[...]
    n = _rms(x, g_ref[...]) * (1.0 + sc_ref[0]) + sh_ref[0]
    nb = n.astype(BF16)
    off = 0
    for ref, wd in zip((q_ref, k_ref, v_ref, uv_ref, gt_ref), widths):
        p = _dot(nb, w_ref[:, off:off + wd])
        if ref is q_ref:
            p = p * (HEAD_DIM ** -0.5 * LOG2E)
        ref[...] = p.astype(ref.dtype)
        off += wd


def _proj_call(x2, sh, sc, g, w_in_bf, seq, widths):
    n_tok, d = x2.shape
    tm = min(PROJ_TILE, seq)
    per_b = seq // tm
    cols = w_in_bf.shape[1]
    tok = lambda w: pl.BlockSpec((tm, w), lambda i: (i, 0))
    vec = pl.BlockSpec((1, 1, d), lambda i: (i // per_b, 0, 0))
    return pl.pallas_call(
        functools.partial(_proj_kernel, widths=widths),
        out_shape=[jax.ShapeDtypeStruct((n_tok, w), BF16) for w in widths],
        grid=(n_tok // tm,),
        in_specs=[tok(d), vec, vec,
                  pl.BlockSpec((1, d), lambda i: (0, 0)),
                  pl.BlockSpec((d, cols), lambda i: (0, 0))],
        out_specs=[tok(w) for w in widths],
        compiler_params=_cparams("arbitrary"),
        name="proj",
    )(x2, sh, sc, g, w_in_bf)


def _attn_kernel(q_ref, k_ref, v_ref, u_ref, o_ref, qm_ref, acc_ref, carry_ref, *, heads, blk):
    qi = pl.program_id(1)
    u_tri = u_ref[...]
    row = lax.broadcasted_iota(jnp.int32, (blk, blk), 0)
    col = lax.broadcasted_iota(jnp.int32, (blk, blk), 1)
    causal = col < row
    per_slab = LANES // HEAD_DIM
    lane = lax.broadcasted_iota(jnp.int32, (blk, LANES), 1)
    own = [(lane >= j * HEAD_DIM) & (lane < (j + 1) * HEAD_DIM) for j in range(per_slab)]
    for h in range(heads):
        slab = slice(h // per_slab * LANES, (h // per_slab + 1) * LANES)
        qm_ref[h] = jnp.where(own[h % per_slab], q_ref[0, :, slab], 0.0).astype(BF16)

    def key_tile(start, first, dead=None):
        mask = causal if first else None
        slabs = [slice(p * LANES, (p + 1) * LANES) for p in range(heads // per_slab)]
        s_all, cum_all, w_all, carries = {}, {}, {}, []

        def scores(h):
            s_all[h] = lax.dot_general(qm_ref[h], k_ref[0, pl.ds(start, blk), slabs[h // per_slab]],
                                       (((1,), (1,)), ((), ())), preferred_element_type=F32)

        def suffix_sums(h):
            s = s_all[h]
            neg_abs = lax.bitcast_convert_type(lax.bitcast_convert_type(s, jnp.uint32) | jnp.uint32(SIGN_BIT), F32)
            sp = jnp.maximum(s, 0.0) + jnp.log(1.0 + jnp.exp2(neg_abs)) * LOG2E
            if mask is not None:
                sp = jnp.where(mask, sp, 0.0)
            hi, lo = _split_bf16_trunc(sp)
            cum_all[h] = _dot(hi, u_tri) + _dot(lo, u_tri)

        def weights(h):
            cum = cum_all[h]
            if first:
                w = jnp.exp2(s_all[h] - cum)
                carry = cum[:, 0:1]
            else:
                carry = carry_ref[h]
                if dead is not None:
                    carry = jnp.where(dead, DEAD_CARRY, carry)
                w = jnp.exp2((s_all[h] - carry) - cum)
                carry = carry + cum[:, 0:1]
            w_all[h] = (w if mask is None else jnp.where(mask, w, 0.0)).astype(BF16)
            carry_ref[h] = carry
            carries.append(carry)

        def values(p):
            vp = v_ref[0, pl.ds(start, blk), slabs[p]]
            upd = None
            for j in range(per_slab):
                pv = _dot(w_all[p * per_slab + j], vp)
                upd = pv if upd is None else jnp.where(own[j], pv, upd)
            acc_ref[:, slabs[p]] = upd if first else acc_ref[:, slabs[p]] + upd

        for t in range(heads + 2):
            if t < heads:
                scores(t)
            if 1 <= t <= heads:
                suffix_sums(t - 1)
            if t >= 2:
                weights(t - 2)
                if (t - 2) % per_slab == per_slab - 1:
                    values((t - 2) // per_slab)
        return jnp.min(functools.reduce(jnp.minimum, carries))

    key_tile(pl.multiple_of(qi * blk, blk), True)
    cmin = key_tile(pl.multiple_of(jnp.maximum(qi - 1, 0) * blk, blk), False, dead=qi == 0)

    def cond(state):
        i, cmin = state
        return (i < qi) & (cmin < ATTN_EXIT_LOG2)

    def body(state):
        i, _ = state
        return i + 1, key_tile(pl.multiple_of((qi - 1 - i) * blk, blk), False)

    lax.while_loop(cond, body, (jnp.int32(1), cmin))
    o_ref[0] = acc_ref[...].astype(o_ref.dtype)


def _attn_call(q, k, v, heads):
    bsz, seq, width = q.shape
    blk = min(ATTN_BLOCK, seq)
    r = lax.broadcasted_iota(jnp.int32, (blk, blk), 0)
    c = lax.broadcasted_iota(jnp.int32, (blk, blk), 1)
    u_tri = (r >= c).astype(BF16)
    full = pl.BlockSpec((1, seq, width), lambda b, i: (b, 0, 0))
    return pl.pallas_call(
        functools.partial(_attn_kernel, heads=heads, blk=blk),
        out_shape=jax.ShapeDtypeStruct((bsz, seq, width), BF16),
        grid=(bsz, seq // blk),
        in_specs=[pl.BlockSpec((1, blk, width), lambda b, i: (b, i, 0)), full, full,
                  pl.BlockSpec((blk, blk), lambda b, i: (0, 0))],
        out_specs=pl.BlockSpec((1, blk, width), lambda b, i: (b, i, 0)),
        scratch_shapes=[pltpu.VMEM((heads, blk, LANES), BF16), pltpu.VMEM((blk, width), F32),
                        pltpu.VMEM((heads, blk, 1), F32)],
        compiler_params=_cparams("arbitrary", "arbitrary"),
        name="attn",
    )(q, k, v, u_tri)


def _gelu_tanh(x):
    c = 0.7978845608028654
    hx = 0.5 * x
    return hx + hx * jnp.tanh(x * (c + (0.044715 * c) * (x * x)))


def _sigmoid(x):
    return 1.0 / (1.0 + jnp.exp(-x))


def _mix_kernel(ya_ref, uv_ref, gt_ref, x_ref, gtm_ref, shf_ref, scf_ref, gsgu_ref, ws_ref, bs_ref,
                wa_ref, wb_ref, wo_ref, gffn_ref, wr_ref, br_ref, utri_ref,
                h_ref, xn_ref, route_ref, route_t_ref, cnt_ref, yb_scr, cnt_scr,
                *, chunk, groups, n_groups, per_group):
    step = pl.program_id(0)
    tm = x_ref.shape[0]
    sgu = gsgu_ref.shape[1]
    gdim = sgu // groups

    @pl.when(step == 0)
    def _():
        cnt_scr[...] = jnp.zeros_like(cnt_scr)

    ya_proj = _dot(ya_ref[...], wa_ref[...])

    act = _gelu_tanh(uv_ref[...].astype(F32))
    u = act[:, :sgu]
    v = act[:, sgu:]
    mu = jnp.mean(v, axis=-1, keepdims=True)
    vc = v - mu
    var = jnp.mean(vc * vc, axis=-1, keepdims=True)
    vn = vc * lax.rsqrt(var + EPS) * gsgu_ref[...]
    r = lax.broadcasted_iota(jnp.int32, (chunk, chunk), 0)
    c = lax.broadcasted_iota(jnp.int32, (chunk, chunk), 1)
    wcs = [jnp.where(r >= c, ws_ref[g], 0.0).astype(BF16) for g in range(groups)]
    pair = LANES // gdim
    lane = lax.broadcasted_iota(jnp.int32, (chunk, LANES), 1)
    for ci in range(tm // chunk):
        rows = slice(ci * chunk, (ci + 1) * chunk)
        for p in range(groups // pair):
            lanes = slice(p * LANES, (p + 1) * LANES)
            slab = vn[rows, lanes]
            mix = bs_ref[:, lanes]
            for j in range(pair):
                sel = (lane >= j * gdim) & (lane < (j + 1) * gdim)
                mix = mix + _dot(wcs[p * pair + j], jnp.where(sel, slab, 0.0).astype(BF16))
            yb_scr[rows, lanes] = (u[rows, lanes] * mix).astype(BF16)

    d = x_ref.shape[1]
    gates2 = 1.0 + jnp.tanh(gt_ref[...].astype(F32))
    merged2 = gates2[:, :d] * ya_proj + gates2[:, d:] * _dot(yb_scr[...], wb_ref[...])
    h = x_ref[...] + (0.5 * gtm_ref[0]) * _dot(merged2.astype(BF16), wo_ref[...])
    h_ref[...] = h

    xn = _rms(h, gffn_ref[...]) * (1.0 + scf_ref[0]) + shf_ref[0]
    xn_ref[...] = _pack_halves(xn)
    logits = _dot3(xn, wr_ref[...]) + br_ref[...]

    lt = logits.T
    sub = lax.broadcasted_iota(jnp.int32, (SUBLANES, tm), 0)
    neg = -jnp.inf
    gl = jnp.where(sub < n_groups, lt[0:SUBLANES], neg)
    gmax = jnp.max(gl, axis=0, keepdims=True)
    g_sel = jnp.min(jnp.where(gl == gmax, sub, SUBLANES), axis=0, keepdims=True)
    g_w = 1.0 / jnp.sum(jnp.exp(gl - gmax), axis=0, keepdims=True)
    el = lt[SUBLANES:2 * SUBLANES]
    for g in range(1, n_groups):
        el = jnp.where(g_sel == g, lt[(g + 1) * SUBLANES:(g + 2) * SUBLANES], el)
    m1 = jnp.max(el, axis=0, keepdims=True)
    j1 = jnp.min(jnp.where(el == m1, sub, SUBLANES), axis=0, keepdims=True)
    el2 = jnp.where(sub == j1, neg, el)
    m2 = jnp.max(el2, axis=0, keepdims=True)
    j2 = jnp.min(jnp.where(el2 == m2, sub, SUBLANES), axis=0, keepdims=True)
    t = jnp.exp(m2 - m1)
    w1 = g_w / (1.0 + t)
    w2 = g_w * t / (1.0 + t)
    e1 = g_sel * per_group + j1
    e2 = g_sel * per_group + j2

    row = lax.broadcasted_iota(jnp.int32, (LANES, tm), 0)
    oh1 = row == e1 + SUBLANES
    oh2 = row == e2 + SUBLANES
    both = jnp.where(oh1 | oh2, 1.0, 0.0)
    before = _dot(both.astype(BF16), utri_ref[...]) + cnt_scr[...]
    rank1 = jnp.sum(jnp.where(oh1, before, 0.0), axis=0, keepdims=True)
    rank2 = jnp.sum(jnp.where(oh2, before, 0.0), axis=0, keepdims=True)
    cnt_scr[...] = cnt_scr[...] + jnp.sum(both, axis=1, keepdims=True)
    cnt_ref[...] = cnt_scr[...]

    rec = jnp.zeros((LANES, tm), F32)
    for idx, val in enumerate((e1.astype(F32), e2.astype(F32), w1, w2, rank1, rank2)):
        rec = jnp.where(row == idx, val, rec)
    route_t_ref[...] = rec[:ROUTE_COLS]
    route_ref[...] = rec.T[:, :ROUTE_COLS]


def _mix_call(ya, uv, gates, x2, gtm, shf, scf, g_sgu, w_spatial, bias_full, wa, wb, wo, g_ffn, wr, br,
              seq, n_groups, per_group):
    n_tok, d = x2.shape
    tm = min(TOKEN_TILE, seq)
    per_b = seq // tm
    groups, chunk, _ = w_spatial.shape
    sgu = g_sgu.shape[1]
    r = lax.broadcasted_iota(jnp.int32, (tm, tm), 0)
    c = lax.broadcasted_iota(jnp.int32, (tm, tm), 1)
    utri = (r < c).astype(BF16)
    tok = lambda w: pl.BlockSpec((tm, w), lambda i: (i, 0))
    vec = pl.BlockSpec((1, 1, d), lambda i: (i // per_b, 0, 0))
    const = lambda a: pl.BlockSpec(a.shape, lambda i: (0,) * a.ndim)
    ins = [ya, uv, gates, x2, gtm, shf, scf, g_sgu, w_spatial, bias_full, wa, wb, wo, g_ffn, wr, br, utri]
    in_specs = [tok(ya.shape[1]), tok(uv.shape[1]), tok(gates.shape[1]), tok(d), vec, vec, vec]
    in_specs += [const(a) for a in ins[7:]]
    return pl.pallas_call(
        functools.partial(_mix_kernel, chunk=chunk, groups=groups, n_groups=n_groups, per_group=per_group),
        out_shape=[jax.ShapeDtypeStruct((n_tok, d), F32), jax.ShapeDtypeStruct((n_tok, d // 2), jnp.uint32),
                   jax.ShapeDtypeStruct((n_tok, ROUTE_COLS), F32), jax.ShapeDtypeStruct((ROUTE_COLS, n_tok), F32),
                   jax.ShapeDtypeStruct((LANES, 1), F32)],
        grid=(n_tok // tm,),
        in_specs=in_specs,
        out_specs=[tok(d), tok(d // 2), tok(ROUTE_COLS), pl.BlockSpec((ROUTE_COLS, tm), lambda i: (0, i)),
                   pl.BlockSpec((LANES, 1), lambda i: (0, 0))],
        scratch_shapes=[pltpu.VMEM((tm, sgu), BF16), pltpu.VMEM((LANES, 1), F32)],
        compiler_params=_cparams("arbitrary"),
        name="mix",
    )(*ins)


def _dispatch_kernel(zblk_ref, base_ref, len_ref, lpos_ref, xn_ref, xs_ref, dump_ref, zero_scr, srt_scr, sem, zsem,
                     *, tm, rows, n_experts):
    step = pl.program_id(0)
    srt_rows = srt_scr.shape[0]

    def wait_rows():
        pltpu.make_async_copy(srt_scr, dump_ref, sem).wait()

    def copy_pieces(n, src, dst_ref, dst, largest):
        size = largest
        while size >= SUBLANES:
            take = (n & size) != 0

            @pl.when(take)
            def _(src=src, dst=dst, size=size):
                pltpu.make_async_copy(srt_scr.at[pl.ds(pl.multiple_of(src, SUBLANES), size)],
                                      dst_ref.at[pl.ds(pl.multiple_of(dst, SUBLANES), size)], sem).start()

            step_by = jnp.where(take, size, 0)
            src, dst = src + step_by, dst + step_by
            size //= 2

    @pl.when(step == 0)
    def _():
        zero_scr[...] = jnp.zeros_like(zero_scr)
        srt_scr[...] = jnp.zeros_like(srt_scr)
        init_dump = pltpu.make_async_copy(srt_scr, dump_ref, zsem)
        init_dump.start()
        init_dump.wait()

        def zero_copy(j):
            start = pl.multiple_of(zblk_ref[j] * rows, rows)
            return pltpu.make_async_copy(zero_scr, xs_ref.at[pl.ds(start, rows)], zsem)

        for j in range(zblk_ref.shape[0]):
            pl.when(zblk_ref[j] >= 0)(lambda j=j: zero_copy(j).start())
        for j in range(zblk_ref.shape[0]):
            pl.when(zblk_ref[j] >= 0)(lambda j=j: zero_copy(j).wait())

    pl.when(step > 0)(wait_rows)
    xb = _unpack_halves(xn_ref[...]).astype(BF16)
    pos = lax.broadcasted_iota(jnp.int32, (srt_rows, tm), 0)
    hit = functools.reduce(jnp.logical_or, [lpos_ref[k:k + 1, :] == pos for k in range(TOP_K)])
    srt_scr[...] = _pack_halves(_dot(jnp.where(hit, 1.0, 0.0).astype(BF16), xb))

    loc = jnp.int32(0)
    for e in range(n_experts):
        n = len_ref[step * n_experts + e]
        copy_pieces(n, loc, xs_ref, base_ref[step * n_experts + e], tm)
        loc = loc + n
    copy_pieces(srt_rows - loc, loc, dump_ref, loc, n_experts * SUBLANES)
    pl.when(step == pl.num_programs(0) - 1)(wait_rows)


def _dispatch_call(zero_blocks, run_base, run_len, lpos, xn, m_pad, seq, n_experts):
    n_tok, dw = xn.shape
    tm = min(ROW_COPY_TILE, seq)
    assert tm & (tm - 1) == 0
    srt_rows = TOP_K * tm + n_experts * SUBLANES
    any_spec = pl.BlockSpec(memory_space=pl.ANY)
    xs, _ = pl.pallas_call(
        functools.partial(_dispatch_kernel, tm=tm, rows=EXPERT_ROWS, n_experts=n_experts),
        out_shape=[jax.ShapeDtypeStruct((m_pad, dw), xn.dtype), jax.ShapeDtypeStruct((srt_rows, dw), xn.dtype)],
        grid_spec=pltpu.PrefetchScalarGridSpec(
            num_scalar_prefetch=3,
            grid=(n_tok // tm,),
            in_specs=[pl.BlockSpec((TOP_K, tm), lambda i, *_: (0, i)),
                      pl.BlockSpec((tm, dw), lambda i, *_: (i, 0))],
            out_specs=[any_spec, any_spec],
            scratch_shapes=[pltpu.VMEM((EXPERT_ROWS, dw), xn.dtype), pltpu.VMEM((srt_rows, dw), xn.dtype),
                            pltpu.SemaphoreType.DMA(()), pltpu.SemaphoreType.DMA(())]),
        compiler_params=_cparams("arbitrary"),
        name="dispatch",
    )(zero_blocks, run_base, run_len, lpos, xn)
    return xs


def _expert_kernel(blk_e_ref, used_ref, xs_ref, wg_ref, wu_ref, wd_ref, y_ref, wg_bf, wu_bf, wd_bf):
    i = pl.program_id(0)
    live = i < used_ref[0]
    new_expert = (i == 0) | (blk_e_ref[i] != blk_e_ref[jnp.maximum(i - 1, 0)])

    @pl.when(live & new_expert)
    def _():
        wg_bf[...] = wg_ref[0].astype(BF16)
        wu_bf[...] = wu_ref[0].astype(BF16)
        wd_bf[...] = wd_ref[0].astype(BF16)

    @pl.when(live)
    def _():
        xb = _unpack_halves(xs_ref[...]).astype(BF16)
        g = _dot(xb, wg_bf[...])
        hid = g * _sigmoid(g) * _dot(xb, wu_bf[...])
        y_ref[...] = _pack_halves(_dot(hid.astype(BF16), wd_bf[...]))

    @pl.when(jnp.logical_not(live))
    def _():
        y_ref[...] = jnp.zeros_like(y_ref)


def _expert_call(blk_e, used, xs, wg, wu, wd):
    m_pad, dw = xs.shape
    _, d, f = wg.shape
    rows = EXPERT_ROWS
    return pl.pallas_call(
        _expert_kernel,
        out_shape=jax.ShapeDtypeStruct((m_pad, dw), xs.dtype),
        grid_spec=pltpu.PrefetchScalarGridSpec(
            num_scalar_prefetch=2,
            grid=(m_pad // rows,),
            in_specs=[pl.BlockSpec((rows, dw), lambda i, be, us: (i, 0)),
                      pl.BlockSpec((1, d, f), lambda i, be, us: (be[i], 0, 0)),
                      pl.BlockSpec((1, d, f), lambda i, be, us: (be[i], 0, 0)),
                      pl.BlockSpec((1, f, d), lambda i, be, us: (be[i], 0, 0))],
            out_specs=pl.BlockSpec((rows, dw), lambda i, be, us: (i, 0)),
            scratch_shapes=[pltpu.VMEM((d, f), BF16), pltpu.VMEM((d, f), BF16), pltpu.VMEM((f, d), BF16)]),
        compiler_params=_cparams("arbitrary"),
        name="experts",
    )(blk_e, used, xs, wg, wu, wd)


def _combine_kernel(base_ref, len_ref, lpos_ref, h_ref, route_ref, gtf_ref, gfin_ref, y_ref, o_ref, land, rows, sem,
                    *, tm, n_experts, final):
    step = pl.program_id(0)
    last = pl.num_programs(0) - 1
    srt_rows = land.shape[0]

    def gather(tile):
        def pieces(n, src, dst, largest):
            size = largest
            while size >= SUBLANES:
                take = (n & size) != 0

                @pl.when(take)
                def _(src=src, dst=dst, size=size):
                    pltpu.make_async_copy(y_ref.at[pl.ds(pl.multiple_of(src, SUBLANES), size)],
                                          land.at[pl.ds(pl.multiple_of(dst, SUBLANES), size)], sem).start()

                step_by = jnp.where(take, size, 0)
                src, dst = src + step_by, dst + step_by
                size //= 2

        loc = jnp.int32(0)
        for e in range(n_experts):
            n = len_ref[tile * n_experts + e]
            pieces(n, base_ref[tile * n_experts + e], loc, tm)
            loc = loc + n
        pieces(srt_rows - loc, jnp.int32(0), loc, n_experts * SUBLANES)

    def wait_rows():
        pltpu.make_async_copy(y_ref.at[pl.ds(0, srt_rows)], land, sem).wait()

    pl.when(step == 0)(lambda: gather(step))
    wait_rows()
    rows[...] = land[...]
    gather(jnp.minimum(step + 1, last))
    vals = _unpack_halves(rows[...]).astype(BF16)
    pos = lax.broadcasted_iota(jnp.int32, (tm, srt_rows), 1)
    route = route_ref[...]
    y = None
    for k in range(TOP_K):
        pick = jnp.where(lpos_ref[:, k:k + 1] == pos, 1.0, 0.0).astype(BF16)
        term = route[:, 2 + k:3 + k] * _dot(pick, vals)
        y = term if y is None else y + term
    h = h_ref[...] + gtf_ref[0] * y
    o_ref[...] = _rms(h, gfin_ref[...]) if final else h
    pl.when(step == last)(wait_rows)


def _combine_call(run_base, run_len, lpos_col, h, route, gtf, g_final, yb, seq, n_experts, final):
    n_tok, d = h.shape
    tm = min(ROW_COPY_TILE, seq)
    per_b = seq // tm
    srt_rows = TOP_K * tm + n_experts * SUBLANES
    tok = lambda w: pl.BlockSpec((tm, w), lambda i, *_: (i, 0))
    return pl.pallas_call(
        functools.partial(_combine_kernel, tm=tm, n_experts=n_experts, final=final),
        out_shape=jax.ShapeDtypeStruct((n_tok, d), F32),
        grid_spec=pltpu.PrefetchScalarGridSpec(
            num_scalar_prefetch=2,
            grid=(n_tok // tm,),
            in_specs=[tok(TOP_K), tok(d), tok(ROUTE_COLS),
                      pl.BlockSpec((1, 1, d), lambda i, *_: (i // per_b, 0, 0)),
                      pl.BlockSpec((1, d), lambda i, *_: (0, 0)),
                      pl.BlockSpec(memory_space=pl.ANY)],
            out_specs=tok(d),
            scratch_shapes=[pltpu.VMEM((srt_rows, yb.shape[1]), yb.dtype),
                            pltpu.VMEM((srt_rows, yb.shape[1]), yb.dtype), pltpu.SemaphoreType.DMA(())]),
        compiler_params=_cparams("arbitrary"),
        name="combine",
    )(run_base, run_len, lpos_col, h, route, gtf, g_final, yb)


def _routing_plan(route_t, counts, n_experts, m_pad, tile):
    del counts
    eid = route_t[0:TOP_K].astype(jnp.int32)
    rank = route_t[4:4 + TOP_K].astype(jnp.int32)

    eid_t = eid.reshape(TOP_K, -1, tile)
    experts = jnp.arange(n_experts, dtype=jnp.int32)
    run_cnt = jnp.sum(eid_t[..., None] == experts, axis=(0, 2), dtype=jnp.int32)
    run_len = (run_cnt + SUBLANES - 1) // SUBLANES * SUBLANES
    rows_e = jnp.sum(run_len, axis=0)
    padded = (rows_e + EXPERT_ROWS - 1) // EXPERT_ROWS * EXPERT_ROWS
    pend = jnp.cumsum(padded)
    pstart = pend - padded
    run_base = pstart[None, :] + jnp.cumsum(run_len, axis=0) - run_len
    before = jnp.cumsum(run_cnt, axis=0) - run_cnt
    to_lpos = (jnp.cumsum(run_len, axis=1) - run_len) - before
    lpos = rank.reshape(TOP_K, -1, tile)
    for e in range(n_experts):
        lpos = lpos + jnp.where(eid_t == e, to_lpos[None, :, e, None], 0)

    nblk = m_pad // EXPERT_ROWS
    blk_row = jnp.arange(nblk, dtype=jnp.int32) * EXPERT_ROWS
    blk_e = jnp.minimum(jnp.sum(pend[None, :] <= blk_row[:, None], axis=1), n_experts - 1).astype(jnp.int32)
    used = (pend[-1:] // EXPERT_ROWS).astype(jnp.int32)
    last_blk = jnp.where(padded > 0, pend // EXPERT_ROWS - 1, -1)
    tail_blk = used + jnp.arange(nblk - eid.size // EXPERT_ROWS, dtype=jnp.int32)
    tail_blk = jnp.where(tail_blk < nblk, tail_blk, -1)
    zero_blocks = jnp.concatenate([last_blk, tail_blk]).astype(jnp.int32)
    return blk_e, used, zero_blocks, run_base.reshape(-1), run_len.reshape(-1), lpos.reshape(TOP_K, -1)


def kernel(x, c, g_mix, g_ffn, w_ada, b_ada, w_in, w_sba_out, g_sgu, w_spatial, b_spatial, w_sgu_out, w_out,
           w_router_group, b_router_group, w_router_expert, b_router_expert, w_expert_gate, w_expert_up,
           w_expert_down, g_final):
    bsz, seq, d = x.shape
    depth = w_in.shape[0]
    sba = w_sba_out.shape[1]
    sgu = g_sgu.shape[1]
    heads = sba // HEAD_DIM
    groups, chunk = w_spatial.shape[1], w_spatial.shape[2]
    n_groups = w_router_group.shape[2]
    n_experts = w_router_expert.shape[2]
    per_group = n_experts // n_groups
    n_tok = bsz * seq
    copy_tile = min(ROW_COPY_TILE, seq)
    m_pad = n_tok * TOP_K + n_experts * (n_tok // copy_tile) * (SUBLANES - 1) + n_experts * EXPERT_ROWS
    m_pad = -(-m_pad // EXPERT_ROWS) * EXPERT_ROWS
    widths = (sba, sba, sba, 2 * sgu, 2 * d)
    assert seq % chunk == 0 and LANES % (sgu // groups) == 0
    assert d % (2 * LANES) == 0
    assert per_group == SUBLANES and n_groups <= SUBLANES and (n_groups + 1) * SUBLANES <= LANES

    h = x.reshape(n_tok, d)
    for l in range(depth):
        mod = _mod_call(c, w_ada[l], b_ada[l])
        sh_m, sc_m, gt_m, sh_f, sc_f, gt_f = [mod[:, i * d:(i + 1) * d].reshape(bsz, 1, d) for i in range(6)]

        col_scale = jnp.concatenate([jnp.ones((w_in.shape[2] - 2 * d,), F32), jnp.full((2 * d,), 0.5, F32)])
        w_in_bf = (w_in[l] * col_scale).astype(BF16)
        q, k, v, uv, gates = _proj_call(h, sh_m, sc_m, g_mix[l].reshape(1, d), w_in_bf, seq, widths)
        ya = _attn_call(q.reshape(bsz, seq, sba), k.reshape(bsz, seq, sba), v.reshape(bsz, seq, sba), heads)

        bias_full = jnp.repeat(b_spatial[l].T, sgu // groups, axis=1)
        gpad = SUBLANES - n_groups
        wr = jnp.concatenate([jnp.pad(w_router_group[l], ((0, 0), (0, gpad))), w_router_expert[l]], axis=1)
        wr = jnp.pad(wr, ((0, 0), (0, LANES - wr.shape[1])))
        br = jnp.concatenate([jnp.pad(b_router_group[l], (0, gpad)), b_router_expert[l]])
        br = jnp.pad(br, (0, LANES - br.shape[0])).reshape(1, LANES)
        h1, xn, route, route_t, counts = _mix_call(
            ya.reshape(n_tok, sba), uv, gates, h, gt_m, sh_f, sc_f, g_sgu[l].reshape(1, sgu), w_spatial[l],
            bias_full, w_sba_out[l].astype(BF16), w_sgu_out[l].astype(BF16), w_out[l].astype(BF16),
            g_ffn[l].reshape(1, d), wr, br, seq, n_groups, per_group)

        blk_e, used, zero_blocks, run_base, run_len, lpos = _routing_plan(
            route_t, counts[SUBLANES:SUBLANES + n_experts, 0], n_experts, m_pad, copy_tile)
        xs = _dispatch_call(zero_blocks, run_base, run_len, lpos, xn, m_pad, seq, n_experts)
        yb = _expert_call(blk_e, used, xs, w_expert_gate[l], w_expert_up[l], w_expert_down[l])
        h = _combine_call(run_base, run_len, lpos.T, h1, route, gt_f, g_final.reshape(1, d), yb, seq, n_experts,
                          final=l == depth - 1)
    return h.reshape(bsz, seq, d)
```

```python
import functools

import jax
import jax.numpy as jnp
from jax import lax
from jax.experimental import pallas as pl
from jax.experimental.pallas import tpu as pltpu

F32 = jnp.float32
BF16 = jnp.bfloat16

HEAD_DIM = 64
TOP_K = 2
EPS = 1e-6
LOG2E = 1.4426950408889634
SIGN_BIT = 0x80000000
BF16_BITS = 0xFFFF0000
ATTN_EXIT_LOG2 = 160.0
DEAD_CARRY = 1e30
LANES = 128
SUBLANES = 8
ATTN_BLOCK = 256
EXPERT_ROWS = 512
TOKEN_TILE = 1024
ROW_COPY_TILE = 512
PROJ_TILE = 512
ROUTE_COLS = 8
VMEM_LIMIT = 56 * 1024 * 1024


def _cparams(*sem):
    return pltpu.CompilerParams(dimension_semantics=sem, vmem_limit_bytes=VMEM_LIMIT)


def _split_bf16(a):
    hi = a.astype(BF16)
    lo = (a - hi.astype(F32)).astype(BF16)
    return hi, lo


def _dot(a, b):
    return jnp.dot(a, b, preferred_element_type=F32)


def _dot3(a, b):
    ah, al = _split_bf16(a)
    bh, bl = _split_bf16(b)
    return _dot(ah, bh) + _dot(ah, bl) + _dot(al, bh)


def _pack_halves(x):
    half = x.shape[1] // 2
    hi = lax.bitcast_convert_type(x[:, :half].astype(BF16).astype(F32), jnp.uint32)
    lo = lax.bitcast_convert_type(x[:, half:].astype(BF16).astype(F32), jnp.uint32)
    return hi | (lo >> 16)


def _unpack_halves(u):
    hi = lax.bitcast_convert_type(u & jnp.uint32(BF16_BITS), F32)
    lo = lax.bitcast_convert_type(u << 16, F32)
    return jnp.concatenate([hi, lo], axis=1)


def _rms(x, g):
    ms = jnp.mean(x * x, axis=-1, keepdims=True)
    return x * lax.rsqrt(ms + EPS) * g


def _mod_kernel(c_ref, w_ref, b_ref, o_ref):
    c = c_ref[...]
    ca = c * (1.0 / (1.0 + jnp.exp(-c)))
    o_ref[...] = _dot3(ca, w_ref[...]) + b_ref[...]


def _mod_call(c, w_ada, b_ada):
    bsz, d = c.shape
    n = w_ada.shape[1]
    tn = n // 4 if n % (4 * LANES) == 0 else n
    return pl.pallas_call(
        _mod_kernel,
        out_shape=jax.ShapeDtypeStruct((bsz, n), F32),
        grid=(n // tn,),
        in_specs=[pl.BlockSpec((bsz, d), lambda j: (0, 0)),
                  pl.BlockSpec((d, tn), lambda j: (0, j)),
                  pl.BlockSpec((1, tn), lambda j: (0, j))],
        out_specs=pl.BlockSpec((bsz, tn), lambda j: (0, j)),
        compiler_params=_cparams("arbitrary"),
        name="mod",
    )(c, w_ada, b_ada.reshape(1, n))


def _gelu_tanh(x):
    c = 0.7978845608028654
    hx = 0.5 * x
    return hx + hx * jnp.tanh(x * (c + (0.044715 * c) * (x * x)))


def _proj_kernel(x_ref, sh_ref, sc_ref, g_ref, gsgu_ref, w_ref, q_ref, k_ref, v_ref, uv_ref, gt_ref, *, widths):
    x = x_ref[...]
    n = _rms(x, g_ref[...]) * (1.0 + sc_ref[0]) + sh_ref[0]
    nb = n.astype(BF16)
    off = 0
    for ref, wd in zip((q_ref, k_ref, v_ref, uv_ref, gt_ref), widths):
        p = _dot(nb, w_ref[:, off:off + wd])
        if ref is q_ref:
            p = p * (HEAD_DIM ** -0.5 * LOG2E)
        elif ref is uv_ref:
            act = _gelu_tanh(p)
            sgu = wd // 2
            v = act[:, sgu:]
            vc = v - jnp.mean(v, axis=-1, keepdims=True)
            var = jnp.mean(vc * vc, axis=-1, keepdims=True)
            p = jnp.concatenate([act[:, :sgu], vc * lax.rsqrt(var + EPS) * gsgu_ref[...]], axis=1)
        elif ref is gt_ref:
            p = 1.0 + jnp.tanh(0.5 * p)
        ref[...] = p.astype(ref.dtype)
        off += wd


def _proj_call(x2, sh, sc, g, g_sgu, w_in_bf, seq, widths):
    n_tok, d = x2.shape
    tm = min(PROJ_TILE, seq)
    per_b = seq // tm
    cols = w_in_bf.shape[1]
    tok = lambda w: pl.BlockSpec((tm, w), lambda i: (i, 0))
    vec = pl.BlockSpec((1, 1, d), lambda i: (i // per_b, 0, 0))
    return pl.pallas_call(
        functools.partial(_proj_kernel, widths=widths),
        out_shape=[jax.ShapeDtypeStruct((n_tok, w), BF16) for w in widths],
        grid=(n_tok // tm,),
        in_specs=[tok(d), vec, vec,
                  pl.BlockSpec((1, d), lambda i: (0, 0)),
                  pl.BlockSpec(g_sgu.shape, lambda i: (0, 0)),
                  pl.BlockSpec((d, cols), lambda i: (0, 0))],
        out_specs=[tok(w) for w in widths],
        compiler_params=_cparams("arbitrary"),
        name="proj",
    )(x2, sh, sc, g, g_sgu, w_in_bf)


def _attn_kernel(q_ref, k_ref, v_ref, u_ref, o_ref, qm_ref, acc_ref, carry_ref, *, heads, blk):
    qi = pl.program_id(1)
    u_tri = u_ref[...]
    row = lax.broadcasted_iota(jnp.int32, (blk, blk), 0)
    col = lax.broadcasted_iota(jnp.int32, (blk, blk), 1)
    causal = col < row
    per_slab = LANES // HEAD_DIM
    lane = lax.broadcasted_iota(jnp.int32, (blk, LANES), 1)
    own = [(lane >= j * HEAD_DIM) & (lane < (j + 1) * HEAD_DIM) for j in range(per_slab)]
    for h in range(heads):
        slab = slice(h // per_slab * LANES, (h // per_slab + 1) * LANES)
        qm_ref[h] = jnp.where(own[h % per_slab], q_ref[0, :, slab], 0.0).astype(BF16)

    def key_tile(start, first, dead=None):
        mask = causal if first else None
        slabs = [slice(p * LANES, (p + 1) * LANES) for p in range(heads // per_slab)]
        s_all, cum_all, w_all, carries = {}, {}, {}, []

        def scores(h):
            s_all[h] = lax.dot_general(qm_ref[h], k_ref[0, pl.ds(start, blk), slabs[h // per_slab]],
                                       (((1,), (1,)), ((), ())), preferred_element_type=F32)

        def suffix_sums(h):
            s = s_all[h]
            neg_abs = lax.bitcast_convert_type(lax.bitcast_convert_type(s, jnp.uint32) | jnp.uint32(SIGN_BIT), F32)
            sp = jnp.maximum(s, 0.0) + jnp.log(1.0 + jnp.exp2(neg_abs)) * LOG2E
            if mask is not None:
                sp = jnp.where(mask, sp, 0.0)
            hi = lax.bitcast_convert_type(lax.bitcast_convert_type(sp, jnp.uint32) & jnp.uint32(BF16_BITS), F32)
            cum_all[h] = _dot(hi.astype(BF16), u_tri) + _dot((sp - hi).astype(BF16), u_tri)

        def weights(h):
            cum = cum_all[h]
            if first:
                w = jnp.exp2(s_all[h] - cum)
                carry = cum[:, 0:1]
            else:
                carry = carry_ref[h]
                if dead is not None:
                    carry = jnp.where(dead, DEAD_CARRY, carry)
                w = jnp.exp2((s_all[h] - carry) - cum)
                carry = carry + cum[:, 0:1]
            w_all[h] = (w if mask is None else jnp.where(mask, w, 0.0)).astype(BF16)
            carry_ref[h] = carry
            carries.append(carry)

        def values(p):
            vp = v_ref[0, pl.ds(start, blk), slabs[p]]
            upd = None
            for j in range(per_slab):
                pv = _dot(w_all[p * per_slab + j], vp)
                upd = pv if upd is None else jnp.where(own[j], pv, upd)
            acc_ref[:, slabs[p]] = upd if first else acc_ref[:, slabs[p]] + upd

        for t in range(heads + 2):
            if t < heads:
                scores(t)
            if 1 <= t <= heads:
                suffix_sums(t - 1)
            if t >= 2:
                weights(t - 2)
                if (t - 2) % per_slab == per_slab - 1:
                    values((t - 2) // per_slab)
        return jnp.min(functools.reduce(jnp.minimum, carries))

    key_tile(pl.multiple_of(qi * blk, blk), True)
    cmin = key_tile(pl.multiple_of(jnp.maximum(qi - 1, 0) * blk, blk), False, dead=qi == 0)

    def cond(state):
        i, cmin = state
        return (i < qi) & (cmin < ATTN_EXIT_LOG2)

    def body(state):
        i, _ = state
        return i + 1, key_tile(pl.multiple_of((qi - 1 - i) * blk, blk), False)

    lax.while_loop(cond, body, (jnp.int32(1), cmin))
    o_ref[0] = acc_ref[...].astype(o_ref.dtype)


def _attn_call(q, k, v, heads):
    bsz, seq, width = q.shape
    blk = min(ATTN_BLOCK, seq)
    r = lax.broadcasted_iota(jnp.int32, (blk, blk), 0)
    c = lax.broadcasted_iota(jnp.int32, (blk, blk), 1)
    u_tri = (r >= c).astype(BF16)
    full = pl.BlockSpec((1, seq, width), lambda b, i: (b, 0, 0))
    return pl.pallas_call(
        functools.partial(_attn_kernel, heads=heads, blk=blk),
        out_shape=jax.ShapeDtypeStruct((bsz, seq, width), BF16),
        grid=(bsz, seq // blk),
        in_specs=[pl.BlockSpec((1, blk, width), lambda b, i: (b, i, 0)), full, full,
                  pl.BlockSpec((blk, blk), lambda b, i: (0, 0))],
        out_specs=pl.BlockSpec((1, blk, width), lambda b, i: (b, i, 0)),
        scratch_shapes=[pltpu.VMEM((heads, blk, LANES), BF16), pltpu.VMEM((blk, width), F32),
                        pltpu.VMEM((heads, blk, 1), F32)],
        compiler_params=_cparams("arbitrary", "arbitrary"),
        name="attn",
    )(q, k, v, u_tri)


def _sigmoid(x):
    return 1.0 / (1.0 + jnp.exp(-x))


def _mix_kernel(ya_ref, uv_ref, gt_ref, x_ref, gtm_ref, shf_ref, scf_ref, ws_ref, bs_ref,
                wa_ref, wb_ref, wo_ref, gffn_ref, wr_ref, br_ref, utri_ref,
                h_ref, xn_ref, route_ref, route_t_ref, cnt_ref, yb_scr, cnt_scr,
                *, chunk, groups, n_groups, per_group):
    step = pl.program_id(0)
    tm = x_ref.shape[0]
    sgu = uv_ref.shape[1] // 2
    gdim = sgu // groups

    @pl.when(step == 0)
    def _():
        cnt_scr[...] = jnp.zeros_like(cnt_scr)

    ya_proj = _dot(ya_ref[...], wa_ref[...])

    u = uv_ref[:, :sgu].astype(F32)
    vn = uv_ref[:, sgu:]
    r = lax.broadcasted_iota(jnp.int32, (chunk, chunk), 0)
    c = lax.broadcasted_iota(jnp.int32, (chunk, chunk), 1)
    wcs = [jnp.where(r >= c, ws_ref[g], 0.0).astype(BF16) for g in range(groups)]
    pair = LANES // gdim
    lane = lax.broadcasted_iota(jnp.int32, (chunk, LANES), 1)
    for ci in range(tm // chunk):
        rows = slice(ci * chunk, (ci + 1) * chunk)
        for p in range(groups // pair):
            lanes = slice(p * LANES, (p + 1) * LANES)
            slab = vn[rows, lanes]
            mix = bs_ref[:, lanes]
            for j in range(pair):
                sel = (lane >= j * gdim) & (lane < (j + 1) * gdim)
                mix = mix + _dot(wcs[p * pair + j], jnp.where(sel, slab, 0.0).astype(BF16))
            yb_scr[rows, lanes] = (u[rows, lanes] * mix).astype(BF16)

    d = x_ref.shape[1]
    gates2 = gt_ref[...].astype(F32)
    merged2 = gates2[:, :d] * ya_proj + gates2[:, d:] * _dot(yb_scr[...], wb_ref[...])
    h = x_ref[...] + (0.5 * gtm_ref[0]) * _dot(merged2.astype(BF16), wo_ref[...])
    h_ref[...] = h

    xn = _rms(h, gffn_ref[...]) * (1.0 + scf_ref[0]) + shf_ref[0]
    xn_ref[...] = _pack_halves(xn)
    logits = _dot3(xn, wr_ref[...]) + br_ref[...]

    lt = logits.T
    sub = lax.broadcasted_iota(jnp.int32, (SUBLANES, tm), 0)
    neg = -jnp.inf
    gl = jnp.where(sub < n_groups, lt[0:SUBLANES], neg)
    gmax = jnp.max(gl, axis=0, keepdims=True)
    g_sel = jnp.min(jnp.where(gl == gmax, sub, SUBLANES), axis=0, keepdims=True)
    g_w = 1.0 / jnp.sum(jnp.exp(gl - gmax), axis=0, keepdims=True)
    el = lt[SUBLANES:2 * SUBLANES]
    for g in range(1, n_groups):
        el = jnp.where(g_sel == g, lt[(g + 1) * SUBLANES:(g + 2) * SUBLANES], el)
    m1 = jnp.max(el, axis=0, keepdims=True)
    j1 = jnp.min(jnp.where(el == m1, sub, SUBLANES), axis=0, keepdims=True)
    el2 = jnp.where(sub == j1, neg, el)
    m2 = jnp.max(el2, axis=0, keepdims=True)
    j2 = jnp.min(jnp.where(el2 == m2, sub, SUBLANES), axis=0, keepdims=True)
    t = jnp.exp(m2 - m1)
    w1 = g_w / (1.0 + t)
    w2 = g_w * t / (1.0 + t)
    e1 = g_sel * per_group + j1
    e2 = g_sel * per_group + j2

    row = lax.broadcasted_iota(jnp.int32, (LANES, tm), 0)
    oh1 = row == e1 + SUBLANES
    oh2 = row == e2 + SUBLANES
    both = jnp.where(oh1 | oh2, 1.0, 0.0)
    before = _dot(both.astype(BF16), utri_ref[...]) + cnt_scr[...]
    rank1 = jnp.sum(jnp.where(oh1, before, 0.0), axis=0, keepdims=True)
    rank2 = jnp.sum(jnp.where(oh2, before, 0.0), axis=0, keepdims=True)
    cnt_scr[...] = cnt_scr[...] + jnp.sum(both, axis=1, keepdims=True)
    cnt_ref[...] = cnt_scr[...]

    rec = jnp.zeros((LANES, tm), F32)
    for idx, val in enumerate((e1.astype(F32), e2.astype(F32), w1, w2, rank1, rank2)):
        rec = jnp.where(row == idx, val, rec)
    route_t_ref[...] = rec[:ROUTE_COLS]
    route_ref[...] = rec.T[:, :ROUTE_COLS]


def _mix_call(ya, uv, gates, x2, gtm, shf, scf, w_spatial, bias_full, wa, wb, wo, g_ffn, wr, br,
              seq, n_groups, per_group):
    n_tok, d = x2.shape
    tm = min(TOKEN_TILE, seq)
    per_b = seq // tm
    groups, chunk, _ = w_spatial.shape
    sgu = uv.shape[1] // 2
    r = lax.broadcasted_iota(jnp.int32, (tm, tm), 0)
    c = lax.broadcasted_iota(jnp.int32, (tm, tm), 1)
    utri = (r < c).astype(BF16)
    tok = lambda w: pl.BlockSpec((tm, w), lambda i: (i, 0))
    vec = pl.BlockSpec((1, 1, d), lambda i: (i // per_b, 0, 0))
    const = lambda a: pl.BlockSpec(a.shape, lambda i: (0,) * a.ndim)
    ins = [ya, uv, gates, x2, gtm, shf, scf, w_spatial, bias_full, wa, wb, wo, g_ffn, wr, br, utri]
    in_specs = [tok(ya.shape[1]), tok(uv.shape[1]), tok(gates.shape[1]), tok(d), vec, vec, vec]
    in_specs += [const(a) for a in ins[7:]]
    return pl.pallas_call(
        functools.partial(_mix_kernel, chunk=chunk, groups=groups, n_groups=n_groups, per_group=per_group),
        out_shape=[jax.ShapeDtypeStruct((n_tok, d), F32), jax.ShapeDtypeStruct((n_tok, d // 2), jnp.uint32),
                   jax.ShapeDtypeStruct((n_tok, ROUTE_COLS), F32), jax.ShapeDtypeStruct((ROUTE_COLS, n_tok), F32),
                   jax.ShapeDtypeStruct((LANES, 1), F32)],
        grid=(n_tok // tm,),
        in_specs=in_specs,
        out_specs=[tok(d), tok(d // 2), tok(ROUTE_COLS), pl.BlockSpec((ROUTE_COLS, tm), lambda i: (0, i)),
                   pl.BlockSpec((LANES, 1), lambda i: (0, 0))],
        scratch_shapes=[pltpu.VMEM((tm, sgu), BF16), pltpu.VMEM((LANES, 1), F32)],
        compiler_params=_cparams("arbitrary"),
        name="mix",
    )(*ins)


def _row_copy(src, s, dst, d, sem):
    return pltpu.make_async_copy(src.at[pl.ds(s, 1)], dst.at[pl.ds(d, 1)], sem)


def _dispatch_kernel(zblk_ref, *refs, tm, rows):
    dest_refs, (xn_ref, xs_ref, zero_scr, src_scr, sem, zsem) = refs[:TOP_K], refs[TOP_K:]
    step = pl.program_id(0)

    def wait_rows():
        for _ in range(TOP_K):
            pltpu.make_async_copy(src_scr, xs_ref.at[pl.ds(0, tm)], sem).wait()

    @pl.when(step == 0)
    def _():
        zero_scr[...] = jnp.zeros_like(zero_scr)

        def zero_copy(j):
            start = pl.multiple_of(zblk_ref[j] * rows, rows)
            return pltpu.make_async_copy(zero_scr, xs_ref.at[pl.ds(start, rows)], zsem)

        for j in range(zblk_ref.shape[0]):
            pl.when(zblk_ref[j] >= 0)(lambda j=j: zero_copy(j).start())
        for j in range(zblk_ref.shape[0]):
            pl.when(zblk_ref[j] >= 0)(lambda j=j: zero_copy(j).wait())

    pl.when(step > 0)(wait_rows)
    src_scr[...] = xn_ref[...]
    for j in range(tm):
        for k, dref in enumerate(dest_refs):
            _row_copy(src_scr, j, xs_ref, dref[0, 0, j], sem).start(priority=k % 2)
    pl.when(step == pl.num_programs(0) - 1)(wait_rows)


def _dest_blocks(dest, tm):
    return [dest[k].reshape(-1, 1, tm) for k in range(TOP_K)]


def _dispatch_call(zero_blocks, dest, xn, m_pad, seq):
    n_tok, dw = xn.shape
    tm = min(ROW_COPY_TILE, seq)
    smem = pl.BlockSpec((1, 1, tm), lambda i, zb: (i, 0, 0), memory_space=pltpu.SMEM)
    return pl.pallas_call(
        functools.partial(_dispatch_kernel, tm=tm, rows=EXPERT_ROWS),
        out_shape=jax.ShapeDtypeStruct((m_pad, dw), xn.dtype),
        grid_spec=pltpu.PrefetchScalarGridSpec(
            num_scalar_prefetch=1,
            grid=(n_tok // tm,),
            in_specs=[smem] * TOP_K + [pl.BlockSpec((tm, dw), lambda i, zb: (i, 0))],
            out_specs=pl.BlockSpec(memory_space=pl.ANY),
            scratch_shapes=[pltpu.VMEM((EXPERT_ROWS, dw), xn.dtype), pltpu.VMEM((tm, dw), xn.dtype),
                            pltpu.SemaphoreType.DMA(()), pltpu.SemaphoreType.DMA(())]),
        compiler_params=_cparams("arbitrary"),
        name="dispatch",
    )(zero_blocks, *_dest_blocks(dest, tm), xn)


def _expert_kernel(blk_e_ref, used_ref, xs_ref, wg_ref, wu_ref, wd_ref, y_ref, wg_bf, wu_bf, wd_bf):
    i = pl.program_id(0)
    live = i < used_ref[0]
    new_expert = (i == 0) | (blk_e_ref[i] != blk_e_ref[jnp.maximum(i - 1, 0)])

    @pl.when(live & new_expert)
    def _():
        wg_bf[...] = wg_ref[0].astype(BF16)
        wu_bf[...] = wu_ref[0].astype(BF16)
        wd_bf[...] = wd_ref[0].astype(BF16)

    @pl.when(live)
    def _():
        xb = _unpack_halves(xs_ref[...]).astype(BF16)
        g = _dot(xb, wg_bf[...])
        hid = g * _sigmoid(g) * _dot(xb, wu_bf[...])
        y_ref[...] = _pack_halves(_dot(hid.astype(BF16), wd_bf[...]))

    @pl.when(jnp.logical_not(live))
    def _():
        y_ref[...] = jnp.zeros_like(y_ref)


def _expert_call(blk_e, used, xs, wg, wu, wd):
    m_pad, dw = xs.shape
    _, d, f = wg.shape
    rows = EXPERT_ROWS
    return pl.pallas_call(
        _expert_kernel,
        out_shape=jax.ShapeDtypeStruct((m_pad, dw), xs.dtype),
        grid_spec=pltpu.PrefetchScalarGridSpec(
            num_scalar_prefetch=2,
            grid=(m_pad // rows,),
            in_specs=[pl.BlockSpec((rows, dw), lambda i, be, us: (i, 0)),
                      pl.BlockSpec((1, d, f), lambda i, be, us: (be[i], 0, 0)),
                      pl.BlockSpec((1, d, f), lambda i, be, us: (be[i], 0, 0)),
                      pl.BlockSpec((1, f, d), lambda i, be, us: (be[i], 0, 0))],
            out_specs=pl.BlockSpec((rows, dw), lambda i, be, us: (i, 0)),
            scratch_shapes=[pltpu.VMEM((d, f), BF16), pltpu.VMEM((d, f), BF16), pltpu.VMEM((f, d), BF16)]),
        compiler_params=_cparams("arbitrary"),
        name="experts",
    )(blk_e, used, xs, wg, wu, wd)


def _combine_kernel(*refs, tm, final):
    cur_refs, nxt_refs = refs[:TOP_K], refs[TOP_K:2 * TOP_K]
    h_ref, route_ref, gtf_ref, gfin_ref, y_ref, o_ref, land, rows, sem = refs[2 * TOP_K:]
    step = pl.program_id(0)

    def gather(dest_refs):
        for j in range(tm):
            for k, dref in enumerate(dest_refs):
                pltpu.make_async_copy(y_ref.at[pl.ds(dref[0, 0, j], 1)], land.at[k, pl.ds(j, 1)],
                                      sem).start(priority=k % 2)

    def wait_rows():
        for k in range(TOP_K):
            pltpu.make_async_copy(y_ref.at[pl.ds(0, tm)], land.at[k], sem).wait()

    pl.when(step == 0)(lambda: gather(cur_refs))
    wait_rows()
    rows[...] = land[...]
    gather(nxt_refs)
    route = route_ref[...]
    y = route[:, 2:3] * _unpack_halves(rows[0]) + route[:, 3:4] * _unpack_halves(rows[1])
    h = h_ref[...] + gtf_ref[0] * y
    o_ref[...] = _rms(h, gfin_ref[...]) if final else h
    pl.when(step == pl.num_programs(0) - 1)(wait_rows)


def _combine_call(dest, h, route, gtf, g_final, yb, seq, final):
    n_tok, d = h.shape
    tm = min(ROW_COPY_TILE, seq)
    per_b = seq // tm
    steps = n_tok // tm
    tok = lambda w: pl.BlockSpec((tm, w), lambda i: (i, 0))
    cur = pl.BlockSpec((1, 1, tm), lambda i: (i, 0, 0), memory_space=pltpu.SMEM)
    nxt = pl.BlockSpec((1, 1, tm), lambda i: (jnp.minimum(i + 1, steps - 1), 0, 0), memory_space=pltpu.SMEM)
    blocks = _dest_blocks(dest, tm)
    return pl.pallas_call(
        functools.partial(_combine_kernel, tm=tm, final=final),
        out_shape=jax.ShapeDtypeStruct((n_tok, d), F32),
        grid=(steps,),
        in_specs=[cur] * TOP_K + [nxt] * TOP_K + [
            tok(d), tok(ROUTE_COLS),
            pl.BlockSpec((1, 1, d), lambda i: (i // per_b, 0, 0)),
            pl.BlockSpec((1, d), lambda i: (0, 0)),
            pl.BlockSpec(memory_space=pl.ANY)],
        out_specs=tok(d),
        scratch_shapes=[pltpu.VMEM((TOP_K, tm, yb.shape[1]), yb.dtype),
                        pltpu.VMEM((TOP_K, tm, yb.shape[1]), yb.dtype), pltpu.SemaphoreType.DMA(())],
        compiler_params=_cparams("arbitrary"),
        name="combine",
    )(*blocks, *blocks, h, route, gtf, g_final, yb)


def _routing_plan(route_t, counts, n_experts, m_pad):
    eid = route_t[0:TOP_K].astype(jnp.int32)
    rank = route_t[4:4 + TOP_K].astype(jnp.int32)
    cnt = counts.astype(jnp.int32)
    padded = (cnt + EXPERT_ROWS - 1) // EXPERT_ROWS * EXPERT_ROWS
    pend = jnp.cumsum(padded)
    pstart = pend - padded
    dest = rank
    for e in range(n_experts):
        dest = dest + jnp.where(eid == e, pstart[e], 0)
    nblk = m_pad // EXPERT_ROWS
    blk_row = jnp.arange(nblk, dtype=jnp.int32) * EXPERT_ROWS
    blk_e = jnp.minimum(jnp.sum(pend[None, :] <= blk_row[:, None], axis=1), n_experts - 1).astype(jnp.int32)
    used = (pend[-1:] // EXPERT_ROWS).astype(jnp.int32)
    last_blk = jnp.where(padded > 0, pend // EXPERT_ROWS - 1, -1)
    tail_blk = used + jnp.arange(n_experts, dtype=jnp.int32)
    tail_blk = jnp.where(tail_blk < nblk, tail_blk, -1)
    zero_blocks = jnp.concatenate([last_blk, tail_blk]).astype(jnp.int32)
    return dest, blk_e, used, zero_blocks


def kernel(x, c, g_mix, g_ffn, w_ada, b_ada, w_in, w_sba_out, g_sgu, w_spatial, b_spatial, w_sgu_out, w_out,
           w_router_group, b_router_group, w_router_expert, b_router_expert, w_expert_gate, w_expert_up,
           w_expert_down, g_final):
    bsz, seq, d = x.shape
    depth = w_in.shape[0]
    sba = w_sba_out.shape[1]
    sgu = g_sgu.shape[1]
    heads = sba // HEAD_DIM
    groups, chunk = w_spatial.shape[1], w_spatial.shape[2]
    n_groups = w_router_group.shape[2]
    n_experts = w_router_expert.shape[2]
    per_group = n_experts // n_groups
    n_tok = bsz * seq
    m_pad = n_tok * TOP_K + n_experts * EXPERT_ROWS
    widths = (sba, sba, sba, 2 * sgu, 2 * d)
    assert seq % chunk == 0 and LANES % (sgu // groups) == 0
    assert d % (2 * LANES) == 0
    assert per_group == SUBLANES and n_groups <= SUBLANES and (n_groups + 1) * SUBLANES <= LANES

    h = x.reshape(n_tok, d)
    for l in range(depth):
        mod = _mod_call(c, w_ada[l], b_ada[l])
        sh_m, sc_m, gt_m, sh_f, sc_f, gt_f = [mod[:, i * d:(i + 1) * d].reshape(bsz, 1, d) for i in range(6)]

        q, k, v, uv, gates = _proj_call(h, sh_m, sc_m, g_mix[l].reshape(1, d), g_sgu[l].reshape(1, sgu),
                                        w_in[l].astype(BF16), seq, widths)
        ya = _attn_call(q.reshape(bsz, seq, sba), k.reshape(bsz, seq, sba), v.reshape(bsz, seq, sba), heads)

        bias_full = jnp.repeat(b_spatial[l].T, sgu // groups, axis=1)
        gpad = SUBLANES - n_groups
        wr = jnp.concatenate([jnp.pad(w_router_group[l], ((0, 0), (0, gpad))), w_router_expert[l]], axis=1)
        wr = jnp.pad(wr, ((0, 0), (0, LANES - wr.shape[1])))
        br = jnp.concatenate([jnp.pad(b_router_group[l], (0, gpad)), b_router_expert[l]])
        br = jnp.pad(br, (0, LANES - br.shape[0])).reshape(1, LANES)
        h1, xn, route, route_t, counts = _mix_call(
            ya.reshape(n_tok, sba), uv, gates, h, gt_m, sh_f, sc_f, w_spatial[l],
            bias_full, w_sba_out[l].astype(BF16), w_sgu_out[l].astype(BF16), w_out[l].astype(BF16),
            g_ffn[l].reshape(1, d), wr, br, seq, n_groups, per_group)

        dest, blk_e, used, zero_blocks = _routing_plan(route_t, counts[SUBLANES:SUBLANES + n_experts, 0],
                                                       n_experts, m_pad)
        xs = _dispatch_call(zero_blocks, dest, xn, m_pad, seq)
        yb = _expert_call(blk_e, used, xs, w_expert_gate[l], w_expert_up[l], w_expert_down[l])
        h = _combine_call(dest, h1, route, gt_f, g_final.reshape(1, d), yb, seq, final=l == depth - 1)
    return h.reshape(bsz, seq, d)
```

```python
import functools

import jax
import jax.numpy as jnp
from jax import lax
from jax.experimental import pallas as pl
from jax.experimental.pallas import tpu as pltpu

F32 = jnp.float32
BF16 = jnp.bfloat16

HEAD_DIM = 64
TOP_K = 2
EPS = 1e-6
LOG2E = 1.4426950408889634
SIGN_BIT = 0x80000000
BF16_BITS = 0xFFFF0000
ATTN_EXIT_LOG2 = 160.0
DEAD_CARRY = 1e30
SUFFIX_LAG = 2
WEIGHTS_LAG = 4
LANES = 128
SUBLANES = 8
ATTN_BLOCK = 256
EXPERT_ROWS = 512
TOKEN_TILE = 1024
ROW_COPY_TILE = 512
PROJ_TILE = 512
ROUTE_COLS = 8
VMEM_LIMIT = 56 * 1024 * 1024


def _cparams(*sem):
    return pltpu.CompilerParams(dimension_semantics=sem, vmem_limit_bytes=VMEM_LIMIT)


def _split_bf16(a):
    hi = a.astype(BF16)
    lo = (a - hi.astype(F32)).astype(BF16)
    return hi, lo


def _dot(a, b):
    return jnp.dot(a, b, preferred_element_type=F32)


def _dot3(a, b):
    ah, al = _split_bf16(a)
    bh, bl = _split_bf16(b)
    return _dot(ah, bh) + _dot(ah, bl) + _dot(al, bh)


def _pack_halves(x):
    half = x.shape[1] // 2
    hi = lax.bitcast_convert_type(x[:, :half].astype(BF16).astype(F32), jnp.uint32)
    lo = lax.bitcast_convert_type(x[:, half:].astype(BF16).astype(F32), jnp.uint32)
    return hi | (lo >> 16)


def _unpack_halves(u):
    hi = lax.bitcast_convert_type(u & jnp.uint32(BF16_BITS), F32)
    lo = lax.bitcast_convert_type(u << 16, F32)
    return jnp.concatenate([hi, lo], axis=1)


def _rms(x, g):
    ms = jnp.mean(x * x, axis=-1, keepdims=True)
    return x * lax.rsqrt(ms + EPS) * g


def _mod_kernel(c_ref, w_ref, b_ref, o_ref):
    c = c_ref[...]
    ca = c * (1.0 / (1.0 + jnp.exp(-c)))
    o_ref[...] = _dot3(ca, w_ref[...]) + b_ref[...]


def _mod_call(c, w_ada, b_ada):
    bsz, d = c.shape
    n = w_ada.shape[1]
    tn = n // 4 if n % (4 * LANES) == 0 else n
    return pl.pallas_call(
        _mod_kernel,
        out_shape=jax.ShapeDtypeStruct((bsz, n), F32),
        grid=(n // tn,),
        in_specs=[pl.BlockSpec((bsz, d), lambda j: (0, 0)),
                  pl.BlockSpec((d, tn), lambda j: (0, j)),
                  pl.BlockSpec((1, tn), lambda j: (0, j))],
        out_specs=pl.BlockSpec((bsz, tn), lambda j: (0, j)),
        compiler_params=_cparams("arbitrary"),
        name="mod",
    )(c, w_ada, b_ada.reshape(1, n))


def _gelu_tanh(x):
    c = 0.7978845608028654
    hx = 0.5 * x
    return hx + hx * jnp.tanh(x * (c + (0.044715 * c) * (x * x)))


def _proj_kernel(x_ref, sh_ref, sc_ref, g_ref, gsgu_ref, w_ref, q_ref, k_ref, v_ref, uv_ref, gt_ref, *, widths):
    x = x_ref[...]
    n = _rms(x, g_ref[...]) * (1.0 + sc_ref[0]) + sh_ref[0]
    nb = n.astype(BF16)
    off = 0
    for ref, wd in zip((q_ref, k_ref, v_ref, uv_ref, gt_ref), widths):
        p = _dot(nb, w_ref[:, off:off + wd])
        if ref is q_ref:
            p = p * (HEAD_DIM ** -0.5 * LOG2E)
        elif ref is uv_ref:
            act = _gelu_tanh(p)
            sgu = wd // 2
            v = act[:, sgu:]
            vc = v - jnp.mean(v, axis=-1, keepdims=True)
            var = jnp.mean(vc * vc, axis=-1, keepdims=True)
            p = jnp.concatenate([act[:, :sgu], vc * lax.rsqrt(var + EPS) * gsgu_ref[...]], axis=1)
        elif ref is gt_ref:
            p = 1.0 + jnp.tanh(0.5 * p)
        ref[...] = p.astype(ref.dtype)
        off += wd


def _proj_call(x2, sh, sc, g, g_sgu, w_in_bf, seq, widths):
    n_tok, d = x2.shape
    tm = min(PROJ_TILE, seq)
    per_b = seq // tm
    cols = w_in_bf.shape[1]
    tok = lambda w: pl.BlockSpec((tm, w), lambda i: (i, 0))
    vec = pl.BlockSpec((1, 1, d), lambda i: (i // per_b, 0, 0))
    return pl.pallas_call(
        functools.partial(_proj_kernel, widths=widths),
        out_shape=[jax.ShapeDtypeStruct((n_tok, w), BF16) for w in widths],
        grid=(n_tok // tm,),
        in_specs=[tok(d), vec, vec,
                  pl.BlockSpec((1, d), lambda i: (0, 0)),
                  pl.BlockSpec(g_sgu.shape, lambda i: (0, 0)),
                  pl.BlockSpec((d, cols), lambda i: (0, 0))],
        out_specs=[tok(w) for w in widths],
        compiler_params=_cparams("arbitrary"),
        name="proj",
    )(x2, sh, sc, g, g_sgu, w_in_bf)


def _attn_kernel(q_ref, k_ref, v_ref, u_ref, o_ref, qm_ref, acc_ref, carry_ref, *, heads, blk):
    qi = pl.program_id(1)
    u_tri = u_ref[...]
    row = lax.broadcasted_iota(jnp.int32, (blk, blk), 0)
    col = lax.broadcasted_iota(jnp.int32, (blk, blk), 1)
    causal = col < row
    per_slab = LANES // HEAD_DIM
    lane = lax.broadcasted_iota(jnp.int32, (blk, LANES), 1)
    own = [(lane >= j * HEAD_DIM) & (lane < (j + 1) * HEAD_DIM) for j in range(per_slab)]
    for h in range(heads):
        slab = slice(h // per_slab * LANES, (h // per_slab + 1) * LANES)
        qm_ref[h] = jnp.where(own[h % per_slab], q_ref[0, :, slab], 0.0).astype(BF16)

    slabs = [slice(p * LANES, (p + 1) * LANES) for p in range(heads // per_slab)]

    def make_tile(start, first, dead=None):
        mask = causal if first else None
        s_all, cum_all, w_all, carries = {}, {}, {}, []

        def scores(h):
            s_all[h] = lax.dot_general(qm_ref[h], k_ref[0, pl.ds(start, blk), slabs[h // per_slab]],
                                       (((1,), (1,)), ((), ())), preferred_element_type=F32)

        def suffix_sums(h):
            s = s_all[h]
            neg_abs = lax.bitcast_convert_type(lax.bitcast_convert_type(s, jnp.uint32) | jnp.uint32(SIGN_BIT), F32)
            sp = jnp.maximum(s, 0.0) + jnp.log(1.0 + jnp.exp2(neg_abs)) * LOG2E
            if mask is not None:
                sp = jnp.where(mask, sp, 0.0)
            hi = lax.bitcast_convert_type(lax.bitcast_convert_type(sp, jnp.uint32) & jnp.uint32(BF16_BITS), F32)
            cum_all[h] = _dot(hi.astype(BF16), u_tri) + _dot((sp - hi).astype(BF16), u_tri)

        def weights(h):
            cum = cum_all[h]
            if first:
                w = jnp.exp2(s_all[h] - cum)
                carry = cum[:, 0:1]
            else:
                carry = carry_ref[h]
                if dead is not None:
                    carry = jnp.where(dead, DEAD_CARRY, carry)
                w = jnp.exp2((s_all[h] - carry) - cum)
                carry = carry + cum[:, 0:1]
            w_all[h] = (w if mask is None else jnp.where(mask, w, 0.0)).astype(BF16)
            carry_ref[h] = carry
            carries.append(carry)
            if h % per_slab == per_slab - 1:
                p = h // per_slab
                vp = v_ref[0, pl.ds(start, blk), slabs[p]]
                upd = None
                for j in range(per_slab):
                    pv = _dot(w_all[p * per_slab + j], vp)
                    upd = pv if upd is None else jnp.where(own[j], pv, upd)
                acc_ref[:, slabs[p]] = upd if first else acc_ref[:, slabs[p]] + upd

        return scores, suffix_sums, weights, carries

    def run_tiles(tiles):
        jobs = [(tile, h) for tile in tiles for h in range(heads)]
        for r in range(len(jobs) + WEIGHTS_LAG):
            if r < len(jobs):
                tile, h = jobs[r]
                tile[0](h)
            if SUFFIX_LAG <= r < len(jobs) + SUFFIX_LAG:
                tile, h = jobs[r - SUFFIX_LAG]
                tile[1](h)
            if r >= WEIGHTS_LAG:
                tile, h = jobs[r - WEIGHTS_LAG]
                tile[2](h)
        return jnp.min(functools.reduce(jnp.minimum, tiles[-1][3]))

    cmin = run_tiles([make_tile(pl.multiple_of(qi * blk, blk), True),
                      make_tile(pl.multiple_of(jnp.maximum(qi - 1, 0) * blk, blk), False, dead=qi == 0)])

    def cond(state):
        i, cmin = state
        return (i < qi) & (cmin < ATTN_EXIT_LOG2)

    def body(state):
        i, _ = state
        return i + 1, run_tiles([make_tile(pl.multiple_of((qi - 1 - i) * blk, blk), False)])

    lax.while_loop(cond, body, (jnp.int32(1), cmin))
    o_ref[0] = acc_ref[...].astype(o_ref.dtype)


def _attn_call(q, k, v, heads):
    bsz, seq, width = q.shape
    blk = min(ATTN_BLOCK, seq)
    r = lax.broadcasted_iota(jnp.int32, (blk, blk), 0)
    c = lax.broadcasted_iota(jnp.int32, (blk, blk), 1)
    u_tri = (r >= c).astype(BF16)
    full = pl.BlockSpec((1, seq, width), lambda b, i: (b, 0, 0))
    return pl.pallas_call(
        functools.partial(_attn_kernel, heads=heads, blk=blk),
        out_shape=jax.ShapeDtypeStruct((bsz, seq, width), BF16),
        grid=(bsz, seq // blk),
        in_specs=[pl.BlockSpec((1, blk, width), lambda b, i: (b, i, 0)), full, full,
                  pl.BlockSpec((blk, blk), lambda b, i: (0, 0))],
        out_specs=pl.BlockSpec((1, blk, width), lambda b, i: (b, i, 0)),
        scratch_shapes=[pltpu.VMEM((heads, blk, LANES), BF16), pltpu.VMEM((blk, width), F32),
                        pltpu.VMEM((heads, blk, 1), F32)],
        compiler_params=_cparams("arbitrary", "arbitrary"),
        name="attn",
    )(q, k, v, u_tri)


def _sigmoid(x):
    return 1.0 / (1.0 + jnp.exp(-x))


def _mix_kernel(ya_ref, uv_ref, gt_ref, x_ref, gtm_ref, shf_ref, scf_ref, ws_ref, bs_ref,
                wa_ref, wb_ref, wo_ref, gffn_ref, wr_ref, br_ref, utri_ref,
                h_ref, xn_ref, route_ref, route_t_ref, cnt_ref, yb_scr, cnt_scr,
                *, chunk, groups, n_groups, per_group):
    step = pl.program_id(0)
    tm = x_ref.shape[0]
    sgu = uv_ref.shape[1] // 2
    gdim = sgu // groups

    @pl.when(step == 0)
    def _():
        cnt_scr[...] = jnp.zeros_like(cnt_scr)

    ya_proj = _dot(ya_ref[...], wa_ref[...])

    u = uv_ref[:, :sgu].astype(F32)
    vn = uv_ref[:, sgu:]
    r = lax.broadcasted_iota(jnp.int32, (chunk, chunk), 0)
    c = lax.broadcasted_iota(jnp.int32, (chunk, chunk), 1)
    wcs = [jnp.where(r >= c, ws_ref[g], 0.0).astype(BF16) for g in range(groups)]
    pair = LANES // gdim
    lane = lax.broadcasted_iota(jnp.int32, (chunk, LANES), 1)
    for ci in range(tm // chunk):
        rows = slice(ci * chunk, (ci + 1) * chunk)
        for p in range(groups // pair):
            lanes = slice(p * LANES, (p + 1) * LANES)
            slab = vn[rows, lanes]
            mix = bs_ref[:, lanes]
            for j in range(pair):
                sel = (lane >= j * gdim) & (lane < (j + 1) * gdim)
                mix = mix + _dot(wcs[p * pair + j], jnp.where(sel, slab, 0.0).astype(BF16))
            yb_scr[rows, lanes] = (u[rows, lanes] * mix).astype(BF16)

    d = x_ref.shape[1]
    gates2 = gt_ref[...].astype(F32)
    merged2 = gates2[:, :d] * ya_proj + gates2[:, d:] * _dot(yb_scr[...], wb_ref[...])
    h = x_ref[...] + (0.5 * gtm_ref[0]) * _dot(merged2.astype(BF16), wo_ref[...])
    h_ref[...] = h

    xn = _rms(h, gffn_ref[...]) * (1.0 + scf_ref[0]) + shf_ref[0]
    xn_ref[...] = _pack_halves(xn)
    logits = _dot3(xn, wr_ref[...]) + br_ref[...]

    lt = logits.T
    sub = lax.broadcasted_iota(jnp.int32, (SUBLANES, tm), 0)
    neg = -jnp.inf
    gl = jnp.where(sub < n_groups, lt[0:SUBLANES], neg)
    gmax = jnp.max(gl, axis=0, keepdims=True)
    g_sel = jnp.min(jnp.where(gl == gmax, sub, SUBLANES), axis=0, keepdims=True)
    g_w = 1.0 / jnp.sum(jnp.exp(gl - gmax), axis=0, keepdims=True)
    el = lt[SUBLANES:2 * SUBLANES]
    for g in range(1, n_groups):
        el = jnp.where(g_sel == g, lt[(g + 1) * SUBLANES:(g + 2) * SUBLANES], el)
    m1 = jnp.max(el, axis=0, keepdims=True)
    j1 = jnp.min(jnp.where(el == m1, sub, SUBLANES), axis=0, keepdims=True)
    el2 = jnp.where(sub == j1, neg, el)
    m2 = jnp.max(el2, axis=0, keepdims=True)
    j2 = jnp.min(jnp.where(el2 == m2, sub, SUBLANES), axis=0, keepdims=True)
    t = jnp.exp(m2 - m1)
    w1 = g_w / (1.0 + t)
    w2 = g_w * t / (1.0 + t)
    e1 = g_sel * per_group + j1
    e2 = g_sel * per_group + j2

    row = lax.broadcasted_iota(jnp.int32, (LANES, tm), 0)
    oh1 = row == e1 + SUBLANES
    oh2 = row == e2 + SUBLANES
    both = jnp.where(oh1 | oh2, 1.0, 0.0)
    before = _dot(both.astype(BF16), utri_ref[...]) + cnt_scr[...]
    rank1 = jnp.sum(jnp.where(oh1, before, 0.0), axis=0, keepdims=True)
    rank2 = jnp.sum(jnp.where(oh2, before, 0.0), axis=0, keepdims=True)
    cnt_scr[...] = cnt_scr[...] + jnp.sum(both, axis=1, keepdims=True)
    cnt_ref[...] = cnt_scr[...]

    rec = jnp.zeros((LANES, tm), F32)
    for idx, val in enumerate((e1.astype(F32), e2.astype(F32), w1, w2, rank1, rank2)):
        rec = jnp.where(row == idx, val, rec)
    route_t_ref[...] = rec[:ROUTE_COLS]
    route_ref[...] = rec.T[:, :ROUTE_COLS]


def _mix_call(ya, uv, gates, x2, gtm, shf, scf, w_spatial, bias_full, wa, wb, wo, g_ffn, wr, br,
              seq, n_groups, per_group):
    n_tok, d = x2.shape
    tm = min(TOKEN_TILE, seq)
    per_b = seq // tm
    groups, chunk, _ = w_spatial.shape
    sgu = uv.shape[1] // 2
    r = lax.broadcasted_iota(jnp.int32, (tm, tm), 0)
    c = lax.broadcasted_iota(jnp.int32, (tm, tm), 1)
    utri = (r < c).astype(BF16)
    tok = lambda w: pl.BlockSpec((tm, w), lambda i: (i, 0))
    vec = pl.BlockSpec((1, 1, d), lambda i: (i // per_b, 0, 0))
    const = lambda a: pl.BlockSpec(a.shape, lambda i: (0,) * a.ndim)
    ins = [ya, uv, gates, x2, gtm, shf, scf, w_spatial, bias_full, wa, wb, wo, g_ffn, wr, br, utri]
    in_specs = [tok(ya.shape[1]), tok(uv.shape[1]), tok(gates.shape[1]), tok(d), vec, vec, vec]
    in_specs += [const(a) for a in ins[7:]]
    return pl.pallas_call(
        functools.partial(_mix_kernel, chunk=chunk, groups=groups, n_groups=n_groups, per_group=per_group),
        out_shape=[jax.ShapeDtypeStruct((n_tok, d), F32), jax.ShapeDtypeStruct((n_tok, d // 2), jnp.uint32),
                   jax.ShapeDtypeStruct((n_tok, ROUTE_COLS), F32), jax.ShapeDtypeStruct((ROUTE_COLS, n_tok), F32),
                   jax.ShapeDtypeStruct((LANES, 1), F32)],
        grid=(n_tok // tm,),
        in_specs=in_specs,
        out_specs=[tok(d), tok(d // 2), tok(ROUTE_COLS), pl.BlockSpec((ROUTE_COLS, tm), lambda i: (0, i)),
                   pl.BlockSpec((LANES, 1), lambda i: (0, 0))],
        scratch_shapes=[pltpu.VMEM((tm, sgu), BF16), pltpu.VMEM((LANES, 1), F32)],
        compiler_params=_cparams("arbitrary"),
        name="mix",
    )(*ins)


def _row_copy(src, s, dst, d, sem):
    return pltpu.make_async_copy(src.at[pl.ds(s, 1)], dst.at[pl.ds(d, 1)], sem)


def _dispatch_kernel(zblk_ref, *refs, tm, rows):
    dest_refs, (xn_ref, xs_ref, zero_scr, src_scr, sem, zsem) = refs[:TOP_K], refs[TOP_K:]
    step = pl.program_id(0)

    def wait_rows():
        for _ in range(TOP_K):
            pltpu.make_async_copy(src_scr, xs_ref.at[pl.ds(0, tm)], sem).wait()

    @pl.when(step == 0)
    def _():
        zero_scr[...] = jnp.zeros_like(zero_scr)

        def zero_copy(j):
            start = pl.multiple_of(zblk_ref[j] * rows, rows)
            return pltpu.make_async_copy(zero_scr, xs_ref.at[pl.ds(start, rows)], zsem)

        for j in range(zblk_ref.shape[0]):
            pl.when(zblk_ref[j] >= 0)(lambda j=j: zero_copy(j).start())
        for j in range(zblk_ref.shape[0]):
            pl.when(zblk_ref[j] >= 0)(lambda j=j: zero_copy(j).wait())

    pl.when(step > 0)(wait_rows)
    src_scr[...] = xn_ref[...]
    for j in range(tm):
        for k, dref in enumerate(dest_refs):
            _row_copy(src_scr, j, xs_ref, dref[0, 0, j], sem).start(priority=k % 2)
    pl.when(step == pl.num_programs(0) - 1)(wait_rows)


def _dest_blocks(dest, tm):
    return [dest[k].reshape(-1, 1, tm) for k in range(TOP_K)]


def _dispatch_call(zero_blocks, dest, xn, m_pad, seq):
    n_tok, dw = xn.shape
    tm = min(ROW_COPY_TILE, seq)
    smem = pl.BlockSpec((1, 1, tm), lambda i, zb: (i, 0, 0), memory_space=pltpu.SMEM)
    return pl.pallas_call(
        functools.partial(_dispatch_kernel, tm=tm, rows=EXPERT_ROWS),
        out_shape=jax.ShapeDtypeStruct((m_pad, dw), xn.dtype),
        grid_spec=pltpu.PrefetchScalarGridSpec(
            num_scalar_prefetch=1,
            grid=(n_tok // tm,),
            in_specs=[smem] * TOP_K + [pl.BlockSpec((tm, dw), lambda i, zb: (i, 0))],
            out_specs=pl.BlockSpec(memory_space=pl.ANY),
            scratch_shapes=[pltpu.VMEM((EXPERT_ROWS, dw), xn.dtype), pltpu.VMEM((tm, dw), xn.dtype),
                            pltpu.SemaphoreType.DMA(()), pltpu.SemaphoreType.DMA(())]),
        compiler_params=_cparams("arbitrary"),
        name="dispatch",
    )(zero_blocks, *_dest_blocks(dest, tm), xn)


def _expert_kernel(blk_e_ref, used_ref, xs_ref, wg_ref, wu_ref, wd_ref, y_ref, wg_bf, wu_bf, wd_bf):
    i = pl.program_id(0)
    live = i < used_ref[0]
    new_expert = (i == 0) | (blk_e_ref[i] != blk_e_ref[jnp.maximum(i - 1, 0)])

    @pl.when(live & new_expert)
    def _():
        wg_bf[...] = wg_ref[0].astype(BF16)
        wu_bf[...] = wu_ref[0].astype(BF16)
        wd_bf[...] = wd_ref[0].astype(BF16)

    @pl.when(live)
    def _():
        xb = _unpack_halves(xs_ref[...]).astype(BF16)
        g = _dot(xb, wg_bf[...])
        hid = g * _sigmoid(g) * _dot(xb, wu_bf[...])
        y_ref[...] = _pack_halves(_dot(hid.astype(BF16), wd_bf[...]))

    @pl.when(jnp.logical_not(live))
    def _():
        y_ref[...] = jnp.zeros_like(y_ref)


def _expert_call(blk_e, used, xs, wg, wu, wd):
    m_pad, dw = xs.shape
    _, d, f = wg.shape
    rows = EXPERT_ROWS
    return pl.pallas_call(
        _expert_kernel,
        out_shape=jax.ShapeDtypeStruct((m_pad, dw), xs.dtype),
        grid_spec=pltpu.PrefetchScalarGridSpec(
            num_scalar_prefetch=2,
            grid=(m_pad // rows,),
            in_specs=[pl.BlockSpec((rows, dw), lambda i, be, us: (i, 0)),
                      pl.BlockSpec((1, d, f), lambda i, be, us: (be[i], 0, 0)),
                      pl.BlockSpec((1, d, f), lambda i, be, us: (be[i], 0, 0)),
                      pl.BlockSpec((1, f, d), lambda i, be, us: (be[i], 0, 0))],
            out_specs=pl.BlockSpec((rows, dw), lambda i, be, us: (i, 0)),
            scratch_shapes=[pltpu.VMEM((d, f), BF16), pltpu.VMEM((d, f), BF16), pltpu.VMEM((f, d), BF16)]),
        compiler_params=_cparams("arbitrary"),
        name="experts",
    )(blk_e, used, xs, wg, wu, wd)


def _combine_kernel(*refs, tm, final):
    cur_refs, nxt_refs = refs[:TOP_K], refs[TOP_K:2 * TOP_K]
    h_ref, route_ref, gtf_ref, gfin_ref, y_ref, o_ref, land, rows, sem = refs[2 * TOP_K:]
    step = pl.program_id(0)

    def gather(dest_refs):
        for j in range(tm):
            for k, dref in enumerate(dest_refs):
                pltpu.make_async_copy(y_ref.at[pl.ds(dref[0, 0, j], 1)], land.at[k, pl.ds(j, 1)],
                                      sem).start(priority=k % 2)

    def wait_rows():
        for k in range(TOP_K):
            pltpu.make_async_copy(y_ref.at[pl.ds(0, tm)], land.at[k], sem).wait()

    pl.when(step == 0)(lambda: gather(cur_refs))
    wait_rows()
    rows[...] = land[...]
    gather(nxt_refs)
    route = route_ref[...]
    y = route[:, 2:3] * _unpack_halves(rows[0]) + route[:, 3:4] * _unpack_halves(rows[1])
    h = h_ref[...] + gtf_ref[0] * y
    o_ref[...] = _rms(h, gfin_ref[...]) if final else h
    pl.when(step == pl.num_programs(0) - 1)(wait_rows)


def _combine_call(dest, h, route, gtf, g_final, yb, seq, final):
    n_tok, d = h.shape
    tm = min(ROW_COPY_TILE, seq)
    per_b = seq // tm
    steps = n_tok // tm
    tok = lambda w: pl.BlockSpec((tm, w), lambda i: (i, 0))
    cur = pl.BlockSpec((1, 1, tm), lambda i: (i, 0, 0), memory_space=pltpu.SMEM)
    nxt = pl.BlockSpec((1, 1, tm), lambda i: (jnp.minimum(i + 1, steps - 1), 0, 0), memory_space=pltpu.SMEM)
    blocks = _dest_blocks(dest, tm)
    return pl.pallas_call(
        functools.partial(_combine_kernel, tm=tm, final=final),
        out_shape=jax.ShapeDtypeStruct((n_tok, d), F32),
        grid=(steps,),
        in_specs=[cur] * TOP_K + [nxt] * TOP_K + [
            tok(d), tok(ROUTE_COLS),
            pl.BlockSpec((1, 1, d), lambda i: (i // per_b, 0, 0)),
            pl.BlockSpec((1, d), lambda i: (0, 0)),
            pl.BlockSpec(memory_space=pl.ANY)],
        out_specs=tok(d),
        scratch_shapes=[pltpu.VMEM((TOP_K, tm, yb.shape[1]), yb.dtype),
                        pltpu.VMEM((TOP_K, tm, yb.shape[1]), yb.dtype), pltpu.SemaphoreType.DMA(())],
        compiler_params=_cparams("arbitrary"),
        name="combine",
    )(*blocks, *blocks, h, route, gtf, g_final, yb)


def _routing_plan(route_t, counts, n_experts, m_pad):
    eid = route_t[0:TOP_K].astype(jnp.int32)
    rank = route_t[4:4 + TOP_K].astype(jnp.int32)
    cnt = counts.astype(jnp.int32)
    padded = (cnt + EXPERT_ROWS - 1) // EXPERT_ROWS * EXPERT_ROWS
    pend = jnp.cumsum(padded)
    pstart = pend - padded
    dest = rank
    for e in range(n_experts):
        dest = dest + jnp.where(eid == e, pstart[e], 0)
    nblk = m_pad // EXPERT_ROWS
    blk_row = jnp.arange(nblk, dtype=jnp.int32) * EXPERT_ROWS
    blk_e = jnp.minimum(jnp.sum(pend[None, :] <= blk_row[:, None], axis=1), n_experts - 1).astype(jnp.int32)
    used = (pend[-1:] // EXPERT_ROWS).astype(jnp.int32)
    last_blk = jnp.where(padded > 0, pend // EXPERT_ROWS - 1, -1)
    tail_blk = used + jnp.arange(n_experts, dtype=jnp.int32)
    tail_blk = jnp.where(tail_blk < nblk, tail_blk, -1)
    zero_blocks = jnp.concatenate([last_blk, tail_blk]).astype(jnp.int32)
    return dest, blk_e, used, zero_blocks


def kernel(x, c, g_mix, g_ffn, w_ada, b_ada, w_in, w_sba_out, g_sgu, w_spatial, b_spatial, w_sgu_out, w_out,
           w_router_group, b_router_group, w_router_expert, b_router_expert, w_expert_gate, w_expert_up,
           w_expert_down, g_final):
    bsz, seq, d = x.shape
    depth = w_in.shape[0]
    sba = w_sba_out.shape[1]
    sgu = g_sgu.shape[1]
    heads = sba // HEAD_DIM
    groups, chunk = w_spatial.shape[1], w_spatial.shape[2]
    n_groups = w_router_group.shape[2]
    n_experts = w_router_expert.shape[2]
    per_group = n_experts // n_groups
    n_tok = bsz * seq
    m_pad = n_tok * TOP_K + n_experts * EXPERT_ROWS
    widths = (sba, sba, sba, 2 * sgu, 2 * d)
    assert seq % chunk == 0 and LANES % (sgu // groups) == 0
    assert d % (2 * LANES) == 0
    assert per_group == SUBLANES and n_groups <= SUBLANES and (n_groups + 1) * SUBLANES <= LANES

    h = x.reshape(n_tok, d)
    for l in range(depth):
        mod = _mod_call(c, w_ada[l], b_ada[l])
        sh_m, sc_m, gt_m, sh_f, sc_f, gt_f = [mod[:, i * d:(i + 1) * d].reshape(bsz, 1, d) for i in range(6)]

        q, k, v, uv, gates = _proj_call(h, sh_m, sc_m, g_mix[l].reshape(1, d), g_sgu[l].reshape(1, sgu),
                                        w_in[l].astype(BF16), seq, widths)
        ya = _attn_call(q.reshape(bsz, seq, sba), k.reshape(bsz, seq, sba), v.reshape(bsz, seq, sba), heads)

        bias_full = jnp.repeat(b_spatial[l].T, sgu // groups, axis=1)
        gpad = SUBLANES - n_groups
        wr = jnp.concatenate([jnp.pad(w_router_group[l], ((0, 0), (0, gpad))), w_router_expert[l]], axis=1)
        wr = jnp.pad(wr, ((0, 0), (0, LANES - wr.shape[1])))
        br = jnp.concatenate([jnp.pad(b_router_group[l], (0, gpad)), b_router_expert[l]])
        br = jnp.pad(br, (0, LANES - br.shape[0])).reshape(1, LANES)
        h1, xn, route, route_t, counts = _mix_call(
            ya.reshape(n_tok, sba), uv, gates, h, gt_m, sh_f, sc_f, w_spatial[l],
            bias_full, w_sba_out[l].astype(BF16), w_sgu_out[l].astype(BF16), w_out[l].astype(BF16),
            g_ffn[l].reshape(1, d), wr, br, seq, n_groups, per_group)

        dest, blk_e, used, zero_blocks = _routing_plan(route_t, counts[SUBLANES:SUBLANES + n_experts, 0],
                                                       n_experts, m_pad)
        xs = _dispatch_call(zero_blocks, dest, xn, m_pad, seq)
        yb = _expert_call(blk_e, used, xs, w_expert_gate[l], w_expert_up[l], w_expert_down[l])
        h = _combine_call(dest, h1, route, gt_f, g_final.reshape(1, d), yb, seq, final=l == depth - 1)
    return h.reshape(bsz, seq, d)
```

```python
import functools

import jax
import jax.numpy as jnp
from jax import lax
from jax.experimental import pallas as pl
from jax.experimental.pallas import tpu as pltpu

F32 = jnp.float32
BF16 = jnp.bfloat16

HEAD_DIM = 64
TOP_K = 2
EPS = 1e-6
LOG2E = 1.4426950408889634
SIGN_BIT = 0x80000000
BF16_BITS = 0xFFFF0000
ATTN_EXIT_LOG2 = 160.0
DEAD_CARRY = 1e30
SUFFIX_LAG = 2
WEIGHTS_LAG = 4
LANES = 128
SUBLANES = 8
ATTN_BLOCK = 256
EXPERT_ROWS = 512
TOKEN_TILE = 1024
ROW_COPY_TILE = 512
PROJ_TILE = 512
ROUTE_COLS = 8
VMEM_LIMIT = 56 * 1024 * 1024


def _cparams(*sem):
    return pltpu.CompilerParams(dimension_semantics=sem, vmem_limit_bytes=VMEM_LIMIT)


def _split_bf16(a):
    hi = a.astype(BF16)
    lo = (a - hi.astype(F32)).astype(BF16)
    return hi, lo


def _dot(a, b):
    return jnp.dot(a, b, preferred_element_type=F32)


def _dot3(a, b):
    ah, al = _split_bf16(a)
    bh, bl = _split_bf16(b)
    return _dot(ah, bh) + _dot(ah, bl) + _dot(al, bh)


def _pack_halves(x):
    half = x.shape[1] // 2
    hi = lax.bitcast_convert_type(x[:, :half].astype(BF16).astype(F32), jnp.uint32)
    lo = lax.bitcast_convert_type(x[:, half:].astype(BF16).astype(F32), jnp.uint32)
    return hi | (lo >> 16)


def _unpack_halves(u):
    hi = lax.bitcast_convert_type(u & jnp.uint32(BF16_BITS), F32)
    lo = lax.bitcast_convert_type(u << 16, F32)
    return jnp.concatenate([hi, lo], axis=1)


def _rms(x, g):
    ms = jnp.mean(x * x, axis=-1, keepdims=True)
    return x * lax.rsqrt(ms + EPS) * g


def _mod_kernel(c_ref, w_ref, b_ref, o_ref):
    c = c_ref[...]
    ca = c * (1.0 / (1.0 + jnp.exp(-c)))
    o_ref[...] = _dot3(ca, w_ref[...]) + b_ref[...]


def _mod_call(c, w_ada, b_ada):
    bsz, d = c.shape
    n = w_ada.shape[1]
    tn = n // 4 if n % (4 * LANES) == 0 else n
    return pl.pallas_call(
        _mod_kernel,
        out_shape=jax.ShapeDtypeStruct((bsz, n), F32),
        grid=(n // tn,),
        in_specs=[pl.BlockSpec((bsz, d), lambda j: (0, 0)),
                  pl.BlockSpec((d, tn), lambda j: (0, j)),
                  pl.BlockSpec((1, tn), lambda j: (0, j))],
        out_specs=pl.BlockSpec((bsz, tn), lambda j: (0, j)),
        compiler_params=_cparams("arbitrary"),
        name="mod",
    )(c, w_ada, b_ada.reshape(1, n))


def _gelu_tanh(x):
    c = 0.7978845608028654
    hx = 0.5 * x
    return hx + hx * jnp.tanh(x * (c + (0.044715 * c) * (x * x)))


def _proj_kernel(x_ref, sh_ref, sc_ref, g_ref, gsgu_ref, w_ref, q_ref, k_ref, v_ref, uv_ref, gt_ref, *, widths):
    x = x_ref[...]
    n = _rms(x, g_ref[...]) * (1.0 + sc_ref[0]) + sh_ref[0]
    nb = n.astype(BF16)
    off = 0
    for ref, wd in zip((q_ref, k_ref, v_ref, uv_ref, gt_ref), widths):
        p = _dot(nb, w_ref[:, off:off + wd])
        if ref is q_ref:
            p = p * (HEAD_DIM ** -0.5 * LOG2E)
        elif ref is uv_ref:
            act = _gelu_tanh(p)
            sgu = wd // 2
            v = act[:, sgu:]
            vc = v - jnp.mean(v, axis=-1, keepdims=True)
            var = jnp.mean(vc * vc, axis=-1, keepdims=True)
            p = jnp.concatenate([act[:, :sgu], vc * lax.rsqrt(var + EPS) * gsgu_ref[...]], axis=1)
        elif ref is gt_ref:
            p = 1.0 + jnp.tanh(0.5 * p)
        ref[...] = p.astype(ref.dtype)
        off += wd


def _proj_call(x2, sh, sc, g, g_sgu, w_in_bf, seq, widths):
    n_tok, d = x2.shape
    tm = min(PROJ_TILE, seq)
    per_b = seq // tm
    cols = w_in_bf.shape[1]
    tok = lambda w: pl.BlockSpec((tm, w), lambda i: (i, 0))
    vec = pl.BlockSpec((1, 1, d), lambda i: (i // per_b, 0, 0))
    return pl.pallas_call(
        functools.partial(_proj_kernel, widths=widths),
        out_shape=[jax.ShapeDtypeStruct((n_tok, w), BF16) for w in widths],
        grid=(n_tok // tm,),
        in_specs=[tok(d), vec, vec,
                  pl.BlockSpec((1, d), lambda i: (0, 0)),
                  pl.BlockSpec(g_sgu.shape, lambda i: (0, 0)),
                  pl.BlockSpec((d, cols), lambda i: (0, 0))],
        out_specs=[tok(w) for w in widths],
        compiler_params=_cparams("arbitrary"),
        name="proj",
    )(x2, sh, sc, g, g_sgu, w_in_bf)


def _attn_kernel(q_ref, k_ref, v_ref, u_ref, o_ref, qm_ref, acc_ref, carry_ref, *, heads, blk):
    qi = pl.program_id(1)
    u_tri = u_ref[...]
    row = lax.broadcasted_iota(jnp.int32, (blk, blk), 0)
    col = lax.broadcasted_iota(jnp.int32, (blk, blk), 1)
    causal = col < row
    per_slab = LANES // HEAD_DIM
    lane = lax.broadcasted_iota(jnp.int32, (blk, LANES), 1)
    own = [(lane >= j * HEAD_DIM) & (lane < (j + 1) * HEAD_DIM) for j in range(per_slab)]
    for h in range(heads):
        slab = slice(h // per_slab * LANES, (h // per_slab + 1) * LANES)
        qm_ref[h] = jnp.where(own[h % per_slab], q_ref[0, :, slab], 0.0).astype(BF16)

    slabs = [slice(p * LANES, (p + 1) * LANES) for p in range(heads // per_slab)]

    def make_tile(start, first, dead=None):
        mask = causal if first else None
        s_all, cum_all, w_all, carries = {}, {}, {}, []

        def scores(h):
            s_all[h] = lax.dot_general(qm_ref[h], k_ref[0, pl.ds(start, blk), slabs[h // per_slab]],
                                       (((1,), (1,)), ((), ())), preferred_element_type=F32)

        def suffix_sums(h):
            s = s_all[h]
            neg_abs = lax.bitcast_convert_type(lax.bitcast_convert_type(s, jnp.uint32) | jnp.uint32(SIGN_BIT), F32)
            sp = jnp.maximum(s, 0.0) + jnp.log(1.0 + jnp.exp2(neg_abs)) * LOG2E
            if mask is not None:
                sp = jnp.where(mask, sp, 0.0)
            hi = lax.bitcast_convert_type(lax.bitcast_convert_type(sp, jnp.uint32) & jnp.uint32(BF16_BITS), F32)
            cum_all[h] = _dot(hi.astype(BF16), u_tri) + _dot((sp - hi).astype(BF16), u_tri)

        def weights(h):
            cum = cum_all[h]
            if first:
                w = jnp.exp2(s_all[h] - cum)
                carry = cum[:, 0:1]
            else:
                carry = carry_ref[h]
                if dead is not None:
                    carry = jnp.where(dead, DEAD_CARRY, carry)
                w = jnp.exp2((s_all[h] - carry) - cum)
                carry = carry + cum[:, 0:1]
            w_all[h] = (w if mask is None else jnp.where(mask, w, 0.0)).astype(BF16)
            carry_ref[h] = carry
            carries.append(carry)
            if h % per_slab == per_slab - 1:
                p = h // per_slab
                vp = v_ref[0, pl.ds(start, blk), slabs[p]]
                upd = None
                for j in range(per_slab):
                    pv = _dot(w_all[p * per_slab + j], vp)
                    upd = pv if upd is None else jnp.where(own[j], pv, upd)
                acc_ref[:, slabs[p]] = upd if first else acc_ref[:, slabs[p]] + upd

        return scores, suffix_sums, weights, carries

    def run_tiles(tiles):
        jobs = [(tile, h) for tile in tiles for h in range(heads)]
        for r in range(len(jobs) + WEIGHTS_LAG):
            if r < len(jobs):
                tile, h = jobs[r]
                tile[0](h)
            if SUFFIX_LAG <= r < len(jobs) + SUFFIX_LAG:
                tile, h = jobs[r - SUFFIX_LAG]
                tile[1](h)
            if r >= WEIGHTS_LAG:
                tile, h = jobs[r - WEIGHTS_LAG]
                tile[2](h)
        return jnp.min(functools.reduce(jnp.minimum, tiles[-1][3]))

    cmin = run_tiles([make_tile(pl.multiple_of(qi * blk, blk), True),
                      make_tile(pl.multiple_of(jnp.maximum(qi - 1, 0) * blk, blk), False, dead=qi == 0)])

    def cond(state):
        i, cmin = state
        return (i < qi) & (cmin < ATTN_EXIT_LOG2)

    def body(state):
        i, _ = state
        return i + 1, run_tiles([make_tile(pl.multiple_of((qi - 1 - i) * blk, blk), False)])

    lax.while_loop(cond, body, (jnp.int32(1), cmin))
    o_ref[0] = acc_ref[...].astype(o_ref.dtype)


def _attn_call(q, k, v, heads):
    bsz, seq, width = q.shape
    blk = min(ATTN_BLOCK, seq)
    r = lax.broadcasted_iota(jnp.int32, (blk, blk), 0)
    c = lax.broadcasted_iota(jnp.int32, (blk, blk), 1)
    u_tri = (r >= c).astype(BF16)
    full = pl.BlockSpec((1, seq, width), lambda b, i: (b, 0, 0))
    return pl.pallas_call(
        functools.partial(_attn_kernel, heads=heads, blk=blk),
        out_shape=jax.ShapeDtypeStruct((bsz, seq, width), BF16),
        grid=(bsz, seq // blk),
        in_specs=[pl.BlockSpec((1, blk, width), lambda b, i: (b, i, 0)), full, full,
                  pl.BlockSpec((blk, blk), lambda b, i: (0, 0))],
        out_specs=pl.BlockSpec((1, blk, width), lambda b, i: (b, i, 0)),
        scratch_shapes=[pltpu.VMEM((heads, blk, LANES), BF16), pltpu.VMEM((blk, width), F32),
                        pltpu.VMEM((heads, blk, 1), F32)],
        compiler_params=_cparams("arbitrary", "arbitrary"),
        name="attn",
    )(q, k, v, u_tri)


def _sigmoid(x):
    return 1.0 / (1.0 + jnp.exp(-x))


def _mix_kernel(ya_ref, uv_ref, gt_ref, x_ref, gtm_ref, shf_ref, scf_ref, ws_ref, bs_ref,
                wa_ref, wb_ref, wo_ref, gffn_ref, wr_ref, br_ref, utri_ref,
                h_ref, xn_ref, route_ref, route_t_ref, cnt_ref, yb_scr, cnt_scr,
                *, chunk, groups, n_groups, per_group):
    step = pl.program_id(0)
    tm = x_ref.shape[0]
    sgu = uv_ref.shape[1] // 2
    gdim = sgu // groups

    @pl.when(step == 0)
    def _():
        cnt_scr[...] = jnp.zeros_like(cnt_scr)

    ya_proj = _dot(ya_ref[...], wa_ref[...])

    u = uv_ref[:, :sgu].astype(F32)
    vn = uv_ref[:, sgu:]
    r = lax.broadcasted_iota(jnp.int32, (chunk, chunk), 0)
    c = lax.broadcasted_iota(jnp.int32, (chunk, chunk), 1)
    wcs = [jnp.where(r >= c, ws_ref[g], 0.0).astype(BF16) for g in range(groups)]
    pair = LANES // gdim
    lane = lax.broadcasted_iota(jnp.int32, (chunk, LANES), 1)
    for ci in range(tm // chunk):
        rows = slice(ci * chunk, (ci + 1) * chunk)
        for p in range(groups // pair):
            lanes = slice(p * LANES, (p + 1) * LANES)
            slab = vn[rows, lanes]
            mix = bs_ref[:, lanes]
            for j in range(pair):
                sel = (lane >= j * gdim) & (lane < (j + 1) * gdim)
                mix = mix + _dot(wcs[p * pair + j], jnp.where(sel, slab, 0.0).astype(BF16))
            yb_scr[rows, lanes] = (u[rows, lanes] * mix).astype(BF16)

    d = x_ref.shape[1]
    gates2 = gt_ref[...].astype(F32)
    merged2 = gates2[:, :d] * ya_proj + gates2[:, d:] * _dot(yb_scr[...], wb_ref[...])
    h = x_ref[...] + (0.5 * gtm_ref[0]) * _dot(merged2.astype(BF16), wo_ref[...])
    h_ref[...] = h

    xn = _rms(h, gffn_ref[...]) * (1.0 + scf_ref[0]) + shf_ref[0]
    xn_ref[...] = _pack_halves(xn)
    logits = _dot3(xn, wr_ref[...]) + br_ref[...]

    lt = logits.T
    sub = lax.broadcasted_iota(jnp.int32, (SUBLANES, tm), 0)
    neg = -jnp.inf
    gl = jnp.where(sub < n_groups, lt[0:SUBLANES], neg)
    gmax = jnp.max(gl, axis=0, keepdims=True)
    g_sel = jnp.min(jnp.where(gl == gmax, sub, SUBLANES), axis=0, keepdims=True)
    g_w = 1.0 / jnp.sum(jnp.exp(gl - gmax), axis=0, keepdims=True)
    el = lt[SUBLANES:2 * SUBLANES]
    for g in range(1, n_groups):
        el = jnp.where(g_sel == g, lt[(g + 1) * SUBLANES:(g + 2) * SUBLANES], el)
    m1 = jnp.max(el, axis=0, keepdims=True)
    j1 = jnp.min(jnp.where(el == m1, sub, SUBLANES), axis=0, keepdims=True)
    el2 = jnp.where(sub == j1, neg, el)
    m2 = jnp.max(el2, axis=0, keepdims=True)
    j2 = jnp.min(jnp.where(el2 == m2, sub, SUBLANES), axis=0, keepdims=True)
    t = jnp.exp(m2 - m1)
    w1 = g_w / (1.0 + t)
    w2 = g_w * t / (1.0 + t)
    e1 = g_sel * per_group + j1
    e2 = g_sel * per_group + j2

    row = lax.broadcasted_iota(jnp.int32, (LANES, tm), 0)
    oh1 = row == e1 + SUBLANES
    oh2 = row == e2 + SUBLANES
    both = jnp.where(oh1 | oh2, 1.0, 0.0)
    before = _dot(both.astype(BF16), utri_ref[...]) + cnt_scr[...]
    rank1 = jnp.sum(jnp.where(oh1, before, 0.0), axis=0, keepdims=True)
    rank2 = jnp.sum(jnp.where(oh2, before, 0.0), axis=0, keepdims=True)
    cnt_scr[...] = cnt_scr[...] + jnp.sum(both, axis=1, keepdims=True)
    cnt_ref[...] = cnt_scr[...]

    rec = jnp.zeros((LANES, tm), F32)
    for idx, val in enumerate((e1.astype(F32), e2.astype(F32), w1, w2, rank1, rank2)):
        rec = jnp.where(row == idx, val, rec)
    route_t_ref[...] = rec[:ROUTE_COLS]
    route_ref[...] = rec.T[:, :ROUTE_COLS]


def _mix_call(ya, uv, gates, x2, gtm, shf, scf, w_spatial, bias_full, wa, wb, wo, g_ffn, wr, br,
              seq, n_groups, per_group):
    n_tok, d = x2.shape
    tm = min(TOKEN_TILE, seq)
    per_b = seq // tm
    groups, chunk, _ = w_spatial.shape
    sgu = uv.shape[1] // 2
    r = lax.broadcasted_iota(jnp.int32, (tm, tm), 0)
    c = lax.broadcasted_iota(jnp.int32, (tm, tm), 1)
    utri = (r < c).astype(BF16)
    tok = lambda w: pl.BlockSpec((tm, w), lambda i: (i, 0))
    vec = pl.BlockSpec((1, 1, d), lambda i: (i // per_b, 0, 0))
    const = lambda a: pl.BlockSpec(a.shape, lambda i: (0,) * a.ndim)
    ins = [ya, uv, gates, x2, gtm, shf, scf, w_spatial, bias_full, wa, wb, wo, g_ffn, wr, br, utri]
    in_specs = [tok(ya.shape[1]), tok(uv.shape[1]), tok(gates.shape[1]), tok(d), vec, vec, vec]
    in_specs += [const(a) for a in ins[7:]]
    return pl.pallas_call(
        functools.partial(_mix_kernel, chunk=chunk, groups=groups, n_groups=n_groups, per_group=per_group),
        out_shape=[jax.ShapeDtypeStruct((n_tok, d), F32), jax.ShapeDtypeStruct((n_tok, d // 2), jnp.uint32),
                   jax.ShapeDtypeStruct((n_tok, ROUTE_COLS), F32), jax.ShapeDtypeStruct((ROUTE_COLS, n_tok), F32),
                   jax.ShapeDtypeStruct((LANES, 1), F32)],
        grid=(n_tok // tm,),
        in_specs=in_specs,
        out_specs=[tok(d), tok(d // 2), tok(ROUTE_COLS), pl.BlockSpec((ROUTE_COLS, tm), lambda i: (0, i)),
                   pl.BlockSpec((LANES, 1), lambda i: (0, 0))],
        scratch_shapes=[pltpu.VMEM((tm, sgu), BF16), pltpu.VMEM((LANES, 1), F32)],
        compiler_params=_cparams("arbitrary"),
        name="mix",
    )(*ins)


def _row_copy(src, s, dst, d, sem):
    return pltpu.make_async_copy(src.at[pl.ds(s, 1)], dst.at[pl.ds(d, 1)], sem)


def _invert_kernel(*refs, tm):
    dest_refs, (zeros_ref, src_ref, sem) = refs[:TOP_K], refs[TOP_K:]
    step = pl.program_id(0)

    @pl.when(step == 0)
    def _():
        fill = pltpu.make_async_copy(zeros_ref, src_ref, sem)
        fill.start()
        fill.wait()

    base = step * tm
    for j in range(tm):
        for dref in dest_refs:
            src_ref[dref[0, 0, j]] = base + j


def _dest_blocks(dest, tm):
    return [dest[k].reshape(-1, 1, tm) for k in range(TOP_K)]


def _invert_call(dest, m_pad, seq):
    n_tok = dest.shape[1]
    tm = min(ROW_COPY_TILE, seq)
    smem = pl.BlockSpec((1, 1, tm), lambda i: (i, 0, 0), memory_space=pltpu.SMEM)
    return pl.pallas_call(
        functools.partial(_invert_kernel, tm=tm),
        out_shape=jax.ShapeDtypeStruct((m_pad,), jnp.int32),
        grid=(n_tok // tm,),
        in_specs=[smem] * TOP_K + [pl.BlockSpec(memory_space=pl.ANY)],
        out_specs=pl.BlockSpec((m_pad,), lambda i: (0,), memory_space=pltpu.SMEM),
        scratch_shapes=[pltpu.SemaphoreType.DMA(())],
        compiler_params=_cparams("arbitrary"),
        name="invert",
    )(*_dest_blocks(dest, tm), jnp.zeros((m_pad,), jnp.int32))


def _expert_kernel(blk_e_ref, used_ref, src_ref, nxt_ref, xn_ref, wg_ref, wu_ref, wd_ref, y_ref,
                   land, xb_scr, wg_bf, wu_bf, wd_bf, sem, *, rows):
    i = pl.program_id(0)
    used = used_ref[0]
    live = i < used
    new_expert = (i == 0) | (blk_e_ref[i] != blk_e_ref[jnp.maximum(i - 1, 0)])

    def gather(tok_ref):
        for j in range(rows):
            _row_copy(xn_ref, tok_ref[0, 0, j], land, j, sem).start(priority=j % 2)

    def wait_rows():
        pltpu.make_async_copy(xn_ref.at[pl.ds(0, rows)], land, sem).wait()

    pl.when(i == 0)(lambda: gather(src_ref))

    @pl.when(live & new_expert)
    def _():
        wg_bf[...] = wg_ref[0].astype(BF16)
        wu_bf[...] = wu_ref[0].astype(BF16)
        wd_bf[...] = wd_ref[0].astype(BF16)

    @pl.when(live)
    def _():
        wait_rows()
        xb_scr[...] = _unpack_halves(land[...]).astype(BF16)
        gather(nxt_ref)
        xb = xb_scr[...]
        g = _dot(xb, wg_bf[...])
        hid = g * _sigmoid(g) * _dot(xb, wu_bf[...])
        y_ref[...] = _pack_halves(_dot(hid.astype(BF16), wd_bf[...]))

    @pl.when(jnp.logical_not(live))
    def _():
        y_ref[...] = jnp.zeros_like(y_ref)

    pl.when((i == used) | (live & (i == pl.num_programs(0) - 1)))(wait_rows)


def _expert_call(blk_e, used, source, xn, wg, wu, wd):
    n_tok, dw = xn.shape
    _, d, f = wg.shape
    rows = EXPERT_ROWS
    nblk = source.shape[0] // rows
    src_blocks = source.reshape(nblk, 1, rows)
    cur = pl.BlockSpec((1, 1, rows), lambda i, be, us: (i, 0, 0), memory_space=pltpu.SMEM)
    nxt = pl.BlockSpec((1, 1, rows), lambda i, be, us: (jnp.minimum(i + 1, nblk - 1), 0, 0),
                       memory_space=pltpu.SMEM)
    return pl.pallas_call(
        functools.partial(_expert_kernel, rows=rows),
        out_shape=jax.ShapeDtypeStruct((nblk * rows, dw), xn.dtype),
        grid_spec=pltpu.PrefetchScalarGridSpec(
            num_scalar_prefetch=2,
            grid=(nblk,),
            in_specs=[cur, nxt, pl.BlockSpec(memory_space=pl.ANY),
                      pl.BlockSpec((1, d, f), lambda i, be, us: (be[i], 0, 0)),
                      pl.BlockSpec((1, d, f), lambda i, be, us: (be[i], 0, 0)),
                      pl.BlockSpec((1, f, d), lambda i, be, us: (be[i], 0, 0))],
            out_specs=pl.BlockSpec((rows, dw), lambda i, be, us: (i, 0)),
            scratch_shapes=[pltpu.VMEM((rows, dw), xn.dtype), pltpu.VMEM((rows, d), BF16),
                            pltpu.VMEM((d, f), BF16), pltpu.VMEM((d, f), BF16), pltpu.VMEM((f, d), BF16),
                            pltpu.SemaphoreType.DMA(())]),
        compiler_params=_cparams("arbitrary"),
        name="experts",
    )(blk_e, used, src_blocks, src_blocks, xn, wg, wu, wd)


def _combine_kernel(*refs, tm, final):
    cur_refs, nxt_refs = refs[:TOP_K], refs[TOP_K:2 * TOP_K]
    h_ref, route_ref, gtf_ref, gfin_ref, y_ref, o_ref, land, rows, sem = refs[2 * TOP_K:]
    step = pl.program_id(0)

    def gather(dest_refs):
        for j in range(tm):
            for k, dref in enumerate(dest_refs):
                pltpu.make_async_copy(y_ref.at[pl.ds(dref[0, 0, j], 1)], land.at[k, pl.ds(j, 1)],
                                      sem).start(priority=k % 2)

    def wait_rows():
        for k in range(TOP_K):
            pltpu.make_async_copy(y_ref.at[pl.ds(0, tm)], land.at[k], sem).wait()

    pl.when(step == 0)(lambda: gather(cur_refs))
    wait_rows()
    rows[...] = land[...]
    gather(nxt_refs)
    route = route_ref[...]
    y = route[:, 2:3] * _unpack_halves(rows[0]) + route[:, 3:4] * _unpack_halves(rows[1])
    h = h_ref[...] + gtf_ref[0] * y
    o_ref[...] = _rms(h, gfin_ref[...]) if final else h
    pl.when(step == pl.num_programs(0) - 1)(wait_rows)


def _combine_call(dest, h, route, gtf, g_final, yb, seq, final):
    n_tok, d = h.shape
    tm = min(ROW_COPY_TILE, seq)
    per_b = seq // tm
    steps = n_tok // tm
    tok = lambda w: pl.BlockSpec((tm, w), lambda i: (i, 0))
    cur = pl.BlockSpec((1, 1, tm), lambda i: (i, 0, 0), memory_space=pltpu.SMEM)
    nxt = pl.BlockSpec((1, 1, tm), lambda i: (jnp.minimum(i + 1, steps - 1), 0, 0), memory_space=pltpu.SMEM)
    blocks = _dest_blocks(dest, tm)
    return pl.pallas_call(
        functools.partial(_combine_kernel, tm=tm, final=final),
        out_shape=jax.ShapeDtypeStruct((n_tok, d), F32),
        grid=(steps,),
        in_specs=[cur] * TOP_K + [nxt] * TOP_K + [
            tok(d), tok(ROUTE_COLS),
            pl.BlockSpec((1, 1, d), lambda i: (i // per_b, 0, 0)),
            pl.BlockSpec((1, d), lambda i: (0, 0)),
            pl.BlockSpec(memory_space=pl.ANY)],
        out_specs=tok(d),
        scratch_shapes=[pltpu.VMEM((TOP_K, tm, yb.shape[1]), yb.dtype),
                        pltpu.VMEM((TOP_K, tm, yb.shape[1]), yb.dtype), pltpu.SemaphoreType.DMA(())],
        compiler_params=_cparams("arbitrary"),
        name="combine",
    )(*blocks, *blocks, h, route, gtf, g_final, yb)


def _routing_plan(route_t, counts, n_experts, m_pad):
    eid = route_t[0:TOP_K].astype(jnp.int32)
    rank = route_t[4:4 + TOP_K].astype(jnp.int32)
    cnt = counts.astype(jnp.int32)
    padded = (cnt + EXPERT_ROWS - 1) // EXPERT_ROWS * EXPERT_ROWS
    pend = jnp.cumsum(padded)
    pstart = pend - padded
    dest = rank
    for e in range(n_experts):
        dest = dest + jnp.where(eid == e, pstart[e], 0)
    nblk = m_pad // EXPERT_ROWS
    blk_row = jnp.arange(nblk, dtype=jnp.int32) * EXPERT_ROWS
    blk_e = jnp.minimum(jnp.sum(pend[None, :] <= blk_row[:, None], axis=1), n_experts - 1).astype(jnp.int32)
    used = (pend[-1:] // EXPERT_ROWS).astype(jnp.int32)
    return dest, blk_e, used


def kernel(x, c, g_mix, g_ffn, w_ada, b_ada, w_in, w_sba_out, g_sgu, w_spatial, b_spatial, w_sgu_out, w_out,
           w_router_group, b_router_group, w_router_expert, b_router_expert, w_expert_gate, w_expert_up,
           w_expert_down, g_final):
    bsz, seq, d = x.shape
    depth = w_in.shape[0]
    sba = w_sba_out.shape[1]
    sgu = g_sgu.shape[1]
    heads = sba // HEAD_DIM
    groups, chunk = w_spatial.shape[1], w_spatial.shape[2]
    n_groups = w_router_group.shape[2]
    n_experts = w_router_expert.shape[2]
    per_group = n_experts // n_groups
    n_tok = bsz * seq
    m_pad = n_tok * TOP_K + n_experts * EXPERT_ROWS
    widths = (sba, sba, sba, 2 * sgu, 2 * d)
    assert seq % chunk == 0 and LANES % (sgu // groups) == 0
    assert d % (2 * LANES) == 0
    assert per_group == SUBLANES and n_groups <= SUBLANES and (n_groups + 1) * SUBLANES <= LANES

    h = x.reshape(n_tok, d)
    for l in range(depth):
        mod = _mod_call(c, w_ada[l], b_ada[l])
        sh_m, sc_m, gt_m, sh_f, sc_f, gt_f = [mod[:, i * d:(i + 1) * d].reshape(bsz, 1, d) for i in range(6)]

        q, k, v, uv, gates = _proj_call(h, sh_m, sc_m, g_mix[l].reshape(1, d), g_sgu[l].reshape(1, sgu),
                                        w_in[l].astype(BF16), seq, widths)
        ya = _attn_call(q.reshape(bsz, seq, sba), k.reshape(bsz, seq, sba), v.reshape(bsz, seq, sba), heads)

        bias_full = jnp.repeat(b_spatial[l].T, sgu // groups, axis=1)
        gpad = SUBLANES - n_groups
        wr = jnp.concatenate([jnp.pad(w_router_group[l], ((0, 0), (0, gpad))), w_router_expert[l]], axis=1)
        wr = jnp.pad(wr, ((0, 0), (0, LANES - wr.shape[1])))
        br = jnp.concatenate([jnp.pad(b_router_group[l], (0, gpad)), b_router_expert[l]])
        br = jnp.pad(br, (0, LANES - br.shape[0])).reshape(1, LANES)
        h1, xn, route, route_t, counts = _mix_call(
            ya.reshape(n_tok, sba), uv, gates, h, gt_m, sh_f, sc_f, w_spatial[l],
            bias_full, w_sba_out[l].astype(BF16), w_sgu_out[l].astype(BF16), w_out[l].astype(BF16),
            g_ffn[l].reshape(1, d), wr, br, seq, n_groups, per_group)

        dest, blk_e, used = _routing_plan(route_t, counts[SUBLANES:SUBLANES + n_experts, 0], n_experts, m_pad)
        source = _invert_call(dest, m_pad, seq)
        yb = _expert_call(blk_e, used, source, xn, w_expert_gate[l], w_expert_up[l], w_expert_down[l])
        h = _combine_call(dest, h1, route, gt_f, g_final.reshape(1, d), yb, seq, final=l == depth - 1)
    return h.reshape(bsz, seq, d)
```

```python
import functools

import jax
import jax.numpy as jnp
from jax import lax
from jax.experimental import pallas as pl
from jax.experimental.pallas import tpu as pltpu

F32 = jnp.float32
BF16 = jnp.bfloat16

HEAD_DIM = 64
TOP_K = 2
EPS = 1e-6
LOG2E = 1.4426950408889634
SIGN_BIT = 0x80000000
BF16_BITS = 0xFFFF0000
ATTN_EXIT_LOG2 = 160.0
DEAD_CARRY = 1e30
SUFFIX_LAG = 2
WEIGHTS_LAG = 4
LANES = 128
SUBLANES = 8
ATTN_BLOCK = 256
EXPERT_ROWS = 512
TOKEN_TILE = 1024
ROW_COPY_TILE = 512
PROJ_TILE = 512
ROUTE_COLS = 8
VMEM_LIMIT = 56 * 1024 * 1024


def _cparams(*sem):
    return pltpu.CompilerParams(dimension_semantics=sem, vmem_limit_bytes=VMEM_LIMIT)


def _split_bf16(a):
    hi = a.astype(BF16)
    lo = (a - hi.astype(F32)).astype(BF16)
    return hi, lo


def _dot(a, b):
    return jnp.dot(a, b, preferred_element_type=F32)


def _dot3(a, b):
    ah, al = _split_bf16(a)
    bh, bl = _split_bf16(b)
    return _dot(ah, bh) + _dot(ah, bl) + _dot(al, bh)


def _pack_halves(x):
    half = x.shape[1] // 2
    hi = lax.bitcast_convert_type(x[:, :half].astype(BF16).astype(F32), jnp.uint32)
    lo = lax.bitcast_convert_type(x[:, half:].astype(BF16).astype(F32), jnp.uint32)
    return hi | (lo >> 16)


def _unpack_halves(u):
    hi = lax.bitcast_convert_type(u & jnp.uint32(BF16_BITS), F32)
    lo = lax.bitcast_convert_type(u << 16, F32)
    return jnp.concatenate([hi, lo], axis=1)


def _rms(x, g):
    ms = jnp.mean(x * x, axis=-1, keepdims=True)
    return x * lax.rsqrt(ms + EPS) * g


def _mod_kernel(c_ref, w_ref, b_ref, o_ref):
    c = c_ref[...]
    ca = c * (1.0 / (1.0 + jnp.exp(-c)))
    o_ref[...] = _dot3(ca, w_ref[...]) + b_ref[...]


def _mod_call(c, w_ada, b_ada):
    bsz, d = c.shape
    n = w_ada.shape[1]
    tn = n // 4 if n % (4 * LANES) == 0 else n
    return pl.pallas_call(
        _mod_kernel,
        out_shape=jax.ShapeDtypeStruct((bsz, n), F32),
        grid=(n // tn,),
        in_specs=[pl.BlockSpec((bsz, d), lambda j: (0, 0)),
                  pl.BlockSpec((d, tn), lambda j: (0, j)),
                  pl.BlockSpec((1, tn), lambda j: (0, j))],
        out_specs=pl.BlockSpec((bsz, tn), lambda j: (0, j)),
        compiler_params=_cparams("arbitrary"),
        name="mod",
    )(c, w_ada, b_ada.reshape(1, n))


def _gelu_tanh(x):
    c = 0.7978845608028654
    hx = 0.5 * x
    return hx + hx * jnp.tanh(x * (c + (0.044715 * c) * (x * x)))


def _proj_kernel(x_ref, sh_ref, sc_ref, g_ref, gsgu_ref, w_ref, q_ref, k_ref, v_ref, uv_ref, gt_ref, *, widths):
    x = x_ref[...]
    n = _rms(x, g_ref[...]) * (1.0 + sc_ref[0]) + sh_ref[0]
    nb = n.astype(BF16)
    off = 0
    for ref, wd in zip((q_ref, k_ref, v_ref, uv_ref, gt_ref), widths):
        p = _dot(nb, w_ref[:, off:off + wd])
        if ref is q_ref:
            p = p * (HEAD_DIM ** -0.5 * LOG2E)
        elif ref is uv_ref:
            act = _gelu_tanh(p)
            sgu = wd // 2
            v = act[:, sgu:]
            vc = v - jnp.mean(v, axis=-1, keepdims=True)
            var = jnp.mean(vc * vc, axis=-1, keepdims=True)
            p = jnp.concatenate([act[:, :sgu], vc * lax.rsqrt(var + EPS) * gsgu_ref[...]], axis=1)
        elif ref is gt_ref:
            p = 1.0 + jnp.tanh(0.5 * p)
        ref[...] = p.astype(ref.dtype)
        off += wd


def _proj_call(x2, sh, sc, g, g_sgu, w_in_bf, seq, widths):
    n_tok, d = x2.shape
    tm = min(PROJ_TILE, seq)
    per_b = seq // tm
    cols = w_in_bf.shape[1]
    tok = lambda w: pl.BlockSpec((tm, w), lambda i: (i, 0))
    vec = pl.BlockSpec((1, 1, d), lambda i: (i // per_b, 0, 0))
    return pl.pallas_call(
        functools.partial(_proj_kernel, widths=widths),
        out_shape=[jax.ShapeDtypeStruct((n_tok, w), BF16) for w in widths],
        grid=(n_tok // tm,),
        in_specs=[tok(d), vec, vec,
                  pl.BlockSpec((1, d), lambda i: (0, 0)),
                  pl.BlockSpec(g_sgu.shape, lambda i: (0, 0)),
                  pl.BlockSpec((d, cols), lambda i: (0, 0))],
        out_specs=[tok(w) for w in widths],
        compiler_params=_cparams("arbitrary"),
        name="proj",
    )(x2, sh, sc, g, g_sgu, w_in_bf)


def _attn_kernel(q_ref, k_ref, v_ref, u_ref, o_ref, qm_ref, acc_ref, carry_ref, *, heads, blk):
    qi = pl.program_id(1)
    u_tri = u_ref[...]
    row = lax.broadcasted_iota(jnp.int32, (blk, blk), 0)
    col = lax.broadcasted_iota(jnp.int32, (blk, blk), 1)
    causal = col < row
    per_slab = LANES // HEAD_DIM
    lane = lax.broadcasted_iota(jnp.int32, (blk, LANES), 1)
    own = [(lane >= j * HEAD_DIM) & (lane < (j + 1) * HEAD_DIM) for j in range(per_slab)]
    for h in range(heads):
        slab = slice(h // per_slab * LANES, (h // per_slab + 1) * LANES)
        qm_ref[h] = jnp.where(own[h % per_slab], q_ref[0, :, slab], 0.0).astype(BF16)

    slabs = [slice(p * LANES, (p + 1) * LANES) for p in range(heads // per_slab)]

    def make_tile(start, first, dead=None):
        mask = causal if first else None
        s_all, cum_all, w_all, carries = {}, {}, {}, []

        def scores(h):
            s_all[h] = lax.dot_general(qm_ref[h], k_ref[0, pl.ds(start, blk), slabs[h // per_slab]],
                                       (((1,), (1,)), ((), ())), preferred_element_type=F32)

        def suffix_sums(h):
            s = s_all[h]
            neg_abs = lax.bitcast_convert_type(lax.bitcast_convert_type(s, jnp.uint32) | jnp.uint32(SIGN_BIT), F32)
            sp = jnp.maximum(s, 0.0) + jnp.log(1.0 + jnp.exp2(neg_abs)) * LOG2E
            if mask is not None:
                sp = jnp.where(mask, sp, 0.0)
            hi = lax.bitcast_convert_type(lax.bitcast_convert_type(sp, jnp.uint32) & jnp.uint32(BF16_BITS), F32)
            cum_all[h] = _dot(hi.astype(BF16), u_tri) + _dot((sp - hi).astype(BF16), u_tri)

        def weights(h):
            cum = cum_all[h]
            if first:
                w = jnp.exp2(s_all[h] - cum)
                carry = cum[:, 0:1]
            else:
                carry = carry_ref[h]
                if dead is not None:
                    carry = jnp.where(dead, DEAD_CARRY, carry)
                w = jnp.exp2((s_all[h] - carry) - cum)
                carry = carry + cum[:, 0:1]
            w_all[h] = (w if mask is None else jnp.where(mask, w, 0.0)).astype(BF16)
            carry_ref[h] = carry
            carries.append(carry)
            if h % per_slab == per_slab - 1:
                p = h // per_slab
                vp = v_ref[0, pl.ds(start, blk), slabs[p]]
                upd = None
                for j in range(per_slab):
                    pv = _dot(w_all[p * per_slab + j], vp)
                    upd = pv if upd is None else jnp.where(own[j], pv, upd)
                acc_ref[:, slabs[p]] = upd if first else acc_ref[:, slabs[p]] + upd

        return scores, suffix_sums, weights, carries

    def run_tiles(tiles):
        jobs = [(tile, h) for tile in tiles for h in range(heads)]
        for r in range(len(jobs) + WEIGHTS_LAG):
            if r < len(jobs):
                tile, h = jobs[r]
                tile[0](h)
            if SUFFIX_LAG <= r < len(jobs) + SUFFIX_LAG:
                tile, h = jobs[r - SUFFIX_LAG]
                tile[1](h)
            if r >= WEIGHTS_LAG:
                tile, h = jobs[r - WEIGHTS_LAG]
                tile[2](h)
        return jnp.min(functools.reduce(jnp.minimum, tiles[-1][3]))

    cmin = run_tiles([make_tile(pl.multiple_of(qi * blk, blk), True),
                      make_tile(pl.multiple_of(jnp.maximum(qi - 1, 0) * blk, blk), False, dead=qi == 0)])

    def cond(state):
        i, cmin = state
        return (i < qi) & (cmin < ATTN_EXIT_LOG2)

    def body(state):
        i, _ = state
        return i + 1, run_tiles([make_tile(pl.multiple_of((qi - 1 - i) * blk, blk), False)])

    lax.while_loop(cond, body, (jnp.int32(1), cmin))
    o_ref[0] = acc_ref[...].astype(o_ref.dtype)


def _attn_call(q, k, v, heads):
    bsz, seq, width = q.shape
    blk = min(ATTN_BLOCK, seq)
    r = lax.broadcasted_iota(jnp.int32, (blk, blk), 0)
    c = lax.broadcasted_iota(jnp.int32, (blk, blk), 1)
    u_tri = (r >= c).astype(BF16)
    full = pl.BlockSpec((1, seq, width), lambda b, i: (b, 0, 0))
    return pl.pallas_call(
        functools.partial(_attn_kernel, heads=heads, blk=blk),
        out_shape=jax.ShapeDtypeStruct((bsz, seq, width), BF16),
        grid=(bsz, seq // blk),
        in_specs=[pl.BlockSpec((1, blk, width), lambda b, i: (b, i, 0)), full, full,
                  pl.BlockSpec((blk, blk), lambda b, i: (0, 0))],
        out_specs=pl.BlockSpec((1, blk, width), lambda b, i: (b, i, 0)),
        scratch_shapes=[pltpu.VMEM((heads, blk, LANES), BF16), pltpu.VMEM((blk, width), F32),
                        pltpu.VMEM((heads, blk, 1), F32)],
        compiler_params=_cparams("arbitrary", "arbitrary"),
        name="attn",
    )(q, k, v, u_tri)


def _sigmoid(x):
    return 1.0 / (1.0 + jnp.exp(-x))


def _mix_kernel(ya_ref, uv_ref, gt_ref, x_ref, gtm_ref, shf_ref, scf_ref, ws_ref, bs_ref,
                wa_ref, wb_ref, wo_ref, gffn_ref, wr_ref, br_ref, utri_ref,
                h_ref, xn_ref, route_ref, route_t_ref, cnt_ref, yb_scr, cnt_scr,
                *, chunk, groups, n_groups, per_group):
    step = pl.program_id(0)
    tm = x_ref.shape[0]
    sgu = uv_ref.shape[1] // 2
    gdim = sgu // groups

    @pl.when(step == 0)
    def _():
        cnt_scr[...] = jnp.zeros_like(cnt_scr)

    ya_proj = _dot(ya_ref[...], wa_ref[...])

    u = uv_ref[:, :sgu].astype(F32)
    vn = uv_ref[:, sgu:]
    r = lax.broadcasted_iota(jnp.int32, (chunk, chunk), 0)
    c = lax.broadcasted_iota(jnp.int32, (chunk, chunk), 1)
    wcs = [jnp.where(r >= c, ws_ref[g], 0.0).astype(BF16) for g in range(groups)]
    pair = LANES // gdim
    lane = lax.broadcasted_iota(jnp.int32, (chunk, LANES), 1)
    for ci in range(tm // chunk):
        rows = slice(ci * chunk, (ci + 1) * chunk)
        for p in range(groups // pair):
            lanes = slice(p * LANES, (p + 1) * LANES)
            slab = vn[rows, lanes]
            mix = bs_ref[:, lanes]
            for j in range(pair):
                sel = (lane >= j * gdim) & (lane < (j + 1) * gdim)
                mix = mix + _dot(wcs[p * pair + j], jnp.where(sel, slab, 0.0).astype(BF16))
            yb_scr[rows, lanes] = (u[rows, lanes] * mix).astype(BF16)

    d = x_ref.shape[1]
    gates2 = gt_ref[...].astype(F32)
    merged2 = gates2[:, :d] * ya_proj + gates2[:, d:] * _dot(yb_scr[...], wb_ref[...])
    h = x_ref[...] + (0.5 * gtm_ref[0]) * _dot(merged2.astype(BF16), wo_ref[...])
    h_ref[...] = h

    xn = _rms(h, gffn_ref[...]) * (1.0 + scf_ref[0]) + shf_ref[0]
    xn_ref[...] = _pack_halves(xn)
    logits = _dot3(xn, wr_ref[...]) + br_ref[...]

    lt = logits.T
    sub = lax.broadcasted_iota(jnp.int32, (SUBLANES, tm), 0)
    neg = -jnp.inf
    gl = jnp.where(sub < n_groups, lt[0:SUBLANES], neg)
    gmax = jnp.max(gl, axis=0, keepdims=True)
    g_sel = jnp.min(jnp.where(gl == gmax, sub, SUBLANES), axis=0, keepdims=True)
    g_w = 1.0 / jnp.sum(jnp.exp(gl - gmax), axis=0, keepdims=True)
    el = lt[SUBLANES:2 * SUBLANES]
    for g in range(1, n_groups):
        el = jnp.where(g_sel == g, lt[(g + 1) * SUBLANES:(g + 2) * SUBLANES], el)
    m1 = jnp.max(el, axis=0, keepdims=True)
    j1 = jnp.min(jnp.where(el == m1, sub, SUBLANES), axis=0, keepdims=True)
    el2 = jnp.where(sub == j1, neg, el)
    m2 = jnp.max(el2, axis=0, keepdims=True)
    j2 = jnp.min(jnp.where(el2 == m2, sub, SUBLANES), axis=0, keepdims=True)
    t = jnp.exp(m2 - m1)
    w1 = g_w / (1.0 + t)
    w2 = g_w * t / (1.0 + t)
    e1 = g_sel * per_group + j1
    e2 = g_sel * per_group + j2

    row = lax.broadcasted_iota(jnp.int32, (LANES, tm), 0)
    oh1 = row == e1 + SUBLANES
    oh2 = row == e2 + SUBLANES
    both = jnp.where(oh1 | oh2, 1.0, 0.0)
    before = _dot(both.astype(BF16), utri_ref[...]) + cnt_scr[...]
    rank1 = jnp.sum(jnp.where(oh1, before, 0.0), axis=0, keepdims=True)
    rank2 = jnp.sum(jnp.where(oh2, before, 0.0), axis=0, keepdims=True)
    cnt_scr[...] = cnt_scr[...] + jnp.sum(both, axis=1, keepdims=True)
    cnt_ref[...] = cnt_scr[...]

    rec = jnp.zeros((LANES, tm), F32)
    for idx, val in enumerate((e1.astype(F32), e2.astype(F32), w1, w2, rank1, rank2)):
        rec = jnp.where(row == idx, val, rec)
    route_t_ref[...] = rec[:ROUTE_COLS]
    route_ref[...] = rec.T[:, :ROUTE_COLS]


def _mix_call(ya, uv, gates, x2, gtm, shf, scf, w_spatial, bias_full, wa, wb, wo, g_ffn, wr, br,
              seq, n_groups, per_group):
    n_tok, d = x2.shape
    tm = min(TOKEN_TILE, seq)
    per_b = seq // tm
    groups, chunk, _ = w_spatial.shape
    sgu = uv.shape[1] // 2
    r = lax.broadcasted_iota(jnp.int32, (tm, tm), 0)
    c = lax.broadcasted_iota(jnp.int32, (tm, tm), 1)
    utri = (r < c).astype(BF16)
    tok = lambda w: pl.BlockSpec((tm, w), lambda i: (i, 0))
    vec = pl.BlockSpec((1, 1, d), lambda i: (i // per_b, 0, 0))
    const = lambda a: pl.BlockSpec(a.shape, lambda i: (0,) * a.ndim)
    ins = [ya, uv, gates, x2, gtm, shf, scf, w_spatial, bias_full, wa, wb, wo, g_ffn, wr, br, utri]
    in_specs = [tok(ya.shape[1]), tok(uv.shape[1]), tok(gates.shape[1]), tok(d), vec, vec, vec]
    in_specs += [const(a) for a in ins[7:]]
    return pl.pallas_call(
        functools.partial(_mix_kernel, chunk=chunk, groups=groups, n_groups=n_groups, per_group=per_group),
        out_shape=[jax.ShapeDtypeStruct((n_tok, d), F32), jax.ShapeDtypeStruct((n_tok, d // 2), jnp.uint32),
                   jax.ShapeDtypeStruct((n_tok, ROUTE_COLS), F32), jax.ShapeDtypeStruct((ROUTE_COLS, n_tok), F32),
                   jax.ShapeDtypeStruct((LANES, 1), F32)],
        grid=(n_tok // tm,),
        in_specs=in_specs,
        out_specs=[tok(d), tok(d // 2), tok(ROUTE_COLS), pl.BlockSpec((ROUTE_COLS, tm), lambda i: (0, i)),
                   pl.BlockSpec((LANES, 1), lambda i: (0, 0))],
        scratch_shapes=[pltpu.VMEM((tm, sgu), BF16), pltpu.VMEM((LANES, 1), F32)],
        compiler_params=_cparams("arbitrary"),
        name="mix",
    )(*ins)


def _row_copy(src, s, dst, d, sem):
    return pltpu.make_async_copy(src.at[pl.ds(s, 1)], dst.at[pl.ds(d, 1)], sem)


def _invert_kernel(*refs, tm):
    dest_refs, (zeros_ref, src_ref, sem) = refs[:TOP_K], refs[TOP_K:]
    step = pl.program_id(0)

    @pl.when(step == 0)
    def _():
        fill = pltpu.make_async_copy(zeros_ref, src_ref, sem)
        fill.start()
        fill.wait()

    base = step * tm
    for j in range(tm):
        for dref in dest_refs:
            src_ref[dref[0, 0, j]] = base + j


def _dest_blocks(dest, tm):
    return [dest[k].reshape(-1, 1, tm) for k in range(TOP_K)]


def _invert_call(dest, m_pad, seq):
    n_tok = dest.shape[1]
    tm = min(ROW_COPY_TILE, seq)
    smem = pl.BlockSpec((1, 1, tm), lambda i: (i, 0, 0), memory_space=pltpu.SMEM)
    return pl.pallas_call(
        functools.partial(_invert_kernel, tm=tm),
        out_shape=jax.ShapeDtypeStruct((m_pad,), jnp.int32),
        grid=(n_tok // tm,),
        in_specs=[smem] * TOP_K + [pl.BlockSpec(memory_space=pl.ANY)],
        out_specs=pl.BlockSpec((m_pad,), lambda i: (0,), memory_space=pltpu.SMEM),
        scratch_shapes=[pltpu.SemaphoreType.DMA(())],
        compiler_params=_cparams("arbitrary"),
        name="invert",
    )(*_dest_blocks(dest, tm), jnp.zeros((m_pad,), jnp.int32))


def _expert_kernel(blk_e_ref, used_ref, tok0_ref, tok1_ref, tok2_ref, xn_ref, wg_ref, wu_ref, wd_ref, y_ref,
                   land, xb_scr, wg_bf, wu_bf, wd_bf, sem, *, rows):
    i = pl.program_id(0)
    used = used_ref[0]
    live = i < used
    slot = i % 2
    new_expert = (i == 0) | (blk_e_ref[i] != blk_e_ref[jnp.maximum(i - 1, 0)])

    def gather(tok_ref, s):
        for j in range(rows):
            _row_copy(xn_ref, tok_ref[0, 0, j], land.at[s], j, sem.at[s]).start(priority=j % 2)

    def wait_rows(s):
        pltpu.make_async_copy(xn_ref.at[pl.ds(0, rows)], land.at[s], sem.at[s]).wait()

    @pl.when(i == 0)
    def _():
        gather(tok0_ref, 0)
        gather(tok1_ref, 1)

    @pl.when(live & new_expert)
    def _():
        wg_bf[...] = wg_ref[0].astype(BF16)
        wu_bf[...] = wu_ref[0].astype(BF16)
        wd_bf[...] = wd_ref[0].astype(BF16)

    @pl.when(live)
    def _():
        wait_rows(slot)
        xb_scr[...] = _unpack_halves(land[slot]).astype(BF16)
        gather(tok2_ref, slot)
        xb = xb_scr[...]
        g = _dot(xb, wg_bf[...])
        hid = g * _sigmoid(g) * _dot(xb, wu_bf[...])
        y_ref[...] = _pack_halves(_dot(hid.astype(BF16), wd_bf[...]))

    @pl.when(jnp.logical_not(live))
    def _():
        y_ref[...] = jnp.zeros_like(y_ref)

    @pl.when((i == used) | (live & (i == pl.num_programs(0) - 1)))
    def _():
        wait_rows(0)
        wait_rows(1)


def _expert_call(blk_e, used, source, xn, wg, wu, wd):
    n_tok, dw = xn.shape
    _, d, f = wg.shape
    rows = EXPERT_ROWS
    nblk = source.shape[0] // rows
    src_blocks = source.reshape(nblk, 1, rows)
    toks = [pl.BlockSpec((1, 1, rows), lambda i, be, us, a=a: (jnp.minimum(i + a, nblk - 1), 0, 0),
                         memory_space=pltpu.SMEM) for a in range(3)]
    return pl.pallas_call(
        functools.partial(_expert_kernel, rows=rows),
        out_shape=jax.ShapeDtypeStruct((nblk * rows, dw), xn.dtype),
        grid_spec=pltpu.PrefetchScalarGridSpec(
            num_scalar_prefetch=2,
            grid=(nblk,),
            in_specs=toks + [pl.BlockSpec(memory_space=pl.ANY),
                      pl.BlockSpec((1, d, f), lambda i, be, us: (be[i], 0, 0)),
                      pl.BlockSpec((1, d, f), lambda i, be, us: (be[i], 0, 0)),
                      pl.BlockSpec((1, f, d), lambda i, be, us: (be[i], 0, 0))],
            out_specs=pl.BlockSpec((rows, dw), lambda i, be, us: (i, 0)),
            scratch_shapes=[pltpu.VMEM((2, rows, dw), xn.dtype), pltpu.VMEM((rows, d), BF16),
                            pltpu.VMEM((d, f), BF16), pltpu.VMEM((d, f), BF16), pltpu.VMEM((f, d), BF16),
                            pltpu.SemaphoreType.DMA((2,))]),
        compiler_params=_cparams("arbitrary"),
        name="experts",
    )(blk_e, used, src_blocks, src_blocks, src_blocks, xn, wg, wu, wd)


def _combine_kernel(*refs, tm, final):
    cur_refs, nxt_refs = refs[:TOP_K], refs[TOP_K:2 * TOP_K]
    h_ref, route_ref, gtf_ref, gfin_ref, y_ref, o_ref, land, rows, sem = refs[2 * TOP_K:]
    step = pl.program_id(0)

    def gather(dest_refs):
        for j in range(tm):
            for k, dref in enumerate(dest_refs):
                pltpu.make_async_copy(y_ref.at[pl.ds(dref[0, 0, j], 1)], land.at[k, pl.ds(j, 1)],
                                      sem).start(priority=k % 2)

    def wait_rows():
        for k in range(TOP_K):
            pltpu.make_async_copy(y_ref.at[pl.ds(0, tm)], land.at[k], sem).wait()

    pl.when(step == 0)(lambda: gather(cur_refs))
    wait_rows()
    rows[...] = land[...]
    gather(nxt_refs)
    route = route_ref[...]
    y = route[:, 2:3] * _unpack_halves(rows[0]) + route[:, 3:4] * _unpack_halves(rows[1])
    h = h_ref[...] + gtf_ref[0] * y
    o_ref[...] = _rms(h, gfin_ref[...]) if final else h
    pl.when(step == pl.num_programs(0) - 1)(wait_rows)


def _combine_call(dest, h, route, gtf, g_final, yb, seq, final):
    n_tok, d = h.shape
    tm = min(ROW_COPY_TILE, seq)
    per_b = seq // tm
    steps = n_tok // tm
    tok = lambda w: pl.BlockSpec((tm, w), lambda i: (i, 0))
    cur = pl.BlockSpec((1, 1, tm), lambda i: (i, 0, 0), memory_space=pltpu.SMEM)
    nxt = pl.BlockSpec((1, 1, tm), lambda i: (jnp.minimum(i + 1, steps - 1), 0, 0), memory_space=pltpu.SMEM)
    blocks = _dest_blocks(dest, tm)
    return pl.pallas_call(
        functools.partial(_combine_kernel, tm=tm, final=final),
        out_shape=jax.ShapeDtypeStruct((n_tok, d), F32),
        grid=(steps,),
        in_specs=[cur] * TOP_K + [nxt] * TOP_K + [
            tok(d), tok(ROUTE_COLS),
            pl.BlockSpec((1, 1, d), lambda i: (i // per_b, 0, 0)),
            pl.BlockSpec((1, d), lambda i: (0, 0)),
            pl.BlockSpec(memory_space=pl.ANY)],
        out_specs=tok(d),
        scratch_shapes=[pltpu.VMEM((TOP_K, tm, yb.shape[1]), yb.dtype),
                        pltpu.VMEM((TOP_K, tm, yb.shape[1]), yb.dtype), pltpu.SemaphoreType.DMA(())],
        compiler_params=_cparams("arbitrary"),
        name="combine",
    )(*blocks, *blocks, h, route, gtf, g_final, yb)


def _routing_plan(route_t, counts, n_experts, m_pad):
    eid = route_t[0:TOP_K].astype(jnp.int32)
    rank = route_t[4:4 + TOP_K].astype(jnp.int32)
    cnt = counts.astype(jnp.int32)
    padded = (cnt + EXPERT_ROWS - 1) // EXPERT_ROWS * EXPERT_ROWS
    pend = jnp.cumsum(padded)
    pstart = pend - padded
    dest = rank
    for e in range(n_experts):
        dest = dest + jnp.where(eid == e, pstart[e], 0)
    nblk = m_pad // EXPERT_ROWS
    blk_row = jnp.arange(nblk, dtype=jnp.int32) * EXPERT_ROWS
    blk_e = jnp.minimum(jnp.sum(pend[None, :] <= blk_row[:, None], axis=1), n_experts - 1).astype(jnp.int32)
    used = (pend[-1:] // EXPERT_ROWS).astype(jnp.int32)
    return dest, blk_e, used


def kernel(x, c, g_mix, g_ffn, w_ada, b_ada, w_in, w_sba_out, g_sgu, w_spatial, b_spatial, w_sgu_out, w_out,
           w_router_group, b_router_group, w_router_expert, b_router_expert, w_expert_gate, w_expert_up,
           w_expert_down, g_final):
    bsz, seq, d = x.shape
    depth = w_in.shape[0]
    sba = w_sba_out.shape[1]
    sgu = g_sgu.shape[1]
    heads = sba // HEAD_DIM
    groups, chunk = w_spatial.shape[1], w_spatial.shape[2]
    n_groups = w_router_group.shape[2]
    n_experts = w_router_expert.shape[2]
    per_group = n_experts // n_groups
    n_tok = bsz * seq
    m_pad = n_tok * TOP_K + n_experts * EXPERT_ROWS
    widths = (sba, sba, sba, 2 * sgu, 2 * d)
    assert seq % chunk == 0 and LANES % (sgu // groups) == 0
    assert d % (2 * LANES) == 0
    assert per_group == SUBLANES and n_groups <= SUBLANES and (n_groups + 1) * SUBLANES <= LANES

    h = x.reshape(n_tok, d)
    for l in range(depth):
        mod = _mod_call(c, w_ada[l], b_ada[l])
        sh_m, sc_m, gt_m, sh_f, sc_f, gt_f = [mod[:, i * d:(i + 1) * d].reshape(bsz, 1, d) for i in range(6)]

        q, k, v, uv, gates = _proj_call(h, sh_m, sc_m, g_mix[l].reshape(1, d), g_sgu[l].reshape(1, sgu),
                                        w_in[l].astype(BF16), seq, widths)
        ya = _attn_call(q.reshape(bsz, seq, sba), k.reshape(bsz, seq, sba), v.reshape(bsz, seq, sba), heads)

        bias_full = jnp.repeat(b_spatial[l].T, sgu // groups, axis=1)
        gpad = SUBLANES - n_groups
        wr = jnp.concatenate([jnp.pad(w_router_group[l], ((0, 0), (0, gpad))), w_router_expert[l]], axis=1)
        wr = jnp.pad(wr, ((0, 0), (0, LANES - wr.shape[1])))
        br = jnp.concatenate([jnp.pad(b_router_group[l], (0, gpad)), b_router_expert[l]])
        br = jnp.pad(br, (0, LANES - br.shape[0])).reshape(1, LANES)
        h1, xn, route, route_t, counts = _mix_call(
            ya.reshape(n_tok, sba), uv, gates, h, gt_m, sh_f, sc_f, w_spatial[l],
            bias_full, w_sba_out[l].astype(BF16), w_sgu_out[l].astype(BF16), w_out[l].astype(BF16),
            g_ffn[l].reshape(1, d), wr, br, seq, n_groups, per_group)

        dest, blk_e, used = _routing_plan(route_t, counts[SUBLANES:SUBLANES + n_experts, 0], n_experts, m_pad)
        source = _invert_call(dest, m_pad, seq)
        yb = _expert_call(blk_e, used, source, xn, w_expert_gate[l], w_expert_up[l], w_expert_down[l])
        h = _combine_call(dest, h1, route, gt_f, g_final.reshape(1, d), yb, seq, final=l == depth - 1)
    return h.reshape(bsz, seq, d)
```

```python
import functools

import jax
import jax.numpy as jnp
from jax import lax
from jax.experimental import pallas as pl
from jax.experimental.pallas import tpu as pltpu

F32 = jnp.float32
BF16 = jnp.bfloat16

HEAD_DIM = 64
TOP_K = 2
EPS = 1e-6
LOG2E = 1.4426950408889634
SIGN_BIT = 0x80000000
BF16_BITS = 0xFFFF0000
ATTN_EXIT_LOG2 = 160.0
DEAD_CARRY = 1e30
SUFFIX_LAG = 2
WEIGHTS_LAG = 4
LANES = 128
SUBLANES = 8
ATTN_BLOCK = 256
EXPERT_ROWS = 512
TOKEN_TILE = 1024
ROW_COPY_TILE = 512
PROJ_TILE = 512
ROUTE_COLS = 8
VMEM_LIMIT = 56 * 1024 * 1024


def _cparams(*sem):
    return pltpu.CompilerParams(dimension_semantics=sem, vmem_limit_bytes=VMEM_LIMIT)


def _split_bf16(a):
    hi = a.astype(BF16)
    lo = (a - hi.astype(F32)).astype(BF16)
    return hi, lo


def _dot(a, b):
    return jnp.dot(a, b, preferred_element_type=F32)


def _dot3(a, b):
    ah, al = _split_bf16(a)
    bh, bl = _split_bf16(b)
    return _dot(ah, bh) + _dot(ah, bl) + _dot(al, bh)


def _pack_halves(x):
    half = x.shape[1] // 2
    hi = lax.bitcast_convert_type(x[:, :half].astype(BF16).astype(F32), jnp.uint32)
    lo = lax.bitcast_convert_type(x[:, half:].astype(BF16).astype(F32), jnp.uint32)
    return hi | (lo >> 16)


def _unpack_halves(u):
    hi = lax.bitcast_convert_type(u & jnp.uint32(BF16_BITS), F32)
    lo = lax.bitcast_convert_type(u << 16, F32)
    return jnp.concatenate([hi, lo], axis=1)


def _rms(x, g):
    ms = jnp.mean(x * x, axis=-1, keepdims=True)
    return x * lax.rsqrt(ms + EPS) * g


def _mod_kernel(c_ref, w_ref, b_ref, o_ref):
    c = c_ref[...]
    ca = c * (1.0 / (1.0 + jnp.exp(-c)))
    o_ref[...] = _dot3(ca, w_ref[...]) + b_ref[...]


def _mod_call(c, w_ada, b_ada):
    bsz, d = c.shape
    n = w_ada.shape[1]
    tn = n // 4 if n % (4 * LANES) == 0 else n
    return pl.pallas_call(
        _mod_kernel,
        out_shape=jax.ShapeDtypeStruct((bsz, n), F32),
        grid=(n // tn,),
        in_specs=[pl.BlockSpec((bsz, d), lambda j: (0, 0)),
                  pl.BlockSpec((d, tn), lambda j: (0, j)),
                  pl.BlockSpec((1, tn), lambda j: (0, j))],
        out_specs=pl.BlockSpec((bsz, tn), lambda j: (0, j)),
        compiler_params=_cparams("arbitrary"),
        name="mod",
    )(c, w_ada, b_ada.reshape(1, n))


def _gelu_tanh(x):
    c = 0.7978845608028654
    hx = 0.5 * x
    return hx + hx * jnp.tanh(x * (c + (0.044715 * c) * (x * x)))


def _proj_kernel(x_ref, sh_ref, sc_ref, g_ref, gsgu_ref, w_ref, q_ref, k_ref, v_ref, uv_ref, gt_ref, *, widths):
    x = x_ref[...]
    n = _rms(x, g_ref[...]) * (1.0 + sc_ref[0]) + sh_ref[0]
    nb = n.astype(BF16)
    off = 0
    for ref, wd in zip((q_ref, k_ref, v_ref, uv_ref, gt_ref), widths):
        p = _dot(nb, w_ref[:, off:off + wd])
        if ref is q_ref:
            p = p * (HEAD_DIM ** -0.5 * LOG2E)
        elif ref is uv_ref:
            act = _gelu_tanh(p)
            sgu = wd // 2
            v = act[:, sgu:]
            vc = v - jnp.mean(v, axis=-1, keepdims=True)
            var = jnp.mean(vc * vc, axis=-1, keepdims=True)
            p = jnp.concatenate([act[:, :sgu], vc * lax.rsqrt(var + EPS) * gsgu_ref[...]], axis=1)
        elif ref is gt_ref:
            p = 1.0 + jnp.tanh(0.5 * p)
        ref[...] = p.astype(ref.dtype)
        off += wd


def _proj_call(x2, sh, sc, g, g_sgu, w_in_bf, seq, widths):
    n_tok, d = x2.shape
    tm = min(PROJ_TILE, seq)
    per_b = seq // tm
    cols = w_in_bf.shape[1]
    tok = lambda w: pl.BlockSpec((tm, w), lambda i: (i, 0))
    vec = pl.BlockSpec((1, 1, d), lambda i: (i // per_b, 0, 0))
    return pl.pallas_call(
        functools.partial(_proj_kernel, widths=widths),
        out_shape=[jax.ShapeDtypeStruct((n_tok, w), BF16) for w in widths],
        grid=(n_tok // tm,),
        in_specs=[tok(d), vec, vec,
                  pl.BlockSpec((1, d), lambda i: (0, 0)),
                  pl.BlockSpec(g_sgu.shape, lambda i: (0, 0)),
                  pl.BlockSpec((d, cols), lambda i: (0, 0))],
        out_specs=[tok(w) for w in widths],
        compiler_params=_cparams("arbitrary"),
        name="proj",
    )(x2, sh, sc, g, g_sgu, w_in_bf)


def _attn_kernel(q_ref, k_ref, v_ref, u_ref, o_ref, qm_ref, acc_ref, carry_ref, *, heads, blk):
    qi = pl.program_id(1)
    u_tri = u_ref[...]
    row = lax.broadcasted_iota(jnp.int32, (blk, blk), 0)
    col = lax.broadcasted_iota(jnp.int32, (blk, blk), 1)
    causal = col < row
    per_slab = LANES // HEAD_DIM
    lane = lax.broadcasted_iota(jnp.int32, (blk, LANES), 1)
    own = [(lane >= j * HEAD_DIM) & (lane < (j + 1) * HEAD_DIM) for j in range(per_slab)]
    for h in range(heads):
        slab = slice(h // per_slab * LANES, (h // per_slab + 1) * LANES)
        qm_ref[h] = jnp.where(own[h % per_slab], q_ref[0, :, slab], 0.0).astype(BF16)

    slabs = [slice(p * LANES, (p + 1) * LANES) for p in range(heads // per_slab)]

    def make_tile(start, first, dead=None):
        mask = causal if first else None
        s_all, cum_all, w_all, carries = {}, {}, {}, []

        def scores(h):
            s_all[h] = lax.dot_general(qm_ref[h], k_ref[0, pl.ds(start, blk), slabs[h // per_slab]],
                                       (((1,), (1,)), ((), ())), preferred_element_type=F32)

        def suffix_sums(h):
            s = s_all[h]
            neg_abs = lax.bitcast_convert_type(lax.bitcast_convert_type(s, jnp.uint32) | jnp.uint32(SIGN_BIT), F32)
            sp = jnp.maximum(s, 0.0) + jnp.log(1.0 + jnp.exp2(neg_abs)) * LOG2E
            if mask is not None:
                sp = jnp.where(mask, sp, 0.0)
            hi = lax.bitcast_convert_type(lax.bitcast_convert_type(sp, jnp.uint32) & jnp.uint32(BF16_BITS), F32)
            cum_all[h] = _dot(hi.astype(BF16), u_tri) + _dot((sp - hi).astype(BF16), u_tri)

        def weights(h):
            cum = cum_all[h]
            if first:
                w = jnp.exp2(s_all[h] - cum)
                carry = cum[:, 0:1]
            else:
                carry = carry_ref[h]
                if dead is not None:
                    carry = jnp.where(dead, DEAD_CARRY, carry)
                w = jnp.exp2((s_all[h] - carry) - cum)
                carry = carry + cum[:, 0:1]
            w_all[h] = (w if mask is None else jnp.where(mask, w, 0.0)).astype(BF16)
            carry_ref[h] = carry
            carries.append(carry)
            if h % per_slab == per_slab - 1:
                p = h // per_slab
                vp = v_ref[0, pl.ds(start, blk), slabs[p]]
                upd = None
                for j in range(per_slab):
                    pv = _dot(w_all[p * per_slab + j], vp)
                    upd = pv if upd is None else jnp.where(own[j], pv, upd)
                acc_ref[:, slabs[p]] = upd if first else acc_ref[:, slabs[p]] + upd

        return scores, suffix_sums, weights, carries

    def run_tiles(tiles):
        jobs = [(tile, h) for tile in tiles for h in range(heads)]
        for r in range(len(jobs) + WEIGHTS_LAG):
            if r < len(jobs):
                tile, h = jobs[r]
                tile[0](h)
            if SUFFIX_LAG <= r < len(jobs) + SUFFIX_LAG:
                tile, h = jobs[r - SUFFIX_LAG]
                tile[1](h)
            if r >= WEIGHTS_LAG:
                tile, h = jobs[r - WEIGHTS_LAG]
                tile[2](h)
        return jnp.min(functools.reduce(jnp.minimum, tiles[-1][3]))

    cmin = run_tiles([make_tile(pl.multiple_of(qi * blk, blk), True),
                      make_tile(pl.multiple_of(jnp.maximum(qi - 1, 0) * blk, blk), False, dead=qi == 0)])

    def cond(state):
        i, cmin = state
        return (i < qi) & (cmin < ATTN_EXIT_LOG2)

    def body(state):
        i, _ = state
        return i + 1, run_tiles([make_tile(pl.multiple_of((qi - 1 - i) * blk, blk), False)])

    lax.while_loop(cond, body, (jnp.int32(1), cmin))
    o_ref[0] = acc_ref[...].astype(o_ref.dtype)


def _attn_call(q, k, v, heads):
    bsz, seq, width = q.shape
    blk = min(ATTN_BLOCK, seq)
    r = lax.broadcasted_iota(jnp.int32, (blk, blk), 0)
    c = lax.broadcasted_iota(jnp.int32, (blk, blk), 1)
    u_tri = (r >= c).astype(BF16)
    full = pl.BlockSpec((1, seq, width), lambda b, i: (b, 0, 0))
    return pl.pallas_call(
        functools.partial(_attn_kernel, heads=heads, blk=blk),
        out_shape=jax.ShapeDtypeStruct((bsz, seq, width), BF16),
        grid=(bsz, seq // blk),
        in_specs=[pl.BlockSpec((1, blk, width), lambda b, i: (b, i, 0)), full, full,
                  pl.BlockSpec((blk, blk), lambda b, i: (0, 0))],
        out_specs=pl.BlockSpec((1, blk, width), lambda b, i: (b, i, 0)),
        scratch_shapes=[pltpu.VMEM((heads, blk, LANES), BF16), pltpu.VMEM((blk, width), F32),
                        pltpu.VMEM((heads, blk, 1), F32)],
        compiler_params=_cparams("arbitrary", "arbitrary"),
        name="attn",
    )(q, k, v, u_tri)


def _sigmoid(x):
    return 1.0 / (1.0 + jnp.exp(-x))


def _mix_kernel(ya_ref, uv_ref, gt_ref, x_ref, gtm_ref, shf_ref, scf_ref, ws_ref, bs_ref,
                wa_ref, wb_ref, wo_ref, gffn_ref, wr_ref, br_ref, utri_ref,
                h_ref, xn_ref, route_ref, route_t_ref, cnt_ref, yb_scr, cnt_scr,
                *, chunk, groups, n_groups, per_group):
    step = pl.program_id(0)
    tm = x_ref.shape[0]
    sgu = uv_ref.shape[1] // 2
    gdim = sgu // groups

    @pl.when(step == 0)
    def _():
        cnt_scr[...] = jnp.zeros_like(cnt_scr)

    ya_proj = _dot(ya_ref[...], wa_ref[...])

    u = uv_ref[:, :sgu].astype(F32)
    vn = uv_ref[:, sgu:]
    r = lax.broadcasted_iota(jnp.int32, (chunk, chunk), 0)
    c = lax.broadcasted_iota(jnp.int32, (chunk, chunk), 1)
    wcs = [jnp.where(r >= c, ws_ref[g], 0.0).astype(BF16) for g in range(groups)]
    pair = LANES // gdim
    lane = lax.broadcasted_iota(jnp.int32, (chunk, LANES), 1)
    for ci in range(tm // chunk):
        rows = slice(ci * chunk, (ci + 1) * chunk)
        for p in range(groups // pair):
            lanes = slice(p * LANES, (p + 1) * LANES)
            slab = vn[rows, lanes]
            mix = bs_ref[:, lanes]
            for j in range(pair):
                sel = (lane >= j * gdim) & (lane < (j + 1) * gdim)
                mix = mix + _dot(wcs[p * pair + j], jnp.where(sel, slab, 0.0).astype(BF16))
            yb_scr[rows, lanes] = (u[rows, lanes] * mix).astype(BF16)

    d = x_ref.shape[1]
    gates2 = gt_ref[...].astype(F32)
    merged2 = gates2[:, :d] * ya_proj + gates2[:, d:] * _dot(yb_scr[...], wb_ref[...])
    h = x_ref[...] + (0.5 * gtm_ref[0]) * _dot(merged2.astype(BF16), wo_ref[...])
    h_ref[...] = h

    xn = _rms(h, gffn_ref[...]) * (1.0 + scf_ref[0]) + shf_ref[0]
    xn_ref[...] = _pack_halves(xn)
    logits = _dot3(xn, wr_ref[...]) + br_ref[...]

    lt = logits.T
    sub = lax.broadcasted_iota(jnp.int32, (SUBLANES, tm), 0)
    neg = -jnp.inf
    gl = jnp.where(sub < n_groups, lt[0:SUBLANES], neg)
    gmax = jnp.max(gl, axis=0, keepdims=True)
    g_sel = jnp.min(jnp.where(gl == gmax, sub, SUBLANES), axis=0, keepdims=True)
    g_w = 1.0 / jnp.sum(jnp.exp(gl - gmax), axis=0, keepdims=True)
    el = lt[SUBLANES:2 * SUBLANES]
    for g in range(1, n_groups):
        el = jnp.where(g_sel == g, lt[(g + 1) * SUBLANES:(g + 2) * SUBLANES], el)
    m1 = jnp.max(el, axis=0, keepdims=True)
    j1 = jnp.min(jnp.where(el == m1, sub, SUBLANES), axis=0, keepdims=True)
    el2 = jnp.where(sub == j1, neg, el)
    m2 = jnp.max(el2, axis=0, keepdims=True)
    j2 = jnp.min(jnp.where(el2 == m2, sub, SUBLANES), axis=0, keepdims=True)
    t = jnp.exp(m2 - m1)
    w1 = g_w / (1.0 + t)
    w2 = g_w * t / (1.0 + t)
    e1 = g_sel * per_group + j1
    e2 = g_sel * per_group + j2

    row = lax.broadcasted_iota(jnp.int32, (LANES, tm), 0)
    oh1 = row == e1 + SUBLANES
    oh2 = row == e2 + SUBLANES
    both = jnp.where(oh1 | oh2, 1.0, 0.0)
    before = _dot(both.astype(BF16), utri_ref[...]) + cnt_scr[...]
    rank1 = jnp.sum(jnp.where(oh1, before, 0.0), axis=0, keepdims=True)
    rank2 = jnp.sum(jnp.where(oh2, before, 0.0), axis=0, keepdims=True)
    cnt_scr[...] = cnt_scr[...] + jnp.sum(both, axis=1, keepdims=True)
    cnt_ref[...] = cnt_scr[...]

    rec = jnp.zeros((LANES, tm), F32)
    for idx, val in enumerate((e1.astype(F32), e2.astype(F32), w1, w2, rank1, rank2)):
        rec = jnp.where(row == idx, val, rec)
    route_t_ref[...] = rec[:ROUTE_COLS]
    route_ref[...] = rec.T[:, :ROUTE_COLS]


def _mix_call(ya, uv, gates, x2, gtm, shf, scf, w_spatial, bias_full, wa, wb, wo, g_ffn, wr, br,
              seq, n_groups, per_group):
    n_tok, d = x2.shape
    tm = min(TOKEN_TILE, seq)
    per_b = seq // tm
    groups, chunk, _ = w_spatial.shape
    sgu = uv.shape[1] // 2
    r = lax.broadcasted_iota(jnp.int32, (tm, tm), 0)
    c = lax.broadcasted_iota(jnp.int32, (tm, tm), 1)
    utri = (r < c).astype(BF16)
    tok = lambda w: pl.BlockSpec((tm, w), lambda i: (i, 0))
    vec = pl.BlockSpec((1, 1, d), lambda i: (i // per_b, 0, 0))
    const = lambda a: pl.BlockSpec(a.shape, lambda i: (0,) * a.ndim)
    ins = [ya, uv, gates, x2, gtm, shf, scf, w_spatial, bias_full, wa, wb, wo, g_ffn, wr, br, utri]
    in_specs = [tok(ya.shape[1]), tok(uv.shape[1]), tok(gates.shape[1]), tok(d), vec, vec, vec]
    in_specs += [const(a) for a in ins[7:]]
    return pl.pallas_call(
        functools.partial(_mix_kernel, chunk=chunk, groups=groups, n_groups=n_groups, per_group=per_group),
        out_shape=[jax.ShapeDtypeStruct((n_tok, d), F32), jax.ShapeDtypeStruct((n_tok, d // 2), jnp.uint32),
                   jax.ShapeDtypeStruct((n_tok, ROUTE_COLS), F32), jax.ShapeDtypeStruct((ROUTE_COLS, n_tok), F32),
                   jax.ShapeDtypeStruct((LANES, 1), F32)],
        grid=(n_tok // tm,),
        in_specs=in_specs,
        out_specs=[tok(d), tok(d // 2), tok(ROUTE_COLS), pl.BlockSpec((ROUTE_COLS, tm), lambda i: (0, i)),
                   pl.BlockSpec((LANES, 1), lambda i: (0, 0))],
        scratch_shapes=[pltpu.VMEM((tm, sgu), BF16), pltpu.VMEM((LANES, 1), F32)],
        compiler_params=_cparams("arbitrary"),
        name="mix",
    )(*ins)


def _row_copy(src, s, dst, d, sem):
    return pltpu.make_async_copy(src.at[pl.ds(s, 1)], dst.at[pl.ds(d, 1)], sem)


def _dispatch_kernel(zblk_ref, *refs, tm, rows):
    dest_refs, (xn_ref, xs_ref, zero_scr, src_scr, sem, zsem) = refs[:TOP_K], refs[TOP_K:]
    step = pl.program_id(0)

    def wait_rows():
        for _ in range(TOP_K):
            pltpu.make_async_copy(src_scr, xs_ref.at[pl.ds(0, tm)], sem).wait()

    @pl.when(step == 0)
    def _():
        zero_scr[...] = jnp.zeros_like(zero_scr)

        def zero_copy(j):
            start = pl.multiple_of(zblk_ref[j] * rows, rows)
            return pltpu.make_async_copy(zero_scr, xs_ref.at[pl.ds(start, rows)], zsem)

        for j in range(zblk_ref.shape[0]):
            pl.when(zblk_ref[j] >= 0)(lambda j=j: zero_copy(j).start())
        for j in range(zblk_ref.shape[0]):
            pl.when(zblk_ref[j] >= 0)(lambda j=j: zero_copy(j).wait())

    pl.when(step > 0)(wait_rows)
    src_scr[...] = xn_ref[...]
    for j in range(tm):
        for k, dref in enumerate(dest_refs):
            _row_copy(src_scr, j, xs_ref, dref[0, 0, j], sem).start(priority=k % 2)
    pl.when(step == pl.num_programs(0) - 1)(wait_rows)


def _dest_blocks(dest, tm):
    return [dest[k].reshape(-1, 1, tm) for k in range(TOP_K)]


def _dispatch_call(zero_blocks, dest, xn, m_pad, seq):
    n_tok, dw = xn.shape
    tm = min(ROW_COPY_TILE, seq)
    smem = pl.BlockSpec((1, 1, tm), lambda i, zb: (i, 0, 0), memory_space=pltpu.SMEM)
    return pl.pallas_call(
        functools.partial(_dispatch_kernel, tm=tm, rows=EXPERT_ROWS),
        out_shape=jax.ShapeDtypeStruct((m_pad, dw), xn.dtype),
        grid_spec=pltpu.PrefetchScalarGridSpec(
            num_scalar_prefetch=1,
            grid=(n_tok // tm,),
            in_specs=[smem] * TOP_K + [pl.BlockSpec((tm, dw), lambda i, zb: (i, 0))],
            out_specs=pl.BlockSpec(memory_space=pl.ANY),
            scratch_shapes=[pltpu.VMEM((EXPERT_ROWS, dw), xn.dtype), pltpu.VMEM((tm, dw), xn.dtype),
                            pltpu.SemaphoreType.DMA(()), pltpu.SemaphoreType.DMA(())]),
        compiler_params=_cparams("arbitrary"),
        name="dispatch",
    )(zero_blocks, *_dest_blocks(dest, tm), xn)


def _expert_kernel(blk_e_ref, used_ref, xs_ref, wg_ref, wu_ref, wd_ref, y_ref, wg_bf, wu_bf, wd_bf):
    i = pl.program_id(0)
    live = i < used_ref[0]
    new_expert = (i == 0) | (blk_e_ref[i] != blk_e_ref[jnp.maximum(i - 1, 0)])

    @pl.when(live & new_expert)
    def _():
        wg_bf[...] = wg_ref[0].astype(BF16)
        wu_bf[...] = wu_ref[0].astype(BF16)
        wd_bf[...] = wd_ref[0].astype(BF16)

    @pl.when(live)
    def _():
        xb = _unpack_halves(xs_ref[...]).astype(BF16)
        g = _dot(xb, wg_bf[...])
        hid = g * _sigmoid(g) * _dot(xb, wu_bf[...])
        y = _pack_halves(_dot(hid.astype(BF16), wd_bf[...]))
        for c in range(y_ref.shape[1]):
            y_ref[:, c, :] = y[:, c * LANES:(c + 1) * LANES]

    @pl.when(jnp.logical_not(live))
    def _():
        y_ref[...] = jnp.zeros_like(y_ref)


def _expert_call(blk_e, used, xs, wg, wu, wd):
    m_pad, dw = xs.shape
    _, d, f = wg.shape
    rows = EXPERT_ROWS
    return pl.pallas_call(
        _expert_kernel,
        out_shape=jax.ShapeDtypeStruct((m_pad, dw // LANES, LANES), xs.dtype),
        grid_spec=pltpu.PrefetchScalarGridSpec(
            num_scalar_prefetch=2,
            grid=(m_pad // rows,),
            in_specs=[pl.BlockSpec((rows, dw), lambda i, be, us: (i, 0)),
                      pl.BlockSpec((1, d, f), lambda i, be, us: (be[i], 0, 0)),
                      pl.BlockSpec((1, d, f), lambda i, be, us: (be[i], 0, 0)),
                      pl.BlockSpec((1, f, d), lambda i, be, us: (be[i], 0, 0))],
            out_specs=pl.BlockSpec((rows, dw // LANES, LANES), lambda i, be, us: (i, 0, 0)),
            scratch_shapes=[pltpu.VMEM((d, f), BF16), pltpu.VMEM((d, f), BF16), pltpu.VMEM((f, d), BF16)]),
        compiler_params=_cparams("arbitrary"),
        name="experts",
    )(blk_e, used, xs, wg, wu, wd)


def _combine_kernel(*refs, tm, final):
    cur_refs, nxt_refs = refs[:TOP_K], refs[TOP_K:2 * TOP_K]
    h_ref, route_ref, gtf_ref, gfin_ref, y_ref, o_ref, land, rows, sem = refs[2 * TOP_K:]
    step = pl.program_id(0)

    def gather(dest_refs):
        for j in range(tm):
            for k, dref in enumerate(dest_refs):
                pltpu.make_async_copy(y_ref.at[dref[0, 0, j]], land.at[k, j],
                                      sem).start(priority=k % 2)

    def wait_rows():
        for k in range(TOP_K):
            pltpu.make_async_copy(y_ref.at[pl.ds(0, tm)], land.at[k], sem).wait()

    pl.when(step == 0)(lambda: gather(cur_refs))
    wait_rows()
    rows[...] = land[...]
    gather(nxt_refs)
    route = route_ref[...]
    flat = [jnp.concatenate([rows[k, :, c, :] for c in range(rows.shape[2])], axis=1) for k in range(TOP_K)]
    y = route[:, 2:3] * _unpack_halves(flat[0]) + route[:, 3:4] * _unpack_halves(flat[1])
    h = h_ref[...] + gtf_ref[0] * y
    o_ref[...] = _rms(h, gfin_ref[...]) if final else h
    pl.when(step == pl.num_programs(0) - 1)(wait_rows)


def _combine_call(dest, h, route, gtf, g_final, yb, seq, final):
    n_tok, d = h.shape
    tm = min(ROW_COPY_TILE, seq)
    per_b = seq // tm
    steps = n_tok // tm
    tok = lambda w: pl.BlockSpec((tm, w), lambda i: (i, 0))
    cur = pl.BlockSpec((1, 1, tm), lambda i: (i, 0, 0), memory_space=pltpu.SMEM)
    nxt = pl.BlockSpec((1, 1, tm), lambda i: (jnp.minimum(i + 1, steps - 1), 0, 0), memory_space=pltpu.SMEM)
    blocks = _dest_blocks(dest, tm)
    return pl.pallas_call(
        functools.partial(_combine_kernel, tm=tm, final=final),
        out_shape=jax.ShapeDtypeStruct((n_tok, d), F32),
        grid=(steps,),
        in_specs=[cur] * TOP_K + [nxt] * TOP_K + [
            tok(d), tok(ROUTE_COLS),
            pl.BlockSpec((1, 1, d), lambda i: (i // per_b, 0, 0)),
            pl.BlockSpec((1, d), lambda i: (0, 0)),
            pl.BlockSpec(memory_space=pl.ANY)],
        out_specs=tok(d),
        scratch_shapes=[pltpu.VMEM((TOP_K, tm) + yb.shape[1:], yb.dtype),
                        pltpu.VMEM((TOP_K, tm) + yb.shape[1:], yb.dtype), pltpu.SemaphoreType.DMA(())],
        compiler_params=_cparams("arbitrary"),
        name="combine",
    )(*blocks, *blocks, h, route, gtf, g_final, yb)


def _routing_plan(route_t, counts, n_experts, m_pad):
    eid = route_t[0:TOP_K].astype(jnp.int32)
    rank = route_t[4:4 + TOP_K].astype(jnp.int32)
    cnt = counts.astype(jnp.int32)
    padded = (cnt + EXPERT_ROWS - 1) // EXPERT_ROWS * EXPERT_ROWS
    pend = jnp.cumsum(padded)
    pstart = pend - padded
    dest = rank
    for e in range(n_experts):
        dest = dest + jnp.where(eid == e, pstart[e], 0)
    nblk = m_pad // EXPERT_ROWS
    blk_row = jnp.arange(nblk, dtype=jnp.int32) * EXPERT_ROWS
    blk_e = jnp.minimum(jnp.sum(pend[None, :] <= blk_row[:, None], axis=1), n_experts - 1).astype(jnp.int32)
    used = (pend[-1:] // EXPERT_ROWS).astype(jnp.int32)
    last_blk = jnp.where(padded > 0, pend // EXPERT_ROWS - 1, -1)
    tail_blk = used + jnp.arange(n_experts, dtype=jnp.int32)
    tail_blk = jnp.where(tail_blk < nblk, tail_blk, -1)
    zero_blocks = jnp.concatenate([last_blk, tail_blk]).astype(jnp.int32)
    return dest, blk_e, used, zero_blocks


def kernel(x, c, g_mix, g_ffn, w_ada, b_ada, w_in, w_sba_out, g_sgu, w_spatial, b_spatial, w_sgu_out, w_out,
           w_router_group, b_router_group, w_router_expert, b_router_expert, w_expert_gate, w_expert_up,
           w_expert_down, g_final):
    bsz, seq, d = x.shape
    depth = w_in.shape[0]
    sba = w_sba_out.shape[1]
    sgu = g_sgu.shape[1]
    heads = sba // HEAD_DIM
    groups, chunk = w_spatial.shape[1], w_spatial.shape[2]
    n_groups = w_router_group.shape[2]
    n_experts = w_router_expert.shape[2]
    per_group = n_experts // n_groups
    n_tok = bsz * seq
    m_pad = n_tok * TOP_K + n_experts * EXPERT_ROWS
    widths = (sba, sba, sba, 2 * sgu, 2 * d)
    assert seq % chunk == 0 and LANES % (sgu // groups) == 0
    assert d % (2 * LANES) == 0
    assert per_group == SUBLANES and n_groups <= SUBLANES and (n_groups + 1) * SUBLANES <= LANES

    h = x.reshape(n_tok, d)
    for l in range(depth):
        mod = _mod_call(c, w_ada[l], b_ada[l])
        sh_m, sc_m, gt_m, sh_f, sc_f, gt_f = [mod[:, i * d:(i + 1) * d].reshape(bsz, 1, d) for i in range(6)]

        q, k, v, uv, gates = _proj_call(h, sh_m, sc_m, g_mix[l].reshape(1, d), g_sgu[l].reshape(1, sgu),
                                        w_in[l].astype(BF16), seq, widths)
        ya = _attn_call(q.reshape(bsz, seq, sba), k.reshape(bsz, seq, sba), v.reshape(bsz, seq, sba), heads)

        bias_full = jnp.repeat(b_spatial[l].T, sgu // groups, axis=1)
        gpad = SUBLANES - n_groups
        wr = jnp.concatenate([jnp.pad(w_router_group[l], ((0, 0), (0, gpad))), w_router_expert[l]], axis=1)
        wr = jnp.pad(wr, ((0, 0), (0, LANES - wr.shape[1])))
        br = jnp.concatenate([jnp.pad(b_router_group[l], (0, gpad)), b_router_expert[l]])
        br = jnp.pad(br, (0, LANES - br.shape[0])).reshape(1, LANES)
        h1, xn, route, route_t, counts = _mix_call(
            ya.reshape(n_tok, sba), uv, gates, h, gt_m, sh_f, sc_f, w_spatial[l],
            bias_full, w_sba_out[l].astype(BF16), w_sgu_out[l].astype(BF16), w_out[l].astype(BF16),
            g_ffn[l].reshape(1, d), wr, br, seq, n_groups, per_group)

        dest, blk_e, used, zero_blocks = _routing_plan(route_t, counts[SUBLANES:SUBLANES + n_experts, 0],
                                                       n_experts, m_pad)
        xs = _dispatch_call(zero_blocks, dest, xn, m_pad, seq)
        yb = _expert_call(blk_e, used, xs, w_expert_gate[l], w_expert_up[l], w_expert_down[l])
        h = _combine_call(dest, h1, route, gt_f, g_final.reshape(1, d), yb, seq, final=l == depth - 1)
    return h.reshape(bsz, seq, d)
```

```python
import functools

import jax
import jax.numpy as jnp
from jax import lax
from jax.experimental import pallas as pl
from jax.experimental.pallas import tpu as pltpu

F32 = jnp.float32
BF16 = jnp.bfloat16

HEAD_DIM = 64
TOP_K = 2
EPS = 1e-6
LOG2E = 1.4426950408889634
SIGN_BIT = 0x80000000
BF16_BITS = 0xFFFF0000
ATTN_EXIT_LOG2 = 160.0
DEAD_CARRY = 1e30
SUFFIX_LAG = 2
WEIGHTS_LAG = 4
LANES = 128
SUBLANES = 8
ATTN_BLOCK = 256
EXPERT_ROWS = 512
TOKEN_TILE = 1024
ROW_COPY_TILE = 1024
PROJ_TILE = 512
ROUTE_COLS = 8
VMEM_LIMIT = 56 * 1024 * 1024


def _cparams(*sem):
    return pltpu.CompilerParams(dimension_semantics=sem, vmem_limit_bytes=VMEM_LIMIT)


def _split_bf16(a):
    hi = a.astype(BF16)
    lo = (a - hi.astype(F32)).astype(BF16)
    return hi, lo


def _dot(a, b):
    return jnp.dot(a, b, preferred_element_type=F32)


def _dot3(a, b):
    ah, al = _split_bf16(a)
    bh, bl = _split_bf16(b)
    return _dot(ah, bh) + _dot(ah, bl) + _dot(al, bh)


def _pack_halves(x):
    half = x.shape[1] // 2
    hi = lax.bitcast_convert_type(x[:, :half].astype(BF16).astype(F32), jnp.uint32)
    lo = lax.bitcast_convert_type(x[:, half:].astype(BF16).astype(F32), jnp.uint32)
    return hi | (lo >> 16)


def _unpack_halves(u):
    hi = lax.bitcast_convert_type(u & jnp.uint32(BF16_BITS), F32)
    lo = lax.bitcast_convert_type(u << 16, F32)
    return jnp.concatenate([hi, lo], axis=1)


def _rms(x, g):
    ms = jnp.mean(x * x, axis=-1, keepdims=True)
    return x * lax.rsqrt(ms + EPS) * g


def _mod_kernel(c_ref, w_ref, b_ref, o_ref):
    c = c_ref[...]
    ca = c * (1.0 / (1.0 + jnp.exp(-c)))
    o_ref[...] = _dot3(ca, w_ref[...]) + b_ref[...]


def _mod_call(c, w_ada, b_ada):
    bsz, d = c.shape
    n = w_ada.shape[1]
    tn = n // 4 if n % (4 * LANES) == 0 else n
    return pl.pallas_call(
        _mod_kernel,
        out_shape=jax.ShapeDtypeStruct((bsz, n), F32),
        grid=(n // tn,),
        in_specs=[pl.BlockSpec((bsz, d), lambda j: (0, 0)),
                  pl.BlockSpec((d, tn), lambda j: (0, j)),
                  pl.BlockSpec((1, tn), lambda j: (0, j))],
        out_specs=pl.BlockSpec((bsz, tn), lambda j: (0, j)),
        compiler_params=_cparams("arbitrary"),
        name="mod",
    )(c, w_ada, b_ada.reshape(1, n))


def _gelu_tanh(x):
    c = 0.7978845608028654
    hx = 0.5 * x
    return hx + hx * jnp.tanh(x * (c + (0.044715 * c) * (x * x)))


def _proj_kernel(x_ref, sh_ref, sc_ref, g_ref, gsgu_ref, w_ref, q_ref, k_ref, v_ref, uv_ref, gt_ref, *, widths):
    x = x_ref[...]
    n = _rms(x, g_ref[...]) * (1.0 + sc_ref[0]) + sh_ref[0]
    nb = n.astype(BF16)
    off = 0
    for ref, wd in zip((q_ref, k_ref, v_ref, uv_ref, gt_ref), widths):
        p = _dot(nb, w_ref[:, off:off + wd])
        if ref is q_ref:
            p = p * (HEAD_DIM ** -0.5 * LOG2E)
        elif ref is uv_ref:
            act = _gelu_tanh(p)
            sgu = wd // 2
            v = act[:, sgu:]
            vc = v - jnp.mean(v, axis=-1, keepdims=True)
            var = jnp.mean(vc * vc, axis=-1, keepdims=True)
            p = jnp.concatenate([act[:, :sgu], vc * lax.rsqrt(var + EPS) * gsgu_ref[...]], axis=1)
        elif ref is gt_ref:
            p = 1.0 + jnp.tanh(0.5 * p)
        ref[...] = p.astype(ref.dtype)
        off += wd


def _proj_call(x2, sh, sc, g, g_sgu, w_in_bf, seq, widths):
    n_tok, d = x2.shape
    tm = min(PROJ_TILE, seq)
    per_b = seq // tm
    cols = w_in_bf.shape[1]
    tok = lambda w: pl.BlockSpec((tm, w), lambda i: (i, 0))
    vec = pl.BlockSpec((1, 1, d), lambda i: (i // per_b, 0, 0))
    return pl.pallas_call(
        functools.partial(_proj_kernel, widths=widths),
        out_shape=[jax.ShapeDtypeStruct((n_tok, w), BF16) for w in widths],
        grid=(n_tok // tm,),
        in_specs=[tok(d), vec, vec,
                  pl.BlockSpec((1, d), lambda i: (0, 0)),
                  pl.BlockSpec(g_sgu.shape, lambda i: (0, 0)),
                  pl.BlockSpec((d, cols), lambda i: (0, 0))],
        out_specs=[tok(w) for w in widths],
        compiler_params=_cparams("arbitrary"),
        name="proj",
    )(x2, sh, sc, g, g_sgu, w_in_bf)


def _attn_kernel(q_ref, k_ref, v_ref, u_ref, o_ref, qm_ref, acc_ref, carry_ref, *, heads, blk):
    qi = pl.program_id(1)
    u_tri = u_ref[...]
    row = lax.broadcasted_iota(jnp.int32, (blk, blk), 0)
    col = lax.broadcasted_iota(jnp.int32, (blk, blk), 1)
    causal = col < row
    per_slab = LANES // HEAD_DIM
    lane = lax.broadcasted_iota(jnp.int32, (blk, LANES), 1)
    own = [(lane >= j * HEAD_DIM) & (lane < (j + 1) * HEAD_DIM) for j in range(per_slab)]
    for h in range(heads):
        slab = slice(h // per_slab * LANES, (h // per_slab + 1) * LANES)
        qm_ref[h] = jnp.where(own[h % per_slab], q_ref[0, :, slab], 0.0).astype(BF16)

    slabs = [slice(p * LANES, (p + 1) * LANES) for p in range(heads // per_slab)]

    def make_tile(start, first, dead=None):
        mask = causal if first else None
        s_all, cum_all, w_all, carries = {}, {}, {}, []

        def scores(h):
            s_all[h] = lax.dot_general(qm_ref[h], k_ref[0, pl.ds(start, blk), slabs[h // per_slab]],
                                       (((1,), (1,)), ((), ())), preferred_element_type=F32)

        def suffix_sums(h):
            s = s_all[h]
            neg_abs = lax.bitcast_convert_type(lax.bitcast_convert_type(s, jnp.uint32) | jnp.uint32(SIGN_BIT), F32)
            sp = jnp.maximum(s, 0.0) + jnp.log(1.0 + jnp.exp2(neg_abs)) * LOG2E
            if mask is not None:
                sp = jnp.where(mask, sp, 0.0)
            hi = lax.bitcast_convert_type(lax.bitcast_convert_type(sp, jnp.uint32) & jnp.uint32(BF16_BITS), F32)
            cum_all[h] = _dot(hi.astype(BF16), u_tri) + _dot((sp - hi).astype(BF16), u_tri)

        def weights(h):
            cum = cum_all[h]
            if first:
                w = jnp.exp2(s_all[h] - cum)
                carry = cum[:, 0:1]
            else:
                carry = carry_ref[h]
                if dead is not None:
                    carry = jnp.where(dead, DEAD_CARRY, carry)
                w = jnp.exp2((s_all[h] - carry) - cum)
                carry = carry + cum[:, 0:1]
            w_all[h] = (w if mask is None else jnp.where(mask, w, 0.0)).astype(BF16)
            carry_ref[h] = carry
            carries.append(carry)
            if h % per_slab == per_slab - 1:
                p = h // per_slab
                vp = v_ref[0, pl.ds(start, blk), slabs[p]]
                upd = None
                for j in range(per_slab):
                    pv = _dot(w_all[p * per_slab + j], vp)
                    upd = pv if upd is None else jnp.where(own[j], pv, upd)
                acc_ref[:, slabs[p]] = upd if first else acc_ref[:, slabs[p]] + upd

        return scores, suffix_sums, weights, carries

    def run_tiles(tiles):
        jobs = [(tile, h) for tile in tiles for h in range(heads)]
        for r in range(len(jobs) + WEIGHTS_LAG):
            if r < len(jobs):
                tile, h = jobs[r]
                tile[0](h)
            if SUFFIX_LAG <= r < len(jobs) + SUFFIX_LAG:
                tile, h = jobs[r - SUFFIX_LAG]
                tile[1](h)
            if r >= WEIGHTS_LAG:
                tile, h = jobs[r - WEIGHTS_LAG]
                tile[2](h)
        return jnp.min(functools.reduce(jnp.minimum, tiles[-1][3]))

    cmin = run_tiles([make_tile(pl.multiple_of(qi * blk, blk), True),
                      make_tile(pl.multiple_of(jnp.maximum(qi - 1, 0) * blk, blk), False, dead=qi == 0)])

    def cond(state):
        i, cmin = state
        return (i < qi) & (cmin < ATTN_EXIT_LOG2)

    def body(state):
        i, _ = state
        return i + 1, run_tiles([make_tile(pl.multiple_of((qi - 1 - i) * blk, blk), False)])

    lax.while_loop(cond, body, (jnp.int32(1), cmin))
    o_ref[0] = acc_ref[...].astype(o_ref.dtype)


def _attn_call(q, k, v, heads):
    bsz, seq, width = q.shape
    blk = min(ATTN_BLOCK, seq)
    r = lax.broadcasted_iota(jnp.int32, (blk, blk), 0)
    c = lax.broadcasted_iota(jnp.int32, (blk, blk), 1)
    u_tri = (r >= c).astype(BF16)
    full = pl.BlockSpec((1, seq, width), lambda b, i: (b, 0, 0))
    return pl.pallas_call(
        functools.partial(_attn_kernel, heads=heads, blk=blk),
        out_shape=jax.ShapeDtypeStruct((bsz, seq, width), BF16),
        grid=(bsz, seq // blk),
        in_specs=[pl.BlockSpec((1, blk, width), lambda b, i: (b, i, 0)), full, full,
                  pl.BlockSpec((blk, blk), lambda b, i: (0, 0))],
        out_specs=pl.BlockSpec((1, blk, width), lambda b, i: (b, i, 0)),
        scratch_shapes=[pltpu.VMEM((heads, blk, LANES), BF16), pltpu.VMEM((blk, width), F32),
                        pltpu.VMEM((heads, blk, 1), F32)],
        compiler_params=_cparams("arbitrary", "arbitrary"),
        name="attn",
    )(q, k, v, u_tri)


def _sigmoid(x):
    return 1.0 / (1.0 + jnp.exp(-x))


def _mix_kernel(ya_ref, uv_ref, gt_ref, x_ref, gtm_ref, shf_ref, scf_ref, ws_ref, bs_ref,
                wa_ref, wb_ref, wo_ref, gffn_ref, wr_ref, br_ref, utri_ref,
                h_ref, xn_ref, route_ref, route_t_ref, cnt_ref, yb_scr, cnt_scr,
                *, chunk, groups, n_groups, per_group):
    step = pl.program_id(0)
    tm = x_ref.shape[0]
    sgu = uv_ref.shape[1] // 2
    gdim = sgu // groups

    @pl.when(step == 0)
    def _():
        cnt_scr[...] = jnp.zeros_like(cnt_scr)

    ya_proj = _dot(ya_ref[...], wa_ref[...])

    u = uv_ref[:, :sgu].astype(F32)
    vn = uv_ref[:, sgu:]
    r = lax.broadcasted_iota(jnp.int32, (chunk, chunk), 0)
    c = lax.broadcasted_iota(jnp.int32, (chunk, chunk), 1)
    wcs = [jnp.where(r >= c, ws_ref[g], 0.0).astype(BF16) for g in range(groups)]
    pair = LANES // gdim
    lane = lax.broadcasted_iota(jnp.int32, (chunk, LANES), 1)
    for ci in range(tm // chunk):
        rows = slice(ci * chunk, (ci + 1) * chunk)
        for p in range(groups // pair):
            lanes = slice(p * LANES, (p + 1) * LANES)
            slab = vn[rows, lanes]
            mix = bs_ref[:, lanes]
            for j in range(pair):
                sel = (lane >= j * gdim) & (lane < (j + 1) * gdim)
                mix = mix + _dot(wcs[p * pair + j], jnp.where(sel, slab, 0.0).astype(BF16))
            yb_scr[rows, lanes] = (u[rows, lanes] * mix).astype(BF16)

    d = x_ref.shape[1]
    gates2 = gt_ref[...].astype(F32)
    merged2 = gates2[:, :d] * ya_proj + gates2[:, d:] * _dot(yb_scr[...], wb_ref[...])
    h = x_ref[...] + (0.5 * gtm_ref[0]) * _dot(merged2.astype(BF16), wo_ref[...])
    h_ref[...] = h

    xn = _rms(h, gffn_ref[...]) * (1.0 + scf_ref[0]) + shf_ref[0]
    xn_ref[...] = _pack_halves(xn)
    logits = _dot3(xn, wr_ref[...]) + br_ref[...]

    lt = logits.T
    sub = lax.broadcasted_iota(jnp.int32, (SUBLANES, tm), 0)
    neg = -jnp.inf
    gl = jnp.where(sub < n_groups, lt[0:SUBLANES], neg)
    gmax = jnp.max(gl, axis=0, keepdims=True)
    g_sel = jnp.min(jnp.where(gl == gmax, sub, SUBLANES), axis=0, keepdims=True)
    g_w = 1.0 / jnp.sum(jnp.exp(gl - gmax), axis=0, keepdims=True)
    el = lt[SUBLANES:2 * SUBLANES]
    for g in range(1, n_groups):
        el = jnp.where(g_sel == g, lt[(g + 1) * SUBLANES:(g + 2) * SUBLANES], el)
    m1 = jnp.max(el, axis=0, keepdims=True)
    j1 = jnp.min(jnp.where(el == m1, sub, SUBLANES), axis=0, keepdims=True)
    el2 = jnp.where(sub == j1, neg, el)
    m2 = jnp.max(el2, axis=0, keepdims=True)
    j2 = jnp.min(jnp.where(el2 == m2, sub, SUBLANES), axis=0, keepdims=True)
    t = jnp.exp(m2 - m1)
    w1 = g_w / (1.0 + t)
    w2 = g_w * t / (1.0 + t)
    e1 = g_sel * per_group + j1
    e2 = g_sel * per_group + j2

    row = lax.broadcasted_iota(jnp.int32, (LANES, tm), 0)
    oh1 = row == e1 + SUBLANES
    oh2 = row == e2 + SUBLANES
    both = jnp.where(oh1 | oh2, 1.0, 0.0)
    before = _dot(both.astype(BF16), utri_ref[...]) + cnt_scr[...]
    rank1 = jnp.sum(jnp.where(oh1, before, 0.0), axis=0, keepdims=True)
    rank2 = jnp.sum(jnp.where(oh2, before, 0.0), axis=0, keepdims=True)
    cnt_scr[...] = cnt_scr[...] + jnp.sum(both, axis=1, keepdims=True)
    cnt_ref[...] = cnt_scr[...]

    rec = jnp.zeros((LANES, tm), F32)
    for idx, val in enumerate((e1.astype(F32), e2.astype(F32), w1, w2, rank1, rank2)):
        rec = jnp.where(row == idx, val, rec)
    route_t_ref[...] = rec[:ROUTE_COLS]
    route_ref[...] = rec.T[:, :ROUTE_COLS]


def _mix_call(ya, uv, gates, x2, gtm, shf, scf, w_spatial, bias_full, wa, wb, wo, g_ffn, wr, br,
              seq, n_groups, per_group):
    n_tok, d = x2.shape
    tm = min(TOKEN_TILE, seq)
    per_b = seq // tm
    groups, chunk, _ = w_spatial.shape
    sgu = uv.shape[1] // 2
    r = lax.broadcasted_iota(jnp.int32, (tm, tm), 0)
    c = lax.broadcasted_iota(jnp.int32, (tm, tm), 1)
    utri = (r < c).astype(BF16)
    tok = lambda w: pl.BlockSpec((tm, w), lambda i: (i, 0))
    vec = pl.BlockSpec((1, 1, d), lambda i: (i // per_b, 0, 0))
    const = lambda a: pl.BlockSpec(a.shape, lambda i: (0,) * a.ndim)
    ins = [ya, uv, gates, x2, gtm, shf, scf, w_spatial, bias_full, wa, wb, wo, g_ffn, wr, br, utri]
    in_specs = [tok(ya.shape[1]), tok(uv.shape[1]), tok(gates.shape[1]), tok(d), vec, vec, vec]
    in_specs += [const(a) for a in ins[7:]]
    return pl.pallas_call(
        functools.partial(_mix_kernel, chunk=chunk, groups=groups, n_groups=n_groups, per_group=per_group),
        out_shape=[jax.ShapeDtypeStruct((n_tok, d), F32), jax.ShapeDtypeStruct((n_tok, d // 2), jnp.uint32),
                   jax.ShapeDtypeStruct((n_tok, ROUTE_COLS), F32), jax.ShapeDtypeStruct((ROUTE_COLS, n_tok), F32),
                   jax.ShapeDtypeStruct((LANES, 1), F32)],
        grid=(n_tok // tm,),
        in_specs=in_specs,
        out_specs=[tok(d), tok(d // 2), tok(ROUTE_COLS), pl.BlockSpec((ROUTE_COLS, tm), lambda i: (0, i)),
                   pl.BlockSpec((LANES, 1), lambda i: (0, 0))],
        scratch_shapes=[pltpu.VMEM((tm, sgu), BF16), pltpu.VMEM((LANES, 1), F32)],
        compiler_params=_cparams("arbitrary"),
        name="mix",
    )(*ins)


def _row_copy(src, s, dst, d, sem):
    return pltpu.make_async_copy(src.at[pl.ds(s, 1)], dst.at[pl.ds(d, 1)], sem)


def _dispatch_kernel(zblk_ref, *refs, tm, rows):
    dest_refs, (xn_ref, xs_ref, zero_scr, src_scr, sem, zsem) = refs[:TOP_K], refs[TOP_K:]
    step = pl.program_id(0)

    def wait_rows():
        for _ in range(TOP_K):
            pltpu.make_async_copy(src_scr, xs_ref.at[pl.ds(0, tm)], sem).wait()

    @pl.when(step == 0)
    def _():
        zero_scr[...] = jnp.zeros_like(zero_scr)

        def zero_copy(j):
            start = pl.multiple_of(zblk_ref[j] * rows, rows)
            return pltpu.make_async_copy(zero_scr, xs_ref.at[pl.ds(start, rows)], zsem)

        for j in range(zblk_ref.shape[0]):
            pl.when(zblk_ref[j] >= 0)(lambda j=j: zero_copy(j).start())
        for j in range(zblk_ref.shape[0]):
            pl.when(zblk_ref[j] >= 0)(lambda j=j: zero_copy(j).wait())

    pl.when(step > 0)(wait_rows)
    src_scr[...] = xn_ref[...]
    for j in range(tm):
        for k, dref in enumerate(dest_refs):
            _row_copy(src_scr, j, xs_ref, dref[0, 0, j], sem).start(priority=k % 2)
    pl.when(step == pl.num_programs(0) - 1)(wait_rows)


def _dest_blocks(dest, tm):
    return [dest[k].reshape(-1, 1, tm) for k in range(TOP_K)]


def _dispatch_call(zero_blocks, dest, xn, m_pad, seq):
    n_tok, dw = xn.shape
    tm = min(ROW_COPY_TILE, seq)
    smem = pl.BlockSpec((1, 1, tm), lambda i, zb: (i, 0, 0), memory_space=pltpu.SMEM)
    return pl.pallas_call(
        functools.partial(_dispatch_kernel, tm=tm, rows=EXPERT_ROWS),
        out_shape=jax.ShapeDtypeStruct((m_pad, dw), xn.dtype),
        grid_spec=pltpu.PrefetchScalarGridSpec(
            num_scalar_prefetch=1,
            grid=(n_tok // tm,),
            in_specs=[smem] * TOP_K + [pl.BlockSpec((tm, dw), lambda i, zb: (i, 0))],
            out_specs=pl.BlockSpec(memory_space=pl.ANY),
            scratch_shapes=[pltpu.VMEM((EXPERT_ROWS, dw), xn.dtype), pltpu.VMEM((tm, dw), xn.dtype),
                            pltpu.SemaphoreType.DMA(()), pltpu.SemaphoreType.DMA(())]),
        compiler_params=_cparams("arbitrary"),
        name="dispatch",
    )(zero_blocks, *_dest_blocks(dest, tm), xn)


def _expert_kernel(blk_e_ref, used_ref, xs_ref, wg_ref, wu_ref, wd_ref, y_ref, wg_bf, wu_bf, wd_bf):
    i = pl.program_id(0)
    live = i < used_ref[0]
    new_expert = (i == 0) | (blk_e_ref[i] != blk_e_ref[jnp.maximum(i - 1, 0)])

    @pl.when(live & new_expert)
    def _():
        wg_bf[...] = wg_ref[0].astype(BF16)
        wu_bf[...] = wu_ref[0].astype(BF16)
        wd_bf[...] = wd_ref[0].astype(BF16)

    @pl.when(live)
    def _():
        xb = _unpack_halves(xs_ref[...]).astype(BF16)
        g = _dot(xb, wg_bf[...])
        hid = g * _sigmoid(g) * _dot(xb, wu_bf[...])
        y_ref[...] = _pack_halves(_dot(hid.astype(BF16), wd_bf[...]))

    @pl.when(jnp.logical_not(live))
    def _():
        y_ref[...] = jnp.zeros_like(y_ref)


def _expert_call(blk_e, used, xs, wg, wu, wd):
    m_pad, dw = xs.shape
    _, d, f = wg.shape
    rows = EXPERT_ROWS
    return pl.pallas_call(
        _expert_kernel,
        out_shape=jax.ShapeDtypeStruct((m_pad, dw), xs.dtype),
        grid_spec=pltpu.PrefetchScalarGridSpec(
            num_scalar_prefetch=2,
            grid=(m_pad // rows,),
            in_specs=[pl.BlockSpec((rows, dw), lambda i, be, us: (i, 0)),
                      pl.BlockSpec((1, d, f), lambda i, be, us: (be[i], 0, 0)),
                      pl.BlockSpec((1, d, f), lambda i, be, us: (be[i], 0, 0)),
                      pl.BlockSpec((1, f, d), lambda i, be, us: (be[i], 0, 0))],
            out_specs=pl.BlockSpec((rows, dw), lambda i, be, us: (i, 0)),
            scratch_shapes=[pltpu.VMEM((d, f), BF16), pltpu.VMEM((d, f), BF16), pltpu.VMEM((f, d), BF16)]),
        compiler_params=_cparams("arbitrary"),
        name="experts",
    )(blk_e, used, xs, wg, wu, wd)


def _combine_kernel(*refs, tm, final):
    cur_refs, nxt_refs = refs[:TOP_K], refs[TOP_K:2 * TOP_K]
    h_ref, route_ref, gtf_ref, gfin_ref, y_ref, o_ref, land, rows, sem = refs[2 * TOP_K:]
    step = pl.program_id(0)

    def gather(dest_refs):
        for j in range(tm):
            for k, dref in enumerate(dest_refs):
                pltpu.make_async_copy(y_ref.at[pl.ds(dref[0, 0, j], 1)], land.at[k, pl.ds(j, 1)],
                                      sem).start(priority=k % 2)

    def wait_rows():
        for k in range(TOP_K):
            pltpu.make_async_copy(y_ref.at[pl.ds(0, tm)], land.at[k], sem).wait()

    pl.when(step == 0)(lambda: gather(cur_refs))
    wait_rows()
    rows[...] = land[...]
    gather(nxt_refs)
    route = route_ref[...]
    y = route[:, 2:3] * _unpack_halves(rows[0]) + route[:, 3:4] * _unpack_halves(rows[1])
    h = h_ref[...] + gtf_ref[0] * y
    o_ref[...] = _rms(h, gfin_ref[...]) if final else h
    pl.when(step == pl.num_programs(0) - 1)(wait_rows)


def _combine_call(dest, h, route, gtf, g_final, yb, seq, final):
    n_tok, d = h.shape
    tm = min(ROW_COPY_TILE, seq)
    per_b = seq // tm
    steps = n_tok // tm
    tok = lambda w: pl.BlockSpec((tm, w), lambda i: (i, 0))
    cur = pl.BlockSpec((1, 1, tm), lambda i: (i, 0, 0), memory_space=pltpu.SMEM)
    nxt = pl.BlockSpec((1, 1, tm), lambda i: (jnp.minimum(i + 1, steps - 1), 0, 0), memory_space=pltpu.SMEM)
    blocks = _dest_blocks(dest, tm)
    return pl.pallas_call(
        functools.partial(_combine_kernel, tm=tm, final=final),
        out_shape=jax.ShapeDtypeStruct((n_tok, d), F32),
        grid=(steps,),
        in_specs=[cur] * TOP_K + [nxt] * TOP_K + [
            tok(d), tok(ROUTE_COLS),
            pl.BlockSpec((1, 1, d), lambda i: (i // per_b, 0, 0)),
            pl.BlockSpec((1, d), lambda i: (0, 0)),
            pl.BlockSpec(memory_space=pl.ANY)],
        out_specs=tok(d),
        scratch_shapes=[pltpu.VMEM((TOP_K, tm, yb.shape[1]), yb.dtype),
                        pltpu.VMEM((TOP_K, tm, yb.shape[1]), yb.dtype), pltpu.SemaphoreType.DMA(())],
        compiler_params=_cparams("arbitrary"),
        name="combine",
    )(*blocks, *blocks, h, route, gtf, g_final, yb)


def _routing_plan(route_t, counts, n_experts, m_pad):
    eid = route_t[0:TOP_K].astype(jnp.int32)
    rank = route_t[4:4 + TOP_K].astype(jnp.int32)
    cnt = counts.astype(jnp.int32)
    padded = (cnt + EXPERT_ROWS - 1) // EXPERT_ROWS * EXPERT_ROWS
    pend = jnp.cumsum(padded)
    pstart = pend - padded
    dest = rank
    for e in range(n_experts):
        dest = dest + jnp.where(eid == e, pstart[e], 0)
    nblk = m_pad // EXPERT_ROWS
    blk_row = jnp.arange(nblk, dtype=jnp.int32) * EXPERT_ROWS
    blk_e = jnp.minimum(jnp.sum(pend[None, :] <= blk_row[:, None], axis=1), n_experts - 1).astype(jnp.int32)
    used = (pend[-1:] // EXPERT_ROWS).astype(jnp.int32)
    last_blk = jnp.where(padded > 0, pend // EXPERT_ROWS - 1, -1)
    tail_blk = used + jnp.arange(n_experts, dtype=jnp.int32)
    tail_blk = jnp.where(tail_blk < nblk, tail_blk, -1)
    zero_blocks = jnp.concatenate([last_blk, tail_blk]).astype(jnp.int32)
    return dest, blk_e, used, zero_blocks


def kernel(x, c, g_mix, g_ffn, w_ada, b_ada, w_in, w_sba_out, g_sgu, w_spatial, b_spatial, w_sgu_out, w_out,
           w_router_group, b_router_group, w_router_expert, b_router_expert, w_expert_gate, w_expert_up,
           w_expert_down, g_final):
    bsz, seq, d = x.shape
    depth = w_in.shape[0]
    sba = w_sba_out.shape[1]
    sgu = g_sgu.shape[1]
    heads = sba // HEAD_DIM
    groups, chunk = w_spatial.shape[1], w_spatial.shape[2]
    n_groups = w_router_group.shape[2]
    n_experts = w_router_expert.shape[2]
    per_group = n_experts // n_groups
    n_tok = bsz * seq
    m_pad = n_tok * TOP_K + n_experts * EXPERT_ROWS
    widths = (sba, sba, sba, 2 * sgu, 2 * d)
    assert seq % chunk == 0 and LANES % (sgu // groups) == 0
    assert d % (2 * LANES) == 0
    assert per_group == SUBLANES and n_groups <= SUBLANES and (n_groups + 1) * SUBLANES <= LANES

    h = x.reshape(n_tok, d)
    for l in range(depth):
        mod = _mod_call(c, w_ada[l], b_ada[l])
        sh_m, sc_m, gt_m, sh_f, sc_f, gt_f = [mod[:, i * d:(i + 1) * d].reshape(bsz, 1, d) for i in range(6)]

        q, k, v, uv, gates = _proj_call(h, sh_m, sc_m, g_mix[l].reshape(1, d), g_sgu[l].reshape(1, sgu),
                                        w_in[l].astype(BF16), seq, widths)
        ya = _attn_call(q.reshape(bsz, seq, sba), k.reshape(bsz, seq, sba), v.reshape(bsz, seq, sba), heads)

        bias_full = jnp.repeat(b_spatial[l].T, sgu // groups, axis=1)
        gpad = SUBLANES - n_groups
        wr = jnp.concatenate([jnp.pad(w_router_group[l], ((0, 0), (0, gpad))), w_router_expert[l]], axis=1)
        wr = jnp.pad(wr, ((0, 0), (0, LANES - wr.shape[1])))
        br = jnp.concatenate([jnp.pad(b_router_group[l], (0, gpad)), b_router_expert[l]])
        br = jnp.pad(br, (0, LANES - br.shape[0])).reshape(1, LANES)
        h1, xn, route, route_t, counts = _mix_call(
            ya.reshape(n_tok, sba), uv, gates, h, gt_m, sh_f, sc_f, w_spatial[l],
            bias_full, w_sba_out[l].astype(BF16), w_sgu_out[l].astype(BF16), w_out[l].astype(BF16),
            g_ffn[l].reshape(1, d), wr, br, seq, n_groups, per_group)

        dest, blk_e, used, zero_blocks = _routing_plan(route_t, counts[SUBLANES:SUBLANES + n_experts, 0],
                                                       n_experts, m_pad)
        xs = _dispatch_call(zero_blocks, dest, xn, m_pad, seq)
        yb = _expert_call(blk_e, used, xs, w_expert_gate[l], w_expert_up[l], w_expert_down[l])
        h = _combine_call(dest, h1, route, gt_f, g_final.reshape(1, d), yb, seq, final=l == depth - 1)
    return h.reshape(bsz, seq, d)
```
